```python
import jax, jax.numpy as jnp
from jax import lax
import numpy as np

D_MODEL = 1024
BATCH = 2
SEQ = 8192
DEPTH = 2

D_MIX = D_MODEL
EPS = 1e-6
D_A = D_MIX // 2
HD_A = 64
H_A = D_A // HD_A
DILATED_CFGS = ((128, 1), (512, 4), (2048, 16))
Q_BLOCK = 128
NUM_BUCKETS = 32
REL_MAX_DIST = 1024
D_B = D_MIX // 4
HD_B = 64
H_B = D_B // HD_B
MLSTM_CHUNK = 64
D_C = D_MIX - D_A - D_B
HD_C = 64
H_C = D_C // HD_C
GDN_CHUNK = 64
CONV_W = 5
_SIZES = (D_A, D_A, D_A, D_A,
          D_B, D_B, D_B, 4 * H_B, D_B, D_B,
          3 * D_C, 4 * H_C, D_C)
N_IN = sum(_SIZES)

kernel_name = 'hybrid_dilated_mlstm_gdn_encoder'


def _rmsnorm(t, w):
    tf = t.astype(jnp.float32)
    tf = tf * lax.rsqrt(jnp.mean(tf * tf, axis=-1, keepdims=True) + EPS)
    return (tf * w.astype(jnp.float32)).astype(t.dtype)


def _l2norm(t):
    return t * lax.rsqrt(jnp.sum(t * t, axis=-1, keepdims=True) + EPS)


def _heads(t, n_heads):
    b, s, w = t.shape
    return t.reshape(b, s, n_heads, w // n_heads).transpose(0, 2, 1, 3)


def _merge(t):
    b, h, s, d = t.shape
    return t.transpose(0, 2, 1, 3).reshape(b, s, h * d)


def _gates(t, n_heads):
    b, s, _ = t.shape
    return t.reshape(b, s, 4, n_heads).transpose(2, 0, 3, 1).astype(jnp.float32)


def _rev(t):
    return jnp.flip(t, axis=2)


def _t5_bucket(rel):
    half = NUM_BUCKETS // 2
    max_exact = half // 2
    n = np.abs(rel)
    large = max_exact + (np.log(np.maximum(n, 1) / max_exact) / np.log(REL_MAX_DIST / max_exact)
                         * (half - max_exact)).astype(np.int32)
    large = np.minimum(large, half - 1)
    return (rel > 0).astype(np.int32) * half + np.where(n < max_exact, n, large)


def _dilated_attention(q, k, v, rel_bias):
    s_len = q.shape[2]
    cfgs = []
    for (window, dil) in DILATED_CFGS:
        n_side = (window // 2) // dil
        offs = dil * np.arange(-n_side, n_side + 1)
        bias = rel_bias[_t5_bucket(offs)].T.astype(jnp.float32)
        cfgs.append((jnp.asarray(offs, jnp.int32), bias))

    def block(nb):
        start = nb * Q_BLOCK
        pos = start + jnp.arange(Q_BLOCK)
        qb = lax.dynamic_slice_in_dim(q, start, Q_BLOCK, axis=2)
        outs, lses = [], []
        for offs, bias in cfgs:
            idx = pos[:, None] + offs[None, :]
            valid = (idx >= 0) & (idx < s_len)
            idx = jnp.clip(idx, 0, s_len - 1)
            kg = jnp.take(k, idx, axis=2)
            vg = jnp.take(v, idx, axis=2)
            logits = jnp.einsum('bhqd,bhqjd->bhqj', qb, kg) + bias[None, :, None, :]
            logits = jnp.where(valid[None, None], logits, -jnp.inf)
            lse = jax.nn.logsumexp(logits, axis=-1)
            p = jnp.exp(logits - lse[..., None])
            outs.append(jnp.einsum('bhqj,bhqjd->bhqd', p, vg))
            lses.append(lse)
        wts = jax.nn.softmax(jnp.stack(lses), axis=0)
        return jnp.einsum('cbhq,cbhqd->bhqd', wts, jnp.stack(outs))

    o = lax.map(block, jnp.arange(s_len // Q_BLOCK))
    return o.transpose(1, 2, 0, 3, 4).reshape(q.shape)


def _mlstm_chunked(q, k, v, log_i, log_f):
    b, h, s_len, d = q.shape
    L = MLSTM_CHUNK
    nc = s_len // L

    def chunks(t):
        return jnp.moveaxis(t.reshape(b, h, nc, L, *t.shape[3:]), 2, 0)

    xs = tuple(map(chunks, (q * d ** -0.5, k, v, log_i, log_f)))
    causal = jnp.tril(jnp.ones((L, L), bool))

    def step(carry, inp):
        c_st, n_st, m_st = carry
        qt, kt, vt, it, ft = inp
        bcum = jnp.cumsum(ft, axis=-1)
        dmat = jnp.where(causal, bcum[..., :, None] - bcum[..., None, :] + it[..., None, :], -jnp.inf)
        inter = bcum + m_st[..., None]
        m_t = jnp.maximum(jnp.max(dmat, axis=-1), inter)
        sc = jnp.einsum('bhtd,bhsd->bhts', qt, kt) * jnp.exp(dmat - m_t[..., None])
        iw = jnp.exp(inter - m_t)
        num = iw[..., None] * jnp.einsum('bhtd,bhde->bhte', qt, c_st) + jnp.einsum('bhts,bhse->bhte', sc, vt)
        den = iw * jnp.einsum('bhtd,bhd->bht', qt, n_st) + jnp.sum(sc, axis=-1)
        h_t = num / jnp.maximum(jnp.abs(den), jnp.exp(-m_t))[..., None]
        b_last = bcum[..., -1]
        w_log = b_last[..., None] - bcum + it
        m_new = jnp.maximum(b_last + m_st, jnp.max(w_log, axis=-1))
        ws = jnp.exp(w_log - m_new[..., None])
        dec = jnp.exp(b_last + m_st - m_new)
        c_st = dec[..., None, None] * c_st + jnp.einsum('bhs,bhsd,bhse->bhde', ws, kt, vt)
        n_st = dec[..., None] * n_st + jnp.einsum('bhs,bhsd->bhd', ws, kt)
        return (c_st, n_st, m_new), h_t

    init = (jnp.zeros((b, h, d, v.shape[-1]), jnp.float32), jnp.zeros((b, h, d), jnp.float32),
            jnp.zeros((b, h), jnp.float32))
    _, hs = lax.scan(step, init, xs)
    return jnp.moveaxis(hs, 0, 2).reshape(b, h, s_len, -1)


def _gated_delta_chunked(q, k, v, beta, g):
    b, h, s_len, dk = q.shape
    L = GDN_CHUNK
    nc = s_len // L

    def rs(t):
        return jnp.moveaxis(t.reshape(b, h, nc, L, *t.shape[3:]), 2, 0)

    q, k, v, beta, g = map(rs, (q, k, v, beta, g))
    gam = jnp.cumsum(g, axis=-1)
    ar = jnp.arange(L)
    incl = ar[:, None] >= ar[None, :]
    strict = ar[:, None] > ar[None, :]
    decay = jnp.exp(jnp.where(incl, gam[..., :, None] - gam[..., None, :], -jnp.inf))
    m = jnp.where(strict, beta[..., :, None] * jnp.einsum('nbhid,nbhjd->nbhij', k, k) * decay, 0.0)
    eye = jnp.eye(L, dtype=m.dtype)
    t_inv = lax.linalg.triangular_solve(m + eye, jnp.broadcast_to(eye, m.shape), left_side=True,
                                        lower=True, unit_diagonal=True)
    u = jnp.einsum('nbhij,nbhjd->nbhid', t_inv, v * beta[..., None])
    w = jnp.einsum('nbhij,nbhjd->nbhid', t_inv, k * (beta * jnp.exp(gam))[..., None])
    qg = q * jnp.exp(gam)[..., None]
    a_intra = jnp.einsum('nbhid,nbhjd->nbhij', q, k) * decay
    g_last = gam[..., -1]
    k_dec = k * jnp.exp(g_last[..., None] - gam)[..., None]

    def step(state, inp):
        u_c, w_c, qg_c, a_c, kd_c, gl_c = inp
        v_new = u_c - jnp.einsum('bhid,bhde->bhie', w_c, state)
        o = jnp.einsum('bhid,bhde->bhie', qg_c, state) + jnp.einsum('bhij,bhje->bhie', a_c, v_new)
        state = state * jnp.exp(gl_c)[..., None, None] + jnp.einsum('bhid,bhie->bhde', kd_c, v_new)
        return state, o

    s0 = jnp.zeros((b, h, dk, v.shape[-1]), jnp.float32)
    _, o = lax.scan(step, s0, (u, w, qg, a_intra, k_dec, g_last))
    return jnp.moveaxis(o, 0, 2).reshape(b, h, s_len, -1)


def _short_conv(u, w):
    ch = u.shape[-1]
    return lax.conv_general_dilated(u, w[:, None, :], window_strides=(1,),
                                    padding=[(CONV_W // 2, CONV_W // 2)],
                                    dimension_numbers=('NWC', 'WIO', 'NWC'), feature_group_count=ch)


def _hybrid_layer(x, norm_w, w_in, w_out, qk_norm_w, rel_bias, i_bias, f_bias, m_norm_w,
                  conv_w, a_log, dt_bias, g_norm_w):
    f32 = jnp.float32
    hn = _rmsnorm(x, norm_w)
    proj = hn @ w_in
    splits = [int(s) for s in np.cumsum(_SIZES)[:-1]]
    (a_q, a_k, a_v, a_z, b_q, b_k, b_v, b_if, b_o, b_z, c_qkv, c_ab, c_z) = jnp.split(proj, splits, axis=-1)

    qa = _rmsnorm(_heads(a_q, H_A).astype(f32), qk_norm_w[0]) * HD_A ** -0.5
    ka = _rmsnorm(_heads(a_k, H_A).astype(f32), qk_norm_w[1])
    va = _heads(a_v, H_A).astype(f32)
    y_a = _merge(_dilated_attention(qa, ka, va, rel_bias)) * jax.nn.silu(a_z.astype(f32))

    qb, kb, vb = (_heads(t, H_B).astype(f32) for t in (b_q, b_k, b_v))
    gb = _gates(b_if, H_B)
    log_i = gb[0:2] + i_bias.astype(f32)[:, None, :, None]
    log_f = jax.nn.log_sigmoid(gb[2:4] + f_bias.astype(f32)[:, None, :, None])
    h_fwd = _mlstm_chunked(qb, kb, vb, log_i[0], log_f[0])
    h_bwd = _rev(_mlstm_chunked(_rev(qb), _rev(kb), _rev(vb), _rev(log_i[1]), _rev(log_f[1])))
    h_b = jax.nn.sigmoid(_heads(b_o, H_B).astype(f32)) * (h_fwd + h_bwd)
    y_b = _merge(_rmsnorm(h_b, m_norm_w)) * jax.nn.silu(b_z.astype(f32))

    cqkv = jax.nn.silu(_short_conv(c_qkv.astype(f32), conv_w.astype(f32)))
    c_q, c_k, c_v = jnp.split(cqkv, 3, axis=-1)
    qc = _l2norm(_heads(c_q, H_C)) * HD_C ** -0.5
    kc = _l2norm(_heads(c_k, H_C))
    vc = _heads(c_v, H_C)
    gc = _gates(c_ab, H_C)
    g_log = -jnp.exp(a_log.astype(f32))[:, None, :, None] * jax.nn.softplus(
        gc[0:2] + dt_bias.astype(f32)[:, None, :, None])
    beta = jax.nn.sigmoid(gc[2:4])
    o_fwd = _gated_delta_chunked(qc, kc, vc, beta[0], g_log[0])
    o_bwd = _rev(_gated_delta_chunked(_rev(qc), _rev(kc), _rev(vc), _rev(beta[1]), _rev(g_log[1])))
    y_c = _merge(_rmsnorm(o_fwd + o_bwd, g_norm_w)) * jax.nn.silu(c_z.astype(f32))

    y = jnp.concatenate([y_a, y_b, y_c], axis=-1).astype(x.dtype) @ w_out
    return x + y


def setup_inputs(seed: int = 0) -> dict:
    key = jax.random.key(seed)
    ks = jax.random.split(key, 14)
    nrm = jax.random.normal
    x = nrm(ks[0], (BATCH, SEQ, D_MODEL), jnp.float32)
    norm_w = 1.0 + 0.02 * nrm(ks[1], (DEPTH, D_MODEL), jnp.float32)
    w_in = nrm(ks[2], (DEPTH, D_MODEL, N_IN), jnp.float32) * D_MODEL ** -0.5
    w_out = nrm(ks[3], (DEPTH, D_MIX, D_MODEL), jnp.float32) * D_MIX ** -0.5
    qk_norm_w = 1.0 + 0.02 * nrm(ks[4], (DEPTH, 2, HD_A), jnp.float32)
    rel_bias = 0.1 * nrm(ks[5], (NUM_BUCKETS, H_A), jnp.float32)
    mlstm_i_bias = 0.1 * nrm(ks[6], (DEPTH, 2, H_B), jnp.float32)
    mlstm_f_bias = 3.0 + 3.0 * jax.random.uniform(ks[7], (DEPTH, 2, H_B), jnp.float32)
    mlstm_norm_w = 1.0 + 0.02 * nrm(ks[8], (DEPTH, HD_B), jnp.float32)
    gdn_conv_w = nrm(ks[9], (DEPTH, CONV_W, 3 * D_C), jnp.float32) * CONV_W ** -0.5
    gdn_a_log = jnp.log(jax.random.uniform(ks[10], (DEPTH, 2, H_C), jnp.float32, 1.0, 16.0))
    dt = jnp.exp(jax.random.uniform(ks[11], (DEPTH, 2, H_C), jnp.float32, np.log(1e-3), np.log(1e-1)))
    gdn_dt_bias = dt + jnp.log(-jnp.expm1(-dt))
    gdn_norm_w = 1.0 + 0.02 * nrm(ks[12], (DEPTH, HD_C), jnp.float32)
    return {'x': x, 'norm_w': norm_w, 'w_in': w_in, 'w_out': w_out, 'qk_norm_w': qk_norm_w,
            'rel_bias': rel_bias, 'mlstm_i_bias': mlstm_i_bias, 'mlstm_f_bias': mlstm_f_bias,
            'mlstm_norm_w': mlstm_norm_w, 'gdn_conv_w': gdn_conv_w, 'gdn_a_log': gdn_a_log,
            'gdn_dt_bias': gdn_dt_bias, 'gdn_norm_w': gdn_norm_w}


def reference(x, norm_w, w_in, w_out, qk_norm_w, rel_bias, mlstm_i_bias, mlstm_f_bias,
              mlstm_norm_w, gdn_conv_w, gdn_a_log, gdn_dt_bias, gdn_norm_w):
    for l in range(DEPTH):
        x = _hybrid_layer(x, norm_w[l], w_in[l], w_out[l], qk_norm_w[l], rel_bias,
                          mlstm_i_bias[l], mlstm_f_bias[l], mlstm_norm_w[l], gdn_conv_w[l],
                          gdn_a_log[l], gdn_dt_bias[l], gdn_norm_w[l])
    return x
```

```python
import functools

import numpy as np
import jax
import jax.numpy as jnp
from jax import lax
from jax.experimental import pallas as pl
from jax.experimental.pallas import tpu as pltpu

F32 = jnp.float32
BF16 = jnp.bfloat16

D_MODEL = 1024
DEPTH = 2
EPS = 1e-6
HEAD_DIM = 64
LANES = 128
SUBLANES = 8
D_A, D_B, D_C = 512, 256, 256
DILATED_CFGS = ((128, 1), (512, 4), (2048, 16))
N_SIDE = 64
NUM_BUCKETS = 32
REL_MAX_DIST = 1024
CONV_W = 5
CHUNK = 256
Q_SUPER = 2048
Q_BLK = 128
K_WIN = 256
IN_TILE = 256
OUT_TILE = 512
NEG = -1e30
VMEM_LIMIT = 56 * 1024 * 1024

_OFF_AQ, _OFF_AZ = 0, 1536
_OFF_BQ, _OFF_BIF, _OFF_BO, _OFF_BZ = 2048, 2816, 2832, 3088
_OFF_CQ, _OFF_CAB, _OFF_CZ = 3344, 4112, 4128


def _dot(a, b):
    return jnp.dot(a, b, preferred_element_type=F32)


def _dot_nt(a, b):
    return lax.dot_general(a, b, (((1,), (1,)), ((), ())), preferred_element_type=F32)


def _dot_tn(a, b):
    return lax.dot_general(a, b, (((0,), (0,)), ((), ())), preferred_element_type=F32)


def _split(a):
    hi = a.astype(BF16)
    lo = (a - hi.astype(F32)).astype(BF16)
    return hi, lo


def _iota2(shape, axis):
    return lax.broadcasted_iota(jnp.int32, shape, axis)


def _pair_block_diag(rows, cols):
    r = _iota2((rows, cols), 0)
    c = _iota2((rows, cols), 1)
    return (r < HEAD_DIM) == ((c & HEAD_DIM) == 0)


def _head_sum(t, bd):
    hi, lo = _split(t)
    return _dot(hi, bd) + _dot(lo, bd)


def _softplus(y):
    return jnp.maximum(y, 0.0) + jnp.log1p(jnp.exp(-jnp.abs(y)))


def _sigmoid(y):
    return 1.0 / (1.0 + jnp.exp(-y))


def _in_proj_kernel(x_ref, xp_ref, xn_ref, nw_ref, wa_ref, wb_ref, wc_ref, wg_ref, wz_ref,
                    qkw_ref, gpar_ref, cw_ref,
                    a_ref, b_ref, c_ref, g_ref, z_ref, cext_ref):
    i = pl.program_id(1)
    n = pl.num_programs(1)
    tile = x_ref.shape[1]
    bd = _pair_block_diag(LANES, LANES).astype(BF16)

    xe = jnp.concatenate([xp_ref[0], x_ref[0], xn_ref[0]], axis=0)
    ms = jnp.mean(xe * xe, axis=-1, keepdims=True)
    hne = xe * lax.rsqrt(ms + EPS) * nw_ref[...]
    hn = hne[SUBLANES:SUBLANES + tile].astype(BF16)
    hne = hne.astype(BF16)

    for j in range(6):
        t = _dot(hn, wa_ref[:, j * 256:(j + 1) * 256])
        for half in range(2):
            th = t[:, half * LANES:(half + 1) * LANES]
            if j < 4:
                ss = _head_sum(th * th, bd)
                th = th * lax.rsqrt(ss * (1.0 / HEAD_DIM) + EPS) * qkw_ref[j // 2:j // 2 + 1, :]
            col = j * 256 + half * LANES
            a_ref[0, :, col:col + LANES] = th

    tb = _dot(hn, wb_ref[...])
    b_ref[0, :, 0:D_B] = (tb[:, 0:D_B] * HEAD_DIM ** -0.5).astype(b_ref.dtype)
    b_ref[0, :, D_B:3 * D_B] = tb[:, D_B:3 * D_B].astype(b_ref.dtype)

    p = _dot(hn, wg_ref[...]) + gpar_ref[0:1, :]
    lane = _iota2(p.shape, 1)
    first4 = (lane & (LANES - 1)) < 4
    is_b = lane < 2 * LANES
    val_b = jnp.where(first4, p, -_softplus(-p))
    val_c = jnp.where(first4, -jnp.exp(gpar_ref[1:2, :]) * _softplus(p), _sigmoid(p))
    g_ref[0] = jnp.where(is_b, val_b, val_c)

    ce = _dot(hne, wc_ref[...])
    row = _iota2((tile + 2 * SUBLANES, 1), 0)
    valid = ((row >= SUBLANES) | (i > 0)) & ((row < tile + SUBLANES) | (i < n - 1))
    cext_ref[...] = jnp.where(valid, ce, 0.0)
    conv = cw_ref[0:1, :] * cext_ref[pl.ds(SUBLANES - CONV_W // 2, tile), :]
    for t in range(1, CONV_W):
        conv = conv + cw_ref[t:t + 1, :] * cext_ref[pl.ds(SUBLANES - CONV_W // 2 + t, tile), :]
    s = conv * _sigmoid(conv)
    for j in range(6):
        th = s[:, j * LANES:(j + 1) * LANES]
        if j < 4:
            ss = _head_sum(th * th, bd)
            th = th * lax.rsqrt(ss + EPS)
            if j < 2:
                th = th * HEAD_DIM ** -0.5
        c_ref[0, :, j * LANES:(j + 1) * LANES] = th.astype(c_ref.dtype)

    z_ref[0] = _dot(hn, wz_ref[...]).astype(z_ref.dtype)


def _in_proj(x, nw, wa, wb, wc, wg, wz, qkw, gpar, cw):
    bsz, seq, d = x.shape
    tile = IN_TILE
    nt = seq // tile
    hb = tile // SUBLANES
    full = lambda a: pl.BlockSpec(a.shape, lambda b, i: (0,) * a.ndim)
    out_shapes = (
        jax.ShapeDtypeStruct((bsz, seq, 3 * D_A), F32),
        jax.ShapeDtypeStruct((bsz, seq, 3 * D_B), BF16),
        jax.ShapeDtypeStruct((bsz, seq, 3 * D_C), BF16),
        jax.ShapeDtypeStruct((bsz, seq, 4 * LANES), F32),
        jax.ShapeDtypeStruct((bsz, seq, D_A + 3 * D_B), BF16),
    )
    row_spec = lambda w: pl.BlockSpec((1, tile, w), lambda b, i: (b, i, 0))
    return pl.pallas_call(
        _in_proj_kernel,
        grid=(bsz, nt),
        in_specs=[
            row_spec(d),
            pl.BlockSpec((1, SUBLANES, d), lambda b, i: (b, jnp.maximum(i * hb - 1, 0), 0)),
            pl.BlockSpec((1, SUBLANES, d), lambda b, i: (b, jnp.minimum((i + 1) * hb, nt * hb - 1), 0)),
            full(nw), full(wa), full(wb), full(wc), full(wg), full(wz), full(qkw), full(gpar), full(cw),
        ],
        out_specs=[row_spec(s.shape[-1]) for s in out_shapes],
        out_shape=out_shapes,
        scratch_shapes=[pltpu.VMEM((tile + 2 * SUBLANES, 3 * D_C), F32)],
        compiler_params=pltpu.CompilerParams(
            dimension_semantics=("parallel", "arbitrary"), vmem_limit_bytes=VMEM_LIMIT),
        name="in_proj",
    )(x, x, x, nw, wa, wb, wc, wg, wz, qkw, gpar, cw)


def _t5_bucket(rel):
    half = NUM_BUCKETS // 2
    max_exact = half // 2
    n = np.abs(rel)
    large = max_exact + (np.log(np.maximum(n, 1) / max_exact) / np.log(REL_MAX_DIST / max_exact)
                         * (half - max_exact)).astype(np.int32)
    large = np.minimum(large, half - 1)
    return (rel > 0).astype(np.int32) * half + np.where(n < max_exact, n, large)


def _attn_bias_tiles(rel_bias):
    t = np.arange(Q_BLK)[:, None]
    c = np.arange(K_WIN)[None, :]
    cfg_tiles = []
    for (_, dil) in DILATED_CFGS:
        offs = dil * np.arange(-N_SIDE, N_SIDE + 1)
        bias = rel_bias[_t5_bucket(offs)].T.astype(F32)
        variants = []
        for shift in (0, -N_SIDE, -2 * N_SIDE):
            j = c + shift - t + N_SIDE
            valid = (j >= 0) & (j <= 2 * N_SIDE)
            tile = jnp.where(valid[None], bias[:, np.clip(j, 0, 2 * N_SIDE)], NEG)
            variants.append(tile)
        cfg_tiles.append(jnp.stack(variants, axis=1))
    return jnp.stack(cfg_tiles, axis=0)


def _attn_kernel(q_ref, k_ref, v_ref, bias_ref, o_ref, acc_ref, m_ref, l_ref):
    sb = pl.program_id(2)
    seq = k_ref.shape[1]
    qsup = q_ref.shape[1]
    nblk = qsup // Q_BLK
    head0 = _iota2((Q_BLK, LANES), 1) < HEAD_DIM

    for ci, (_, dil) in enumerate(DILATED_CFGS):
        n_idx = seq // dil
        per_res = nblk // dil

        def body(t, carry, ci=ci, dil=dil, n_idx=n_idx, per_res=per_res):
            r = t % dil
            j = t // dil
            qs = r + dil * (j * Q_BLK)
            i0 = sb * (qsup // dil) + j * Q_BLK
            ws = jnp.clip(i0 - N_SIDE, 0, n_idx - K_WIN)
            var = jnp.where(i0 == 0, 0, jnp.where(i0 == n_idx - Q_BLK, 2, 1))
            ks = r + dil * ws
            if dil == 1:
                qrows = pl.ds(qs, Q_BLK)
                krows = pl.ds(ks, K_WIN)
            else:
                qrows = pl.ds(qs, Q_BLK, stride=dil)
                krows = pl.ds(ks, K_WIN, stride=dil)
            q2 = q_ref[0, qrows, :]
            k2 = k_ref[0, krows, :].astype(BF16)
            v2 = v_ref[0, krows, :].astype(BF16)
            pv, mx, ll = [], [], []
            for h in range(2):
                qh = jnp.where(head0 if h == 0 else ~head0, q2, 0.0).astype(BF16)
                s = _dot_nt(qh, k2) + bias_ref[ci, h, var]
                m = jnp.max(s, axis=-1, keepdims=True)
                p = jnp.exp(s - m)
                ll.append(jnp.sum(p, axis=-1, keepdims=True))
                mx.append(m)
                pv.append(_dot(p.astype(BF16), v2))
            num2 = jnp.where(head0, pv[0], pv[1])
            m2 = jnp.where(head0, mx[0], mx[1])
            l2 = jnp.where(head0, ll[0], ll[1])
            if ci == 0:
                acc_ref[qrows, :] = num2
                m_ref[qrows, :] = m2
                l_ref[qrows, :] = l2
            else:
                mo = m_ref[qrows, :]
                mn = jnp.maximum(mo, m2)
                a = jnp.exp(mo - mn)
                b = jnp.exp(m2 - mn)
                acc_ref[qrows, :] = acc_ref[qrows, :] * a + num2 * b
                l_ref[qrows, :] = l_ref[qrows, :] * a + l2 * b
                m_ref[qrows, :] = mn
            return carry

        lax.fori_loop(0, nblk, body, 0)

    o_ref[0] = (acc_ref[...] / l_ref[...]).astype(o_ref.dtype)


def _attention(aqkv, bias_tiles):
    bsz, seq, _ = aqkv.shape
    npair = D_A // LANES
    qsup = Q_SUPER
    assert seq % qsup == 0 and seq // DILATED_CFGS[-1][1] >= K_WIN
    return pl.pallas_call(
        _attn_kernel,
        grid=(bsz, npair, seq // qsup),
        in_specs=[
            pl.BlockSpec((1, qsup, LANES), lambda b, p, s: (b, s, p)),
            pl.BlockSpec((1, seq, LANES), lambda b, p, s: (b, 0, npair + p)),
            pl.BlockSpec((1, seq, LANES), lambda b, p, s: (b, 0, 2 * npair + p)),
            pl.BlockSpec((len(DILATED_CFGS), 2, 3, Q_BLK, K_WIN), lambda b, p, s: (0, p, 0, 0, 0)),
        ],
        out_specs=pl.BlockSpec((1, qsup, LANES), lambda b, p, s: (b, s, p)),
        out_shape=jax.ShapeDtypeStruct((bsz, seq, D_A), F32),
        scratch_shapes=[pltpu.VMEM((qsup, LANES), F32)] * 3,
        compiler_params=pltpu.CompilerParams(
            dimension_semantics=("parallel", "parallel", "arbitrary"), vmem_limit_bytes=VMEM_LIMIT),
        name="dilated_attn",
    )(aqkv, aqkv, aqkv, bias_tiles)


def _direction_masks(backward):
    row = _iota2((CHUNK, CHUNK), 0)
    col = _iota2((CHUNK, CHUNK), 1)
    if backward:
        return row <= col, row < col, row ^ col
    return row >= col, row > col, row ^ col


def _gate_cumsums(g, incl, incl_other):
    tri = incl.astype(BF16)
    tri_t = incl_other.astype(BF16)
    ghi, glo = _split(g)
    gb = _dot(tri, ghi) + _dot(tri, glo)
    gt = g.T
    thi, tlo = _split(gt)
    gbt = _dot(thi, tri_t) + _dot(tlo, tri_t)
    return gb, gt, gbt


def _mlstm_chunk(q_ref, k_ref, v_ref, g_ref, h_ref, s_ref, m_ref, backward):
    incl, _, _ = _direction_masks(backward)
    incl_other, _, _ = _direction_masks(not backward)
    q2 = q_ref[0]
    k2 = k_ref[0]
    v2 = v_ref[0]
    g = g_ref[0]
    ci0, cf0 = (2, 6) if backward else (0, 4)
    last = 0 if backward else CHUNK - 1

    gb, gt, gbt = _gate_cumsums(g, incl, incl_other)
    vo = jnp.concatenate([v2, jnp.ones_like(v2)], axis=1)
    state = s_ref[...]
    q_state = _dot(q2, state.astype(BF16))
    head0 = _iota2((CHUNK, LANES), 1) < HEAD_DIM
    sel0 = (_iota2((CHUNK, 2 * LANES), 1) & HEAD_DIM) == 0

    pv, iw, emt, ws, dec, mnew = [], [], [], [], [], []
    for h in range(2):
        li = g[:, ci0 + h:ci0 + h + 1]
        bc = gb[:, cf0 + h:cf0 + h + 1]
        rrow = gt[ci0 + h:ci0 + h + 1, :] - gbt[cf0 + h:cf0 + h + 1, :]
        mst = m_ref[h:h + 1, 0:1]
        dmat = jnp.where(incl, bc + rrow, -jnp.inf)
        inter = bc + mst
        mt = jnp.maximum(jnp.max(dmat, axis=-1, keepdims=True), inter)
        qh = jnp.where(head0 if h == 0 else ~head0, q2, 0).astype(BF16)
        sc = _dot_nt(qh, k2) * jnp.exp(dmat - mt)
        pv.append(_dot(sc.astype(BF16), vo))
        iw.append(jnp.exp(inter - mt))
        emt.append(jnp.exp(-mt))
        blast = gb[last:last + 1, cf0 + h:cf0 + h + 1]
        wlog = blast - bc + li
        mn = jnp.maximum(blast + mst, jnp.max(wlog, axis=0, keepdims=True))
        ws.append(jnp.exp(wlog - mn))
        dec.append(jnp.exp(blast + mst - mn))
        mnew.append(mn)

    tot = jnp.where(sel0, iw[0], iw[1]) * q_state + jnp.where(sel0, pv[0], pv[1])
    num = tot[:, :LANES]
    den = tot[:, LANES:]
    h_ref[0] = num / jnp.maximum(jnp.abs(den), jnp.where(head0, emt[0], emt[1]))

    ws2 = jnp.where(sel0, ws[0], ws[1])
    upd = _dot_tn(k2, (ws2 * vo.astype(F32)).astype(BF16))
    bd = _pair_block_diag(LANES, 2 * LANES)
    dec2 = jnp.where(sel0[0:1, :], dec[0], dec[1])
    s_ref[...] = dec2 * state + jnp.where(bd, upd, 0.0)
    m_ref[0:1, :] = jnp.broadcast_to(mnew[0], (1, LANES))
    m_ref[1:2, :] = jnp.broadcast_to(mnew[1], (1, LANES))


def _mlstm_kernel(qf, kf, vf, gf, qb, kb, vb, gb, hf_ref, hb_ref, sf, mf, sbk, mb):
    @pl.when(pl.program_id(2) == 0)
    def _():
        sf[...] = jnp.zeros_like(sf)
        mf[...] = jnp.zeros_like(mf)
        sbk[...] = jnp.zeros_like(sbk)
        mb[...] = jnp.zeros_like(mb)

    _mlstm_chunk(qf, kf, vf, gf, hf_ref, sf, mf, backward=False)
    _mlstm_chunk(qb, kb, vb, gb, hb_ref, sbk, mb, backward=True)


def _bidir_specs(nc, npair, gate_tile0):
    specs = []
    for rev in (False, True):
        cidx = (lambda c: nc - 1 - c) if rev else (lambda c: c)
        for part in range(3):
            specs.append(pl.BlockSpec(
                (1, CHUNK, LANES), lambda b, p, c, part=part, cidx=cidx: (b, cidx(c), part * npair + p)))
        specs.append(pl.BlockSpec(
            (1, CHUNK, LANES), lambda b, p, c, cidx=cidx: (b, cidx(c), gate_tile0 + p)))
    return specs


def _mlstm(bqkv, gates):
    bsz, seq, _ = bqkv.shape
    npair = D_B // LANES
    nc = seq // CHUNK
    out = jax.ShapeDtypeStruct((bsz, seq, D_B), F32)
    return pl.pallas_call(
        _mlstm_kernel,
        grid=(bsz, npair, nc),
        in_specs=_bidir_specs(nc, npair, 0),
        out_specs=[pl.BlockSpec((1, CHUNK, LANES), lambda b, p, c: (b, c, p)),
                   pl.BlockSpec((1, CHUNK, LANES), lambda b, p, c: (b, nc - 1 - c, p))],
        out_shape=(out, out),
        scratch_shapes=[pltpu.VMEM((LANES, 2 * LANES), F32), pltpu.VMEM((SUBLANES, LANES), F32),
                        pltpu.VMEM((LANES, 2 * LANES), F32), pltpu.VMEM((SUBLANES, LANES), F32)],
        compiler_params=pltpu.CompilerParams(
            dimension_semantics=("parallel", "parallel", "arbitrary"), vmem_limit_bytes=VMEM_LIMIT),
        name="mlstm",
    )(bqkv, bqkv, bqkv, gates, bqkv, bqkv, bqkv, gates)


_INV_BASE = 32


def _unit_tri_inverse(m, strict, xor):
    eye = (xor == 0).astype(F32)
    mb = jnp.where(xor < _INV_BASE, m, 0.0)
    mbb = mb.astype(BF16)
    t = eye - mb
    p = _dot(mbb, mbb)
    k = 2
    while True:
        t = t + _dot(t.astype(BF16), p.astype(BF16))
        k *= 2
        if k >= _INV_BASE:
            break
        pb = p.astype(BF16)
        p = _dot(pb, pb)
    b = _INV_BASE
    while b < CHUNK:
        ml = jnp.where((xor >= b) & (xor < 2 * b) & strict, m, 0.0).astype(BF16)
        tb = t.astype(BF16)
        t = t - _dot(tb, _dot(ml, tb).astype(BF16))
        b *= 2
    return t


def _gdn_chunk(q_ref, k_ref, v_ref, g_ref, o_ref, s_ref, backward):
    incl, strict, xor = _direction_masks(backward)
    incl_other, _, _ = _direction_masks(not backward)
    q2 = q_ref[0]
    k2 = k_ref[0]
    v2 = v_ref[0]
    g = g_ref[0]
    ca0, cb0 = (2, 6) if backward else (0, 4)
    last = 0 if backward else CHUNK - 1

    gb, _, gbt = _gate_cumsums(g, incl, incl_other)
    head0 = _iota2((CHUNK, LANES), 1) < HEAD_DIM
    sel0 = (_iota2((CHUNK, 2 * LANES), 1) & HEAD_DIM) == 0
    state = s_ref[...]
    state_b = state.astype(BF16)

    def pair(c0):
        return jnp.where(head0, c0[0], c0[1])

    gam = [gb[:, ca0 + h:ca0 + h + 1] for h in range(2)]
    beta = [g[:, cb0 + h:cb0 + h + 1] for h in range(2)]
    glast = [gb[last:last + 1, ca0 + h:ca0 + h + 1] for h in range(2)]
    gam2 = pair(gam)
    beta2 = pair(beta)
    egam2 = jnp.exp(gam2)
    kf = k2.astype(F32)
    rhs = jnp.concatenate([v2.astype(F32) * beta2, kf * (beta2 * egam2)], axis=1).astype(BF16)

    uw, a_intra = [], []
    for h in range(2):
        hm = head0 if h == 0 else ~head0
        gamrow = gbt[ca0 + h:ca0 + h + 1, :]
        decay = jnp.exp(jnp.where(incl, gam[h] - gamrow, -jnp.inf))
        kh = jnp.where(hm, k2, 0).astype(BF16)
        qh = jnp.where(hm, q2, 0).astype(BF16)
        m = jnp.where(strict, beta[h] * _dot_nt(kh, k2) * decay, 0.0)
        t_inv = _unit_tri_inverse(m, strict, xor)
        uw.append(_dot(t_inv.astype(BF16), rhs))
        a_intra.append((_dot_nt(qh, k2) * decay).astype(BF16))
    uw2 = jnp.where(sel0, uw[0], uw[1])
    u2 = uw2[:, :LANES]
    w2 = uw2[:, LANES:]

    v_new = u2 - _dot(w2.astype(BF16), state_b)
    v_new_b = v_new.astype(BF16)
    qg = (q2.astype(F32) * egam2).astype(BF16)
    o_ref[0] = _dot(qg, state_b) + jnp.where(head0, _dot(a_intra[0], v_new_b), _dot(a_intra[1], v_new_b))

    glast2 = jnp.where(head0[0:1, :], glast[0], glast[1])
    k_dec = (kf * jnp.exp(glast2 - gam2)).astype(BF16)
    bd = _pair_block_diag(LANES, LANES)
    s_ref[...] = state * jnp.exp(glast2) + jnp.where(bd, _dot_tn(k_dec, v_new_b), 0.0)


def _gdn_kernel(qf, kf, vf, gf, qb, kb, vb, gb, of_ref, ob_ref, sf, sbk):
    @pl.when(pl.program_id(2) == 0)
    def _():
        sf[...] = jnp.zeros_like(sf)
        sbk[...] = jnp.zeros_like(sbk)

    _gdn_chunk(qf, kf, vf, gf, of_ref, sf, backward=False)
    _gdn_chunk(qb, kb, vb, gb, ob_ref, sbk, backward=True)


def _gdn(cqkv, gates):
    bsz, seq, _ = cqkv.shape
    npair = D_C // LANES
    nc = seq // CHUNK
    out = jax.ShapeDtypeStruct((bsz, seq, D_C), F32)
    return pl.pallas_call(
        _gdn_kernel,
        grid=(bsz, npair, nc),
        in_specs=_bidir_specs(nc, npair, 2),
        out_specs=[pl.BlockSpec((1, CHUNK, LANES), lambda b, p, c: (b, c, p)),
                   pl.BlockSpec((1, CHUNK, LANES), lambda b, p, c: (b, nc - 1 - c, p))],
        out_shape=(out, out),
        scratch_shapes=[pltpu.VMEM((LANES, LANES), F32), pltpu.VMEM((LANES, LANES), F32)],
        compiler_params=pltpu.CompilerParams(
            dimension_semantics=("parallel", "parallel", "arbitrary"), vmem_limit_bytes=VMEM_LIMIT),
        name="gdn",
    )(cqkv, cqkv, cqkv, gates, cqkv, cqkv, cqkv, gates)


def _out_kernel(x_ref, oa_ref, hf_ref, hb_ref, cf_ref, cb_ref, z_ref, mw_ref, gw_ref, wo_ref, y_ref):
    bd = _pair_block_diag(LANES, LANES).astype(BF16)
    z = z_ref[0].astype(F32)

    def silu(t):
        return t * _sigmoid(t)

    def head_rmsnorm(t, w):
        parts = []
        for j in range(t.shape[1] // LANES):
            th = t[:, j * LANES:(j + 1) * LANES]
            ss = _head_sum(th * th, bd)
            parts.append(th * lax.rsqrt(ss * (1.0 / HEAD_DIM) + EPS))
        return jnp.concatenate(parts, axis=1) * w

    ya = oa_ref[0].astype(F32) * silu(z[:, 0:D_A])
    hsum = _sigmoid(z[:, D_A:D_A + D_B]) * (hf_ref[0] + hb_ref[0])
    yb = head_rmsnorm(hsum, mw_ref[...]) * silu(z[:, D_A + D_B:D_A + 2 * D_B])
    yc = head_rmsnorm(cf_ref[0] + cb_ref[0], gw_ref[...]) * silu(z[:, D_A + 2 * D_B:])
    y = jnp.concatenate([ya, yb, yc], axis=1).astype(BF16)
    y_ref[0] = x_ref[0] + _dot(y, wo_ref[...])


def _out_proj(x, oa, hf, hb, cf, cb, z, mw, gw, wo):
    bsz, seq, d = x.shape
    tile = OUT_TILE
    row_spec = lambda a: pl.BlockSpec((1, tile, a.shape[-1]), lambda b, i: (b, i, 0))
    full = lambda a: pl.BlockSpec(a.shape, lambda b, i: (0,) * a.ndim)
    return pl.pallas_call(
        _out_kernel,
        grid=(bsz, seq // tile),
        in_specs=[row_spec(a) for a in (x, oa, hf, hb, cf, cb, z)] + [full(mw), full(gw), full(wo)],
        out_specs=pl.BlockSpec((1, tile, d), lambda b, i: (b, i, 0)),
        out_shape=jax.ShapeDtypeStruct(x.shape, x.dtype),
        compiler_params=pltpu.CompilerParams(
            dimension_semantics=("parallel", "parallel"), vmem_limit_bytes=VMEM_LIMIT),
        name="out_proj",
    )(x, oa, hf, hb, cf, cb, z, mw, gw, wo)


def _gate_columns():
    cols = np.full((4, LANES), -1, np.int64)
    for mixer, off, heads in ((0, _OFF_BIF, D_B // HEAD_DIM), (1, _OFF_CAB, D_C // HEAD_DIM)):
        for pair in range(heads // 2):
            for lane in range(8):
                cols[mixer * 2 + pair, lane] = off + (lane // 2) * heads + 2 * pair + lane % 2
    return cols.reshape(-1)


def _gate_params(i_bias, f_bias, a_log, dt_bias):
    rows = jnp.zeros((SUBLANES, 4 * LANES), F32)
    for pair in range(2):
        for lane in range(8):
            kind, direction, head = lane // 4, (lane // 2) % 2, 2 * pair + lane % 2
            col_b = pair * LANES + lane
            col_c = (2 + pair) * LANES + lane
            rows = rows.at[0, col_b].set((i_bias if kind == 0 else f_bias)[direction, head])
            if kind == 0:
                rows = rows.at[0, col_c].set(dt_bias[direction, head])
                rows = rows.at[1, col_c].set(a_log[direction, head])
    return rows


def _layer(x, bias_tiles, norm_w, w_in, w_out, qk_norm_w, i_bias, f_bias, m_norm_w, conv_w, a_log, dt_bias,
           g_norm_w):
    wa = w_in[:, _OFF_AQ:_OFF_AZ].astype(BF16)
    wb = w_in[:, _OFF_BQ:_OFF_BIF].astype(BF16)
    wc = w_in[:, _OFF_CQ:_OFF_CAB].astype(BF16)
    wz = jnp.concatenate([w_in[:, _OFF_AZ:_OFF_BQ], w_in[:, _OFF_BO:_OFF_CQ], w_in[:, _OFF_CZ:]],
                         axis=1).astype(BF16)
    gcols = _gate_columns()
    wg = jnp.where((gcols >= 0)[None, :], w_in[:, np.maximum(gcols, 0)], 0.0).astype(BF16)
    qkw = jnp.zeros((SUBLANES, LANES), F32)
    qkw = qkw.at[0].set(jnp.tile(qk_norm_w[0], 2) * HEAD_DIM ** -0.5).at[1].set(jnp.tile(qk_norm_w[1], 2))
    gpar = _gate_params(i_bias, f_bias, a_log, dt_bias)
    cw = jnp.zeros((SUBLANES, 3 * D_C), F32).at[:CONV_W].set(conv_w)

    aqkv, bqkv, cqkv, gates, z = _in_proj(x, norm_w[None, :], wa, wb, wc, wg, wz, qkw, gpar, cw)
    oa = _attention(aqkv, bias_tiles)
    hf, hb = _mlstm(bqkv, gates)
    cf, cb = _gdn(cqkv, gates)
    mw = jnp.tile(m_norm_w, D_B // HEAD_DIM)[None, :]
    gw = jnp.tile(g_norm_w, D_C // HEAD_DIM)[None, :]
    return _out_proj(x, oa, hf, hb, cf, cb, z, mw, gw, w_out.astype(BF16))


def kernel(x, norm_w, w_in, w_out, qk_norm_w, rel_bias, mlstm_i_bias, mlstm_f_bias, mlstm_norm_w, gdn_conv_w,
           gdn_a_log, gdn_dt_bias, gdn_norm_w):
    bias_tiles = _attn_bias_tiles(rel_bias)
    for l in range(DEPTH):
        x = _layer(x, bias_tiles, norm_w[l], w_in[l], w_out[l], qk_norm_w[l], mlstm_i_bias[l], mlstm_f_bias[l],
                   mlstm_norm_w[l], gdn_conv_w[l], gdn_a_log[l], gdn_dt_bias[l], gdn_norm_w[l])
    return x
```

```python
import functools

import numpy as np
import jax
import jax.numpy as jnp
from jax import lax
from jax.experimental import pallas as pl
from jax.experimental.pallas import tpu as pltpu

F32 = jnp.float32
BF16 = jnp.bfloat16

D_MODEL = 1024
DEPTH = 2
EPS = 1e-6
HEAD_DIM = 64
LANES = 128
SUBLANES = 8
D_A, D_B, D_C = 512, 256, 256
DILATED_CFGS = ((128, 1), (512, 4), (2048, 16))
N_SIDE = 64
NUM_BUCKETS = 32
REL_MAX_DIST = 1024
CONV_W = 5
CHUNK = 256
Q_SUPER = 2048
Q_BLK = 128
K_WIN = 256
ATTN_ILP = 2
IN_TILE = 256
OUT_TILE = 512
NEG = -1e30
VMEM_LIMIT = 56 * 1024 * 1024

_OFF_AQ, _OFF_AZ = 0, 1536
_OFF_BQ, _OFF_BIF, _OFF_BO, _OFF_BZ = 2048, 2816, 2832, 3088
_OFF_CQ, _OFF_CAB, _OFF_CZ = 3344, 4112, 4128


def _dot(a, b):
    return jnp.dot(a, b, preferred_element_type=F32)


def _dot_nt(a, b):
    return lax.dot_general(a, b, (((1,), (1,)), ((), ())), preferred_element_type=F32)


def _dot_tn(a, b):
    return lax.dot_general(a, b, (((0,), (0,)), ((), ())), preferred_element_type=F32)


def _split(a):
    hi = a.astype(BF16)
    lo = (a - hi.astype(F32)).astype(BF16)
    return hi, lo


def _iota2(shape, axis):
    return lax.broadcasted_iota(jnp.int32, shape, axis)


def _pair_block_diag(rows, cols):
    r = _iota2((rows, cols), 0)
    c = _iota2((rows, cols), 1)
    return (r < HEAD_DIM) == ((c & HEAD_DIM) == 0)


def _head_sum(t, bd):
    hi, lo = _split(t)
    return _dot(hi, bd) + _dot(lo, bd)


def _softplus(y):
    return jnp.maximum(y, 0.0) + jnp.log1p(jnp.exp(-jnp.abs(y)))


def _sigmoid(y):
    return 1.0 / (1.0 + jnp.exp(-y))


def _in_proj_kernel(x_ref, xp_ref, xn_ref, nw_ref, wa_ref, wb_ref, wc_ref, wg_ref, wz_ref,
                    qkw_ref, gpar_ref, cw_ref,
                    a_ref, b_ref, c_ref, g_ref, z_ref, cext_ref):
    i = pl.program_id(1)
    n = pl.num_programs(1)
    tile = x_ref.shape[1]
    bd = _pair_block_diag(LANES, LANES).astype(BF16)

    xe = jnp.concatenate([xp_ref[0], x_ref[0], xn_ref[0]], axis=0)
    ms = jnp.mean(xe * xe, axis=-1, keepdims=True)
    hne = xe * lax.rsqrt(ms + EPS) * nw_ref[...]
    hn = hne[SUBLANES:SUBLANES + tile].astype(BF16)
    hne = hne.astype(BF16)

    for j in range(6):
        t = _dot(hn, wa_ref[:, j * 256:(j + 1) * 256])
        for half in range(2):
            th = t[:, half * LANES:(half + 1) * LANES]
            if j < 4:
                ss = _head_sum(th * th, bd)
                th = th * lax.rsqrt(ss * (1.0 / HEAD_DIM) + EPS) * qkw_ref[j // 2:j // 2 + 1, :]
            col = j * 256 + half * LANES
            a_ref[0, :, col:col + LANES] = th

    tb = _dot(hn, wb_ref[...])
    b_ref[0, :, 0:D_B] = (tb[:, 0:D_B] * HEAD_DIM ** -0.5).astype(b_ref.dtype)
    b_ref[0, :, D_B:3 * D_B] = tb[:, D_B:3 * D_B].astype(b_ref.dtype)

    p = _dot(hn, wg_ref[...]) + gpar_ref[0:1, :]
    lane = _iota2(p.shape, 1)
    first4 = (lane & (LANES - 1)) < 4
    is_b = lane < 2 * LANES
    val_b = jnp.where(first4, p, -_softplus(-p))
    val_c = jnp.where(first4, -jnp.exp(gpar_ref[1:2, :]) * _softplus(p), _sigmoid(p))
    g_ref[0] = jnp.where(is_b, val_b, val_c)

    ce = _dot(hne, wc_ref[...])
    row = _iota2((tile + 2 * SUBLANES, 1), 0)
    valid = ((row >= SUBLANES) | (i > 0)) & ((row < tile + SUBLANES) | (i < n - 1))
    cext_ref[...] = jnp.where(valid, ce, 0.0)
    conv = cw_ref[0:1, :] * cext_ref[pl.ds(SUBLANES - CONV_W // 2, tile), :]
    for t in range(1, CONV_W):
        conv = conv + cw_ref[t:t + 1, :] * cext_ref[pl.ds(SUBLANES - CONV_W // 2 + t, tile), :]
    s = conv * _sigmoid(conv)
    for j in range(6):
        th = s[:, j * LANES:(j + 1) * LANES]
        if j < 4:
            ss = _head_sum(th * th, bd)
            th = th * lax.rsqrt(ss + EPS)
            if j < 2:
                th = th * HEAD_DIM ** -0.5
        c_ref[0, :, j * LANES:(j + 1) * LANES] = th.astype(c_ref.dtype)

    z_ref[0] = _dot(hn, wz_ref[...]).astype(z_ref.dtype)


def _in_proj(x, nw, wa, wb, wc, wg, wz, qkw, gpar, cw):
    bsz, seq, d = x.shape
    tile = IN_TILE
    nt = seq // tile
    hb = tile // SUBLANES
    full = lambda a: pl.BlockSpec(a.shape, lambda b, i: (0,) * a.ndim)
    out_shapes = (
        jax.ShapeDtypeStruct((bsz, seq, 3 * D_A), F32),
        jax.ShapeDtypeStruct((bsz, seq, 3 * D_B), BF16),
        jax.ShapeDtypeStruct((bsz, seq, 3 * D_C), BF16),
        jax.ShapeDtypeStruct((bsz, seq, 4 * LANES), F32),
        jax.ShapeDtypeStruct((bsz, seq, D_A + 3 * D_B), BF16),
    )
    row_spec = lambda w: pl.BlockSpec((1, tile, w), lambda b, i: (b, i, 0))
    return pl.pallas_call(
        _in_proj_kernel,
        grid=(bsz, nt),
        in_specs=[
            row_spec(d),
            pl.BlockSpec((1, SUBLANES, d), lambda b, i: (b, jnp.maximum(i * hb - 1, 0), 0)),
            pl.BlockSpec((1, SUBLANES, d), lambda b, i: (b, jnp.minimum((i + 1) * hb, nt * hb - 1), 0)),
            full(nw), full(wa), full(wb), full(wc), full(wg), full(wz), full(qkw), full(gpar), full(cw),
        ],
        out_specs=[row_spec(s.shape[-1]) for s in out_shapes],
        out_shape=out_shapes,
        scratch_shapes=[pltpu.VMEM((tile + 2 * SUBLANES, 3 * D_C), F32)],
        compiler_params=pltpu.CompilerParams(
            dimension_semantics=("parallel", "arbitrary"), vmem_limit_bytes=VMEM_LIMIT),
        name="in_proj",
    )(x, x, x, nw, wa, wb, wc, wg, wz, qkw, gpar, cw)


def _t5_bucket(rel):
    half = NUM_BUCKETS // 2
    max_exact = half // 2
    n = np.abs(rel)
    large = max_exact + (np.log(np.maximum(n, 1) / max_exact) / np.log(REL_MAX_DIST / max_exact)
                         * (half - max_exact)).astype(np.int32)
    large = np.minimum(large, half - 1)
    return (rel > 0).astype(np.int32) * half + np.where(n < max_exact, n, large)


def _attn_bias_tiles(rel_bias):
    period = 640
    wide = K_WIN + 2 * N_SIDE
    lead = Q_BLK - 1 + N_SIDE
    cfg_tiles = []
    for (_, dil) in DILATED_CFGS:
        offs = dil * np.arange(-N_SIDE, N_SIDE + 1)
        bias = rel_bias[_t5_bucket(offs)].T.astype(F32)
        heads = bias.shape[0]
        vec = jnp.concatenate([jnp.full((heads, lead + N_SIDE), NEG, F32), bias,
                               jnp.full((heads, period - lead - 3 * N_SIDE - 1), NEG, F32)], axis=1)
        rows = jnp.tile(vec, (1, Q_BLK))[:, :Q_BLK * (period - 1)].reshape(heads, Q_BLK, period - 1)
        ext = rows[:, :, lead:lead + wide]
        variants = [ext[:, :, 2 * N_SIDE:2 * N_SIDE + K_WIN], ext[:, :, N_SIDE:N_SIDE + K_WIN],
                    ext[:, :, 0:K_WIN]]
        cfg_tiles.append(jnp.stack(variants, axis=1))
    return jnp.stack(cfg_tiles, axis=0)


def _attn_kernel(q_ref, k_ref, v_ref, bias_ref, o_ref, acc_ref, m_ref, l_ref):
    sb = pl.program_id(2)
    seq = k_ref.shape[1]
    qsup = q_ref.shape[1]
    nblk = qsup // Q_BLK
    head0 = _iota2((Q_BLK, LANES), 1) < HEAD_DIM

    for ci, (_, dil) in enumerate(DILATED_CFGS):
        n_idx = seq // dil
        per_res = nblk // dil

        def body(it, carry, ci=ci, dil=dil, n_idx=n_idx, per_res=per_res):
            blocks = []
            for u in range(ATTN_ILP):
                t = it * ATTN_ILP + u
                r = t % dil
                j = t // dil
                qs = r + dil * (j * Q_BLK)
                i0 = sb * (qsup // dil) + j * Q_BLK
                ws = jnp.clip(i0 - N_SIDE, 0, n_idx - K_WIN)
                var = jnp.where(i0 == 0, 0, jnp.where(i0 == n_idx - Q_BLK, 2, 1))
                ks = r + dil * ws
                if dil == 1:
                    qrows = pl.ds(qs, Q_BLK)
                    krows = pl.ds(ks, K_WIN)
                else:
                    qrows = pl.ds(qs, Q_BLK, stride=dil)
                    krows = pl.ds(ks, K_WIN, stride=dil)
                q2 = q_ref[0, qrows, :]
                blocks.append(dict(qrows=qrows, var=var,
                                   qh=[jnp.where(head0 if h == 0 else ~head0, q2, 0.0).astype(BF16) for h in range(2)],
                                   k2=k_ref[0, krows, :].astype(BF16), v2=v_ref[0, krows, :].astype(BF16)))
            chains = [(blk, h) for blk in blocks for h in range(2)]
            s = [_dot_nt(blk["qh"][h], blk["k2"]) + bias_ref[ci, h, blk["var"]] for blk, h in chains]
            mx = [jnp.max(sh, axis=-1, keepdims=True) for sh in s]
            p = [jnp.exp(sh - m) for sh, m in zip(s, mx)]
            ll = [jnp.sum(ph, axis=-1, keepdims=True) for ph in p]
            pv = [_dot(ph.astype(BF16), blk["v2"]) for ph, (blk, _) in zip(p, chains)]
            for u, blk in enumerate(blocks):
                qrows = blk["qrows"]
                num2 = jnp.where(head0, pv[2 * u], pv[2 * u + 1])
                m2 = jnp.where(head0, mx[2 * u], mx[2 * u + 1])
                l2 = jnp.where(head0, ll[2 * u], ll[2 * u + 1])
                if ci == 0:
                    acc_ref[qrows, :] = num2
                    m_ref[qrows, :] = m2
                    l_ref[qrows, :] = l2
                else:
                    mo = m_ref[qrows, :]
                    mn = jnp.maximum(mo, m2)
                    a = jnp.exp(mo - mn)
                    b = jnp.exp(m2 - mn)
                    acc_ref[qrows, :] = acc_ref[qrows, :] * a + num2 * b
                    l_ref[qrows, :] = l_ref[qrows, :] * a + l2 * b
                    m_ref[qrows, :] = mn
            return carry

        lax.fori_loop(0, nblk // ATTN_ILP, body, 0)

    o_ref[0] = (acc_ref[...] / l_ref[...]).astype(o_ref.dtype)


def _attention(aqkv, bias_tiles):
    bsz, seq, _ = aqkv.shape
    npair = D_A // LANES
    qsup = Q_SUPER
    assert seq % qsup == 0 and seq // DILATED_CFGS[-1][1] >= K_WIN
    return pl.pallas_call(
        _attn_kernel,
        grid=(bsz, npair, seq // qsup),
        in_specs=[
            pl.BlockSpec((1, qsup, LANES), lambda b, p, s: (b, s, p)),
            pl.BlockSpec((1, seq, LANES), lambda b, p, s: (b, 0, npair + p)),
            pl.BlockSpec((1, seq, LANES), lambda b, p, s: (b, 0, 2 * npair + p)),
            pl.BlockSpec((len(DILATED_CFGS), 2, 3, Q_BLK, K_WIN), lambda b, p, s: (0, p, 0, 0, 0)),
        ],
        out_specs=pl.BlockSpec((1, qsup, LANES), lambda b, p, s: (b, s, p)),
        out_shape=jax.ShapeDtypeStruct((bsz, seq, D_A), F32),
        scratch_shapes=[pltpu.VMEM((qsup, LANES), F32)] * 3,
        compiler_params=pltpu.CompilerParams(
            dimension_semantics=("parallel", "parallel", "arbitrary"), vmem_limit_bytes=VMEM_LIMIT),
        name="dilated_attn",
    )(aqkv, aqkv, aqkv, bias_tiles)


def _direction_masks(backward):
    row = _iota2((CHUNK, CHUNK), 0)
    col = _iota2((CHUNK, CHUNK), 1)
    if backward:
        return row <= col, row < col, row ^ col
    return row >= col, row > col, row ^ col


def _gate_cumsums(g, incl, incl_other):
    tri = incl.astype(BF16)
    tri_t = incl_other.astype(BF16)
    ghi, glo = _split(g)
    gb = _dot(tri, ghi) + _dot(tri, glo)
    gt = g.T
    thi, tlo = _split(gt)
    gbt = _dot(thi, tri_t) + _dot(tlo, tri_t)
    return gb, gt, gbt


def _mlstm_kernel(qf, kf, vf, gf, qb, kb, vb, gb, hf_ref, hb_ref, sf, mf, sbk, mb):
    @pl.when(pl.program_id(2) == 0)
    def _():
        sf[...] = jnp.zeros_like(sf)
        mf[...] = jnp.zeros_like(mf)
        sbk[...] = jnp.zeros_like(sbk)
        mb[...] = jnp.zeros_like(mb)

    head0 = _iota2((CHUNK, LANES), 1) < HEAD_DIM
    sel0 = (_iota2((CHUNK, 2 * LANES), 1) & HEAD_DIM) == 0
    bd = _pair_block_diag(LANES, 2 * LANES)
    fwd_masks = _direction_masks(False)
    bwd_masks = _direction_masks(True)

    dirs = []
    for backward, refs, h_ref, s_ref, m_ref in ((False, (qf, kf, vf, gf), hf_ref, sf, mf),
                                                (True, (qb, kb, vb, gb), hb_ref, sbk, mb)):
        incl = (bwd_masks if backward else fwd_masks)[0]
        incl_other = (fwd_masks if backward else bwd_masks)[0]
        ci0, cf0 = (2, 6) if backward else (0, 4)
        last = 0 if backward else CHUNK - 1
        q2, k2, v2, g = (r[0] for r in refs)
        gbc, gt, gbt = _gate_cumsums(g, incl, incl_other)
        heads = []
        for h in range(2):
            li = g[:, ci0 + h:ci0 + h + 1]
            bc = gbc[:, cf0 + h:cf0 + h + 1]
            rrow = gt[ci0 + h:ci0 + h + 1, :] - gbt[cf0 + h:cf0 + h + 1, :]
            mst = m_ref[h:h + 1, 0:1]
            dmat = jnp.where(incl, bc + rrow, -jnp.inf)
            inter = bc + mst
            mt = jnp.maximum(jnp.max(dmat, axis=-1, keepdims=True), inter)
            blast = gbc[last:last + 1, cf0 + h:cf0 + h + 1]
            wlog = blast - bc + li
            mn = jnp.maximum(blast + mst, jnp.max(wlog, axis=0, keepdims=True))
            heads.append(dict(decay=jnp.exp(dmat - mt), iw=jnp.exp(inter - mt), emt=jnp.exp(-mt),
                              ws=jnp.exp(wlog - mn), dec=jnp.exp(blast + mst - mn), mnew=mn))
        state = s_ref[...]
        dirs.append(dict(q2=q2, k2=k2, heads=heads, state=state, h_ref=h_ref, s_ref=s_ref, m_ref=m_ref,
                         vo=jnp.concatenate([v2, jnp.ones_like(v2)], axis=1)))

    chains = [(d, h) for d in dirs for h in range(2)]
    q_state = [_dot(d["q2"], d["state"].astype(BF16)) for d in dirs]
    qk = [_dot_nt(jnp.where(head0 if h == 0 else ~head0, d["q2"], 0).astype(BF16), d["k2"]) for d, h in chains]
    sc = [(qkh * d["heads"][h]["decay"]).astype(BF16) for (d, h), qkh in zip(chains, qk)]
    pv = [_dot(sch, d["vo"]) for (d, _), sch in zip(chains, sc)]
    upd = [_dot_tn(d["k2"], (jnp.where(sel0, d["heads"][0]["ws"], d["heads"][1]["ws"])
                             * d["vo"].astype(F32)).astype(BF16)) for d in dirs]
    for i, d in enumerate(dirs):
        h0, h1 = d["heads"]
        tot = jnp.where(sel0, h0["iw"], h1["iw"]) * q_state[i] + jnp.where(sel0, pv[2 * i], pv[2 * i + 1])
        den = jnp.maximum(jnp.abs(tot[:, LANES:]), jnp.where(head0, h0["emt"], h1["emt"]))
        d["h_ref"][0] = tot[:, :LANES] / den
        dec2 = jnp.where(sel0[0:1, :], h0["dec"], h1["dec"])
        d["s_ref"][...] = dec2 * d["state"] + jnp.where(bd, upd[i], 0.0)
        d["m_ref"][0:1, :] = jnp.broadcast_to(h0["mnew"], (1, LANES))
        d["m_ref"][1:2, :] = jnp.broadcast_to(h1["mnew"], (1, LANES))


def _bidir_specs(nc, npair, gate_tile0):
    specs = []
    for rev in (False, True):
        cidx = (lambda c: nc - 1 - c) if rev else (lambda c: c)
        for part in range(3):
            specs.append(pl.BlockSpec(
                (1, CHUNK, LANES), lambda b, p, c, part=part, cidx=cidx: (b, cidx(c), part * npair + p)))
        specs.append(pl.BlockSpec(
            (1, CHUNK, LANES), lambda b, p, c, cidx=cidx: (b, cidx(c), gate_tile0 + p)))
    return specs


def _mlstm(bqkv, gates):
    bsz, seq, _ = bqkv.shape
    npair = D_B // LANES
    nc = seq // CHUNK
    out = jax.ShapeDtypeStruct((bsz, seq, D_B), F32)
    return pl.pallas_call(
        _mlstm_kernel,
        grid=(bsz, npair, nc),
        in_specs=_bidir_specs(nc, npair, 0),
        out_specs=[pl.BlockSpec((1, CHUNK, LANES), lambda b, p, c: (b, c, p)),
                   pl.BlockSpec((1, CHUNK, LANES), lambda b, p, c: (b, nc - 1 - c, p))],
        out_shape=(out, out),
        scratch_shapes=[pltpu.VMEM((LANES, 2 * LANES), F32), pltpu.VMEM((SUBLANES, LANES), F32),
                        pltpu.VMEM((LANES, 2 * LANES), F32), pltpu.VMEM((SUBLANES, LANES), F32)],
        compiler_params=pltpu.CompilerParams(
            dimension_semantics=("parallel", "parallel", "arbitrary"), vmem_limit_bytes=VMEM_LIMIT),
        name="mlstm",
    )(bqkv, bqkv, bqkv, gates, bqkv, bqkv, bqkv, gates)


_INV_BASE = 32


def _unit_tri_inverses(ms, xor):
    eye = (xor == 0).astype(F32)
    base = xor < _INV_BASE
    mbs = [jnp.where(base, m, 0.0) for m in ms]
    ts = [eye - mb for mb in mbs]
    pbs = [mb.astype(BF16) for mb in mbs]
    k = 1
    while True:
        pbs = [_dot(pb, pb).astype(BF16) for pb in pbs]
        k *= 2
        ts = [t + _dot(t.astype(BF16), pb) for t, pb in zip(ts, pbs)]
        if 2 * k >= _INV_BASE:
            break
    b = _INV_BASE
    while b < CHUNK:
        level = (xor >= b) & (xor < 2 * b)
        tbs = [t.astype(BF16) for t in ts]
        xs = [_dot(jnp.where(level, m, 0.0).astype(BF16), tb).astype(BF16) for m, tb in zip(ms, tbs)]
        ts = [t - _dot(tb, x) for t, tb, x in zip(ts, tbs, xs)]
        b *= 2
    return ts


def _gdn_kernel(qf, kf, vf, gf, qb, kb, vb, gb, of_ref, ob_ref, sf, sbk):
    @pl.when(pl.program_id(2) == 0)
    def _():
        sf[...] = jnp.zeros_like(sf)
        sbk[...] = jnp.zeros_like(sbk)

    head0 = _iota2((CHUNK, LANES), 1) < HEAD_DIM
    sel0 = (_iota2((CHUNK, 2 * LANES), 1) & HEAD_DIM) == 0
    bd = _pair_block_diag(LANES, LANES)
    fwd_masks = _direction_masks(False)
    bwd_masks = _direction_masks(True)
    xor = fwd_masks[2]

    dirs = []
    for backward, refs, o_ref, s_ref in ((False, (qf, kf, vf, gf), of_ref, sf),
                                         (True, (qb, kb, vb, gb), ob_ref, sbk)):
        incl, strict, _ = bwd_masks if backward else fwd_masks
        incl_other = (fwd_masks if backward else bwd_masks)[0]
        ca0, cb0 = (2, 6) if backward else (0, 4)
        last = 0 if backward else CHUNK - 1
        q2, k2, v2, g = (r[0] for r in refs)
        gbc, _, gbt = _gate_cumsums(g, incl, incl_other)
        gam = [gbc[:, ca0 + h:ca0 + h + 1] for h in range(2)]
        beta = [g[:, cb0 + h:cb0 + h + 1] for h in range(2)]
        glast2 = jnp.where(head0[0:1, :], gbc[last:last + 1, ca0:ca0 + 1], gbc[last:last + 1, ca0 + 1:ca0 + 2])
        gam2 = jnp.where(head0, gam[0], gam[1])
        beta2 = jnp.where(head0, beta[0], beta[1])
        egam2 = jnp.exp(gam2)
        kf32 = k2.astype(F32)
        rhs = jnp.concatenate([v2.astype(F32) * beta2, kf32 * (beta2 * egam2)], axis=1).astype(BF16)
        decay = [jnp.exp(jnp.where(incl, gam[h] - gbt[ca0 + h:ca0 + h + 1, :], -jnp.inf)) for h in range(2)]
        state = s_ref[...]
        dirs.append(dict(q2=q2, k2=k2, kf32=kf32, rhs=rhs, decay=decay, beta=beta, strict=strict, gam2=gam2,
                         egam2=egam2, glast2=glast2, state=state, state_b=state.astype(BF16), o_ref=o_ref,
                         s_ref=s_ref))

    chains = [(d, h) for d in dirs for h in range(2)]

    def head_lanes(x, h):
        return jnp.where(head0 if h == 0 else ~head0, x, 0).astype(BF16)

    kk = [_dot_nt(head_lanes(d["k2"], h), d["k2"]) for d, h in chains]
    ms = [jnp.where(d["strict"], d["beta"][h] * kkh * d["decay"][h], 0.0) for (d, h), kkh in zip(chains, kk)]
    qk = [_dot_nt(head_lanes(d["q2"], h), d["k2"]) for d, h in chains]
    a_intra = [(qkh * d["decay"][h]).astype(BF16) for (d, h), qkh in zip(chains, qk)]
    t_inv = _unit_tri_inverses(ms, xor)
    uw = [_dot(t.astype(BF16), d["rhs"]) for (d, _), t in zip(chains, t_inv)]

    uw2 = [jnp.where(sel0, uw[2 * i], uw[2 * i + 1]) for i in range(2)]
    w_state = [_dot(uw2[i][:, LANES:].astype(BF16), d["state_b"]) for i, d in enumerate(dirs)]
    v_new = [(uw2[i][:, :LANES] - w_state[i]).astype(BF16) for i in range(2)]
    q_state = [_dot((d["q2"].astype(F32) * d["egam2"]).astype(BF16), d["state_b"]) for d in dirs]
    av = [_dot(a_intra[j], v_new[j // 2]) for j in range(4)]
    upd = [_dot_tn((d["kf32"] * jnp.exp(d["glast2"] - d["gam2"])).astype(BF16), v_new[i])
           for i, d in enumerate(dirs)]
    for i, d in enumerate(dirs):
        d["o_ref"][0] = q_state[i] + jnp.where(head0, av[2 * i], av[2 * i + 1])
        d["s_ref"][...] = d["state"] * jnp.exp(d["glast2"]) + jnp.where(bd, upd[i], 0.0)


def _gdn(cqkv, gates):
    bsz, seq, _ = cqkv.shape
    npair = D_C // LANES
    nc = seq // CHUNK
    out = jax.ShapeDtypeStruct((bsz, seq, D_C), F32)
    return pl.pallas_call(
        _gdn_kernel,
        grid=(bsz, npair, nc),
        in_specs=_bidir_specs(nc, npair, 2),
        out_specs=[pl.BlockSpec((1, CHUNK, LANES), lambda b, p, c: (b, c, p)),
                   pl.BlockSpec((1, CHUNK, LANES), lambda b, p, c: (b, nc - 1 - c, p))],
        out_shape=(out, out),
        scratch_shapes=[pltpu.VMEM((LANES, LANES), F32), pltpu.VMEM((LANES, LANES), F32)],
        compiler_params=pltpu.CompilerParams(
            dimension_semantics=("parallel", "parallel", "arbitrary"), vmem_limit_bytes=VMEM_LIMIT),
        name="gdn",
    )(cqkv, cqkv, cqkv, gates, cqkv, cqkv, cqkv, gates)


def _out_kernel(x_ref, oa_ref, hf_ref, hb_ref, cf_ref, cb_ref, z_ref, mw_ref, gw_ref, wo_ref, y_ref):
    bd = _pair_block_diag(LANES, LANES).astype(BF16)
    z = z_ref[0].astype(F32)

    def silu(t):
        return t * _sigmoid(t)

    def head_rmsnorm(t, w):
        parts = []
        for j in range(t.shape[1] // LANES):
            th = t[:, j * LANES:(j + 1) * LANES]
            ss = _head_sum(th * th, bd)
            parts.append(th * lax.rsqrt(ss * (1.0 / HEAD_DIM) + EPS))
        return jnp.concatenate(parts, axis=1) * w

    ya = oa_ref[0].astype(F32) * silu(z[:, 0:D_A])
    hsum = _sigmoid(z[:, D_A:D_A + D_B]) * (hf_ref[0] + hb_ref[0])
    yb = head_rmsnorm(hsum, mw_ref[...]) * silu(z[:, D_A + D_B:D_A + 2 * D_B])
    yc = head_rmsnorm(cf_ref[0] + cb_ref[0], gw_ref[...]) * silu(z[:, D_A + 2 * D_B:])
    y = jnp.concatenate([ya, yb, yc], axis=1).astype(BF16)
    y_ref[0] = x_ref[0] + _dot(y, wo_ref[...])


def _out_proj(x, oa, hf, hb, cf, cb, z, mw, gw, wo):
    bsz, seq, d = x.shape
    tile = OUT_TILE
    row_spec = lambda a: pl.BlockSpec((1, tile, a.shape[-1]), lambda b, i: (b, i, 0))
    full = lambda a: pl.BlockSpec(a.shape, lambda b, i: (0,) * a.ndim)
    return pl.pallas_call(
        _out_kernel,
        grid=(bsz, seq // tile),
        in_specs=[row_spec(a) for a in (x, oa, hf, hb, cf, cb, z)] + [full(mw), full(gw), full(wo)],
        out_specs=pl.BlockSpec((1, tile, d), lambda b, i: (b, i, 0)),
        out_shape=jax.ShapeDtypeStruct(x.shape, x.dtype),
        compiler_params=pltpu.CompilerParams(
            dimension_semantics=("parallel", "parallel"), vmem_limit_bytes=VMEM_LIMIT),
        name="out_proj",
    )(x, oa, hf, hb, cf, cb, z, mw, gw, wo)


def _gate_columns():
    cols = np.full((4, LANES), -1, np.int64)
    for mixer, off, heads in ((0, _OFF_BIF, D_B // HEAD_DIM), (1, _OFF_CAB, D_C // HEAD_DIM)):
        for pair in range(heads // 2):
            for lane in range(8):
                cols[mixer * 2 + pair, lane] = off + (lane // 2) * heads + 2 * pair + lane % 2
    return cols.reshape(-1)


def _gate_params(i_bias, f_bias, a_log, dt_bias):
    nh = i_bias.shape[1]
    flat = jnp.concatenate([i_bias.reshape(-1), f_bias.reshape(-1), dt_bias.reshape(-1), a_log.reshape(-1),
                            jnp.zeros((1,), F32)])
    idx = np.full((SUBLANES, 4 * LANES), flat.shape[0] - 1, np.int32)
    for pair in range(2):
        for lane in range(8):
            kind, direction, head = lane // 4, (lane // 2) % 2, 2 * pair + lane % 2
            src = direction * nh + head
            idx[0, pair * LANES + lane] = kind * 2 * nh + src
            if kind == 0:
                idx[0, (2 + pair) * LANES + lane] = 2 * 2 * nh + src
                idx[1, (2 + pair) * LANES + lane] = 3 * 2 * nh + src
    return flat[idx]


def _layer(x, bias_tiles, norm_w, w_in, w_out, qk_norm_w, i_bias, f_bias, m_norm_w, conv_w, a_log, dt_bias,
           g_norm_w):
    wa = w_in[:, _OFF_AQ:_OFF_AZ].astype(BF16)
    wb = w_in[:, _OFF_BQ:_OFF_BIF].astype(BF16)
    wc = w_in[:, _OFF_CQ:_OFF_CAB].astype(BF16)
    wz = jnp.concatenate([w_in[:, _OFF_AZ:_OFF_BQ], w_in[:, _OFF_BO:_OFF_CQ], w_in[:, _OFF_CZ:]],
                         axis=1).astype(BF16)
    gcols = _gate_columns()
    wg = jnp.where((gcols >= 0)[None, :], w_in[:, np.maximum(gcols, 0)], 0.0).astype(BF16)
    qkw = jnp.pad(jnp.tile(qk_norm_w, (1, 2)) * jnp.array([[HEAD_DIM ** -0.5], [1.0]], F32),
                  ((0, SUBLANES - 2), (0, 0)))
    gpar = _gate_params(i_bias, f_bias, a_log, dt_bias)
    cw = jnp.pad(conv_w, ((0, SUBLANES - CONV_W), (0, 0)))

    aqkv, bqkv, cqkv, gates, z = _in_proj(x, norm_w[None, :], wa, wb, wc, wg, wz, qkw, gpar, cw)
    oa = _attention(aqkv, bias_tiles)
    hf, hb = _mlstm(bqkv, gates)
    cf, cb = _gdn(cqkv, gates)
    mw = jnp.tile(m_norm_w, D_B // HEAD_DIM)[None, :]
    gw = jnp.tile(g_norm_w, D_C // HEAD_DIM)[None, :]
    return _out_proj(x, oa, hf, hb, cf, cb, z, mw, gw, w_out.astype(BF16))


def kernel(x, norm_w, w_in, w_out, qk_norm_w, rel_bias, mlstm_i_bias, mlstm_f_bias, mlstm_norm_w, gdn_conv_w,
           gdn_a_log, gdn_dt_bias, gdn_norm_w):
    bias_tiles = _attn_bias_tiles(rel_bias)
    for l in range(DEPTH):
        x = _layer(x, bias_tiles, norm_w[l], w_in[l], w_out[l], qk_norm_w[l], mlstm_i_bias[l], mlstm_f_bias[l],
                   mlstm_norm_w[l], gdn_conv_w[l], gdn_a_log[l], gdn_dt_bias[l], gdn_norm_w[l])
    return x
```

```python
import functools

import numpy as np
import jax
import jax.numpy as jnp
from jax import lax
from jax.experimental import pallas as pl
from jax.experimental.pallas import tpu as pltpu

F32 = jnp.float32
BF16 = jnp.bfloat16

D_MODEL = 1024
DEPTH = 2
EPS = 1e-6
HEAD_DIM = 64
LANES = 128
SUBLANES = 8
MXU_WIDTH = 256
D_A, D_B, D_C = 512, 256, 256
DILATED_CFGS = ((128, 1), (512, 4), (2048, 16))
N_SIDE = 64
NUM_BUCKETS = 32
REL_MAX_DIST = 1024
CONV_W = 5
CHUNK = 256
Q_SUPER = 2048
Q_BLK = 128
K_WIN = 256
ATTN_ILP = 4
IN_TILE = 256
OUT_TILE = 512
NEG = -1e30
LOG2E = 1.4426950408889634
VMEM_LIMIT = 56 * 1024 * 1024

_OFF_AQ, _OFF_AZ = 0, 1536
_OFF_BQ, _OFF_BIF, _OFF_BO, _OFF_BZ = 2048, 2816, 2832, 3088
_OFF_CQ, _OFF_CAB, _OFF_CZ = 3344, 4112, 4128


def _dot(a, b):
    return jnp.dot(a, b, preferred_element_type=F32)


def _dot_nt(a, b):
    return lax.dot_general(a, b, (((1,), (1,)), ((), ())), preferred_element_type=F32)


def _dot_tn(a, b):
    return lax.dot_general(a, b, (((0,), (0,)), ((), ())), preferred_element_type=F32)


def _split(a):
    hi = a.astype(BF16)
    lo = (a - hi.astype(F32)).astype(BF16)
    return hi, lo


def _iota2(shape, axis):
    return lax.broadcasted_iota(jnp.int32, shape, axis)


def _pair_block_diag(rows, cols):
    r = _iota2((rows, cols), 0)
    c = _iota2((rows, cols), 1)
    return (r < HEAD_DIM) == ((c & HEAD_DIM) == 0)


def _head_block_diag(n):
    return (_iota2((n, n), 0) ^ _iota2((n, n), 1)) < HEAD_DIM


def _head_sum(t, bd):
    hi, lo = _split(t)
    return _dot(hi, bd) + _dot(lo, bd)


def _softplus(y):
    return jnp.maximum(y, 0.0) + jnp.log1p(jnp.exp(-jnp.abs(y)))


def _sigmoid(y):
    return 1.0 / (1.0 + jnp.exp(-y))


def _in_proj_kernel(x_ref, xp_ref, xn_ref, nw_ref, wa_ref, wb_ref, wc_ref, wg_ref, wz_ref,
                    qkw_ref, gpar_ref, cw_ref,
                    a_ref, b_ref, c_ref, g_ref, z_ref, cext_ref):
    i = pl.program_id(1)
    n = pl.num_programs(1)
    tile = x_ref.shape[1]
    bd = _head_block_diag(MXU_WIDTH).astype(BF16)

    xe = jnp.concatenate([xp_ref[0], x_ref[0], xn_ref[0]], axis=0)
    ms = jnp.mean(xe * xe, axis=-1, keepdims=True)
    hne = xe * lax.rsqrt(ms + EPS) * nw_ref[...]
    hn = hne[SUBLANES:SUBLANES + tile].astype(BF16)
    hne = hne.astype(BF16)

    ta = [_dot(hn, wa_ref[:, j * MXU_WIDTH:(j + 1) * MXU_WIDTH]) for j in range(3 * D_A // MXU_WIDTH)]
    tb = _dot(hn, wb_ref[...])
    p = _dot(hn, wg_ref[...]) + gpar_ref[0:1, :]
    ce = _dot(hne, wc_ref[...])
    z_ref[0] = _dot(hn, wz_ref[...]).astype(z_ref.dtype)

    row = _iota2((tile + 2 * SUBLANES, 1), 0)
    valid = ((row >= SUBLANES) | (i > 0)) & ((row < tile + SUBLANES) | (i < n - 1))
    cext_ref[...] = jnp.where(valid, ce, 0.0)
    conv = cw_ref[0:1, :] * cext_ref[pl.ds(SUBLANES - CONV_W // 2, tile), :]
    for t in range(1, CONV_W):
        conv = conv + cw_ref[t:t + 1, :] * cext_ref[pl.ds(SUBLANES - CONV_W // 2 + t, tile), :]
    s = conv * _sigmoid(conv)
    tc = [s[:, j * MXU_WIDTH:(j + 1) * MXU_WIDTH] for j in range(3 * D_C // MXU_WIDTH)]

    ssa = [_head_sum(t * t, bd) for t in ta[:4]]
    ssc = [_head_sum(t * t, bd) for t in tc[:2]]

    for j, t in enumerate(ta):
        if j < 4:
            t = t * lax.rsqrt(ssa[j] * (1.0 / HEAD_DIM) + EPS) * qkw_ref[j // 2:j // 2 + 1, :]
        a_ref[0, :, j * MXU_WIDTH:(j + 1) * MXU_WIDTH] = t

    b_ref[0, :, 0:D_B] = (tb[:, 0:D_B] * HEAD_DIM ** -0.5).astype(b_ref.dtype)
    b_ref[0, :, D_B:3 * D_B] = tb[:, D_B:3 * D_B].astype(b_ref.dtype)

    lane = _iota2(p.shape, 1)
    first4 = (lane & (LANES - 1)) < 4
    is_b = lane < 2 * LANES
    val_b = jnp.where(first4, p, -_softplus(-p))
    val_c = jnp.where(first4, -jnp.exp(gpar_ref[1:2, :]) * _softplus(p), _sigmoid(p))
    g_ref[0] = jnp.where(is_b, val_b, val_c)

    c_ref[0, :, 0:D_C] = (tc[0] * lax.rsqrt(ssc[0] + EPS) * HEAD_DIM ** -0.5).astype(c_ref.dtype)
    c_ref[0, :, D_C:2 * D_C] = (tc[1] * lax.rsqrt(ssc[1] + EPS)).astype(c_ref.dtype)
    c_ref[0, :, 2 * D_C:3 * D_C] = tc[2].astype(c_ref.dtype)


def _in_proj(x, nw, wa, wb, wc, wg, wz, qkw, gpar, cw):
    bsz, seq, d = x.shape
    tile = IN_TILE
    nt = seq // tile
    hb = tile // SUBLANES
    full = lambda a: pl.BlockSpec(a.shape, lambda b, i: (0,) * a.ndim)
    out_shapes = (
        jax.ShapeDtypeStruct((bsz, seq, 3 * D_A), F32),
        jax.ShapeDtypeStruct((bsz, seq, 3 * D_B), BF16),
        jax.ShapeDtypeStruct((bsz, seq, 3 * D_C), BF16),
        jax.ShapeDtypeStruct((bsz, seq, 4 * LANES), F32),
        jax.ShapeDtypeStruct((bsz, seq, D_A + 3 * D_B), BF16),
    )
    row_spec = lambda w: pl.BlockSpec((1, tile, w), lambda b, i: (b, i, 0))
    return pl.pallas_call(
        _in_proj_kernel,
        grid=(bsz, nt),
        in_specs=[
            row_spec(d),
            pl.BlockSpec((1, SUBLANES, d), lambda b, i: (b, jnp.maximum(i * hb - 1, 0), 0)),
            pl.BlockSpec((1, SUBLANES, d), lambda b, i: (b, jnp.minimum((i + 1) * hb, nt * hb - 1), 0)),
            full(nw), full(wa), full(wb), full(wc), full(wg), full(wz), full(qkw), full(gpar), full(cw),
        ],
        out_specs=[row_spec(s.shape[-1]) for s in out_shapes],
        out_shape=out_shapes,
        scratch_shapes=[pltpu.VMEM((tile + 2 * SUBLANES, 3 * D_C), F32)],
        compiler_params=pltpu.CompilerParams(
            dimension_semantics=("parallel", "arbitrary"), vmem_limit_bytes=VMEM_LIMIT),
        name="in_proj",
    )(x, x, x, nw, wa, wb, wc, wg, wz, qkw, gpar, cw)


def _t5_bucket(rel):
    half = NUM_BUCKETS // 2
    max_exact = half // 2
    n = np.abs(rel)
    large = max_exact + (np.log(np.maximum(n, 1) / max_exact) / np.log(REL_MAX_DIST / max_exact)
                         * (half - max_exact)).astype(np.int32)
    large = np.minimum(large, half - 1)
    return (rel > 0).astype(np.int32) * half + np.where(n < max_exact, n, large)


def _attn_bias_tiles(rel_bias):
    period = 640
    wide = K_WIN + 2 * N_SIDE
    lead = Q_BLK - 1 + N_SIDE
    cfg_tiles = []
    for (_, dil) in DILATED_CFGS:
        offs = dil * np.arange(-N_SIDE, N_SIDE + 1)
        bias = rel_bias[_t5_bucket(offs)].T.astype(F32) * LOG2E
        heads = bias.shape[0]
        vec = jnp.concatenate([jnp.full((heads, lead + N_SIDE), NEG, F32), bias,
                               jnp.full((heads, period - lead - 3 * N_SIDE - 1), NEG, F32)], axis=1)
        rows = jnp.tile(vec, (1, Q_BLK))[:, :Q_BLK * (period - 1)].reshape(heads, Q_BLK, period - 1)
        ext = rows[:, :, lead:lead + wide]
        variants = [ext[:, :, 2 * N_SIDE:2 * N_SIDE + K_WIN], ext[:, :, N_SIDE:N_SIDE + K_WIN],
                    ext[:, :, 0:K_WIN]]
        cfg_tiles.append(jnp.stack(variants, axis=1))
    return jnp.stack(cfg_tiles, axis=0)


def _attn_kernel(q_ref, k_ref, v_ref, bias_ref, o_ref, acc_ref, m_ref, l_ref, kw_ref, vw_ref):
    sb = pl.program_id(2)
    seq = k_ref.shape[1]
    qsup = q_ref.shape[1]
    nblk = qsup // Q_BLK
    head0 = _iota2((Q_BLK, LANES), 1) < HEAD_DIM
    sel0 = (_iota2((Q_BLK, 2 * LANES), 1) & HEAD_DIM) == 0

    wide = DILATED_CFGS[-1][1]

    @pl.when(sb == 0)
    def _():
        def regroup(c, carry):
            r = c % wide
            i = c // wide
            src = pl.ds(r + wide * (i * K_WIN), K_WIN, stride=wide)
            dst = pl.ds(pl.multiple_of(r * (seq // wide) + i * K_WIN, K_WIN), K_WIN)
            kw_ref[dst, :] = k_ref[0, src, :].astype(BF16)
            vw_ref[dst, :] = v_ref[0, src, :].astype(BF16)
            return carry

        lax.fori_loop(0, seq // K_WIN, regroup, 0)

    groups = [(pos, ci, g) for pos, ci in enumerate(reversed(range(len(DILATED_CFGS))))
              for g in range(nblk // ATTN_ILP)]

    def scores(pos, ci, g):
        dil = DILATED_CFGS[ci][1]
        n_idx = seq // dil
        blocks = []
        for u in range(ATTN_ILP):
            t = g * ATTN_ILP + u
            r, j = t % dil, t // dil
            qs = r + dil * (j * Q_BLK)
            i0 = sb * (qsup // dil) + j * Q_BLK
            ws = jnp.clip(i0 - N_SIDE, 0, n_idx - K_WIN)
            var = jnp.where(i0 == 0, 0, jnp.where(i0 == n_idx - Q_BLK, 2, 1))
            qrows = pl.ds(qs, Q_BLK) if dil == 1 else pl.ds(qs, Q_BLK, stride=dil)
            if dil == wide:
                krows = pl.ds(pl.multiple_of(r * n_idx + ws, N_SIDE), K_WIN)
                k2 = kw_ref[krows, :]
                v2 = vw_ref[krows, :]
            else:
                ks = r + dil * ws
                krows = pl.ds(ks, K_WIN) if dil == 1 else pl.ds(ks, K_WIN, stride=dil)
                k2 = k_ref[0, krows, :].astype(BF16)
                v2 = v_ref[0, krows, :].astype(BF16)
            q2 = q_ref[0, qrows, :].astype(BF16)
            blocks.append(dict(qrows=qrows, var=var, k2=k2,
                               qh=[jnp.where(head0 if h == 0 else ~head0, q2, jnp.zeros_like(q2)) for h in range(2)],
                               vo=jnp.concatenate([v2, jnp.ones_like(v2)], axis=1)))
        chains = [(blk, h) for blk in blocks for h in range(2)]
        s = [_dot_nt(blk["qh"][h], blk["k2"]) + bias_ref[ci, h, blk["var"]] for blk, h in chains]
        mx = [jnp.max(sh, axis=-1, keepdims=True) for sh in s]
        p = [jnp.exp2(sh - m).astype(BF16) for sh, m in zip(s, mx)]
        return dict(pos=pos, blocks=blocks, chains=chains, mx=mx, p=p)

    def accumulate(st):
        pv = [_dot(ph, blk["vo"]) for ph, (blk, _) in zip(st["p"], st["chains"])]
        for u, blk in enumerate(st["blocks"]):
            qrows = blk["qrows"]
            tot = jnp.where(sel0, pv[2 * u], pv[2 * u + 1])
            num2 = tot[:, :LANES]
            l2 = tot[:, LANES:]
            m2 = jnp.where(head0, st["mx"][2 * u], st["mx"][2 * u + 1])
            if st["pos"] == 0:
                acc_ref[qrows, :] = num2
                m_ref[qrows, :] = m2
                l_ref[qrows, :] = l2
            else:
                mo = m_ref[qrows, :]
                mn = jnp.maximum(mo, m2)
                a = jnp.exp2(mo - mn)
                b = jnp.exp2(m2 - mn)
                acc_ref[qrows, :] = acc_ref[qrows, :] * a + num2 * b
                l_ref[qrows, :] = l_ref[qrows, :] * a + l2 * b
                m_ref[qrows, :] = mn

    pending = scores(*groups[0])
    for nxt in groups[1:]:
        upcoming = scores(*nxt)
        accumulate(pending)
        pending = upcoming
    accumulate(pending)

    o_ref[0] = (acc_ref[...] / l_ref[...]).astype(o_ref.dtype)


def _attention(aqkv, bias_tiles):
    bsz, seq, _ = aqkv.shape
    npair = D_A // LANES
    qsup = Q_SUPER
    assert seq % qsup == 0 and seq // DILATED_CFGS[-1][1] >= K_WIN
    return pl.pallas_call(
        _attn_kernel,
        grid=(bsz, npair, seq // qsup),
        in_specs=[
            pl.BlockSpec((1, qsup, LANES), lambda b, p, s: (b, s, p)),
            pl.BlockSpec((1, seq, LANES), lambda b, p, s: (b, 0, npair + p)),
            pl.BlockSpec((1, seq, LANES), lambda b, p, s: (b, 0, 2 * npair + p)),
            pl.BlockSpec((len(DILATED_CFGS), 2, 3, Q_BLK, K_WIN), lambda b, p, s: (0, p, 0, 0, 0)),
        ],
        out_specs=pl.BlockSpec((1, qsup, LANES), lambda b, p, s: (b, s, p)),
        out_shape=jax.ShapeDtypeStruct((bsz, seq, D_A), F32),
        scratch_shapes=[pltpu.VMEM((qsup, LANES), F32)] * 3 + [pltpu.VMEM((seq, LANES), BF16)] * 2,
        compiler_params=pltpu.CompilerParams(
            dimension_semantics=("parallel", "parallel", "arbitrary"), vmem_limit_bytes=VMEM_LIMIT),
        name="dilated_attn",
    )(aqkv, aqkv, aqkv, bias_tiles)


def _direction_masks(backward):
    row = _iota2((CHUNK, CHUNK), 0)
    col = _iota2((CHUNK, CHUNK), 1)
    if backward:
        return row <= col, row < col, row ^ col
    return row >= col, row > col, row ^ col


def _gate_cumsums(g, incl, incl_other):
    tri = incl.astype(BF16)
    tri_t = incl_other.astype(BF16)
    ghi, glo = _split(g)
    gb = _dot(tri, ghi) + _dot(tri, glo)
    gt = g.T
    thi, tlo = _split(gt)
    gbt = _dot(thi, tri_t) + _dot(tlo, tri_t)
    return gb, gt, gbt


def _mlstm_kernel(qf, kf, vf, gf, qb, kb, vb, gb, hf_ref, hb_ref, sf, mf, sbk, mb):
    @pl.when(pl.program_id(2) == 0)
    def _():
        sf[...] = jnp.zeros_like(sf)
        mf[...] = jnp.zeros_like(mf)
        sbk[...] = jnp.zeros_like(sbk)
        mb[...] = jnp.zeros_like(mb)

    head0 = _iota2((CHUNK, LANES), 1) < HEAD_DIM
    sel0 = (_iota2((CHUNK, 2 * LANES), 1) & HEAD_DIM) == 0
    bd = _pair_block_diag(LANES, 2 * LANES)
    fwd_masks = _direction_masks(False)
    bwd_masks = _direction_masks(True)

    dirs = []
    for backward, refs, h_ref, s_ref, m_ref in ((False, (qf, kf, vf, gf), hf_ref, sf, mf),
                                                (True, (qb, kb, vb, gb), hb_ref, sbk, mb)):
        incl = (bwd_masks if backward else fwd_masks)[0]
        incl_other = (fwd_masks if backward else bwd_masks)[0]
        ci0, cf0 = (2, 6) if backward else (0, 4)
        last = 0 if backward else CHUNK - 1
        q2, k2, v2, g = (r[0] for r in refs)
        gbc, gt, gbt = _gate_cumsums(g, incl, incl_other)
        heads = []
        for h in range(2):
            li = g[:, ci0 + h:ci0 + h + 1]
            bc = gbc[:, cf0 + h:cf0 + h + 1]
            rrow = gt[ci0 + h:ci0 + h + 1, :] - gbt[cf0 + h:cf0 + h + 1, :]
            mst = m_ref[h:h + 1, 0:1]
            dmat = jnp.where(incl, bc + rrow, -jnp.inf)
            inter = bc + mst
            mt = jnp.maximum(jnp.max(dmat, axis=-1, keepdims=True), inter)
            blast = gbc[last:last + 1, cf0 + h:cf0 + h + 1]
            wlog = blast - bc + li
            mn = jnp.maximum(blast + mst, jnp.max(wlog, axis=0, keepdims=True))
            heads.append(dict(decay=jnp.exp(dmat - mt), iw=jnp.exp(inter - mt), emt=jnp.exp(-mt),
                              ws=jnp.exp(wlog - mn), dec=jnp.exp(blast + mst - mn), mnew=mn))
        state = s_ref[...]
        dirs.append(dict(q2=q2, k2=k2, heads=heads, state=state, h_ref=h_ref, s_ref=s_ref, m_ref=m_ref,
                         vo=jnp.concatenate([v2, jnp.ones_like(v2)], axis=1)))

    chains = [(d, h) for d in dirs for h in range(2)]
    q_state = [_dot(d["q2"], d["state"].astype(BF16)) for d in dirs]
    qk = [_dot_nt(jnp.where(head0 if h == 0 else ~head0, d["q2"], 0).astype(BF16), d["k2"]) for d, h in chains]
    sc = [(qkh * d["heads"][h]["decay"]).astype(BF16) for (d, h), qkh in zip(chains, qk)]
    pv = [_dot(sch, d["vo"]) for (d, _), sch in zip(chains, sc)]
    upd = [_dot_tn(d["k2"], (jnp.where(sel0, d["heads"][0]["ws"], d["heads"][1]["ws"])
                             * d["vo"].astype(F32)).astype(BF16)) for d in dirs]
    for i, d in enumerate(dirs):
        h0, h1 = d["heads"]
        tot = jnp.where(sel0, h0["iw"], h1["iw"]) * q_state[i] + jnp.where(sel0, pv[2 * i], pv[2 * i + 1])
        den = jnp.maximum(jnp.abs(tot[:, LANES:]), jnp.where(head0, h0["emt"], h1["emt"]))
        d["h_ref"][0] = tot[:, :LANES] / den
        dec2 = jnp.where(sel0[0:1, :], h0["dec"], h1["dec"])
        d["s_ref"][...] = dec2 * d["state"] + jnp.where(bd, upd[i], 0.0)
        d["m_ref"][0:1, :] = jnp.broadcast_to(h0["mnew"], (1, LANES))
        d["m_ref"][1:2, :] = jnp.broadcast_to(h1["mnew"], (1, LANES))


def _bidir_specs(nc, npair, gate_tile0):
    specs = []
    for rev in (False, True):
        cidx = (lambda c: nc - 1 - c) if rev else (lambda c: c)
        for part in range(3):
            specs.append(pl.BlockSpec(
                (1, CHUNK, LANES), lambda b, p, c, part=part, cidx=cidx: (b, cidx(c), part * npair + p)))
        specs.append(pl.BlockSpec(
            (1, CHUNK, LANES), lambda b, p, c, cidx=cidx: (b, cidx(c), gate_tile0 + p)))
    return specs


def _mlstm(bqkv, gates):
    bsz, seq, _ = bqkv.shape
    npair = D_B // LANES
    nc = seq // CHUNK
    out = jax.ShapeDtypeStruct((bsz, seq, D_B), F32)
    return pl.pallas_call(
        _mlstm_kernel,
        grid=(bsz, npair, nc),
        in_specs=_bidir_specs(nc, npair, 0),
        out_specs=[pl.BlockSpec((1, CHUNK, LANES), lambda b, p, c: (b, c, p)),
                   pl.BlockSpec((1, CHUNK, LANES), lambda b, p, c: (b, nc - 1 - c, p))],
        out_shape=(out, out),
        scratch_shapes=[pltpu.VMEM((LANES, 2 * LANES), F32), pltpu.VMEM((SUBLANES, LANES), F32),
                        pltpu.VMEM((LANES, 2 * LANES), F32), pltpu.VMEM((SUBLANES, LANES), F32)],
        compiler_params=pltpu.CompilerParams(
            dimension_semantics=("parallel", "parallel", "arbitrary"), vmem_limit_bytes=VMEM_LIMIT),
        name="mlstm",
    )(bqkv, bqkv, bqkv, gates, bqkv, bqkv, bqkv, gates)


_INV_BASE = 32


def _unit_tri_inverses(ms, xor):
    eye = (xor == 0).astype(F32)
    base = xor < _INV_BASE
    mbs = [jnp.where(base, m, 0.0) for m in ms]
    ts = [eye - mb for mb in mbs]
    pbs = [mb.astype(BF16) for mb in mbs]
    k = 1
    while True:
        pbs = [_dot(pb, pb).astype(BF16) for pb in pbs]
        k *= 2
        ts = [t + _dot(t.astype(BF16), pb) for t, pb in zip(ts, pbs)]
        if 2 * k >= _INV_BASE:
            break
    b = _INV_BASE
    while b < CHUNK:
        level = (xor >= b) & (xor < 2 * b)
        tbs = [t.astype(BF16) for t in ts]
        xs = [_dot(jnp.where(level, m, 0.0).astype(BF16), tb).astype(BF16) for m, tb in zip(ms, tbs)]
        ts = [t - _dot(tb, x) for t, tb, x in zip(ts, tbs, xs)]
        b *= 2
    return ts


def _gdn_kernel(qf, kf, vf, gf, qb, kb, vb, gb, of_ref, ob_ref, sf, sbk):
    @pl.when(pl.program_id(2) == 0)
    def _():
        sf[...] = jnp.zeros_like(sf)
        sbk[...] = jnp.zeros_like(sbk)

    head0 = _iota2((CHUNK, LANES), 1) < HEAD_DIM
    sel0 = (_iota2((CHUNK, 2 * LANES), 1) & HEAD_DIM) == 0
    bd = _pair_block_diag(LANES, LANES)
    fwd_masks = _direction_masks(False)
    bwd_masks = _direction_masks(True)
    xor = fwd_masks[2]

    dirs = []
    for backward, refs, o_ref, s_ref in ((False, (qf, kf, vf, gf), of_ref, sf),
                                         (True, (qb, kb, vb, gb), ob_ref, sbk)):
        incl, strict, _ = bwd_masks if backward else fwd_masks
        incl_other = (fwd_masks if backward else bwd_masks)[0]
        ca0, cb0 = (2, 6) if backward else (0, 4)
        last = 0 if backward else CHUNK - 1
        q2, k2, v2, g = (r[0] for r in refs)
        gbc, _, gbt = _gate_cumsums(g, incl, incl_other)
        gam = [gbc[:, ca0 + h:ca0 + h + 1] for h in range(2)]
        beta = [g[:, cb0 + h:cb0 + h + 1] for h in range(2)]
        glast2 = jnp.where(head0[0:1, :], gbc[last:last + 1, ca0:ca0 + 1], gbc[last:last + 1, ca0 + 1:ca0 + 2])
        gam2 = jnp.where(head0, gam[0], gam[1])
        beta2 = jnp.where(head0, beta[0], beta[1])
        egam2 = jnp.exp(gam2)
        kf32 = k2.astype(F32)
        rhs = jnp.concatenate([v2.astype(F32) * beta2, kf32 * (beta2 * egam2)], axis=1).astype(BF16)
        decay = [jnp.exp(jnp.where(incl, gam[h] - gbt[ca0 + h:ca0 + h + 1, :], -jnp.inf)) for h in range(2)]
        state = s_ref[...]
        dirs.append(dict(q2=q2, k2=k2, kf32=kf32, rhs=rhs, decay=decay, beta=beta, strict=strict, gam2=gam2,
                         egam2=egam2, glast2=glast2, state=state, state_b=state.astype(BF16), o_ref=o_ref,
                         s_ref=s_ref))

    chains = [(d, h) for d in dirs for h in range(2)]

    def head_lanes(x, h):
        return jnp.where(head0 if h == 0 else ~head0, x, 0).astype(BF16)

    kk = [_dot_nt(head_lanes(d["k2"], h), d["k2"]) for d, h in chains]
    ms = [jnp.where(d["strict"], d["beta"][h] * kkh * d["decay"][h], 0.0) for (d, h), kkh in zip(chains, kk)]
    qk = [_dot_nt(head_lanes(d["q2"], h), d["k2"]) for d, h in chains]
    a_intra = [(qkh * d["decay"][h]).astype(BF16) for (d, h), qkh in zip(chains, qk)]
    t_inv = _unit_tri_inverses(ms, xor)
    uw = [_dot(t.astype(BF16), d["rhs"]) for (d, _), t in zip(chains, t_inv)]

    uw2 = [jnp.where(sel0, uw[2 * i], uw[2 * i + 1]) for i in range(2)]
    w_state = [_dot(uw2[i][:, LANES:].astype(BF16), d["state_b"]) for i, d in enumerate(dirs)]
    v_new = [(uw2[i][:, :LANES] - w_state[i]).astype(BF16) for i in range(2)]
    q_state = [_dot((d["q2"].astype(F32) * d["egam2"]).astype(BF16), d["state_b"]) for d in dirs]
    av = [_dot(a_intra[j], v_new[j // 2]) for j in range(4)]
    upd = [_dot_tn((d["kf32"] * jnp.exp(d["glast2"] - d["gam2"])).astype(BF16), v_new[i])
           for i, d in enumerate(dirs)]
    for i, d in enumerate(dirs):
        d["o_ref"][0] = q_state[i] + jnp.where(head0, av[2 * i], av[2 * i + 1])
        d["s_ref"][...] = d["state"] * jnp.exp(d["glast2"]) + jnp.where(bd, upd[i], 0.0)


def _gdn(cqkv, gates):
    bsz, seq, _ = cqkv.shape
    npair = D_C // LANES
    nc = seq // CHUNK
    out = jax.ShapeDtypeStruct((bsz, seq, D_C), F32)
    return pl.pallas_call(
        _gdn_kernel,
        grid=(bsz, npair, nc),
        in_specs=_bidir_specs(nc, npair, 2),
        out_specs=[pl.BlockSpec((1, CHUNK, LANES), lambda b, p, c: (b, c, p)),
                   pl.BlockSpec((1, CHUNK, LANES), lambda b, p, c: (b, nc - 1 - c, p))],
        out_shape=(out, out),
        scratch_shapes=[pltpu.VMEM((LANES, LANES), F32), pltpu.VMEM((LANES, LANES), F32)],
        compiler_params=pltpu.CompilerParams(
            dimension_semantics=("parallel", "parallel", "arbitrary"), vmem_limit_bytes=VMEM_LIMIT),
        name="gdn",
    )(cqkv, cqkv, cqkv, gates, cqkv, cqkv, cqkv, gates)


def _out_kernel(x_ref, oa_ref, hf_ref, hb_ref, cf_ref, cb_ref, z_ref, mw_ref, gw_ref, wo_ref, y_ref):
    bd = _head_block_diag(MXU_WIDTH).astype(BF16)
    z = z_ref[0].astype(F32)

    def silu(t):
        return t * _sigmoid(t)

    hsum = _sigmoid(z[:, D_A:D_A + D_B]) * (hf_ref[0] + hb_ref[0])
    csum = cf_ref[0] + cb_ref[0]
    ssb = _head_sum(hsum * hsum, bd)
    ssc = _head_sum(csum * csum, bd)
    ya = oa_ref[0].astype(F32) * silu(z[:, 0:D_A])
    yb = hsum * lax.rsqrt(ssb * (1.0 / HEAD_DIM) + EPS) * mw_ref[...] * silu(z[:, D_A + D_B:D_A + 2 * D_B])
    yc = csum * lax.rsqrt(ssc * (1.0 / HEAD_DIM) + EPS) * gw_ref[...] * silu(z[:, D_A + 2 * D_B:])
    y = jnp.concatenate([ya, yb, yc], axis=1).astype(BF16)
    y_ref[0] = x_ref[0] + _dot(y, wo_ref[...])


def _out_proj(x, oa, hf, hb, cf, cb, z, mw, gw, wo):
    bsz, seq, d = x.shape
    tile = OUT_TILE
    row_spec = lambda a: pl.BlockSpec((1, tile, a.shape[-1]), lambda b, i: (b, i, 0))
    full = lambda a: pl.BlockSpec(a.shape, lambda b, i: (0,) * a.ndim)
    return pl.pallas_call(
        _out_kernel,
        grid=(bsz, seq // tile),
        in_specs=[row_spec(a) for a in (x, oa, hf, hb, cf, cb, z)] + [full(mw), full(gw), full(wo)],
        out_specs=pl.BlockSpec((1, tile, d), lambda b, i: (b, i, 0)),
        out_shape=jax.ShapeDtypeStruct(x.shape, x.dtype),
        compiler_params=pltpu.CompilerParams(
            dimension_semantics=("parallel", "parallel"), vmem_limit_bytes=VMEM_LIMIT),
        name="out_proj",
    )(x, oa, hf, hb, cf, cb, z, mw, gw, wo)


def _gate_columns():
    cols = np.full((4, LANES), -1, np.int64)
    for mixer, off, heads in ((0, _OFF_BIF, D_B // HEAD_DIM), (1, _OFF_CAB, D_C // HEAD_DIM)):
        for pair in range(heads // 2):
            for lane in range(8):
                cols[mixer * 2 + pair, lane] = off + (lane // 2) * heads + 2 * pair + lane % 2
    return cols.reshape(-1)


def _gate_params(i_bias, f_bias, a_log, dt_bias):
    nh = i_bias.shape[1]
    flat = jnp.concatenate([i_bias.reshape(-1), f_bias.reshape(-1), dt_bias.reshape(-1), a_log.reshape(-1),
                            jnp.zeros((1,), F32)])
    idx = np.full((SUBLANES, 4 * LANES), flat.shape[0] - 1, np.int32)
    for pair in range(2):
        for lane in range(8):
            kind, direction, head = lane // 4, (lane // 2) % 2, 2 * pair + lane % 2
            src = direction * nh + head
            idx[0, pair * LANES + lane] = kind * 2 * nh + src
            if kind == 0:
                idx[0, (2 + pair) * LANES + lane] = 2 * 2 * nh + src
                idx[1, (2 + pair) * LANES + lane] = 3 * 2 * nh + src
    return flat[idx]


def _layer(x, bias_tiles, norm_w, w_in, w_out, qk_norm_w, i_bias, f_bias, m_norm_w, conv_w, a_log, dt_bias,
           g_norm_w):
    wa = w_in[:, _OFF_AQ:_OFF_AZ].astype(BF16)
    wb = w_in[:, _OFF_BQ:_OFF_BIF].astype(BF16)
    wc = w_in[:, _OFF_CQ:_OFF_CAB].astype(BF16)
    wz = jnp.concatenate([w_in[:, _OFF_AZ:_OFF_BQ], w_in[:, _OFF_BO:_OFF_CQ], w_in[:, _OFF_CZ:]],
                         axis=1).astype(BF16)
    gcols = _gate_columns()
    wg = jnp.where((gcols >= 0)[None, :], w_in[:, np.maximum(gcols, 0)], 0.0).astype(BF16)
    qkw = jnp.pad(jnp.tile(qk_norm_w, (1, MXU_WIDTH // HEAD_DIM)) * jnp.array([[HEAD_DIM ** -0.5 * LOG2E], [1.0]], F32),
                  ((0, SUBLANES - 2), (0, 0)))
    gpar = _gate_params(i_bias, f_bias, a_log, dt_bias)
    cw = jnp.pad(conv_w, ((0, SUBLANES - CONV_W), (0, 0)))

    aqkv, bqkv, cqkv, gates, z = _in_proj(x, norm_w[None, :], wa, wb, wc, wg, wz, qkw, gpar, cw)
    oa = _attention(aqkv, bias_tiles)
    hf, hb = _mlstm(bqkv, gates)
    cf, cb = _gdn(cqkv, gates)
    mw = jnp.tile(m_norm_w, D_B // HEAD_DIM)[None, :]
    gw = jnp.tile(g_norm_w, D_C // HEAD_DIM)[None, :]
    return _out_proj(x, oa, hf, hb, cf, cb, z, mw, gw, w_out.astype(BF16))


def kernel(x, norm_w, w_in, w_out, qk_norm_w, rel_bias, mlstm_i_bias, mlstm_f_bias, mlstm_norm_w, gdn_conv_w,
           gdn_a_log, gdn_dt_bias, gdn_norm_w):
    bias_tiles = _attn_bias_tiles(rel_bias)
    for l in range(DEPTH):
        x = _layer(x, bias_tiles, norm_w[l], w_in[l], w_out[l], qk_norm_w[l], mlstm_i_bias[l], mlstm_f_bias[l],
                   mlstm_norm_w[l], gdn_conv_w[l], gdn_a_log[l], gdn_dt_bias[l], gdn_norm_w[l])
    return x
```

```python
import functools

import numpy as np
import jax
import jax.numpy as jnp
from jax import lax
from jax.experimental import pallas as pl
from jax.experimental.pallas import tpu as pltpu

F32 = jnp.float32
BF16 = jnp.bfloat16

D_MODEL = 1024
DEPTH = 2
EPS = 1e-6
HEAD_DIM = 64
LANES = 128
SUBLANES = 8
MXU_WIDTH = 256
D_A, D_B, D_C = 512, 256, 256
DILATED_CFGS = ((128, 1), (512, 4), (2048, 16))
N_SIDE = 64
NUM_BUCKETS = 32
REL_MAX_DIST = 1024
CONV_W = 5
CHUNK = 256
Q_SUPER = 2048
Q_BLK = 128
K_WIN = 256
ATTN_ILP = 4
IN_TILE = 256
OUT_TILE = 512
NEG = -1e30
LOG2E = 1.4426950408889634
VMEM_LIMIT = 56 * 1024 * 1024

_OFF_AQ, _OFF_AZ = 0, 1536
_OFF_BQ, _OFF_BIF, _OFF_BO, _OFF_BZ = 2048, 2816, 2832, 3088
_OFF_CQ, _OFF_CAB, _OFF_CZ = 3344, 4112, 4128


def _dot(a, b):
    return jnp.dot(a, b, preferred_element_type=F32)


def _dot_nt(a, b):
    return lax.dot_general(a, b, (((1,), (1,)), ((), ())), preferred_element_type=F32)


def _dot_tn(a, b):
    return lax.dot_general(a, b, (((0,), (0,)), ((), ())), preferred_element_type=F32)


def _split(a):
    hi = a.astype(BF16)
    lo = (a - hi.astype(F32)).astype(BF16)
    return hi, lo


def _iota2(shape, axis):
    return lax.broadcasted_iota(jnp.int32, shape, axis)


def _pair_block_diag(rows, cols):
    r = _iota2((rows, cols), 0)
    c = _iota2((rows, cols), 1)
    return (r < HEAD_DIM) == ((c & HEAD_DIM) == 0)


def _head_block_diag(n):
    return (_iota2((n, n), 0) ^ _iota2((n, n), 1)) < HEAD_DIM


def _head_sum(t, bd):
    hi, lo = _split(t)
    return _dot(hi, bd) + _dot(lo, bd)


def _softplus(y):
    return jnp.maximum(y, 0.0) + jnp.log1p(jnp.exp(-jnp.abs(y)))


def _sigmoid(y):
    return 1.0 / (1.0 + jnp.exp(-y))


def _in_proj_kernel(x_ref, xp_ref, xn_ref, nw_ref, wa_ref, wb_ref, wc_ref, wg_ref, wz_ref,
                    qkw_ref, gpar_ref, cw_ref,
                    a_ref, b_ref, c_ref, g_ref, z_ref, cext_ref):
    i = pl.program_id(1)
    n = pl.num_programs(1)
    tile = x_ref.shape[1]
    bd = _head_block_diag(MXU_WIDTH).astype(BF16)

    xe = jnp.concatenate([xp_ref[0], x_ref[0], xn_ref[0]], axis=0)
    ms = jnp.mean(xe * xe, axis=-1, keepdims=True)
    hne = xe * lax.rsqrt(ms + EPS) * nw_ref[...]
    hn = hne[SUBLANES:SUBLANES + tile].astype(BF16)
    hne = hne.astype(BF16)

    ta = [_dot(hn, wa_ref[:, j * MXU_WIDTH:(j + 1) * MXU_WIDTH]) for j in range(3 * D_A // MXU_WIDTH)]
    tb = _dot(hn, wb_ref[...])
    p = _dot(hn, wg_ref[...]) + gpar_ref[0:1, :]
    ce = _dot(hne, wc_ref[...])
    z_ref[0] = _dot(hn, wz_ref[...]).astype(z_ref.dtype)

    row = _iota2((tile + 2 * SUBLANES, 1), 0)
    valid = ((row >= SUBLANES) | (i > 0)) & ((row < tile + SUBLANES) | (i < n - 1))
    cext_ref[...] = jnp.where(valid, ce, 0.0)
    conv = cw_ref[0:1, :] * cext_ref[pl.ds(SUBLANES - CONV_W // 2, tile), :]
    for t in range(1, CONV_W):
        conv = conv + cw_ref[t:t + 1, :] * cext_ref[pl.ds(SUBLANES - CONV_W // 2 + t, tile), :]
    s = conv * _sigmoid(conv)
    tc = [s[:, j * MXU_WIDTH:(j + 1) * MXU_WIDTH] for j in range(3 * D_C // MXU_WIDTH)]

    ssa = [_head_sum(t * t, bd) for t in ta[:4]]
    ssc = [_head_sum(t * t, bd) for t in tc[:2]]

    for j, t in enumerate(ta):
        if j < 4:
            t = t * lax.rsqrt(ssa[j] * (1.0 / HEAD_DIM) + EPS) * qkw_ref[j // 2:j // 2 + 1, :]
        a_ref[0, :, j * MXU_WIDTH:(j + 1) * MXU_WIDTH] = t

    b_ref[0, :, 0:D_B] = (tb[:, 0:D_B] * HEAD_DIM ** -0.5).astype(b_ref.dtype)
    b_ref[0, :, D_B:3 * D_B] = tb[:, D_B:3 * D_B].astype(b_ref.dtype)

    lane = _iota2(p.shape, 1)
    first4 = (lane & 7) < 4
    is_b = lane < 16
    val_b = jnp.where(first4, p, -_softplus(-p))
    val_c = jnp.where(first4, -jnp.exp(gpar_ref[1:2, :]) * _softplus(p), _sigmoid(p))
    g_ref[0] = jnp.where(is_b, val_b, val_c)

    c_ref[0, :, 0:D_C] = (tc[0] * lax.rsqrt(ssc[0] + EPS) * HEAD_DIM ** -0.5).astype(c_ref.dtype)
    c_ref[0, :, D_C:2 * D_C] = (tc[1] * lax.rsqrt(ssc[1] + EPS)).astype(c_ref.dtype)
    c_ref[0, :, 2 * D_C:3 * D_C] = tc[2].astype(c_ref.dtype)


def _in_proj(x, nw, wa, wb, wc, wg, wz, qkw, gpar, cw):
    bsz, seq, d = x.shape
    tile = IN_TILE
    nt = seq // tile
    hb = tile // SUBLANES
    full = lambda a: pl.BlockSpec(a.shape, lambda b, i: (0,) * a.ndim)
    out_shapes = (
        jax.ShapeDtypeStruct((bsz, seq, 3 * D_A), F32),
        jax.ShapeDtypeStruct((bsz, seq, 3 * D_B), BF16),
        jax.ShapeDtypeStruct((bsz, seq, 3 * D_C), BF16),
        jax.ShapeDtypeStruct((bsz, seq, LANES), F32),
        jax.ShapeDtypeStruct((bsz, seq, D_A + 3 * D_B), BF16),
    )
    row_spec = lambda w: pl.BlockSpec((1, tile, w), lambda b, i: (b, i, 0))
    return pl.pallas_call(
        _in_proj_kernel,
        grid=(bsz, nt),
        in_specs=[
            row_spec(d),
            pl.BlockSpec((1, SUBLANES, d), lambda b, i: (b, jnp.maximum(i * hb - 1, 0), 0)),
            pl.BlockSpec((1, SUBLANES, d), lambda b, i: (b, jnp.minimum((i + 1) * hb, nt * hb - 1), 0)),
            full(nw), full(wa), full(wb), full(wc), full(wg), full(wz), full(qkw), full(gpar), full(cw),
        ],
        out_specs=[row_spec(s.shape[-1]) for s in out_shapes],
        out_shape=out_shapes,
        scratch_shapes=[pltpu.VMEM((tile + 2 * SUBLANES, 3 * D_C), F32)],
        compiler_params=pltpu.CompilerParams(
            dimension_semantics=("parallel", "arbitrary"), vmem_limit_bytes=VMEM_LIMIT),
        name="in_proj",
    )(x, x, x, nw, wa, wb, wc, wg, wz, qkw, gpar, cw)


def _t5_bucket(rel):
    half = NUM_BUCKETS // 2
    max_exact = half // 2
    n = np.abs(rel)
    large = max_exact + (np.log(np.maximum(n, 1) / max_exact) / np.log(REL_MAX_DIST / max_exact)
                         * (half - max_exact)).astype(np.int32)
    large = np.minimum(large, half - 1)
    return (rel > 0).astype(np.int32) * half + np.where(n < max_exact, n, large)


def _attn_bias_tiles(rel_bias):
    period = 640
    wide = K_WIN + 2 * N_SIDE
    lead = Q_BLK - 1 + N_SIDE
    cfg_tiles = []
    for (_, dil) in DILATED_CFGS:
        offs = dil * np.arange(-N_SIDE, N_SIDE + 1)
        bias = rel_bias[_t5_bucket(offs)].T.astype(F32) * LOG2E
        heads = bias.shape[0]
        vec = jnp.concatenate([jnp.full((heads, lead + N_SIDE), NEG, F32), bias,
                               jnp.full((heads, period - lead - 3 * N_SIDE - 1), NEG, F32)], axis=1)
        rows = jnp.tile(vec, (1, Q_BLK))[:, :Q_BLK * (period - 1)].reshape(heads, Q_BLK, period - 1)
        ext = rows[:, :, lead:lead + wide]
        variants = [ext[:, :, 2 * N_SIDE:2 * N_SIDE + K_WIN], ext[:, :, N_SIDE:N_SIDE + K_WIN],
                    ext[:, :, 0:K_WIN]]
        cfg_tiles.append(jnp.stack(variants, axis=1))
    return jnp.stack(cfg_tiles, axis=0)


def _attn_kernel(q_ref, k_ref, v_ref, bias_ref, o_ref, acc_ref, m_ref, l_ref, kw_ref, vw_ref):
    sb = pl.program_id(2)
    seq = k_ref.shape[1]
    qsup = q_ref.shape[1]
    nblk = qsup // Q_BLK
    head0 = _iota2((Q_BLK, LANES), 1) < HEAD_DIM
    sel0 = (_iota2((Q_BLK, 2 * LANES), 1) & HEAD_DIM) == 0

    wide = DILATED_CFGS[-1][1]

    @pl.when(sb == 0)
    def _():
        def regroup(c, carry):
            r = c % wide
            i = c // wide
            src = pl.ds(r + wide * (i * K_WIN), K_WIN, stride=wide)
            dst = pl.ds(pl.multiple_of(r * (seq // wide) + i * K_WIN, K_WIN), K_WIN)
            kw_ref[dst, :] = k_ref[0, src, :].astype(BF16)
            vw_ref[dst, :] = v_ref[0, src, :].astype(BF16)
            return carry

        lax.fori_loop(0, seq // K_WIN, regroup, 0)

    groups = [(pos, ci, g) for pos, ci in enumerate(reversed(range(len(DILATED_CFGS))))
              for g in range(nblk // ATTN_ILP)]

    def scores(pos, ci, g):
        dil = DILATED_CFGS[ci][1]
        n_idx = seq // dil
        blocks = []
        for u in range(ATTN_ILP):
            t = g * ATTN_ILP + u
            r, j = t % dil, t // dil
            qs = r + dil * (j * Q_BLK)
            i0 = sb * (qsup // dil) + j * Q_BLK
            ws = jnp.clip(i0 - N_SIDE, 0, n_idx - K_WIN)
            var = jnp.where(i0 == 0, 0, jnp.where(i0 == n_idx - Q_BLK, 2, 1))
            qrows = pl.ds(qs, Q_BLK) if dil == 1 else pl.ds(qs, Q_BLK, stride=dil)
            if dil == wide:
                krows = pl.ds(pl.multiple_of(r * n_idx + ws, N_SIDE), K_WIN)
                k2 = kw_ref[krows, :]
                v2 = vw_ref[krows, :]
            else:
                ks = r + dil * ws
                krows = pl.ds(ks, K_WIN) if dil == 1 else pl.ds(ks, K_WIN, stride=dil)
                k2 = k_ref[0, krows, :].astype(BF16)
                v2 = v_ref[0, krows, :].astype(BF16)
            q2 = q_ref[0, qrows, :].astype(BF16)
            blocks.append(dict(qrows=qrows, var=var, k2=k2,
                               qh=[jnp.where(head0 if h == 0 else ~head0, q2, jnp.zeros_like(q2)) for h in range(2)],
                               vo=jnp.concatenate([v2, jnp.ones_like(v2)], axis=1)))
        chains = [(blk, h) for blk in blocks for h in range(2)]
        s = [_dot_nt(blk["qh"][h], blk["k2"]) + bias_ref[ci, h, blk["var"]] for blk, h in chains]
        mx = [jnp.max(sh, axis=-1, keepdims=True) for sh in s]
        p = [jnp.exp2(sh - m).astype(BF16) for sh, m in zip(s, mx)]
        return dict(pos=pos, blocks=blocks, chains=chains, mx=mx, p=p)

    def accumulate(st):
        pv = [_dot(ph, blk["vo"]) for ph, (blk, _) in zip(st["p"], st["chains"])]
        for u, blk in enumerate(st["blocks"]):
            qrows = blk["qrows"]
            tot = jnp.where(sel0, pv[2 * u], pv[2 * u + 1])
            num2 = tot[:, :LANES]
            l2 = tot[:, LANES:]
            m2 = jnp.where(head0, st["mx"][2 * u], st["mx"][2 * u + 1])
            if st["pos"] == 0:
                acc_ref[qrows, :] = num2
                m_ref[qrows, :] = m2
                l_ref[qrows, :] = l2
            else:
                mo = m_ref[qrows, :]
                mn = jnp.maximum(mo, m2)
                a = jnp.exp2(mo - mn)
                b = jnp.exp2(m2 - mn)
                acc_ref[qrows, :] = acc_ref[qrows, :] * a + num2 * b
                l_ref[qrows, :] = l_ref[qrows, :] * a + l2 * b
                m_ref[qrows, :] = mn

    pending = scores(*groups[0])
    for nxt in groups[1:]:
        upcoming = scores(*nxt)
        accumulate(pending)
        pending = upcoming
    accumulate(pending)

    o_ref[0] = (acc_ref[...] / l_ref[...]).astype(o_ref.dtype)


def _attention(aqkv, bias_tiles):
    bsz, seq, _ = aqkv.shape
    npair = D_A // LANES
    qsup = Q_SUPER
    assert seq % qsup == 0 and seq // DILATED_CFGS[-1][1] >= K_WIN
    return pl.pallas_call(
        _attn_kernel,
        grid=(bsz, npair, seq // qsup),
        in_specs=[
            pl.BlockSpec((1, qsup, LANES), lambda b, p, s: (b, s, p)),
            pl.BlockSpec((1, seq, LANES), lambda b, p, s: (b, 0, npair + p)),
            pl.BlockSpec((1, seq, LANES), lambda b, p, s: (b, 0, 2 * npair + p)),
            pl.BlockSpec((len(DILATED_CFGS), 2, 3, Q_BLK, K_WIN), lambda b, p, s: (0, p, 0, 0, 0)),
        ],
        out_specs=pl.BlockSpec((1, qsup, LANES), lambda b, p, s: (b, s, p)),
        out_shape=jax.ShapeDtypeStruct((bsz, seq, D_A), F32),
        scratch_shapes=[pltpu.VMEM((qsup, LANES), F32)] * 3 + [pltpu.VMEM((seq, LANES), BF16)] * 2,
        compiler_params=pltpu.CompilerParams(
            dimension_semantics=("parallel", "parallel", "arbitrary"), vmem_limit_bytes=VMEM_LIMIT),
        name="dilated_attn",
    )(aqkv, aqkv, aqkv, bias_tiles)


def _direction_masks(backward):
    row = _iota2((CHUNK, CHUNK), 0)
    col = _iota2((CHUNK, CHUNK), 1)
    if backward:
        return row <= col, row < col, row ^ col
    return row >= col, row > col, row ^ col


def _gate_cumsums(g, incl, incl_other):
    tri = incl.astype(BF16)
    tri_t = incl_other.astype(BF16)
    ghi, glo = _split(g)
    gb = _dot(tri, ghi) + _dot(tri, glo)
    gt = g.T
    thi, tlo = _split(gt)
    gbt = _dot(thi, tri_t) + _dot(tlo, tri_t)
    return gb, gt, gbt


def _gate_lane(mixer, pair, kind, backward, head):
    return 16 * mixer + 8 * pair + 4 * kind + 2 * int(backward) + head


def _lane_expander(lanes):
    n = len(lanes)
    row = _iota2((LANES, n * LANES), 0)
    blk = jnp.right_shift(_iota2((LANES, n * LANES), 1), LANES.bit_length() - 1)
    want = sum(jnp.where(blk == j, lane, 0) for j, lane in enumerate(lanes))
    return (row == want).astype(BF16)


def _expand_lanes(x, expander):
    hi, lo = _split(x)
    return _dot(hi, expander) + _dot(lo, expander)


def _mlstm_kernel(qf, kf, vf, gf, qb, kb, vb, gb, hf_ref, hb_ref, s_ref, m_ref):
    @pl.when(pl.program_id(1) == 0)
    def _():
        s_ref[...] = jnp.zeros_like(s_ref)
        m_ref[...] = jnp.zeros_like(m_ref)

    npair = D_B // LANES
    head0 = _iota2((CHUNK, LANES), 1) < HEAD_DIM
    sel0 = (_iota2((CHUNK, 2 * LANES), 1) & HEAD_DIM) == 0
    bd = _pair_block_diag(LANES, 2 * LANES)
    fwd_masks = _direction_masks(False)
    bwd_masks = _direction_masks(True)
    wide = lambda x: jnp.concatenate([x, x], axis=1)

    groups = []
    for backward, (q_ref, k_ref, v_ref, g_ref), h_ref in ((False, (qf, kf, vf, gf), hf_ref),
                                                          (True, (qb, kb, vb, gb), hb_ref)):
        incl = (bwd_masks if backward else fwd_masks)[0]
        incl_other = (fwd_masks if backward else bwd_masks)[0]
        last = 0 if backward else CHUNK - 1
        g = g_ref[0]
        gbc, gt, gbt = _gate_cumsums(g, incl, incl_other)
        i_lanes = [_gate_lane(0, p, 0, backward, h) for p in range(npair) for h in range(2)]
        f_lanes = [_gate_lane(0, p, 1, backward, h) for p in range(npair) for h in range(2)]
        li_all = _expand_lanes(g, _lane_expander(i_lanes))
        bc_all = _expand_lanes(gbc, _lane_expander(f_lanes))
        for p in range(npair):
            idx = 2 * int(backward) + p
            heads = []
            for h in range(2):
                j = 2 * p + h
                li = li_all[:, j * LANES:(j + 1) * LANES]
                bc = bc_all[:, j * LANES:(j + 1) * LANES]
                rrow = gt[i_lanes[j]:i_lanes[j] + 1, :] - gbt[f_lanes[j]:f_lanes[j] + 1, :]
                mst = m_ref[idx, h:h + 1, :]
                dmat = jnp.where(incl, wide(bc) + rrow, -jnp.inf)
                inter = bc + mst
                mt = jnp.maximum(jnp.max(dmat, axis=-1, keepdims=True), inter)
                blast = bc[last:last + 1, :]
                wlog = blast - bc + li
                mn = jnp.maximum(blast + mst, jnp.max(wlog, axis=0, keepdims=True))
                heads.append(dict(decay=jnp.exp(dmat - wide(mt)), iw=jnp.exp(inter - mt), emt=jnp.exp(-mt),
                                  ws=jnp.exp(wlog - mn), dec=jnp.exp(blast + mst - mn), mnew=mn))
            cols = slice(p * LANES, (p + 1) * LANES)
            v2 = v_ref[0, :, cols]
            groups.append(dict(q2=q_ref[0, :, cols], k2=k_ref[0, :, cols], heads=heads, state=s_ref[idx], idx=idx,
                               h_ref=h_ref, cols=cols,
                               vo=jnp.concatenate([v2, jnp.ones_like(v2)], axis=1)))

    chains = [(d, h) for d in groups for h in range(2)]
    q_state = [_dot(d["q2"], d["state"].astype(BF16)) for d in groups]
    qk = [_dot_nt(jnp.where(head0 if h == 0 else ~head0, d["q2"], jnp.zeros_like(d["q2"])), d["k2"])
          for d, h in chains]
    sc = [(qkh * d["heads"][h]["decay"]).astype(BF16) for (d, h), qkh in zip(chains, qk)]
    pv = [_dot(sch, d["vo"]) for (d, _), sch in zip(chains, sc)]
    upd = [_dot_tn(d["k2"], (jnp.where(sel0, wide(d["heads"][0]["ws"]), wide(d["heads"][1]["ws"]))
                             * d["vo"].astype(F32)).astype(BF16)) for d in groups]
    for i, d in enumerate(groups):
        h0, h1 = d["heads"]
        tot = jnp.where(sel0, wide(h0["iw"]), wide(h1["iw"])) * q_state[i] + jnp.where(sel0, pv[2 * i], pv[2 * i + 1])
        den = jnp.maximum(jnp.abs(tot[:, LANES:]), jnp.where(head0, h0["emt"], h1["emt"]))
        d["h_ref"][0, :, d["cols"]] = tot[:, :LANES] / den
        dec2 = jnp.where(sel0[0:1, :], wide(h0["dec"]), wide(h1["dec"]))
        s_ref[d["idx"]] = dec2 * d["state"] + jnp.where(bd, upd[i], 0.0)
        m_ref[d["idx"], 0:1, :] = h0["mnew"]
        m_ref[d["idx"], 1:2, :] = h1["mnew"]


def _bidir_specs(nc, width):
    specs = []
    for rev in (False, True):
        cidx = (lambda c: nc - 1 - c) if rev else (lambda c: c)
        for part in range(3):
            specs.append(pl.BlockSpec((1, CHUNK, width), lambda b, c, part=part, cidx=cidx: (b, cidx(c), part)))
        specs.append(pl.BlockSpec((1, CHUNK, LANES), lambda b, c, cidx=cidx: (b, cidx(c), 0)))
    return specs


def _mlstm(bqkv, gates):
    bsz, seq, _ = bqkv.shape
    nc = seq // CHUNK
    nstate = 2 * (D_B // LANES)
    out = jax.ShapeDtypeStruct((bsz, seq, D_B), F32)
    return pl.pallas_call(
        _mlstm_kernel,
        grid=(bsz, nc),
        in_specs=_bidir_specs(nc, D_B),
        out_specs=[pl.BlockSpec((1, CHUNK, D_B), lambda b, c: (b, c, 0)),
                   pl.BlockSpec((1, CHUNK, D_B), lambda b, c: (b, nc - 1 - c, 0))],
        out_shape=(out, out),
        scratch_shapes=[pltpu.VMEM((nstate, LANES, 2 * LANES), F32), pltpu.VMEM((nstate, SUBLANES, LANES), F32)],
        compiler_params=pltpu.CompilerParams(
            dimension_semantics=("parallel", "arbitrary"), vmem_limit_bytes=VMEM_LIMIT),
        name="mlstm",
    )(bqkv, bqkv, bqkv, gates, bqkv, bqkv, bqkv, gates)


_INV_BASE = 32


def _unit_tri_inverses(ms, xor):
    eye = (xor == 0).astype(F32)
    base = xor < _INV_BASE
    mbs = [jnp.where(base, m, 0.0) for m in ms]
    ts = [eye - mb for mb in mbs]
    pbs = [mb.astype(BF16) for mb in mbs]
    k = 1
    while True:
        pbs = [_dot(pb, pb).astype(BF16) for pb in pbs]
        k *= 2
        ts = [t + _dot(t.astype(BF16), pb) for t, pb in zip(ts, pbs)]
        if 2 * k >= _INV_BASE:
            break
    b = _INV_BASE
    while b < CHUNK:
        level = (xor >= b) & (xor < 2 * b)
        tbs = [t.astype(BF16) for t in ts]
        xs = [_dot(jnp.where(level, m, 0.0).astype(BF16), tb).astype(BF16) for m, tb in zip(ms, tbs)]
        ts = [t - _dot(tb, x) for t, tb, x in zip(ts, tbs, xs)]
        b *= 2
    return ts


def _gdn_kernel(qf, kf, vf, gf, qb, kb, vb, gb, of_ref, ob_ref, s_ref):
    @pl.when(pl.program_id(1) == 0)
    def _():
        s_ref[...] = jnp.zeros_like(s_ref)

    npair = D_C // LANES
    head0 = _iota2((CHUNK, LANES), 1) < HEAD_DIM
    sel0 = (_iota2((CHUNK, 2 * LANES), 1) & HEAD_DIM) == 0
    bd = _pair_block_diag(LANES, LANES)
    fwd_masks = _direction_masks(False)
    bwd_masks = _direction_masks(True)
    xor = fwd_masks[2]

    dirs = []
    for backward, (q_ref, k_ref, v_ref, g_ref), o_ref in ((False, (qf, kf, vf, gf), of_ref),
                                                          (True, (qb, kb, vb, gb), ob_ref)):
        incl, strict, _ = bwd_masks if backward else fwd_masks
        incl_other = (fwd_masks if backward else bwd_masks)[0]
        last = 0 if backward else CHUNK - 1
        g = g_ref[0]
        gbc, _, gbt = _gate_cumsums(g, incl, incl_other)
        for p in range(npair):
            cols = slice(p * LANES, (p + 1) * LANES)
            q2, k2, v2 = q_ref[0, :, cols], k_ref[0, :, cols], v_ref[0, :, cols]
            a_lane = [_gate_lane(1, p, 0, backward, h) for h in range(2)]
            b_lane = [_gate_lane(1, p, 1, backward, h) for h in range(2)]
            gam = [gbc[:, a:a + 1] for a in a_lane]
            beta = [g[:, b:b + 1] for b in b_lane]
            glast2 = jnp.where(head0[0:1, :], gbc[last:last + 1, a_lane[0]:a_lane[0] + 1],
                               gbc[last:last + 1, a_lane[1]:a_lane[1] + 1])
            gam2 = jnp.where(head0, gam[0], gam[1])
            beta2 = jnp.where(head0, beta[0], beta[1])
            egam2 = jnp.exp(gam2)
            kf32 = k2.astype(F32)
            rhs = jnp.concatenate([v2.astype(F32) * beta2, kf32 * (beta2 * egam2)], axis=1).astype(BF16)
            decay = [jnp.exp(jnp.where(incl, gam[h] - gbt[a_lane[h]:a_lane[h] + 1, :], -jnp.inf)) for h in range(2)]
            idx = 2 * int(backward) + p
            state = s_ref[idx]
            dirs.append(dict(q2=q2, k2=k2, kf32=kf32, rhs=rhs, decay=decay, beta=beta, strict=strict, gam2=gam2,
                             egam2=egam2, glast2=glast2, state=state, state_b=state.astype(BF16), o_ref=o_ref,
                             cols=cols, idx=idx))

    chains = [(d, h) for d in dirs for h in range(2)]

    def head_lanes(x, h):
        return jnp.where(head0 if h == 0 else ~head0, x, 0).astype(BF16)

    kk = [_dot_nt(head_lanes(d["k2"], h), d["k2"]) for d, h in chains]
    ms = [jnp.where(d["strict"], d["beta"][h] * kkh * d["decay"][h], 0.0) for (d, h), kkh in zip(chains, kk)]
    qk = [_dot_nt(head_lanes(d["q2"], h), d["k2"]) for d, h in chains]
    a_intra = [(qkh * d["decay"][h]).astype(BF16) for (d, h), qkh in zip(chains, qk)]
    t_inv = _unit_tri_inverses(ms, xor)
    uw = [_dot(t.astype(BF16), d["rhs"]) for (d, _), t in zip(chains, t_inv)]

    uw2 = [jnp.where(sel0, uw[2 * i], uw[2 * i + 1]) for i in range(len(dirs))]
    w_state = [_dot(uw2[i][:, LANES:].astype(BF16), d["state_b"]) for i, d in enumerate(dirs)]
    v_new = [(uw2[i][:, :LANES] - w_state[i]).astype(BF16) for i in range(len(dirs))]
    q_state = [_dot((d["q2"].astype(F32) * d["egam2"]).astype(BF16), d["state_b"]) for d in dirs]
    av = [_dot(a_intra[j], v_new[j // 2]) for j in range(len(chains))]
    upd = [_dot_tn((d["kf32"] * jnp.exp(d["glast2"] - d["gam2"])).astype(BF16), v_new[i])
           for i, d in enumerate(dirs)]
    for i, d in enumerate(dirs):
        d["o_ref"][0, :, d["cols"]] = q_state[i] + jnp.where(head0, av[2 * i], av[2 * i + 1])
        s_ref[d["idx"]] = d["state"] * jnp.exp(d["glast2"]) + jnp.where(bd, upd[i], 0.0)


def _gdn(cqkv, gates):
    bsz, seq, _ = cqkv.shape
    nc = seq // CHUNK
    nstate = 2 * (D_C // LANES)
    out = jax.ShapeDtypeStruct((bsz, seq, D_C), F32)
    return pl.pallas_call(
        _gdn_kernel,
        grid=(bsz, nc),
        in_specs=_bidir_specs(nc, D_C),
        out_specs=[pl.BlockSpec((1, CHUNK, D_C), lambda b, c: (b, c, 0)),
                   pl.BlockSpec((1, CHUNK, D_C), lambda b, c: (b, nc - 1 - c, 0))],
        out_shape=(out, out),
        scratch_shapes=[pltpu.VMEM((nstate, LANES, LANES), F32)],
        compiler_params=pltpu.CompilerParams(
            dimension_semantics=("parallel", "arbitrary"), vmem_limit_bytes=VMEM_LIMIT),
        name="gdn",
    )(cqkv, cqkv, cqkv, gates, cqkv, cqkv, cqkv, gates)


def _out_kernel(x_ref, oa_ref, hf_ref, hb_ref, cf_ref, cb_ref, z_ref, mw_ref, gw_ref, wo_ref, y_ref):
    bd = _head_block_diag(MXU_WIDTH).astype(BF16)
    z = z_ref[0].astype(F32)

    def silu(t):
        return t * _sigmoid(t)

    hsum = _sigmoid(z[:, D_A:D_A + D_B]) * (hf_ref[0] + hb_ref[0])
    csum = cf_ref[0] + cb_ref[0]
    ssb = _head_sum(hsum * hsum, bd)
    ssc = _head_sum(csum * csum, bd)
    ya = oa_ref[0].astype(F32) * silu(z[:, 0:D_A])
    yb = hsum * lax.rsqrt(ssb * (1.0 / HEAD_DIM) + EPS) * mw_ref[...] * silu(z[:, D_A + D_B:D_A + 2 * D_B])
    yc = csum * lax.rsqrt(ssc * (1.0 / HEAD_DIM) + EPS) * gw_ref[...] * silu(z[:, D_A + 2 * D_B:])
    y = jnp.concatenate([ya, yb, yc], axis=1).astype(BF16)
    y_ref[0] = x_ref[0] + _dot(y, wo_ref[...])


def _out_proj(x, oa, hf, hb, cf, cb, z, mw, gw, wo):
    bsz, seq, d = x.shape
    tile = OUT_TILE
    row_spec = lambda a: pl.BlockSpec((1, tile, a.shape[-1]), lambda b, i: (b, i, 0))
    full = lambda a: pl.BlockSpec(a.shape, lambda b, i: (0,) * a.ndim)
    return pl.pallas_call(
        _out_kernel,
        grid=(bsz, seq // tile),
        in_specs=[row_spec(a) for a in (x, oa, hf, hb, cf, cb, z)] + [full(mw), full(gw), full(wo)],
        out_specs=pl.BlockSpec((1, tile, d), lambda b, i: (b, i, 0)),
        out_shape=jax.ShapeDtypeStruct(x.shape, x.dtype),
        compiler_params=pltpu.CompilerParams(
            dimension_semantics=("parallel", "parallel"), vmem_limit_bytes=VMEM_LIMIT),
        name="out_proj",
    )(x, oa, hf, hb, cf, cb, z, mw, gw, wo)


def _gate_layout():
    return [(_gate_lane(mixer, pair, kind, direction, hh), mixer, kind, direction, 2 * pair + hh)
            for mixer in range(2) for pair in range(2) for kind in range(2) for direction in range(2)
            for hh in range(2)]


def _gate_weights(w_in):
    nh = D_B // HEAD_DIM
    cols = np.full((LANES,), -1, np.int64)
    for lane, mixer, kind, direction, head in _gate_layout():
        cols[lane] = (_OFF_BIF, _OFF_CAB)[mixer] + (2 * kind + direction) * nh + head
    return jnp.where((cols >= 0)[None, :], w_in[:, np.maximum(cols, 0)], 0.0)


def _gate_params(i_bias, f_bias, a_log, dt_bias):
    nh = i_bias.shape[1]
    flat = jnp.concatenate([i_bias.reshape(-1), f_bias.reshape(-1), dt_bias.reshape(-1), a_log.reshape(-1),
                            jnp.zeros((1,), F32)])
    idx = np.full((SUBLANES, LANES), flat.shape[0] - 1, np.int32)
    for lane, mixer, kind, direction, head in _gate_layout():
        src = direction * nh + head
        if mixer == 0:
            idx[0, lane] = kind * 2 * nh + src
        elif kind == 0:
            idx[0, lane] = 2 * 2 * nh + src
            idx[1, lane] = 3 * 2 * nh + src
    return flat[idx]


def _layer(x, bias_tiles, norm_w, w_in, w_out, qk_norm_w, i_bias, f_bias, m_norm_w, conv_w, a_log, dt_bias,
           g_norm_w):
    wa = w_in[:, _OFF_AQ:_OFF_AZ].astype(BF16)
    wb = w_in[:, _OFF_BQ:_OFF_BIF].astype(BF16)
    wc = w_in[:, _OFF_CQ:_OFF_CAB].astype(BF16)
    wz = jnp.concatenate([w_in[:, _OFF_AZ:_OFF_BQ], w_in[:, _OFF_BO:_OFF_CQ], w_in[:, _OFF_CZ:]],
                         axis=1).astype(BF16)
    wg = _gate_weights(w_in).astype(BF16)
    qkw = jnp.pad(jnp.tile(qk_norm_w, (1, MXU_WIDTH // HEAD_DIM)) * jnp.array([[HEAD_DIM ** -0.5 * LOG2E], [1.0]], F32),
                  ((0, SUBLANES - 2), (0, 0)))
    gpar = _gate_params(i_bias, f_bias, a_log, dt_bias)
    cw = jnp.pad(conv_w, ((0, SUBLANES - CONV_W), (0, 0)))

    aqkv, bqkv, cqkv, gates, z = _in_proj(x, norm_w[None, :], wa, wb, wc, wg, wz, qkw, gpar, cw)
    oa = _attention(aqkv, bias_tiles)
    hf, hb = _mlstm(bqkv, gates)
    cf, cb = _gdn(cqkv, gates)
    mw = jnp.tile(m_norm_w, D_B // HEAD_DIM)[None, :]
    gw = jnp.tile(g_norm_w, D_C // HEAD_DIM)[None, :]
    return _out_proj(x, oa, hf, hb, cf, cb, z, mw, gw, w_out.astype(BF16))


def kernel(x, norm_w, w_in, w_out, qk_norm_w, rel_bias, mlstm_i_bias, mlstm_f_bias, mlstm_norm_w, gdn_conv_w,
           gdn_a_log, gdn_dt_bias, gdn_norm_w):
    bias_tiles = _attn_bias_tiles(rel_bias)
    for l in range(DEPTH):
        x = _layer(x, bias_tiles, norm_w[l], w_in[l], w_out[l], qk_norm_w[l], mlstm_i_bias[l], mlstm_f_bias[l],
                   mlstm_norm_w[l], gdn_conv_w[l], gdn_a_log[l], gdn_dt_bias[l], gdn_norm_w[l])
    return x
```

```python
import functools

import numpy as np
import jax
import jax.numpy as jnp
from jax import lax
from jax.experimental import pallas as pl
from jax.experimental.pallas import tpu as pltpu

F32 = jnp.float32
BF16 = jnp.bfloat16

D_MODEL = 1024
DEPTH = 2
EPS = 1e-6
HEAD_DIM = 64
LANES = 128
SUBLANES = 8
MXU_WIDTH = 256
D_A, D_B, D_C = 512, 256, 256
DILATED_CFGS = ((128, 1), (512, 4), (2048, 16))
N_SIDE = 64
NUM_BUCKETS = 32
REL_MAX_DIST = 1024
CONV_W = 5
CHUNK = 256
Q_SUPER = 2048
Q_BLK = 128
K_WIN = 256
ATTN_ILP = 2
IN_TILE = 512
OUT_TILE = 512
NEG = -1e30
LOG2E = 1.4426950408889634
VMEM_LIMIT = 56 * 1024 * 1024

_OFF_AQ, _OFF_AZ = 0, 1536
_OFF_BQ, _OFF_BIF, _OFF_BO, _OFF_BZ = 2048, 2816, 2832, 3088
_OFF_CQ, _OFF_CAB, _OFF_CZ = 3344, 4112, 4128


def _dot(a, b):
    return jnp.dot(a, b, preferred_element_type=F32)


def _dot_nt(a, b):
    return lax.dot_general(a, b, (((1,), (1,)), ((), ())), preferred_element_type=F32)


def _dot_tn(a, b):
    return lax.dot_general(a, b, (((0,), (0,)), ((), ())), preferred_element_type=F32)


def _split(a):
    hi = a.astype(BF16)
    lo = (a - hi.astype(F32)).astype(BF16)
    return hi, lo


def _iota2(shape, axis):
    return lax.broadcasted_iota(jnp.int32, shape, axis)


def _pair_block_diag(rows, cols):
    r = _iota2((rows, cols), 0)
    c = _iota2((rows, cols), 1)
    return (r < HEAD_DIM) == ((c & HEAD_DIM) == 0)


def _head_block_diag(n):
    return (_iota2((n, n), 0) ^ _iota2((n, n), 1)) < HEAD_DIM


def _head_sum(t, bd):
    return _dot(t.astype(BF16), bd)


def _softplus(y):
    return jnp.maximum(y, 0.0) + jnp.log1p(jnp.exp(-jnp.abs(y)))


def _sigmoid(y):
    return 1.0 / (1.0 + jnp.exp(-y))


def _in_proj_kernel(x_ref, xp_ref, xn_ref, nw_ref, wa_ref, wb_ref, wc_ref, wg_ref, wz_ref,
                    qkw_ref, gpar_ref, cw_ref,
                    a_ref, b_ref, c_ref, g_ref, z_ref, cext_ref):
    i = pl.program_id(1)
    n = pl.num_programs(1)
    tile = x_ref.shape[1]
    bd = _head_block_diag(MXU_WIDTH).astype(BF16)

    xe = jnp.concatenate([xp_ref[0], x_ref[0], xn_ref[0]], axis=0)
    ms = jnp.mean(xe * xe, axis=-1, keepdims=True)
    hne = xe * lax.rsqrt(ms + EPS) * nw_ref[...]
    hn = hne[SUBLANES:SUBLANES + tile].astype(BF16)
    hne = hne.astype(BF16)

    ta = [_dot(hn, wa_ref[:, j * MXU_WIDTH:(j + 1) * MXU_WIDTH]) for j in range(3 * D_A // MXU_WIDTH)]
    tb = _dot(hn, wb_ref[...])
    p = _dot(hn, wg_ref[...]) + gpar_ref[0:1, :]
    ce = _dot(hne, wc_ref[...])
    z_ref[0] = _dot(hn, wz_ref[...]).astype(z_ref.dtype)

    row = _iota2((tile + 2 * SUBLANES, 1), 0)
    valid = ((row >= SUBLANES) | (i > 0)) & ((row < tile + SUBLANES) | (i < n - 1))
    cext_ref[...] = jnp.where(valid, ce, 0.0)
    conv = cw_ref[0:1, :] * cext_ref[pl.ds(SUBLANES - CONV_W // 2, tile), :]
    for t in range(1, CONV_W):
        conv = conv + cw_ref[t:t + 1, :] * cext_ref[pl.ds(SUBLANES - CONV_W // 2 + t, tile), :]
    s = conv * _sigmoid(conv)
    tc = [s[:, j * MXU_WIDTH:(j + 1) * MXU_WIDTH] for j in range(3 * D_C // MXU_WIDTH)]

    ssa = [_head_sum(t * t, bd) for t in ta[:4]]
    ssc = [_head_sum(t * t, bd) for t in tc[:2]]

    for j, t in enumerate(ta):
        if j < 4:
            t = t * lax.rsqrt(ssa[j] * (1.0 / HEAD_DIM) + EPS) * qkw_ref[j // 2:j // 2 + 1, :]
        a_ref[0, :, j * MXU_WIDTH:(j + 1) * MXU_WIDTH] = t

    b_ref[0, :, 0:D_B] = (tb[:, 0:D_B] * HEAD_DIM ** -0.5).astype(b_ref.dtype)
    b_ref[0, :, D_B:3 * D_B] = tb[:, D_B:3 * D_B].astype(b_ref.dtype)

    lane = _iota2(p.shape, 1)
    first4 = (lane & 7) < 4
    is_b = lane < 16
    val_b = jnp.where(first4, p, -_softplus(-p)) * LOG2E
    val_c = jnp.where(first4, -jnp.exp(gpar_ref[1:2, :]) * _softplus(p) * LOG2E, _sigmoid(p))
    g_ref[0] = jnp.where(is_b, val_b, val_c)

    c_ref[0, :, 0:D_C] = (tc[0] * lax.rsqrt(ssc[0] + EPS) * HEAD_DIM ** -0.5).astype(c_ref.dtype)
    c_ref[0, :, D_C:2 * D_C] = (tc[1] * lax.rsqrt(ssc[1] + EPS)).astype(c_ref.dtype)
    c_ref[0, :, 2 * D_C:3 * D_C] = tc[2].astype(c_ref.dtype)


def _in_proj(x, nw, wa, wb, wc, wg, wz, qkw, gpar, cw):
    bsz, seq, d = x.shape
    tile = IN_TILE
    nt = seq // tile
    hb = tile // SUBLANES
    full = lambda a: pl.BlockSpec(a.shape, lambda b, i: (0,) * a.ndim)
    out_shapes = (
        jax.ShapeDtypeStruct((bsz, seq, 3 * D_A), F32),
        jax.ShapeDtypeStruct((bsz, seq, 3 * D_B), BF16),
        jax.ShapeDtypeStruct((bsz, seq, 3 * D_C), BF16),
        jax.ShapeDtypeStruct((bsz, seq, LANES), F32),
        jax.ShapeDtypeStruct((bsz, seq, D_A + 3 * D_B), BF16),
    )
    row_spec = lambda w: pl.BlockSpec((1, tile, w), lambda b, i: (b, i, 0))
    return pl.pallas_call(
        _in_proj_kernel,
        grid=(bsz, nt),
        in_specs=[
            row_spec(d),
            pl.BlockSpec((1, SUBLANES, d), lambda b, i: (b, jnp.maximum(i * hb - 1, 0), 0)),
            pl.BlockSpec((1, SUBLANES, d), lambda b, i: (b, jnp.minimum((i + 1) * hb, nt * hb - 1), 0)),
            full(nw), full(wa), full(wb), full(wc), full(wg), full(wz), full(qkw), full(gpar), full(cw),
        ],
        out_specs=[row_spec(s.shape[-1]) for s in out_shapes],
        out_shape=out_shapes,
        scratch_shapes=[pltpu.VMEM((tile + 2 * SUBLANES, 3 * D_C), F32)],
        compiler_params=pltpu.CompilerParams(
            dimension_semantics=("parallel", "arbitrary"), vmem_limit_bytes=VMEM_LIMIT),
        name="in_proj",
    )(x, x, x, nw, wa, wb, wc, wg, wz, qkw, gpar, cw)


def _t5_bucket(rel):
    half = NUM_BUCKETS // 2
    max_exact = half // 2
    n = np.abs(rel)
    large = max_exact + (np.log(np.maximum(n, 1) / max_exact) / np.log(REL_MAX_DIST / max_exact)
                         * (half - max_exact)).astype(np.int32)
    large = np.minimum(large, half - 1)
    return (rel > 0).astype(np.int32) * half + np.where(n < max_exact, n, large)


def _attn_bias_tiles(rel_bias):
    period = 640
    wide = K_WIN + 2 * N_SIDE
    lead = Q_BLK - 1 + N_SIDE
    cfg_tiles = []
    for (_, dil) in DILATED_CFGS:
        offs = dil * np.arange(-N_SIDE, N_SIDE + 1)
        bias = rel_bias[_t5_bucket(offs)].T.astype(F32) * LOG2E
        heads = bias.shape[0]
        vec = jnp.concatenate([jnp.full((heads, lead + N_SIDE), NEG, F32), bias,
                               jnp.full((heads, period - lead - 3 * N_SIDE - 1), NEG, F32)], axis=1)
        rows = jnp.tile(vec, (1, Q_BLK))[:, :Q_BLK * (period - 1)].reshape(heads, Q_BLK, period - 1)
        ext = rows[:, :, lead:lead + wide]
        variants = [ext[:, :, 2 * N_SIDE:2 * N_SIDE + K_WIN], ext[:, :, N_SIDE:N_SIDE + K_WIN],
                    ext[:, :, 0:K_WIN]]
        cfg_tiles.append(jnp.stack(variants, axis=1))
    return jnp.stack(cfg_tiles, axis=0)


def _attn_kernel(q_ref, k_ref, v_ref, bias_ref, o_ref, acc_ref, m_ref, l_ref, kw_ref, vw_ref):
    sb = pl.program_id(2)
    seq = k_ref.shape[1]
    qsup = q_ref.shape[1]
    nblk = qsup // Q_BLK
    head0 = _iota2((Q_BLK, LANES), 1) < HEAD_DIM
    sel0 = (_iota2((Q_BLK, 2 * LANES), 1) & HEAD_DIM) == 0

    wide = DILATED_CFGS[-1][1]

    @pl.when(sb == 0)
    def _():
        def regroup(c, carry):
            r = c % wide
            i = c // wide
            src = pl.ds(r + wide * (i * K_WIN), K_WIN, stride=wide)
            dst = pl.ds(pl.multiple_of(r * (seq // wide) + i * K_WIN, K_WIN), K_WIN)
            kw_ref[dst, :] = k_ref[0, src, :].astype(BF16)
            vw_ref[dst, :] = v_ref[0, src, :].astype(BF16)
            return carry

        lax.fori_loop(0, seq // K_WIN, regroup, 0)

    groups = [(pos, ci, g) for pos, ci in enumerate(reversed(range(len(DILATED_CFGS))))
              for g in range(nblk // ATTN_ILP)]

    def scores(pos, ci, g):
        dil = DILATED_CFGS[ci][1]
        n_idx = seq // dil
        blocks = []
        for u in range(ATTN_ILP):
            t = g * ATTN_ILP + u
            r, j = t % dil, t // dil
            qs = r + dil * (j * Q_BLK)
            i0 = sb * (qsup // dil) + j * Q_BLK
            ws = jnp.clip(i0 - N_SIDE, 0, n_idx - K_WIN)
            var = jnp.where(i0 == 0, 0, jnp.where(i0 == n_idx - Q_BLK, 2, 1))
            qrows = pl.ds(qs, Q_BLK) if dil == 1 else pl.ds(qs, Q_BLK, stride=dil)
            if dil == wide:
                krows = pl.ds(pl.multiple_of(r * n_idx + ws, N_SIDE), K_WIN)
                k2 = kw_ref[krows, :]
                v2 = vw_ref[krows, :]
            else:
                ks = r + dil * ws
                krows = pl.ds(ks, K_WIN) if dil == 1 else pl.ds(ks, K_WIN, stride=dil)
                k2 = k_ref[0, krows, :].astype(BF16)
                v2 = v_ref[0, krows, :].astype(BF16)
            q2 = q_ref[0, qrows, :].astype(BF16)
            blocks.append(dict(qrows=qrows, var=var, k2=k2,
                               qh=[jnp.where(head0 if h == 0 else ~head0, q2, jnp.zeros_like(q2)) for h in range(2)],
                               vo=jnp.concatenate([v2, jnp.ones_like(v2)], axis=1)))
        chains = [(blk, h) for blk in blocks for h in range(2)]
        s = [_dot_nt(blk["qh"][h], blk["k2"]) + bias_ref[ci, h, blk["var"]] for blk, h in chains]
        mx = [jnp.max(sh, axis=-1, keepdims=True) for sh in s]
        p = [jnp.exp2(sh - m).astype(BF16) for sh, m in zip(s, mx)]
        return dict(pos=pos, blocks=blocks, chains=chains, mx=mx, p=p)

    def accumulate(st):
        pv = [_dot(ph, blk["vo"]) for ph, (blk, _) in zip(st["p"], st["chains"])]
        for u, blk in enumerate(st["blocks"]):
            qrows = blk["qrows"]
            tot = jnp.where(sel0, pv[2 * u], pv[2 * u + 1])
            num2 = tot[:, :LANES]
            l2 = tot[:, LANES:]
            m2 = jnp.where(head0, st["mx"][2 * u], st["mx"][2 * u + 1])
            if st["pos"] == 0:
                acc_ref[qrows, :] = num2
                m_ref[qrows, :] = m2
                l_ref[qrows, :] = l2
            else:
                mo = m_ref[qrows, :]
                mn = jnp.maximum(mo, m2)
                a = jnp.exp2(mo - mn)
                b = jnp.exp2(m2 - mn)
                acc_ref[qrows, :] = acc_ref[qrows, :] * a + num2 * b
                l_ref[qrows, :] = l_ref[qrows, :] * a + l2 * b
                m_ref[qrows, :] = mn

    pending = scores(*groups[0])
    for nxt in groups[1:]:
        upcoming = scores(*nxt)
        accumulate(pending)
        pending = upcoming
    accumulate(pending)

    o_ref[0] = (acc_ref[...] / l_ref[...]).astype(o_ref.dtype)


def _attention(aqkv, bias_tiles):
    bsz, seq, _ = aqkv.shape
    npair = D_A // LANES
    qsup = Q_SUPER
    assert seq % qsup == 0 and seq // DILATED_CFGS[-1][1] >= K_WIN
    return pl.pallas_call(
        _attn_kernel,
        grid=(bsz, npair, seq // qsup),
        in_specs=[
            pl.BlockSpec((1, qsup, LANES), lambda b, p, s: (b, s, p)),
            pl.BlockSpec((1, seq, LANES), lambda b, p, s: (b, 0, npair + p)),
            pl.BlockSpec((1, seq, LANES), lambda b, p, s: (b, 0, 2 * npair + p)),
            pl.BlockSpec((len(DILATED_CFGS), 2, 3, Q_BLK, K_WIN), lambda b, p, s: (0, p, 0, 0, 0)),
        ],
        out_specs=pl.BlockSpec((1, qsup, LANES), lambda b, p, s: (b, s, p)),
        out_shape=jax.ShapeDtypeStruct((bsz, seq, D_A), BF16),
        scratch_shapes=[pltpu.VMEM((qsup, LANES), F32)] * 3 + [pltpu.VMEM((seq, LANES), BF16)] * 2,
        compiler_params=pltpu.CompilerParams(
            dimension_semantics=("parallel", "parallel", "arbitrary"), vmem_limit_bytes=VMEM_LIMIT),
        name="dilated_attn",
    )(aqkv, aqkv, aqkv, bias_tiles)


def _direction_masks(backward):
    row = _iota2((CHUNK, CHUNK), 0)
    col = _iota2((CHUNK, CHUNK), 1)
    if backward:
        return row <= col, row < col, row ^ col
    return row >= col, row > col, row ^ col


def _gate_cumsums(g, incl, incl_other):
    tri = incl.astype(BF16)
    tri_t = incl_other.astype(BF16)
    ghi, glo = _split(g)
    gb = _dot(tri, ghi) + _dot(tri, glo)
    gt = g.T
    thi, tlo = _split(gt)
    gbt = _dot(thi, tri_t) + _dot(tlo, tri_t)
    return gb, gt, gbt


def _gate_lane(mixer, pair, kind, backward, head):
    return 16 * mixer + 8 * pair + 4 * kind + 2 * int(backward) + head


def _lane_expander(lanes):
    n = len(lanes)
    row = _iota2((LANES, n * LANES), 0)
    blk = jnp.right_shift(_iota2((LANES, n * LANES), 1), LANES.bit_length() - 1)
    want = sum(jnp.where(blk == j, lane, 0) for j, lane in enumerate(lanes))
    return (row == want).astype(BF16)


def _expand_lanes(x, expander):
    hi, lo = _split(x)
    return _dot(hi, expander) + _dot(lo, expander)


def _mlstm_kernel(qf, kf, vf, gf, qb, kb, vb, gb, hf_ref, hb_ref, s_ref, m_ref):
    @pl.when(pl.program_id(1) == 0)
    def _():
        s_ref[...] = jnp.zeros_like(s_ref)
        m_ref[...] = jnp.zeros_like(m_ref)

    npair = D_B // LANES
    head0 = _iota2((CHUNK, LANES), 1) < HEAD_DIM
    sel0 = (_iota2((CHUNK, 2 * LANES), 1) & HEAD_DIM) == 0
    bd = _pair_block_diag(LANES, 2 * LANES)
    fwd_masks = _direction_masks(False)
    bwd_masks = _direction_masks(True)
    wide = lambda x: jnp.concatenate([x, x], axis=1)

    groups = []
    for backward, (q_ref, k_ref, v_ref, g_ref), h_ref in ((False, (qf, kf, vf, gf), hf_ref),
                                                          (True, (qb, kb, vb, gb), hb_ref)):
        incl = (bwd_masks if backward else fwd_masks)[0]
        incl_other = (fwd_masks if backward else bwd_masks)[0]
        last = 0 if backward else CHUNK - 1
        g = g_ref[0]
        gbc, gt, gbt = _gate_cumsums(g, incl, incl_other)
        i_lanes = [_gate_lane(0, p, 0, backward, h) for p in range(npair) for h in range(2)]
        f_lanes = [_gate_lane(0, p, 1, backward, h) for p in range(npair) for h in range(2)]
        li_all = _expand_lanes(g, _lane_expander(i_lanes))
        bc_all = _expand_lanes(gbc, _lane_expander(f_lanes))
        for p in range(npair):
            idx = 2 * int(backward) + p
            heads = []
            for h in range(2):
                j = 2 * p + h
                li = li_all[:, j * LANES:(j + 1) * LANES]
                bc = bc_all[:, j * LANES:(j + 1) * LANES]
                rrow = gt[i_lanes[j]:i_lanes[j] + 1, :] - gbt[f_lanes[j]:f_lanes[j] + 1, :]
                mst = m_ref[idx, h:h + 1, :]
                dmat = jnp.where(incl, wide(bc) + rrow, -jnp.inf)
                inter = bc + mst
                mt = jnp.maximum(jnp.max(dmat, axis=-1, keepdims=True), inter)
                blast = bc[last:last + 1, :]
                wlog = blast - bc + li
                mn = jnp.maximum(blast + mst, jnp.max(wlog, axis=0, keepdims=True))
                heads.append(dict(decay=jnp.exp2(dmat - wide(mt)), iw=jnp.exp2(inter - mt), emt=jnp.exp2(-mt),
                                  ws=jnp.exp2(wlog - mn), dec=jnp.exp2(blast + mst - mn), mnew=mn))
            cols = slice(p * LANES, (p + 1) * LANES)
            v2 = v_ref[0, :, cols]
            groups.append(dict(q2=q_ref[0, :, cols], k2=k_ref[0, :, cols], heads=heads, state=s_ref[idx], idx=idx,
                               h_ref=h_ref, cols=cols,
                               vo=jnp.concatenate([v2, jnp.ones_like(v2)], axis=1)))

    chains = [(d, h) for d in groups for h in range(2)]
    q_state = [_dot(d["q2"], d["state"].astype(BF16)) for d in groups]
    qk = [_dot_nt(jnp.where(head0 if h == 0 else ~head0, d["q2"], jnp.zeros_like(d["q2"])), d["k2"])
          for d, h in chains]
    sc = [(qkh * d["heads"][h]["decay"]).astype(BF16) for (d, h), qkh in zip(chains, qk)]
    pv = [_dot(sch, d["vo"]) for (d, _), sch in zip(chains, sc)]
    upd = [_dot_tn(d["k2"], (jnp.where(sel0, wide(d["heads"][0]["ws"]), wide(d["heads"][1]["ws"]))
                             * d["vo"].astype(F32)).astype(BF16)) for d in groups]
    for i, d in enumerate(groups):
        h0, h1 = d["heads"]
        tot = jnp.where(sel0, wide(h0["iw"]), wide(h1["iw"])) * q_state[i] + jnp.where(sel0, pv[2 * i], pv[2 * i + 1])
        den = jnp.maximum(jnp.abs(tot[:, LANES:]), jnp.where(head0, h0["emt"], h1["emt"]))
        d["h_ref"][0, :, d["cols"]] = (tot[:, :LANES] / den).astype(d["h_ref"].dtype)
        dec2 = jnp.where(sel0[0:1, :], wide(h0["dec"]), wide(h1["dec"]))
        s_ref[d["idx"]] = dec2 * d["state"] + jnp.where(bd, upd[i], 0.0)
        m_ref[d["idx"], 0:1, :] = h0["mnew"]
        m_ref[d["idx"], 1:2, :] = h1["mnew"]


def _bidir_specs(nc, width):
    specs = []
    for rev in (False, True):
        cidx = (lambda c: nc - 1 - c) if rev else (lambda c: c)
        for part in range(3):
            specs.append(pl.BlockSpec((1, CHUNK, width), lambda b, c, part=part, cidx=cidx: (b, cidx(c), part)))
        specs.append(pl.BlockSpec((1, CHUNK, LANES), lambda b, c, cidx=cidx: (b, cidx(c), 0)))
    return specs


def _mlstm(bqkv, gates):
    bsz, seq, _ = bqkv.shape
    nc = seq // CHUNK
    nstate = 2 * (D_B // LANES)
    out = jax.ShapeDtypeStruct((bsz, seq, D_B), BF16)
    return pl.pallas_call(
        _mlstm_kernel,
        grid=(bsz, nc),
        in_specs=_bidir_specs(nc, D_B),
        out_specs=[pl.BlockSpec((1, CHUNK, D_B), lambda b, c: (b, c, 0)),
                   pl.BlockSpec((1, CHUNK, D_B), lambda b, c: (b, nc - 1 - c, 0))],
        out_shape=(out, out),
        scratch_shapes=[pltpu.VMEM((nstate, LANES, 2 * LANES), F32), pltpu.VMEM((nstate, SUBLANES, LANES), F32)],
        compiler_params=pltpu.CompilerParams(
            dimension_semantics=("parallel", "arbitrary"), vmem_limit_bytes=VMEM_LIMIT),
        name="mlstm",
    )(bqkv, bqkv, bqkv, gates, bqkv, bqkv, bqkv, gates)


_INV_BASE = 32


def _unit_tri_inverses(ms, xor):
    eye = (xor == 0).astype(F32)
    base = xor < _INV_BASE
    mbs = [jnp.where(base, m, 0.0) for m in ms]
    ts = [eye - mb for mb in mbs]
    pbs = [mb.astype(BF16) for mb in mbs]
    k = 1
    while True:
        pbs = [_dot(pb, pb).astype(BF16) for pb in pbs]
        k *= 2
        ts = [t + _dot(t.astype(BF16), pb) for t, pb in zip(ts, pbs)]
        if 2 * k >= _INV_BASE:
            break
    b = _INV_BASE
    while b < CHUNK:
        level = (xor >= b) & (xor < 2 * b)
        tbs = [t.astype(BF16) for t in ts]
        xs = [_dot(jnp.where(level, m, 0.0).astype(BF16), tb).astype(BF16) for m, tb in zip(ms, tbs)]
        ts = [t - _dot(tb, x) for t, tb, x in zip(ts, tbs, xs)]
        b *= 2
    return ts


def _gdn_kernel(qf, kf, vf, gf, qb, kb, vb, gb, of_ref, ob_ref, s_ref):
    @pl.when(pl.program_id(1) == 0)
    def _():
        s_ref[...] = jnp.zeros_like(s_ref)

    npair = D_C // LANES
    head0 = _iota2((CHUNK, LANES), 1) < HEAD_DIM
    sel0 = (_iota2((CHUNK, 2 * LANES), 1) & HEAD_DIM) == 0
    bd = _pair_block_diag(LANES, LANES)
    fwd_masks = _direction_masks(False)
    bwd_masks = _direction_masks(True)
    xor = fwd_masks[2]

    dirs = []
    for backward, (q_ref, k_ref, v_ref, g_ref), o_ref in ((False, (qf, kf, vf, gf), of_ref),
                                                          (True, (qb, kb, vb, gb), ob_ref)):
        incl, strict, _ = bwd_masks if backward else fwd_masks
        incl_other = (fwd_masks if backward else bwd_masks)[0]
        last = 0 if backward else CHUNK - 1
        g = g_ref[0]
        gbc, _, gbt = _gate_cumsums(g, incl, incl_other)
        for p in range(npair):
            cols = slice(p * LANES, (p + 1) * LANES)
            q2, k2, v2 = q_ref[0, :, cols], k_ref[0, :, cols], v_ref[0, :, cols]
            a_lane = [_gate_lane(1, p, 0, backward, h) for h in range(2)]
            b_lane = [_gate_lane(1, p, 1, backward, h) for h in range(2)]
            gam = [gbc[:, a:a + 1] for a in a_lane]
            beta = [g[:, b:b + 1] for b in b_lane]
            glast2 = jnp.where(head0[0:1, :], gbc[last:last + 1, a_lane[0]:a_lane[0] + 1],
                               gbc[last:last + 1, a_lane[1]:a_lane[1] + 1])
            gam2 = jnp.where(head0, gam[0], gam[1])
            beta2 = jnp.where(head0, beta[0], beta[1])
            egam2 = jnp.exp2(gam2)
            kf32 = k2.astype(F32)
            rhs = jnp.concatenate([v2.astype(F32) * beta2, kf32 * (beta2 * egam2)], axis=1).astype(BF16)
            decay = [jnp.exp2(jnp.where(incl, gam[h] - gbt[a_lane[h]:a_lane[h] + 1, :], -jnp.inf)) for h in range(2)]
            idx = 2 * int(backward) + p
            state = s_ref[idx]
            dirs.append(dict(q2=q2, k2=k2, kf32=kf32, rhs=rhs, decay=decay, beta=beta, strict=strict, gam2=gam2,
                             egam2=egam2, glast2=glast2, state=state, state_b=state.astype(BF16), o_ref=o_ref,
                             cols=cols, idx=idx))

    chains = [(d, h) for d in dirs for h in range(2)]

    def head_lanes(x, h):
        return jnp.where(head0 if h == 0 else ~head0, x, 0).astype(BF16)

    kk = [_dot_nt(head_lanes(d["k2"], h), d["k2"]) for d, h in chains]
    ms = [jnp.where(d["strict"], d["beta"][h] * kkh * d["decay"][h], 0.0) for (d, h), kkh in zip(chains, kk)]
    qk = [_dot_nt(head_lanes(d["q2"], h), d["k2"]) for d, h in chains]
    a_intra = [(qkh * d["decay"][h]).astype(BF16) for (d, h), qkh in zip(chains, qk)]
    t_inv = _unit_tri_inverses(ms, xor)
    uw = [_dot(t.astype(BF16), d["rhs"]) for (d, _), t in zip(chains, t_inv)]

    uw2 = [jnp.where(sel0, uw[2 * i], uw[2 * i + 1]) for i in range(len(dirs))]
    w_state = [_dot(uw2[i][:, LANES:].astype(BF16), d["state_b"]) for i, d in enumerate(dirs)]
    v_new = [(uw2[i][:, :LANES] - w_state[i]).astype(BF16) for i in range(len(dirs))]
    q_state = [_dot((d["q2"].astype(F32) * d["egam2"]).astype(BF16), d["state_b"]) for d in dirs]
    av = [_dot(a_intra[j], v_new[j // 2]) for j in range(len(chains))]
    upd = [_dot_tn((d["kf32"] * jnp.exp2(d["glast2"] - d["gam2"])).astype(BF16), v_new[i])
           for i, d in enumerate(dirs)]
    for i, d in enumerate(dirs):
        d["o_ref"][0, :, d["cols"]] = (q_state[i] + jnp.where(head0, av[2 * i], av[2 * i + 1])).astype(d["o_ref"].dtype)
        s_ref[d["idx"]] = d["state"] * jnp.exp2(d["glast2"]) + jnp.where(bd, upd[i], 0.0)


def _gdn(cqkv, gates):
    bsz, seq, _ = cqkv.shape
    nc = seq // CHUNK
    nstate = 2 * (D_C // LANES)
    out = jax.ShapeDtypeStruct((bsz, seq, D_C), BF16)
    return pl.pallas_call(
        _gdn_kernel,
        grid=(bsz, nc),
        in_specs=_bidir_specs(nc, D_C),
        out_specs=[pl.BlockSpec((1, CHUNK, D_C), lambda b, c: (b, c, 0)),
                   pl.BlockSpec((1, CHUNK, D_C), lambda b, c: (b, nc - 1 - c, 0))],
        out_shape=(out, out),
        scratch_shapes=[pltpu.VMEM((nstate, LANES, LANES), F32)],
        compiler_params=pltpu.CompilerParams(
            dimension_semantics=("parallel", "arbitrary"), vmem_limit_bytes=VMEM_LIMIT),
        name="gdn",
    )(cqkv, cqkv, cqkv, gates, cqkv, cqkv, cqkv, gates)


def _out_kernel(x_ref, oa_ref, hf_ref, hb_ref, cf_ref, cb_ref, z_ref, mw_ref, gw_ref, wo_ref, y_ref):
    bd = _head_block_diag(MXU_WIDTH).astype(BF16)
    z = z_ref[0].astype(F32)

    def silu(t):
        return t * _sigmoid(t)

    hsum = _sigmoid(z[:, D_A:D_A + D_B]) * (hf_ref[0].astype(F32) + hb_ref[0].astype(F32))
    csum = cf_ref[0].astype(F32) + cb_ref[0].astype(F32)
    ssb = _head_sum(hsum * hsum, bd)
    ssc = _head_sum(csum * csum, bd)
    ya = oa_ref[0].astype(F32) * silu(z[:, 0:D_A])
    yb = hsum * lax.rsqrt(ssb * (1.0 / HEAD_DIM) + EPS) * mw_ref[...] * silu(z[:, D_A + D_B:D_A + 2 * D_B])
    yc = csum * lax.rsqrt(ssc * (1.0 / HEAD_DIM) + EPS) * gw_ref[...] * silu(z[:, D_A + 2 * D_B:])
    y = jnp.concatenate([ya, yb, yc], axis=1).astype(BF16)
    y_ref[0] = x_ref[0] + _dot(y, wo_ref[...])


def _out_proj(x, oa, hf, hb, cf, cb, z, mw, gw, wo):
    bsz, seq, d = x.shape
    tile = OUT_TILE
    row_spec = lambda a: pl.BlockSpec((1, tile, a.shape[-1]), lambda b, i: (b, i, 0))
    full = lambda a: pl.BlockSpec(a.shape, lambda b, i: (0,) * a.ndim)
    return pl.pallas_call(
        _out_kernel,
        grid=(bsz, seq // tile),
        in_specs=[row_spec(a) for a in (x, oa, hf, hb, cf, cb, z)] + [full(mw), full(gw), full(wo)],
        out_specs=pl.BlockSpec((1, tile, d), lambda b, i: (b, i, 0)),
        out_shape=jax.ShapeDtypeStruct(x.shape, x.dtype),
        compiler_params=pltpu.CompilerParams(
            dimension_semantics=("parallel", "parallel"), vmem_limit_bytes=VMEM_LIMIT),
        name="out_proj",
    )(x, oa, hf, hb, cf, cb, z, mw, gw, wo)


def _gate_layout():
    return [(_gate_lane(mixer, pair, kind, direction, hh), mixer, kind, direction, 2 * pair + hh)
            for mixer in range(2) for pair in range(2) for kind in range(2) for direction in range(2)
            for hh in range(2)]


def _gate_weights(w_in):
    nh = D_B // HEAD_DIM
    cols = np.full((LANES,), -1, np.int64)
    for lane, mixer, kind, direction, head in _gate_layout():
        cols[lane] = (_OFF_BIF, _OFF_CAB)[mixer] + (2 * kind + direction) * nh + head
    return jnp.where((cols >= 0)[None, :], w_in[:, np.maximum(cols, 0)], 0.0)


def _gate_params(i_bias, f_bias, a_log, dt_bias):
    nh = i_bias.shape[1]
    flat = jnp.concatenate([i_bias.reshape(-1), f_bias.reshape(-1), dt_bias.reshape(-1), a_log.reshape(-1),
                            jnp.zeros((1,), F32)])
    idx = np.full((SUBLANES, LANES), flat.shape[0] - 1, np.int32)
    for lane, mixer, kind, direction, head in _gate_layout():
        src = direction * nh + head
        if mixer == 0:
            idx[0, lane] = kind * 2 * nh + src
        elif kind == 0:
            idx[0, lane] = 2 * 2 * nh + src
            idx[1, lane] = 3 * 2 * nh + src
    return flat[idx]


def _layer(x, bias_tiles, norm_w, w_in, w_out, qk_norm_w, i_bias, f_bias, m_norm_w, conv_w, a_log, dt_bias,
           g_norm_w):
    wa = w_in[:, _OFF_AQ:_OFF_AZ].astype(BF16)
    wb = w_in[:, _OFF_BQ:_OFF_BIF].astype(BF16)
    wc = w_in[:, _OFF_CQ:_OFF_CAB].astype(BF16)
    wz = jnp.concatenate([w_in[:, _OFF_AZ:_OFF_BQ], w_in[:, _OFF_BO:_OFF_CQ], w_in[:, _OFF_CZ:]],
                         axis=1).astype(BF16)
    wg = _gate_weights(w_in).astype(BF16)
    qkw = jnp.pad(jnp.tile(qk_norm_w, (1, MXU_WIDTH // HEAD_DIM)) * jnp.array([[HEAD_DIM ** -0.5 * LOG2E], [1.0]], F32),
                  ((0, SUBLANES - 2), (0, 0)))
    gpar = _gate_params(i_bias, f_bias, a_log, dt_bias)
    cw = jnp.pad(conv_w, ((0, SUBLANES - CONV_W), (0, 0)))

    aqkv, bqkv, cqkv, gates, z = _in_proj(x, norm_w[None, :], wa, wb, wc, wg, wz, qkw, gpar, cw)
    oa = _attention(aqkv, bias_tiles)
    hf, hb = _mlstm(bqkv, gates)
    cf, cb = _gdn(cqkv, gates)
    mw = jnp.tile(m_norm_w, D_B // HEAD_DIM)[None, :]
    gw = jnp.tile(g_norm_w, D_C // HEAD_DIM)[None, :]
    return _out_proj(x, oa, hf, hb, cf, cb, z, mw, gw, w_out.astype(BF16))


def kernel(x, norm_w, w_in, w_out, qk_norm_w, rel_bias, mlstm_i_bias, mlstm_f_bias, mlstm_norm_w, gdn_conv_w,
           gdn_a_log, gdn_dt_bias, gdn_norm_w):
    bias_tiles = _attn_bias_tiles(rel_bias)
    for l in range(DEPTH):
        x = _layer(x, bias_tiles, norm_w[l], w_in[l], w_out[l], qk_norm_w[l], mlstm_i_bias[l], mlstm_f_bias[l],
                   mlstm_norm_w[l], gdn_conv_w[l], gdn_a_log[l], gdn_dt_bias[l], gdn_norm_w[l])
    return x
```

```python
import functools

import numpy as np
import jax
import jax.numpy as jnp
from jax import lax
from jax.experimental import pallas as pl
from jax.experimental.pallas import tpu as pltpu

F32 = jnp.float32
BF16 = jnp.bfloat16

D_MODEL = 1024
DEPTH = 2
EPS = 1e-6
HEAD_DIM = 64
LANES = 128
SUBLANES = 8
MXU_WIDTH = 256
D_A, D_B, D_C = 512, 256, 256
DILATED_CFGS = ((128, 1), (512, 4), (2048, 16))
N_SIDE = 64
NUM_BUCKETS = 32
REL_MAX_DIST = 1024
CONV_W = 5
CHUNK = 256
Q_SUPER = 2048
Q_BLK = 128
K_WIN = 256
ATTN_ILP = 2
IN_TILE = 512
OUT_TILE = 512
NEG = -1e30
LOG2E = 1.4426950408889634
VMEM_LIMIT = 56 * 1024 * 1024

_OFF_AQ, _OFF_AZ = 0, 1536
_OFF_BQ, _OFF_BIF, _OFF_BO, _OFF_BZ = 2048, 2816, 2832, 3088
_OFF_CQ, _OFF_CAB, _OFF_CZ = 3344, 4112, 4128


def _dot(a, b):
    return jnp.dot(a, b, preferred_element_type=F32)


def _dot_nt(a, b):
    return lax.dot_general(a, b, (((1,), (1,)), ((), ())), preferred_element_type=F32)


def _dot_tn(a, b):
    return lax.dot_general(a, b, (((0,), (0,)), ((), ())), preferred_element_type=F32)


def _split(a):
    hi = a.astype(BF16)
    lo = (a - hi.astype(F32)).astype(BF16)
    return hi, lo


def _iota2(shape, axis):
    return lax.broadcasted_iota(jnp.int32, shape, axis)


def _pair_block_diag(rows, cols):
    r = _iota2((rows, cols), 0)
    c = _iota2((rows, cols), 1)
    return (r < HEAD_DIM) == ((c & HEAD_DIM) == 0)


def _head_block_diag(n):
    return (_iota2((n, n), 0) ^ _iota2((n, n), 1)) < HEAD_DIM


def _head_sum(t, bd):
    return _dot(t.astype(BF16), bd)


def _softplus(y):
    return jnp.maximum(y, 0.0) + jnp.log1p(jnp.exp(-jnp.abs(y)))


def _sigmoid(y):
    return 1.0 / (1.0 + jnp.exp(-y))


def _in_proj_kernel(x_ref, xp_ref, xn_ref, nw_ref, wa_ref, wb_ref, wc_ref, wg_ref, wz_ref,
                    qkw_ref, gpar_ref, cw_ref,
                    a_ref, b_ref, c_ref, g_ref, z_ref, cext_ref):
    i = pl.program_id(1)
    n = pl.num_programs(1)
    tile = x_ref.shape[1]
    bd = _head_block_diag(MXU_WIDTH).astype(BF16)

    xe = jnp.concatenate([xp_ref[0], x_ref[0], xn_ref[0]], axis=0)
    ms = jnp.mean(xe * xe, axis=-1, keepdims=True)
    hne = xe * lax.rsqrt(ms + EPS) * nw_ref[...]
    hn = hne[SUBLANES:SUBLANES + tile].astype(BF16)
    hne = hne.astype(BF16)

    ta = [_dot(hn, wa_ref[:, j * MXU_WIDTH:(j + 1) * MXU_WIDTH]) for j in range(3 * D_A // MXU_WIDTH)]
    tb = _dot(hn, wb_ref[...])
    p = _dot(hn, wg_ref[...]) + gpar_ref[0:1, :]
    ce = _dot(hne, wc_ref[...])
    z_ref[0] = _dot(hn, wz_ref[...]).astype(z_ref.dtype)

    row = _iota2((tile + 2 * SUBLANES, 1), 0)
    valid = ((row >= SUBLANES) | (i > 0)) & ((row < tile + SUBLANES) | (i < n - 1))
    cext_ref[...] = jnp.where(valid, ce, 0.0)
    conv = cw_ref[0:1, :] * cext_ref[pl.ds(SUBLANES - CONV_W // 2, tile), :]
    for t in range(1, CONV_W):
        conv = conv + cw_ref[t:t + 1, :] * cext_ref[pl.ds(SUBLANES - CONV_W // 2 + t, tile), :]
    s = conv * _sigmoid(conv)
    tc = [s[:, j * MXU_WIDTH:(j + 1) * MXU_WIDTH] for j in range(3 * D_C // MXU_WIDTH)]

    ssa = [_head_sum(t * t, bd) for t in ta[:4]]
    ssc = [_head_sum(t * t, bd) for t in tc[:2]]

    for j, t in enumerate(ta):
        if j < 4:
            t = t * lax.rsqrt(ssa[j] * (1.0 / HEAD_DIM) + EPS) * qkw_ref[j // 2:j // 2 + 1, :]
        a_ref[0, :, j * MXU_WIDTH:(j + 1) * MXU_WIDTH] = t

    b_ref[0, :, 0:D_B] = (tb[:, 0:D_B] * HEAD_DIM ** -0.5).astype(b_ref.dtype)
    b_ref[0, :, D_B:3 * D_B] = tb[:, D_B:3 * D_B].astype(b_ref.dtype)

    lane = _iota2(p.shape, 1)
    first4 = (lane & 7) < 4
    is_b = lane < 16
    val_b = jnp.where(first4, p, -_softplus(-p)) * LOG2E
    val_c = jnp.where(first4, -jnp.exp(gpar_ref[1:2, :]) * _softplus(p) * LOG2E, _sigmoid(p))
    g_ref[0] = jnp.where(is_b, val_b, val_c)

    c_ref[0, :, 0:D_C] = (tc[0] * lax.rsqrt(ssc[0] + EPS) * HEAD_DIM ** -0.5).astype(c_ref.dtype)
    c_ref[0, :, D_C:2 * D_C] = (tc[1] * lax.rsqrt(ssc[1] + EPS)).astype(c_ref.dtype)
    c_ref[0, :, 2 * D_C:3 * D_C] = tc[2].astype(c_ref.dtype)


def _in_proj(x, nw, wa, wb, wc, wg, wz, qkw, gpar, cw):
    bsz, seq, d = x.shape
    tile = IN_TILE
    nt = seq // tile
    hb = tile // SUBLANES
    full = lambda a: pl.BlockSpec(a.shape, lambda b, i: (0,) * a.ndim)
    out_shapes = (
        jax.ShapeDtypeStruct((bsz, seq, 3 * D_A), F32),
        jax.ShapeDtypeStruct((bsz, seq, 3 * D_B), BF16),
        jax.ShapeDtypeStruct((bsz, seq, 3 * D_C), BF16),
        jax.ShapeDtypeStruct((bsz, seq, LANES), F32),
        jax.ShapeDtypeStruct((bsz, seq, D_A + 3 * D_B), BF16),
    )
    row_spec = lambda w: pl.BlockSpec((1, tile, w), lambda b, i: (b, i, 0))
    return pl.pallas_call(
        _in_proj_kernel,
        grid=(bsz, nt),
        in_specs=[
            row_spec(d),
            pl.BlockSpec((1, SUBLANES, d), lambda b, i: (b, jnp.maximum(i * hb - 1, 0), 0)),
            pl.BlockSpec((1, SUBLANES, d), lambda b, i: (b, jnp.minimum((i + 1) * hb, nt * hb - 1), 0)),
            full(nw), full(wa), full(wb), full(wc), full(wg), full(wz), full(qkw), full(gpar), full(cw),
        ],
        out_specs=[row_spec(s.shape[-1]) for s in out_shapes],
        out_shape=out_shapes,
        scratch_shapes=[pltpu.VMEM((tile + 2 * SUBLANES, 3 * D_C), F32)],
        compiler_params=pltpu.CompilerParams(
            dimension_semantics=("parallel", "arbitrary"), vmem_limit_bytes=VMEM_LIMIT),
        name="in_proj",
    )(x, x, x, nw, wa, wb, wc, wg, wz, qkw, gpar, cw)


def _t5_bucket(rel):
    half = NUM_BUCKETS // 2
    max_exact = half // 2
    n = np.abs(rel)
    large = max_exact + (np.log(np.maximum(n, 1) / max_exact) / np.log(REL_MAX_DIST / max_exact)
                         * (half - max_exact)).astype(np.int32)
    large = np.minimum(large, half - 1)
    return (rel > 0).astype(np.int32) * half + np.where(n < max_exact, n, large)


def _attn_bias_tiles(rel_bias):
    period = 640
    wide = K_WIN + 2 * N_SIDE
    lead = Q_BLK - 1 + N_SIDE
    cfg_tiles = []
    for (_, dil) in DILATED_CFGS:
        offs = dil * np.arange(-N_SIDE, N_SIDE + 1)
        bias = rel_bias[_t5_bucket(offs)].T.astype(F32) * LOG2E
        heads = bias.shape[0]
        vec = jnp.concatenate([jnp.full((heads, lead + N_SIDE), NEG, F32), bias,
                               jnp.full((heads, period - lead - 3 * N_SIDE - 1), NEG, F32)], axis=1)
        rows = jnp.tile(vec, (1, Q_BLK))[:, :Q_BLK * (period - 1)].reshape(heads, Q_BLK, period - 1)
        ext = rows[:, :, lead:lead + wide]
        variants = [ext[:, :, 2 * N_SIDE:2 * N_SIDE + K_WIN], ext[:, :, N_SIDE:N_SIDE + K_WIN],
                    ext[:, :, 0:K_WIN]]
        cfg_tiles.append(jnp.stack(variants, axis=1))
    return jnp.stack(cfg_tiles, axis=0)


def _attn_kernel(q_ref, k_ref, v_ref, bias_ref, o_ref, acc_ref, m_ref, l_ref, kw_ref, vw_ref):
    sb = pl.program_id(2)
    seq = k_ref.shape[1]
    qsup = q_ref.shape[1]
    nblk = qsup // Q_BLK
    head0 = _iota2((Q_BLK, LANES), 1) < HEAD_DIM
    sel0 = (_iota2((Q_BLK, 2 * LANES), 1) & HEAD_DIM) == 0

    wide = DILATED_CFGS[-1][1]

    @pl.when(sb == 0)
    def _():
        def regroup(c, carry):
            r = c % wide
            i = c // wide
            src = pl.ds(r + wide * (i * K_WIN), K_WIN, stride=wide)
            dst = pl.ds(pl.multiple_of(r * (seq // wide) + i * K_WIN, K_WIN), K_WIN)
            kw_ref[dst, :] = k_ref[0, src, :].astype(BF16)
            vw_ref[dst, :] = v_ref[0, src, :].astype(BF16)
            return carry

        lax.fori_loop(0, seq // K_WIN, regroup, 0)

    groups = [(pos, ci, g) for pos, ci in enumerate(reversed(range(len(DILATED_CFGS))))
              for g in range(nblk // ATTN_ILP)]

    def scores(pos, ci, g):
        dil = DILATED_CFGS[ci][1]
        n_idx = seq // dil
        blocks = []
        for u in range(ATTN_ILP):
            t = g * ATTN_ILP + u
            r, j = t % dil, t // dil
            qs = r + dil * (j * Q_BLK)
            i0 = sb * (qsup // dil) + j * Q_BLK
            ws = jnp.clip(i0 - N_SIDE, 0, n_idx - K_WIN)
            var = jnp.where(i0 == 0, 0, jnp.where(i0 == n_idx - Q_BLK, 2, 1))
            qrows = pl.ds(qs, Q_BLK) if dil == 1 else pl.ds(qs, Q_BLK, stride=dil)
            if dil == wide:
                krows = pl.ds(pl.multiple_of(r * n_idx + ws, N_SIDE), K_WIN)
                k2 = kw_ref[krows, :]
                v2 = vw_ref[krows, :]
            else:
                ks = r + dil * ws
                krows = pl.ds(ks, K_WIN) if dil == 1 else pl.ds(ks, K_WIN, stride=dil)
                k2 = k_ref[0, krows, :].astype(BF16)
                v2 = v_ref[0, krows, :].astype(BF16)
            q2 = q_ref[0, qrows, :].astype(BF16)
            blocks.append(dict(qrows=qrows, var=var, k2=k2,
                               qh=[jnp.where(head0 if h == 0 else ~head0, q2, jnp.zeros_like(q2)) for h in range(2)],
                               vo=jnp.concatenate([v2, jnp.ones_like(v2)], axis=1)))
        chains = [(blk, h) for blk in blocks for h in range(2)]
        s = [_dot_nt(blk["qh"][h], blk["k2"]) + bias_ref[ci, h, blk["var"]] for blk, h in chains]
        mx = [jnp.max(sh, axis=-1, keepdims=True) for sh in s]
        p = [jnp.exp2(sh - m).astype(BF16) for sh, m in zip(s, mx)]
        return dict(pos=pos, blocks=blocks, chains=chains, mx=mx, p=p)

    def accumulate(st):
        pv = [_dot(ph, blk["vo"]) for ph, (blk, _) in zip(st["p"], st["chains"])]
        for u, blk in enumerate(st["blocks"]):
            qrows = blk["qrows"]
            tot = jnp.where(sel0, pv[2 * u], pv[2 * u + 1])
            num2 = tot[:, :LANES]
            l2 = tot[:, LANES:]
            m2 = jnp.where(head0, st["mx"][2 * u], st["mx"][2 * u + 1])
            if st["pos"] == 0:
                acc_ref[qrows, :] = num2
                m_ref[qrows, :] = m2
                l_ref[qrows, :] = l2
            else:
                mo = m_ref[qrows, :]
                mn = jnp.maximum(mo, m2)
                a = jnp.exp2(mo - mn)
                b = jnp.exp2(m2 - mn)
                acc_ref[qrows, :] = acc_ref[qrows, :] * a + num2 * b
                l_ref[qrows, :] = l_ref[qrows, :] * a + l2 * b
                m_ref[qrows, :] = mn

    pending = scores(*groups[0])
    for nxt in groups[1:]:
        upcoming = scores(*nxt)
        accumulate(pending)
        pending = upcoming
    accumulate(pending)

    o_ref[0] = (acc_ref[...] / l_ref[...]).astype(o_ref.dtype)


def _attention(aqkv, bias_tiles):
    bsz, seq, _ = aqkv.shape
    npair = D_A // LANES
    qsup = Q_SUPER
    assert seq % qsup == 0 and seq // DILATED_CFGS[-1][1] >= K_WIN
    return pl.pallas_call(
        _attn_kernel,
        grid=(bsz, npair, seq // qsup),
        in_specs=[
            pl.BlockSpec((1, qsup, LANES), lambda b, p, s: (b, s, p)),
            pl.BlockSpec((1, seq, LANES), lambda b, p, s: (b, 0, npair + p)),
            pl.BlockSpec((1, seq, LANES), lambda b, p, s: (b, 0, 2 * npair + p)),
            pl.BlockSpec((len(DILATED_CFGS), 2, 3, Q_BLK, K_WIN), lambda b, p, s: (0, p, 0, 0, 0)),
        ],
        out_specs=pl.BlockSpec((1, qsup, LANES), lambda b, p, s: (b, s, p)),
        out_shape=jax.ShapeDtypeStruct((bsz, seq, D_A), BF16),
        scratch_shapes=[pltpu.VMEM((qsup, LANES), F32)] * 3 + [pltpu.VMEM((seq, LANES), BF16)] * 2,
        compiler_params=pltpu.CompilerParams(
            dimension_semantics=("parallel", "parallel", "arbitrary"), vmem_limit_bytes=VMEM_LIMIT),
        name="dilated_attn",
    )(aqkv, aqkv, aqkv, bias_tiles)


def _direction_masks(backward):
    row = _iota2((CHUNK, CHUNK), 0)
    col = _iota2((CHUNK, CHUNK), 1)
    if backward:
        return row <= col, row < col, row ^ col
    return row >= col, row > col, row ^ col


def _gate_cumsums(g, incl, incl_other):
    tri = incl.astype(BF16)
    tri_t = incl_other.astype(BF16)
    ghi, glo = _split(g)
    gb = _dot(tri, ghi) + _dot(tri, glo)
    gt = g.T
    thi, tlo = _split(gt)
    gbt = _dot(thi, tri_t) + _dot(tlo, tri_t)
    return gb, gt, gbt


def _gate_lane(mixer, pair, kind, backward, head):
    return 16 * mixer + 8 * pair + 4 * kind + 2 * int(backward) + head


def _lane_expander(lanes):
    n = len(lanes)
    row = _iota2((LANES, n * LANES), 0)
    blk = jnp.right_shift(_iota2((LANES, n * LANES), 1), LANES.bit_length() - 1)
    want = sum(jnp.where(blk == j, lane, 0) for j, lane in enumerate(lanes))
    return (row == want).astype(BF16)


def _expand_lanes(x, expander):
    hi, lo = _split(x)
    return _dot(hi, expander) + _dot(lo, expander)


def _mlstm_stages(qf, kf, vf, gf, qb, kb, vb, gb, hf_ref, hb_ref, s_ref, m_ref, fwd_masks, bwd_masks):
    npair = D_B // LANES
    head0 = _iota2((CHUNK, LANES), 1) < HEAD_DIM
    sel0 = (_iota2((CHUNK, 2 * LANES), 1) & HEAD_DIM) == 0
    bd = _pair_block_diag(LANES, 2 * LANES)
    wide = lambda x: jnp.concatenate([x, x], axis=1)

    groups = []
    for backward, (q_ref, k_ref, v_ref, g_ref), h_ref in ((False, (qf, kf, vf, gf), hf_ref),
                                                          (True, (qb, kb, vb, gb), hb_ref)):
        incl = (bwd_masks if backward else fwd_masks)[0]
        incl_other = (fwd_masks if backward else bwd_masks)[0]
        last = 0 if backward else CHUNK - 1
        g, gbc, gt, gbt = g_ref
        i_lanes = [_gate_lane(0, p, 0, backward, h) for p in range(npair) for h in range(2)]
        f_lanes = [_gate_lane(0, p, 1, backward, h) for p in range(npair) for h in range(2)]
        li_all = _expand_lanes(g, _lane_expander(i_lanes))
        bc_all = _expand_lanes(gbc, _lane_expander(f_lanes))
        yield
        for p in range(npair):
            idx = 2 * int(backward) + p
            heads = []
            for h in range(2):
                j = 2 * p + h
                li = li_all[:, j * LANES:(j + 1) * LANES]
                bc = bc_all[:, j * LANES:(j + 1) * LANES]
                rrow = gt[i_lanes[j]:i_lanes[j] + 1, :] - gbt[f_lanes[j]:f_lanes[j] + 1, :]
                mst = m_ref[idx, h:h + 1, :]
                dmat = jnp.where(incl, wide(bc) + rrow, -jnp.inf)
                inter = bc + mst
                mt = jnp.maximum(jnp.max(dmat, axis=-1, keepdims=True), inter)
                blast = bc[last:last + 1, :]
                wlog = blast - bc + li
                mn = jnp.maximum(blast + mst, jnp.max(wlog, axis=0, keepdims=True))
                heads.append(dict(decay=jnp.exp2(dmat - wide(mt)), iw=jnp.exp2(inter - mt), emt=jnp.exp2(-mt),
                                  ws=jnp.exp2(wlog - mn), dec=jnp.exp2(blast + mst - mn), mnew=mn))
            cols = slice(p * LANES, (p + 1) * LANES)
            v2 = v_ref[0, :, cols]
            groups.append(dict(q2=q_ref[0, :, cols], k2=k_ref[0, :, cols], heads=heads, state=s_ref[idx], idx=idx,
                               h_ref=h_ref, cols=cols,
                               vo=jnp.concatenate([v2, jnp.ones_like(v2)], axis=1)))
            yield

    chains = [(d, h) for d in groups for h in range(2)]
    q_state = [_dot(d["q2"], d["state"].astype(BF16)) for d in groups]
    yield
    qk = [_dot_nt(jnp.where(head0 if h == 0 else ~head0, d["q2"], jnp.zeros_like(d["q2"])), d["k2"])
          for d, h in chains]
    yield
    sc = [(qkh * d["heads"][h]["decay"]).astype(BF16) for (d, h), qkh in zip(chains, qk)]
    yield
    pv = [_dot(sch, d["vo"]) for (d, _), sch in zip(chains, sc)]
    yield
    upd = [_dot_tn(d["k2"], (jnp.where(sel0, wide(d["heads"][0]["ws"]), wide(d["heads"][1]["ws"]))
                             * d["vo"].astype(F32)).astype(BF16)) for d in groups]
    yield
    for i, d in enumerate(groups):
        h0, h1 = d["heads"]
        tot = jnp.where(sel0, wide(h0["iw"]), wide(h1["iw"])) * q_state[i] + jnp.where(sel0, pv[2 * i], pv[2 * i + 1])
        den = jnp.maximum(jnp.abs(tot[:, LANES:]), jnp.where(head0, h0["emt"], h1["emt"]))
        d["h_ref"][0, :, d["cols"]] = (tot[:, :LANES] / den).astype(d["h_ref"].dtype)
        dec2 = jnp.where(sel0[0:1, :], wide(h0["dec"]), wide(h1["dec"]))
        s_ref[d["idx"]] = dec2 * d["state"] + jnp.where(bd, upd[i], 0.0)
        m_ref[d["idx"], 0:1, :] = h0["mnew"]
        m_ref[d["idx"], 1:2, :] = h1["mnew"]


_INV_BASE = 32


def _unit_tri_inverses(ms, xor, tick):
    diag = xor == 0
    one = jnp.ones((), BF16)
    zero = jnp.zeros((), BF16)
    base = xor < _INV_BASE
    pbs = [jnp.where(base, m, zero) for m in ms]
    ts = [jnp.where(diag, one, -pb) for pb in pbs]
    k = 1
    while True:
        pbs = [_dot(pb, pb).astype(BF16) for pb in pbs]
        tick()
        k *= 2
        ts = [_dot(t, jnp.where(diag, one, pb)).astype(BF16) for t, pb in zip(ts, pbs)]
        tick()
        if 2 * k >= _INV_BASE:
            break
    b = _INV_BASE
    while b < CHUNK:
        level = (xor >= b) & (xor < 2 * b)
        xs = [_dot(jnp.where(level, m, zero), t).astype(BF16) for m, t in zip(ms, ts)]
        tick()
        ts = [_dot(t, jnp.where(diag, one, -x)).astype(BF16) for t, x in zip(ts, xs)]
        tick()
        b *= 2
    return ts


def _gdn_step(qf, kf, vf, gf, qb, kb, vb, gb, of_ref, ob_ref, s_ref, fwd_masks, bwd_masks, tick):
    npair = D_C // LANES
    head0 = _iota2((CHUNK, LANES), 1) < HEAD_DIM
    sel0 = (_iota2((CHUNK, 2 * LANES), 1) & HEAD_DIM) == 0
    bd = _pair_block_diag(LANES, LANES)
    xor = fwd_masks[2]

    dirs = []
    for backward, (q_ref, k_ref, v_ref, g_ref), o_ref in ((False, (qf, kf, vf, gf), of_ref),
                                                          (True, (qb, kb, vb, gb), ob_ref)):
        incl, strict, _ = bwd_masks if backward else fwd_masks
        incl_other = (fwd_masks if backward else bwd_masks)[0]
        last = 0 if backward else CHUNK - 1
        g, gbc, _, gbt = g_ref
        for p in range(npair):
            cols = slice(p * LANES, (p + 1) * LANES)
            q2, k2, v2 = q_ref[0, :, cols], k_ref[0, :, cols], v_ref[0, :, cols]
            a_lane = [_gate_lane(1, p, 0, backward, h) for h in range(2)]
            b_lane = [_gate_lane(1, p, 1, backward, h) for h in range(2)]
            gam = [gbc[:, a:a + 1] for a in a_lane]
            beta = [g[:, b:b + 1] for b in b_lane]
            glast2 = jnp.where(head0[0:1, :], gbc[last:last + 1, a_lane[0]:a_lane[0] + 1],
                               gbc[last:last + 1, a_lane[1]:a_lane[1] + 1])
            gam2 = jnp.where(head0, gam[0], gam[1])
            beta2 = jnp.where(head0, beta[0], beta[1])
            egam2 = jnp.exp2(gam2)
            kf32 = k2.astype(F32)
            rhs = jnp.concatenate([v2.astype(F32) * beta2, kf32 * (beta2 * egam2)], axis=1).astype(BF16)
            decay = [jnp.exp2(jnp.where(incl, gam[h] - gbt[a_lane[h]:a_lane[h] + 1, :], -jnp.inf)) for h in range(2)]
            idx = 2 * int(backward) + p
            state = s_ref[idx]
            dirs.append(dict(q2=q2, k2=k2, kf32=kf32, rhs=rhs, decay=decay, beta=beta, strict=strict, gam2=gam2,
                             egam2=egam2, glast2=glast2, state=state, state_b=state.astype(BF16), o_ref=o_ref,
                             cols=cols, idx=idx))

    chains = [(d, h) for d in dirs for h in range(2)]

    def head_lanes(x, h):
        return jnp.where(head0 if h == 0 else ~head0, x, 0).astype(BF16)

    kk = [_dot_nt(head_lanes(d["k2"], h), d["k2"]) for d, h in chains]
    tick()
    ms = [jnp.where(d["strict"], d["beta"][h] * kkh * d["decay"][h], 0.0).astype(BF16)
          for (d, h), kkh in zip(chains, kk)]
    qk = [_dot_nt(head_lanes(d["q2"], h), d["k2"]) for d, h in chains]
    tick()
    a_intra = [(qkh * d["decay"][h]).astype(BF16) for (d, h), qkh in zip(chains, qk)]
    t_inv = _unit_tri_inverses(ms, xor, tick)
    uw = [_dot(t, d["rhs"]) for (d, _), t in zip(chains, t_inv)]
    tick()

    uw2 = [jnp.where(sel0, uw[2 * i], uw[2 * i + 1]) for i in range(len(dirs))]
    w_state = [_dot(uw2[i][:, LANES:].astype(BF16), d["state_b"]) for i, d in enumerate(dirs)]
    tick()
    v_new = [(uw2[i][:, :LANES] - w_state[i]).astype(BF16) for i in range(len(dirs))]
    q_state = [_dot((d["q2"].astype(F32) * d["egam2"]).astype(BF16), d["state_b"]) for d in dirs]
    av = [_dot(a_intra[j], v_new[j // 2]) for j in range(len(chains))]
    tick()
    upd = [_dot_tn((d["kf32"] * jnp.exp2(d["glast2"] - d["gam2"])).astype(BF16), v_new[i])
           for i, d in enumerate(dirs)]
    for i, d in enumerate(dirs):
        d["o_ref"][0, :, d["cols"]] = (q_state[i] + jnp.where(head0, av[2 * i], av[2 * i + 1])).astype(d["o_ref"].dtype)
        s_ref[d["idx"]] = d["state"] * jnp.exp2(d["glast2"]) + jnp.where(bd, upd[i], 0.0)


def _recurrent_kernel(bqf, bkf, bvf, cqf, ckf, cvf, gf, bqb, bkb, bvb, cqb, ckb, cvb, gb,
                      hf_ref, hb_ref, of_ref, ob_ref, ms_ref, mm_ref, gs_ref):
    @pl.when(pl.program_id(1) == 0)
    def _():
        ms_ref[...] = jnp.zeros_like(ms_ref)
        mm_ref[...] = jnp.zeros_like(mm_ref)
        gs_ref[...] = jnp.zeros_like(gs_ref)

    fwd_masks = _direction_masks(False)
    bwd_masks = _direction_masks(True)
    gf = (gf[0],) + _gate_cumsums(gf[0], fwd_masks[0], bwd_masks[0])
    gb = (gb[0],) + _gate_cumsums(gb[0], bwd_masks[0], fwd_masks[0])
    mlstm = _mlstm_stages(bqf, bkf, bvf, gf, bqb, bkb, bvb, gb, hf_ref, hb_ref, ms_ref, mm_ref, fwd_masks, bwd_masks)
    _gdn_step(cqf, ckf, cvf, gf, cqb, ckb, cvb, gb, of_ref, ob_ref, gs_ref, fwd_masks, bwd_masks,
              lambda: next(mlstm, None))
    for _ in mlstm:
        pass


def _recurrent_mixers(bqkv, cqkv, gates):
    bsz, seq, _ = bqkv.shape
    nc = seq // CHUNK
    nstate = 2 * (D_B // LANES)
    in_specs = []
    for rev in (False, True):
        cidx = (lambda c: nc - 1 - c) if rev else (lambda c: c)
        for width in (D_B, D_C):
            for part in range(3):
                in_specs.append(pl.BlockSpec((1, CHUNK, width), lambda b, c, part=part, cidx=cidx: (b, cidx(c), part)))
        in_specs.append(pl.BlockSpec((1, CHUNK, LANES), lambda b, c, cidx=cidx: (b, cidx(c), 0)))
    out_b = jax.ShapeDtypeStruct((bsz, seq, D_B), BF16)
    out_c = jax.ShapeDtypeStruct((bsz, seq, D_C), BF16)
    fwd_spec = lambda w: pl.BlockSpec((1, CHUNK, w), lambda b, c: (b, c, 0))
    bwd_spec = lambda w: pl.BlockSpec((1, CHUNK, w), lambda b, c: (b, nc - 1 - c, 0))
    return pl.pallas_call(
        _recurrent_kernel,
        grid=(bsz, nc),
        in_specs=in_specs,
        out_specs=[fwd_spec(D_B), bwd_spec(D_B), fwd_spec(D_C), bwd_spec(D_C)],
        out_shape=(out_b, out_b, out_c, out_c),
        scratch_shapes=[pltpu.VMEM((nstate, LANES, 2 * LANES), F32), pltpu.VMEM((nstate, SUBLANES, LANES), F32),
                        pltpu.VMEM((nstate, LANES, LANES), F32)],
        compiler_params=pltpu.CompilerParams(
            dimension_semantics=("parallel", "arbitrary"), vmem_limit_bytes=VMEM_LIMIT),
        name="recurrent_mixers",
    )(bqkv, bqkv, bqkv, cqkv, cqkv, cqkv, gates, bqkv, bqkv, bqkv, cqkv, cqkv, cqkv, gates)


def _out_kernel(x_ref, oa_ref, hf_ref, hb_ref, cf_ref, cb_ref, z_ref, mw_ref, gw_ref, wo_ref, y_ref):
    bd = _head_block_diag(MXU_WIDTH).astype(BF16)
    z = z_ref[0].astype(F32)

    def silu(t):
        return t * _sigmoid(t)

    hsum = _sigmoid(z[:, D_A:D_A + D_B]) * (hf_ref[0].astype(F32) + hb_ref[0].astype(F32))
    csum = cf_ref[0].astype(F32) + cb_ref[0].astype(F32)
    ssb = _head_sum(hsum * hsum, bd)
    ssc = _head_sum(csum * csum, bd)
    ya = oa_ref[0].astype(F32) * silu(z[:, 0:D_A])
    yb = hsum * lax.rsqrt(ssb * (1.0 / HEAD_DIM) + EPS) * mw_ref[...] * silu(z[:, D_A + D_B:D_A + 2 * D_B])
    yc = csum * lax.rsqrt(ssc * (1.0 / HEAD_DIM) + EPS) * gw_ref[...] * silu(z[:, D_A + 2 * D_B:])
    y = jnp.concatenate([ya, yb, yc], axis=1).astype(BF16)
    y_ref[0] = x_ref[0] + _dot(y, wo_ref[...])


def _out_proj(x, oa, hf, hb, cf, cb, z, mw, gw, wo):
    bsz, seq, d = x.shape
    tile = OUT_TILE
    row_spec = lambda a: pl.BlockSpec((1, tile, a.shape[-1]), lambda b, i: (b, i, 0))
    full = lambda a: pl.BlockSpec(a.shape, lambda b, i: (0,) * a.ndim)
    return pl.pallas_call(
        _out_kernel,
        grid=(bsz, seq // tile),
        in_specs=[row_spec(a) for a in (x, oa, hf, hb, cf, cb, z)] + [full(mw), full(gw), full(wo)],
        out_specs=pl.BlockSpec((1, tile, d), lambda b, i: (b, i, 0)),
        out_shape=jax.ShapeDtypeStruct(x.shape, x.dtype),
        compiler_params=pltpu.CompilerParams(
            dimension_semantics=("parallel", "parallel"), vmem_limit_bytes=VMEM_LIMIT),
        name="out_proj",
    )(x, oa, hf, hb, cf, cb, z, mw, gw, wo)


def _gate_layout():
    return [(_gate_lane(mixer, pair, kind, direction, hh), mixer, kind, direction, 2 * pair + hh)
            for mixer in range(2) for pair in range(2) for kind in range(2) for direction in range(2)
            for hh in range(2)]


def _gate_weights(w_in):
    nh = D_B // HEAD_DIM
    cols = np.full((LANES,), -1, np.int64)
    for lane, mixer, kind, direction, head in _gate_layout():
        cols[lane] = (_OFF_BIF, _OFF_CAB)[mixer] + (2 * kind + direction) * nh + head
    return jnp.where((cols >= 0)[None, :], w_in[:, np.maximum(cols, 0)], 0.0)


def _gate_params(i_bias, f_bias, a_log, dt_bias):
    nh = i_bias.shape[1]
    flat = jnp.concatenate([i_bias.reshape(-1), f_bias.reshape(-1), dt_bias.reshape(-1), a_log.reshape(-1),
                            jnp.zeros((1,), F32)])
    idx = np.full((SUBLANES, LANES), flat.shape[0] - 1, np.int32)
    for lane, mixer, kind, direction, head in _gate_layout():
        src = direction * nh + head
        if mixer == 0:
            idx[0, lane] = kind * 2 * nh + src
        elif kind == 0:
            idx[0, lane] = 2 * 2 * nh + src
            idx[1, lane] = 3 * 2 * nh + src
    return flat[idx]


def _layer(x, bias_tiles, norm_w, w_in, w_out, qk_norm_w, i_bias, f_bias, m_norm_w, conv_w, a_log, dt_bias,
           g_norm_w):
    wa = w_in[:, _OFF_AQ:_OFF_AZ].astype(BF16)
    wb = w_in[:, _OFF_BQ:_OFF_BIF].astype(BF16)
    wc = w_in[:, _OFF_CQ:_OFF_CAB].astype(BF16)
    wz = jnp.concatenate([w_in[:, _OFF_AZ:_OFF_BQ], w_in[:, _OFF_BO:_OFF_CQ], w_in[:, _OFF_CZ:]],
                         axis=1).astype(BF16)
    wg = _gate_weights(w_in).astype(BF16)
    qkw = jnp.pad(jnp.tile(qk_norm_w, (1, MXU_WIDTH // HEAD_DIM)) * jnp.array([[HEAD_DIM ** -0.5 * LOG2E], [1.0]], F32),
                  ((0, SUBLANES - 2), (0, 0)))
    gpar = _gate_params(i_bias, f_bias, a_log, dt_bias)
    cw = jnp.pad(conv_w, ((0, SUBLANES - CONV_W), (0, 0)))

    aqkv, bqkv, cqkv, gates, z = _in_proj(x, norm_w[None, :], wa, wb, wc, wg, wz, qkw, gpar, cw)
    oa = _attention(aqkv, bias_tiles)
    hf, hb, cf, cb = _recurrent_mixers(bqkv, cqkv, gates)
    mw = jnp.tile(m_norm_w, D_B // HEAD_DIM)[None, :]
    gw = jnp.tile(g_norm_w, D_C // HEAD_DIM)[None, :]
    return _out_proj(x, oa, hf, hb, cf, cb, z, mw, gw, w_out.astype(BF16))


def kernel(x, norm_w, w_in, w_out, qk_norm_w, rel_bias, mlstm_i_bias, mlstm_f_bias, mlstm_norm_w, gdn_conv_w,
           gdn_a_log, gdn_dt_bias, gdn_norm_w):
    bias_tiles = _attn_bias_tiles(rel_bias)
    for l in range(DEPTH):
        x = _layer(x, bias_tiles, norm_w[l], w_in[l], w_out[l], qk_norm_w[l], mlstm_i_bias[l], mlstm_f_bias[l],
                   mlstm_norm_w[l], gdn_conv_w[l], gdn_a_log[l], gdn_dt_bias[l], gdn_norm_w[l])
    return x
```

```python
import functools

import numpy as np
import jax
import jax.numpy as jnp
from jax import lax
from jax.experimental import pallas as pl
from jax.experimental.pallas import tpu as pltpu

F32 = jnp.float32
BF16 = jnp.bfloat16

D_MODEL = 1024
DEPTH = 2
EPS = 1e-6
HEAD_DIM = 64
LANES = 128
SUBLANES = 8
MXU_WIDTH = 256
D_A, D_B, D_C = 512, 256, 256
DILATED_CFGS = ((128, 1), (512, 4), (2048, 16))
N_SIDE = 64
NUM_BUCKETS = 32
REL_MAX_DIST = 1024
CONV_W = 5
CHUNK = 256
Q_SUPER = 2048
Q_BLK = 128
K_WIN = 256
ATTN_ILP = 2
IN_TILE = 512
OUT_TILE = 512
NEG = -1e30
LOG2E = 1.4426950408889634
VMEM_LIMIT = 56 * 1024 * 1024

_OFF_AQ, _OFF_AZ = 0, 1536
_OFF_BQ, _OFF_BIF, _OFF_BO, _OFF_BZ = 2048, 2816, 2832, 3088
_OFF_CQ, _OFF_CAB, _OFF_CZ = 3344, 4112, 4128


def _dot(a, b):
    return jnp.dot(a, b, preferred_element_type=F32)


def _dot_nt(a, b):
    return lax.dot_general(a, b, (((1,), (1,)), ((), ())), preferred_element_type=F32)


def _dot_tn(a, b):
    return lax.dot_general(a, b, (((0,), (0,)), ((), ())), preferred_element_type=F32)


def _split(a):
    hi = a.astype(BF16)
    lo = (a - hi.astype(F32)).astype(BF16)
    return hi, lo


def _iota2(shape, axis):
    return lax.broadcasted_iota(jnp.int32, shape, axis)


def _pair_block_diag(rows, cols):
    r = _iota2((rows, cols), 0)
    c = _iota2((rows, cols), 1)
    return (r < HEAD_DIM) == ((c & HEAD_DIM) == 0)


def _head_block_diag(n):
    return (_iota2((n, n), 0) ^ _iota2((n, n), 1)) < HEAD_DIM


def _head_sum(t, bd):
    return _dot(t.astype(BF16), bd)


def _softplus(y):
    return jnp.maximum(y, 0.0) + jnp.log1p(jnp.exp(-jnp.abs(y)))


def _sigmoid(y):
    return 1.0 / (1.0 + jnp.exp(-y))


def _in_proj_kernel(x_ref, xp_ref, xn_ref, nw_ref, wa_ref, wb_ref, wc_ref, wg_ref, wz_ref,
                    qkw_ref, gpar_ref, cw_ref,
                    a_ref, kv1_ref, kv4_ref, kv16_ref, b_ref, c_ref, g_ref, z_ref, cext_ref, kvs_ref, hn_ref):
    kv_refs = (kv1_ref, kv4_ref, kv16_ref)
    i = pl.program_id(1)
    n = pl.num_programs(1)
    tile = x_ref.shape[1]
    bd = _head_block_diag(MXU_WIDTH).astype(BF16)

    xe = jnp.concatenate([xp_ref[0], x_ref[0], xn_ref[0]], axis=0)
    ms = jnp.mean(xe * xe, axis=-1, keepdims=True)
    hne = xe * lax.rsqrt(ms + EPS) * nw_ref[...]
    hn_ref[...] = hne[SUBLANES:SUBLANES + tile].astype(BF16)
    hne = hne.astype(BF16)

    ce = _dot(hne, wc_ref[...])
    ta = [_dot(hn_ref[...],wa_ref[:, j * MXU_WIDTH:(j + 1) * MXU_WIDTH]) for j in range(3 * D_A // MXU_WIDTH)]
    tb = _dot(hn_ref[...],wb_ref[...])
    p = _dot(hn_ref[...],wg_ref[...]) + gpar_ref[0:1, :]
    zsplit = 2 * MXU_WIDTH
    z_ref[0, :, :zsplit] = _dot(hn_ref[...],wz_ref[:, :zsplit]).astype(z_ref.dtype)
    ssa = [_head_sum(t * t, bd) for t in ta[:4]]

    row = _iota2((tile + 2 * SUBLANES, 1), 0)
    valid = ((row >= SUBLANES) | (i > 0)) & ((row < tile + SUBLANES) | (i < n - 1))
    cext_ref[...] = jnp.where(valid, ce, 0.0)
    conv = cw_ref[0:1, :] * cext_ref[pl.ds(SUBLANES - CONV_W // 2, tile), :]
    for t in range(1, CONV_W):
        conv = conv + cw_ref[t:t + 1, :] * cext_ref[pl.ds(SUBLANES - CONV_W // 2 + t, tile), :]
    s = conv * _sigmoid(conv)
    tc = [s[:, j * MXU_WIDTH:(j + 1) * MXU_WIDTH] for j in range(3 * D_C // MXU_WIDTH)]

    ssc = [_head_sum(t * t, bd) for t in tc[:2]]
    z_ref[0, :, zsplit:] = _dot(hn_ref[...],wz_ref[:, zsplit:]).astype(z_ref.dtype)

    for j, t in enumerate(ta):
        if j < 4:
            t = t * lax.rsqrt(ssa[j] * (1.0 / HEAD_DIM) + EPS) * qkw_ref[j // 2:j // 2 + 1, :]
        if j < 2:
            a_ref[0, :, j * MXU_WIDTH:(j + 1) * MXU_WIDTH] = t
        else:
            for half in range(2):
                kvs_ref[2 * (j - 2) + half] = t[:, half * LANES:(half + 1) * LANES]
    for lt in range(kvs_ref.shape[0]):
        lanes = slice(lt * LANES, (lt + 1) * LANES)
        kv_refs[0][0, :, lanes] = kvs_ref[lt].astype(BF16)
        for ref, (_, dil) in zip(kv_refs[1:], DILATED_CFGS[1:]):
            for r in range(dil):
                ref[0, r, :, lanes] = kvs_ref[lt, pl.ds(r, tile // dil, stride=dil), :].astype(BF16)

    b_ref[0, :, 0:D_B] = (tb[:, 0:D_B] * HEAD_DIM ** -0.5).astype(b_ref.dtype)
    b_ref[0, :, D_B:3 * D_B] = tb[:, D_B:3 * D_B].astype(b_ref.dtype)

    lane = _iota2(p.shape, 1)
    first4 = (lane & 7) < 4
    is_b = lane < 16
    val_b = jnp.where(first4, p, -_softplus(-p)) * LOG2E
    val_c = jnp.where(first4, -jnp.exp(gpar_ref[1:2, :]) * _softplus(p) * LOG2E, _sigmoid(p))
    g_ref[0] = jnp.where(is_b, val_b, val_c)

    c_ref[0, :, 0:D_C] = (tc[0] * lax.rsqrt(ssc[0] + EPS) * HEAD_DIM ** -0.5).astype(c_ref.dtype)
    c_ref[0, :, D_C:2 * D_C] = (tc[1] * lax.rsqrt(ssc[1] + EPS)).astype(c_ref.dtype)
    c_ref[0, :, 2 * D_C:3 * D_C] = tc[2].astype(c_ref.dtype)


def _in_proj(x, nw, wa, wb, wc, wg, wz, qkw, gpar, cw):
    bsz, seq, d = x.shape
    tile = IN_TILE
    nt = seq // tile
    hb = tile // SUBLANES
    full = lambda a: pl.BlockSpec(a.shape, lambda b, i: (0,) * a.ndim)
    row_spec = lambda w: pl.BlockSpec((1, tile, w), lambda b, i: (b, i, 0))
    outs = [((bsz, seq, D_A), F32, row_spec(D_A))]
    for (_, dil) in DILATED_CFGS:
        if dil == 1:
            outs.append(((bsz, seq, 2 * D_A), BF16, row_spec(2 * D_A)))
        else:
            outs.append(((bsz, dil, seq // dil, 2 * D_A), BF16,
                         pl.BlockSpec((1, dil, tile // dil, 2 * D_A), lambda b, i: (b, 0, i, 0))))
    outs += [((bsz, seq, 3 * D_B), BF16, row_spec(3 * D_B)), ((bsz, seq, 3 * D_C), BF16, row_spec(3 * D_C)),
             ((bsz, seq, LANES), F32, row_spec(LANES)), ((bsz, seq, D_A + 3 * D_B), BF16, row_spec(D_A + 3 * D_B))]
    return pl.pallas_call(
        _in_proj_kernel,
        grid=(bsz, nt),
        in_specs=[
            row_spec(d),
            pl.BlockSpec((1, SUBLANES, d), lambda b, i: (b, jnp.maximum(i * hb - 1, 0), 0)),
            pl.BlockSpec((1, SUBLANES, d), lambda b, i: (b, jnp.minimum((i + 1) * hb, nt * hb - 1), 0)),
            full(nw), full(wa), full(wb), full(wc), full(wg), full(wz), full(qkw), full(gpar), full(cw),
        ],
        out_specs=[spec for _, _, spec in outs],
        out_shape=[jax.ShapeDtypeStruct(shape, dtype) for shape, dtype, _ in outs],
        scratch_shapes=[pltpu.VMEM((tile + 2 * SUBLANES, 3 * D_C), F32), pltpu.VMEM((2 * D_A // LANES, tile, LANES), F32),
                        pltpu.VMEM((tile, d), BF16)],
        compiler_params=pltpu.CompilerParams(
            dimension_semantics=("parallel", "arbitrary"), vmem_limit_bytes=VMEM_LIMIT),
        name="in_proj",
    )(x, x, x, nw, wa, wb, wc, wg, wz, qkw, gpar, cw)


def _t5_bucket(rel):
    half = NUM_BUCKETS // 2
    max_exact = half // 2
    n = np.abs(rel)
    large = max_exact + (np.log(np.maximum(n, 1) / max_exact) / np.log(REL_MAX_DIST / max_exact)
                         * (half - max_exact)).astype(np.int32)
    large = np.minimum(large, half - 1)
    return (rel > 0).astype(np.int32) * half + np.where(n < max_exact, n, large)


def _attn_bias_tiles(rel_bias):
    period = 640
    wide = K_WIN + 2 * N_SIDE
    lead = Q_BLK - 1 + N_SIDE
    cfg_tiles = []
    for (_, dil) in DILATED_CFGS:
        offs = dil * np.arange(-N_SIDE, N_SIDE + 1)
        bias = rel_bias[_t5_bucket(offs)].T.astype(F32) * LOG2E
        heads = bias.shape[0]
        vec = jnp.concatenate([jnp.full((heads, lead + N_SIDE), NEG, F32), bias,
                               jnp.full((heads, period - lead - 3 * N_SIDE - 1), NEG, F32)], axis=1)
        rows = jnp.tile(vec, (1, Q_BLK))[:, :Q_BLK * (period - 1)].reshape(heads, Q_BLK, period - 1)
        ext = rows[:, :, lead:lead + wide]
        variants = [ext[:, :, 2 * N_SIDE:2 * N_SIDE + K_WIN], ext[:, :, N_SIDE:N_SIDE + K_WIN],
                    ext[:, :, 0:K_WIN]]
        cfg_tiles.append(jnp.stack(variants, axis=1))
    return jnp.stack(cfg_tiles, axis=0)


def _attn_kernel(q_ref, *rest):
    ncfg = len(DILATED_CFGS)
    kv_refs = [(rest[2 * c], rest[2 * c + 1]) for c in range(ncfg)]
    bias_ref, o_ref, acc_ref, m_ref, l_ref = rest[2 * ncfg:]
    sb = pl.program_id(2)
    seq = kv_refs[0][0].shape[1]
    qsup = q_ref.shape[1]
    nblk = qsup // Q_BLK
    head0 = _iota2((Q_BLK, LANES), 1) < HEAD_DIM
    sel0 = (_iota2((Q_BLK, 2 * LANES), 1) & HEAD_DIM) == 0

    groups = [(pos, ci, g) for pos, ci in enumerate(reversed(range(len(DILATED_CFGS))))
              for g in range(nblk // ATTN_ILP)]

    def scores(pos, ci, g):
        dil = DILATED_CFGS[ci][1]
        n_idx = seq // dil
        blocks = []
        for u in range(ATTN_ILP):
            t = g * ATTN_ILP + u
            r, j = t % dil, t // dil
            qs = r + dil * (j * Q_BLK)
            i0 = sb * (qsup // dil) + j * Q_BLK
            ws = jnp.clip(i0 - N_SIDE, 0, n_idx - K_WIN)
            var = jnp.where(i0 == 0, 0, jnp.where(i0 == n_idx - Q_BLK, 2, 1))
            qrows = pl.ds(qs, Q_BLK) if dil == 1 else pl.ds(qs, Q_BLK, stride=dil)
            krows = pl.ds(pl.multiple_of(ws, N_SIDE), K_WIN)
            k_ref, v_ref = kv_refs[ci]
            if dil == 1:
                k2, v2 = k_ref[0, krows, :], v_ref[0, krows, :]
            else:
                k2, v2 = k_ref[0, r, krows, :], v_ref[0, r, krows, :]
            q2 = q_ref[0, qrows, :].astype(BF16)
            blocks.append(dict(qrows=qrows, var=var, k2=k2,
                               qh=[jnp.where(head0 if h == 0 else ~head0, q2, jnp.zeros_like(q2)) for h in range(2)],
                               vo=jnp.concatenate([v2, jnp.ones_like(v2)], axis=1)))
        chains = [(blk, h) for blk in blocks for h in range(2)]
        s = [_dot_nt(blk["qh"][h], blk["k2"]) + bias_ref[ci, h, blk["var"]] for blk, h in chains]
        mx = [jnp.max(sh, axis=-1, keepdims=True) for sh in s]
        p = [jnp.exp2(sh - m).astype(BF16) for sh, m in zip(s, mx)]
        return dict(pos=pos, blocks=blocks, chains=chains, mx=mx, p=p)

    def accumulate(st):
        pv = [_dot(ph, blk["vo"]) for ph, (blk, _) in zip(st["p"], st["chains"])]
        for u, blk in enumerate(st["blocks"]):
            qrows = blk["qrows"]
            tot = jnp.where(sel0, pv[2 * u], pv[2 * u + 1])
            num2 = tot[:, :LANES]
            l2 = tot[:, LANES:]
            m2 = jnp.where(head0, st["mx"][2 * u], st["mx"][2 * u + 1])
            if st["pos"] == 0:
                acc_ref[qrows, :] = num2
                m_ref[qrows, :] = m2
                l_ref[qrows, :] = l2
            else:
                mo = m_ref[qrows, :]
                mn = jnp.maximum(mo, m2)
                a = jnp.exp2(mo - mn)
                b = jnp.exp2(m2 - mn)
                acc_ref[qrows, :] = acc_ref[qrows, :] * a + num2 * b
                l_ref[qrows, :] = l_ref[qrows, :] * a + l2 * b
                m_ref[qrows, :] = mn

    pending = scores(*groups[0])
    for nxt in groups[1:]:
        upcoming = scores(*nxt)
        accumulate(pending)
        pending = upcoming
    accumulate(pending)

    o_ref[0] = (acc_ref[...] / l_ref[...]).astype(o_ref.dtype)


def _attention(aq, akv, bias_tiles):
    bsz, seq, _ = aq.shape
    npair = D_A // LANES
    qsup = Q_SUPER
    assert seq % qsup == 0 and seq // DILATED_CFGS[-1][1] >= K_WIN
    in_specs = [pl.BlockSpec((1, qsup, LANES), lambda b, p, s: (b, s, p))]
    operands = [aq]
    for kv, (_, dil) in zip(akv, DILATED_CFGS):
        for part in range(2):
            if dil == 1:
                in_specs.append(pl.BlockSpec((1, seq, LANES), lambda b, p, s, part=part: (b, 0, part * npair + p)))
            else:
                in_specs.append(pl.BlockSpec((1, dil, seq // dil, LANES),
                                             lambda b, p, s, part=part: (b, 0, 0, part * npair + p)))
            operands.append(kv)
    in_specs.append(pl.BlockSpec((len(DILATED_CFGS), 2, 3, Q_BLK, K_WIN), lambda b, p, s: (0, p, 0, 0, 0)))
    return pl.pallas_call(
        _attn_kernel,
        grid=(bsz, npair, seq // qsup),
        in_specs=in_specs,
        out_specs=pl.BlockSpec((1, qsup, LANES), lambda b, p, s: (b, s, p)),
        out_shape=jax.ShapeDtypeStruct((bsz, seq, D_A), BF16),
        scratch_shapes=[pltpu.VMEM((qsup, LANES), F32)] * 3,
        compiler_params=pltpu.CompilerParams(
            dimension_semantics=("parallel", "parallel", "arbitrary"), vmem_limit_bytes=VMEM_LIMIT),
        name="dilated_attn",
    )(*operands, bias_tiles)


def _direction_masks(backward):
    row = _iota2((CHUNK, CHUNK), 0)
    col = _iota2((CHUNK, CHUNK), 1)
    if backward:
        return row <= col, row < col, row ^ col
    return row >= col, row > col, row ^ col


def _gate_cumsums(g, incl, incl_other):
    tri = incl.astype(BF16)
    tri_t = incl_other.astype(BF16)
    ghi, glo = _split(g)
    gb = _dot(tri, ghi) + _dot(tri, glo)
    gt = g.T
    thi, tlo = _split(gt)
    gbt = _dot(thi, tri_t) + _dot(tlo, tri_t)
    return gb, gt, gbt


def _gate_lane(mixer, pair, kind, backward, head):
    return 16 * mixer + 8 * pair + 4 * kind + 2 * int(backward) + head


def _lane_expander(lanes):
    n = len(lanes)
    row = _iota2((LANES, n * LANES), 0)
    blk = jnp.right_shift(_iota2((LANES, n * LANES), 1), LANES.bit_length() - 1)
    want = sum(jnp.where(blk == j, lane, 0) for j, lane in enumerate(lanes))
    return (row == want).astype(BF16)


def _expand_lanes(x, expander):
    hi, lo = _split(x)
    return _dot(hi, expander) + _dot(lo, expander)


def _mlstm_stages(qf, kf, vf, gf, qb, kb, vb, gb, hf_ref, hb_ref, s_ref, m_ref, fwd_masks, bwd_masks):
    npair = D_B // LANES
    head0 = _iota2((CHUNK, LANES), 1) < HEAD_DIM
    sel0 = (_iota2((CHUNK, 2 * LANES), 1) & HEAD_DIM) == 0
    bd = _pair_block_diag(LANES, 2 * LANES)
    wide = lambda x: jnp.concatenate([x, x], axis=1)

    groups = []
    for backward, (q_ref, k_ref, v_ref, g_ref), h_ref in ((False, (qf, kf, vf, gf), hf_ref),
                                                          (True, (qb, kb, vb, gb), hb_ref)):
        incl = (bwd_masks if backward else fwd_masks)[0]
        incl_other = (fwd_masks if backward else bwd_masks)[0]
        last = 0 if backward else CHUNK - 1
        g, gbc, gt, gbt = g_ref
        i_lanes = [_gate_lane(0, p, 0, backward, h) for p in range(npair) for h in range(2)]
        f_lanes = [_gate_lane(0, p, 1, backward, h) for p in range(npair) for h in range(2)]
        li_all = _expand_lanes(g, _lane_expander(i_lanes))
        bc_all = _expand_lanes(gbc, _lane_expander(f_lanes))
        yield
        for p in range(npair):
            idx = 2 * int(backward) + p
            heads = []
            for h in range(2):
                j = 2 * p + h
                li = li_all[:, j * LANES:(j + 1) * LANES]
                bc = bc_all[:, j * LANES:(j + 1) * LANES]
                rrow = gt[i_lanes[j]:i_lanes[j] + 1, :] - gbt[f_lanes[j]:f_lanes[j] + 1, :]
                mst = m_ref[idx, h:h + 1, :]
                dmat = jnp.where(incl, wide(bc) + rrow, -jnp.inf)
                inter = bc + mst
                mt = jnp.maximum(jnp.max(dmat, axis=-1, keepdims=True), inter)
                blast = bc[last:last + 1, :]
                wlog = blast - bc + li
                mn = jnp.maximum(blast + mst, jnp.max(wlog, axis=0, keepdims=True))
                heads.append(dict(decay=jnp.exp2(dmat - wide(mt)), iw=jnp.exp2(inter - mt), emt=jnp.exp2(-mt),
                                  ws=jnp.exp2(wlog - mn), dec=jnp.exp2(blast + mst - mn), mnew=mn))
            cols = slice(p * LANES, (p + 1) * LANES)
            v2 = v_ref[0, :, cols]
            groups.append(dict(q2=q_ref[0, :, cols], k2=k_ref[0, :, cols], heads=heads, state=s_ref[idx], idx=idx,
                               h_ref=h_ref, cols=cols,
                               vo=jnp.concatenate([v2, jnp.ones_like(v2)], axis=1)))
            yield

    chains = [(d, h) for d in groups for h in range(2)]
    q_state = [_dot(d["q2"], d["state"].astype(BF16)) for d in groups]
    yield
    qk = [_dot_nt(jnp.where(head0 if h == 0 else ~head0, d["q2"], jnp.zeros_like(d["q2"])), d["k2"])
          for d, h in chains]
    yield
    sc = [(qkh * d["heads"][h]["decay"]).astype(BF16) for (d, h), qkh in zip(chains, qk)]
    yield
    pv = [_dot(sch, d["vo"]) for (d, _), sch in zip(chains, sc)]
    yield
    upd = [_dot_tn(d["k2"], (jnp.where(sel0, wide(d["heads"][0]["ws"]), wide(d["heads"][1]["ws"]))
                             * d["vo"].astype(F32)).astype(BF16)) for d in groups]
    yield
    for i, d in enumerate(groups):
        h0, h1 = d["heads"]
        tot = jnp.where(sel0, wide(h0["iw"]), wide(h1["iw"])) * q_state[i] + jnp.where(sel0, pv[2 * i], pv[2 * i + 1])
        den = jnp.maximum(jnp.abs(tot[:, LANES:]), jnp.where(head0, h0["emt"], h1["emt"]))
        d["h_ref"][0, :, d["cols"]] = (tot[:, :LANES] / den).astype(d["h_ref"].dtype)
        dec2 = jnp.where(sel0[0:1, :], wide(h0["dec"]), wide(h1["dec"]))
        s_ref[d["idx"]] = dec2 * d["state"] + jnp.where(bd, upd[i], 0.0)
        m_ref[d["idx"], 0:1, :] = h0["mnew"]
        m_ref[d["idx"], 1:2, :] = h1["mnew"]


_INV_BASE = 32


def _unit_tri_inverses(ms, xor, tick):
    diag = xor == 0
    one = jnp.ones((), BF16)
    zero = jnp.zeros((), BF16)
    base = xor < _INV_BASE
    pbs = [jnp.where(base, m, zero) for m in ms]
    ts = [jnp.where(diag, one, -pb) for pb in pbs]
    k = 1
    while True:
        pbs = [_dot(pb, pb).astype(BF16) for pb in pbs]
        tick()
        k *= 2
        ts = [_dot(t, jnp.where(diag, one, pb)).astype(BF16) for t, pb in zip(ts, pbs)]
        tick()
        if 2 * k >= _INV_BASE:
            break
    b = _INV_BASE
    while b < CHUNK:
        level = (xor >= b) & (xor < 2 * b)
        xs = [_dot(jnp.where(level, m, zero), t).astype(BF16) for m, t in zip(ms, ts)]
        tick()
        ts = [_dot(t, jnp.where(diag, one, -x)).astype(BF16) for t, x in zip(ts, xs)]
        tick()
        b *= 2
    return ts


def _gdn_step(qf, kf, vf, gf, qb, kb, vb, gb, of_ref, ob_ref, s_ref, fwd_masks, bwd_masks, tick):
    npair = D_C // LANES
    head0 = _iota2((CHUNK, LANES), 1) < HEAD_DIM
    sel0 = (_iota2((CHUNK, 2 * LANES), 1) & HEAD_DIM) == 0
    bd = _pair_block_diag(LANES, LANES)
    xor = fwd_masks[2]

    dirs = []
    for backward, (q_ref, k_ref, v_ref, g_ref), o_ref in ((False, (qf, kf, vf, gf), of_ref),
                                                          (True, (qb, kb, vb, gb), ob_ref)):
        incl, strict, _ = bwd_masks if backward else fwd_masks
        incl_other = (fwd_masks if backward else bwd_masks)[0]
        last = 0 if backward else CHUNK - 1
        g, gbc, _, gbt = g_ref
        for p in range(npair):
            cols = slice(p * LANES, (p + 1) * LANES)
            q2, k2, v2 = q_ref[0, :, cols], k_ref[0, :, cols], v_ref[0, :, cols]
            a_lane = [_gate_lane(1, p, 0, backward, h) for h in range(2)]
            b_lane = [_gate_lane(1, p, 1, backward, h) for h in range(2)]
            gam = [gbc[:, a:a + 1] for a in a_lane]
            beta = [g[:, b:b + 1] for b in b_lane]
            glast2 = jnp.where(head0[0:1, :], gbc[last:last + 1, a_lane[0]:a_lane[0] + 1],
                               gbc[last:last + 1, a_lane[1]:a_lane[1] + 1])
            gam2 = jnp.where(head0, gam[0], gam[1])
            beta2 = jnp.where(head0, beta[0], beta[1])
            egam2 = jnp.exp2(gam2)
            kf32 = k2.astype(F32)
            rhs = jnp.concatenate([v2.astype(F32) * beta2, kf32 * (beta2 * egam2)], axis=1).astype(BF16)
            decay = [jnp.exp2(jnp.where(incl, gam[h] - gbt[a_lane[h]:a_lane[h] + 1, :], -jnp.inf)) for h in range(2)]
            idx = 2 * int(backward) + p
            state = s_ref[idx]
            dirs.append(dict(q2=q2, k2=k2, kf32=kf32, rhs=rhs, decay=decay, beta=beta, strict=strict, gam2=gam2,
                             egam2=egam2, glast2=glast2, state=state, state_b=state.astype(BF16), o_ref=o_ref,
                             cols=cols, idx=idx))

    chains = [(d, h) for d in dirs for h in range(2)]

    def head_lanes(x, h):
        return jnp.where(head0 if h == 0 else ~head0, x, 0).astype(BF16)

    kk = [_dot_nt(head_lanes(d["k2"], h), d["k2"]) for d, h in chains]
    tick()
    ms = [jnp.where(d["strict"], d["beta"][h] * kkh * d["decay"][h], 0.0).astype(BF16)
          for (d, h), kkh in zip(chains, kk)]
    qk = [_dot_nt(head_lanes(d["q2"], h), d["k2"]) for d, h in chains]
    tick()
    a_intra = [(qkh * d["decay"][h]).astype(BF16) for (d, h), qkh in zip(chains, qk)]
    t_inv = _unit_tri_inverses(ms, xor, tick)
    uw = [_dot(t, d["rhs"]) for (d, _), t in zip(chains, t_inv)]
    tick()

    uw2 = [jnp.where(sel0, uw[2 * i], uw[2 * i + 1]) for i in range(len(dirs))]
    w_state = [_dot(uw2[i][:, LANES:].astype(BF16), d["state_b"]) for i, d in enumerate(dirs)]
    tick()
    v_new = [(uw2[i][:, :LANES] - w_state[i]).astype(BF16) for i in range(len(dirs))]
    q_state = [_dot((d["q2"].astype(F32) * d["egam2"]).astype(BF16), d["state_b"]) for d in dirs]
    av = [_dot(a_intra[j], v_new[j // 2]) for j in range(len(chains))]
    tick()
    upd = [_dot_tn((d["kf32"] * jnp.exp2(d["glast2"] - d["gam2"])).astype(BF16), v_new[i])
           for i, d in enumerate(dirs)]
    for i, d in enumerate(dirs):
        d["o_ref"][0, :, d["cols"]] = (q_state[i] + jnp.where(head0, av[2 * i], av[2 * i + 1])).astype(d["o_ref"].dtype)
        s_ref[d["idx"]] = d["state"] * jnp.exp2(d["glast2"]) + jnp.where(bd, upd[i], 0.0)


def _recurrent_kernel(bqf, bkf, bvf, cqf, ckf, cvf, gf, bqb, bkb, bvb, cqb, ckb, cvb, gb,
                      hf_ref, hb_ref, of_ref, ob_ref, ms_ref, mm_ref, gs_ref):
    @pl.when(pl.program_id(1) == 0)
    def _():
        ms_ref[...] = jnp.zeros_like(ms_ref)
        mm_ref[...] = jnp.zeros_like(mm_ref)
        gs_ref[...] = jnp.zeros_like(gs_ref)

    fwd_masks = _direction_masks(False)
    bwd_masks = _direction_masks(True)
    gf = (gf[0],) + _gate_cumsums(gf[0], fwd_masks[0], bwd_masks[0])
    gb = (gb[0],) + _gate_cumsums(gb[0], bwd_masks[0], fwd_masks[0])
    mlstm = _mlstm_stages(bqf, bkf, bvf, gf, bqb, bkb, bvb, gb, hf_ref, hb_ref, ms_ref, mm_ref, fwd_masks, bwd_masks)
    _gdn_step(cqf, ckf, cvf, gf, cqb, ckb, cvb, gb, of_ref, ob_ref, gs_ref, fwd_masks, bwd_masks,
              lambda: next(mlstm, None))
    for _ in mlstm:
        pass


def _recurrent_mixers(bqkv, cqkv, gates):
    bsz, seq, _ = bqkv.shape
    nc = seq // CHUNK
    nstate = 2 * (D_B // LANES)
    in_specs = []
    for rev in (False, True):
        cidx = (lambda c: nc - 1 - c) if rev else (lambda c: c)
        for width in (D_B, D_C):
            for part in range(3):
                in_specs.append(pl.BlockSpec((1, CHUNK, width), lambda b, c, part=part, cidx=cidx: (b, cidx(c), part)))
        in_specs.append(pl.BlockSpec((1, CHUNK, LANES), lambda b, c, cidx=cidx: (b, cidx(c), 0)))
    out_b = jax.ShapeDtypeStruct((bsz, seq, D_B), BF16)
    out_c = jax.ShapeDtypeStruct((bsz, seq, D_C), BF16)
    fwd_spec = lambda w: pl.BlockSpec((1, CHUNK, w), lambda b, c: (b, c, 0))
    bwd_spec = lambda w: pl.BlockSpec((1, CHUNK, w), lambda b, c: (b, nc - 1 - c, 0))
    return pl.pallas_call(
        _recurrent_kernel,
        grid=(bsz, nc),
        in_specs=in_specs,
        out_specs=[fwd_spec(D_B), bwd_spec(D_B), fwd_spec(D_C), bwd_spec(D_C)],
        out_shape=(out_b, out_b, out_c, out_c),
        scratch_shapes=[pltpu.VMEM((nstate, LANES, 2 * LANES), F32), pltpu.VMEM((nstate, SUBLANES, LANES), F32),
                        pltpu.VMEM((nstate, LANES, LANES), F32)],
        compiler_params=pltpu.CompilerParams(
            dimension_semantics=("parallel", "arbitrary"), vmem_limit_bytes=VMEM_LIMIT),
        name="recurrent_mixers",
    )(bqkv, bqkv, bqkv, cqkv, cqkv, cqkv, gates, bqkv, bqkv, bqkv, cqkv, cqkv, cqkv, gates)


def _out_kernel(x_ref, oa_ref, hf_ref, hb_ref, cf_ref, cb_ref, z_ref, mw_ref, gw_ref, wo_ref, y_ref):
    bd = _head_block_diag(MXU_WIDTH).astype(BF16)
    z = z_ref[0].astype(F32)

    def silu(t):
        return t * _sigmoid(t)

    hsum = _sigmoid(z[:, D_A:D_A + D_B]) * (hf_ref[0].astype(F32) + hb_ref[0].astype(F32))
    csum = cf_ref[0].astype(F32) + cb_ref[0].astype(F32)
    ssb = _head_sum(hsum * hsum, bd)
    ssc = _head_sum(csum * csum, bd)
    ya = oa_ref[0].astype(F32) * silu(z[:, 0:D_A])
    yb = hsum * lax.rsqrt(ssb * (1.0 / HEAD_DIM) + EPS) * mw_ref[...] * silu(z[:, D_A + D_B:D_A + 2 * D_B])
    yc = csum * lax.rsqrt(ssc * (1.0 / HEAD_DIM) + EPS) * gw_ref[...] * silu(z[:, D_A + 2 * D_B:])
    y = jnp.concatenate([ya, yb, yc], axis=1).astype(BF16)
    y_ref[0] = x_ref[0] + _dot(y, wo_ref[...])


def _out_proj(x, oa, hf, hb, cf, cb, z, mw, gw, wo):
    bsz, seq, d = x.shape
    tile = OUT_TILE
    row_spec = lambda a: pl.BlockSpec((1, tile, a.shape[-1]), lambda b, i: (b, i, 0))
    full = lambda a: pl.BlockSpec(a.shape, lambda b, i: (0,) * a.ndim)
    return pl.pallas_call(
        _out_kernel,
        grid=(bsz, seq // tile),
        in_specs=[row_spec(a) for a in (x, oa, hf, hb, cf, cb, z)] + [full(mw), full(gw), full(wo)],
        out_specs=pl.BlockSpec((1, tile, d), lambda b, i: (b, i, 0)),
        out_shape=jax.ShapeDtypeStruct(x.shape, x.dtype),
        compiler_params=pltpu.CompilerParams(
            dimension_semantics=("parallel", "parallel"), vmem_limit_bytes=VMEM_LIMIT),
        name="out_proj",
    )(x, oa, hf, hb, cf, cb, z, mw, gw, wo)


def _gate_layout():
    return [(_gate_lane(mixer, pair, kind, direction, hh), mixer, kind, direction, 2 * pair + hh)
            for mixer in range(2) for pair in range(2) for kind in range(2) for direction in range(2)
            for hh in range(2)]


def _gate_weights(w_in):
    nh = D_B // HEAD_DIM
    cols = np.full((LANES,), -1, np.int64)
    for lane, mixer, kind, direction, head in _gate_layout():
        cols[lane] = (_OFF_BIF, _OFF_CAB)[mixer] + (2 * kind + direction) * nh + head
    return jnp.where((cols >= 0)[None, :], w_in[:, np.maximum(cols, 0)], 0.0)


def _gate_params(i_bias, f_bias, a_log, dt_bias):
    nh = i_bias.shape[1]
    flat = jnp.concatenate([i_bias.reshape(-1), f_bias.reshape(-1), dt_bias.reshape(-1), a_log.reshape(-1),
                            jnp.zeros((1,), F32)])
    idx = np.full((SUBLANES, LANES), flat.shape[0] - 1, np.int32)
    for lane, mixer, kind, direction, head in _gate_layout():
        src = direction * nh + head
        if mixer == 0:
            idx[0, lane] = kind * 2 * nh + src
        elif kind == 0:
            idx[0, lane] = 2 * 2 * nh + src
            idx[1, lane] = 3 * 2 * nh + src
    return flat[idx]


def _layer(x, bias_tiles, norm_w, w_in, w_out, qk_norm_w, i_bias, f_bias, m_norm_w, conv_w, a_log, dt_bias,
           g_norm_w):
    wa = w_in[:, _OFF_AQ:_OFF_AZ].astype(BF16)
    wb = w_in[:, _OFF_BQ:_OFF_BIF].astype(BF16)
    wc = w_in[:, _OFF_CQ:_OFF_CAB].astype(BF16)
    wz = jnp.concatenate([w_in[:, _OFF_AZ:_OFF_BQ], w_in[:, _OFF_BO:_OFF_CQ], w_in[:, _OFF_CZ:]],
                         axis=1).astype(BF16)
    wg = _gate_weights(w_in).astype(BF16)
    qkw = jnp.pad(jnp.tile(qk_norm_w, (1, MXU_WIDTH // HEAD_DIM)) * jnp.array([[HEAD_DIM ** -0.5 * LOG2E], [1.0]], F32),
                  ((0, SUBLANES - 2), (0, 0)))
    gpar = _gate_params(i_bias, f_bias, a_log, dt_bias)
    cw = jnp.pad(conv_w, ((0, SUBLANES - CONV_W), (0, 0)))

    aq, *akv, bqkv, cqkv, gates, z = _in_proj(x, norm_w[None, :], wa, wb, wc, wg, wz, qkw, gpar, cw)
    oa = _attention(aq, akv, bias_tiles)
    hf, hb, cf, cb = _recurrent_mixers(bqkv, cqkv, gates)
    mw = jnp.tile(m_norm_w, D_B // HEAD_DIM)[None, :]
    gw = jnp.tile(g_norm_w, D_C // HEAD_DIM)[None, :]
    return _out_proj(x, oa, hf, hb, cf, cb, z, mw, gw, w_out.astype(BF16))


def kernel(x, norm_w, w_in, w_out, qk_norm_w, rel_bias, mlstm_i_bias, mlstm_f_bias, mlstm_norm_w, gdn_conv_w,
           gdn_a_log, gdn_dt_bias, gdn_norm_w):
    bias_tiles = _attn_bias_tiles(rel_bias)
    for l in range(DEPTH):
        x = _layer(x, bias_tiles, norm_w[l], w_in[l], w_out[l], qk_norm_w[l], mlstm_i_bias[l], mlstm_f_bias[l],
                   mlstm_norm_w[l], gdn_conv_w[l], gdn_a_log[l], gdn_dt_bias[l], gdn_norm_w[l])
    return x
```

```python
import functools

import numpy as np
import jax
import jax.numpy as jnp
from jax import lax
from jax.experimental import pallas as pl
from jax.experimental.pallas import tpu as pltpu

F32 = jnp.float32
BF16 = jnp.bfloat16

D_MODEL = 1024
DEPTH = 2
EPS = 1e-6
HEAD_DIM = 64
LANES = 128
SUBLANES = 8
MXU_WIDTH = 256
D_A, D_B, D_C = 512, 256, 256
DILATED_CFGS = ((128, 1), (512, 4), (2048, 16))
N_SIDE = 64
NUM_BUCKETS = 32
REL_MAX_DIST = 1024
CONV_W = 5
CHUNK = 256
Q_SUPER = 2048
Q_BLK = 128
K_WIN = 256
ATTN_ILP = 2
ATTN_DEPTH = 1
IN_TILE = 512
OUT_TILE = 512
NEG = -1e30
LOG2E = 1.4426950408889634
VMEM_LIMIT = 56 * 1024 * 1024

_OFF_AQ, _OFF_AZ = 0, 1536
_OFF_BQ, _OFF_BIF, _OFF_BO, _OFF_BZ = 2048, 2816, 2832, 3088
_OFF_CQ, _OFF_CAB, _OFF_CZ = 3344, 4112, 4128


def _dot(a, b):
    return jnp.dot(a, b, preferred_element_type=F32)


def _dot_nt(a, b):
    return lax.dot_general(a, b, (((1,), (1,)), ((), ())), preferred_element_type=F32)


def _dot_tn(a, b):
    return lax.dot_general(a, b, (((0,), (0,)), ((), ())), preferred_element_type=F32)


def _split(a):
    hi = a.astype(BF16)
    lo = (a - hi.astype(F32)).astype(BF16)
    return hi, lo


def _iota2(shape, axis):
    return lax.broadcasted_iota(jnp.int32, shape, axis)


def _pair_block_diag(rows, cols):
    r = _iota2((rows, cols), 0)
    c = _iota2((rows, cols), 1)
    return (r < HEAD_DIM) == ((c & HEAD_DIM) == 0)


def _head_block_diag(n):
    return (_iota2((n, n), 0) ^ _iota2((n, n), 1)) < HEAD_DIM


def _head_sum(t, bd):
    return _dot(t.astype(BF16), bd)


def _softplus(y):
    return jnp.maximum(y, 0.0) + jnp.log1p(jnp.exp(-jnp.abs(y)))


def _sigmoid(y):
    return 1.0 / (1.0 + jnp.exp(-y))


def _in_proj_kernel(x_ref, xp_ref, xn_ref, nw_ref, wa_ref, wb_ref, wc_ref, wg_ref, wz_ref,
                    qkw_ref, gpar_ref, cw_ref,
                    a_ref, kv1_ref, kv4_ref, kv16_ref, b_ref, c_ref, g_ref, z_ref, cext_ref, kvs_ref, hn_ref, kvg_ref):
    kv_refs = (kv1_ref, kv4_ref, kv16_ref)
    i = pl.program_id(1)
    n = pl.num_programs(1)
    tile = x_ref.shape[1]
    bd = _head_block_diag(MXU_WIDTH).astype(BF16)

    xe = jnp.concatenate([xp_ref[0], x_ref[0], xn_ref[0]], axis=0)
    ms = jnp.mean(xe * xe, axis=-1, keepdims=True)
    hne = xe * lax.rsqrt(ms + EPS) * nw_ref[...]
    hn_ref[...] = hne[SUBLANES:SUBLANES + tile].astype(BF16)
    hne = hne.astype(BF16)

    ce = _dot(hne, wc_ref[...])
    ta = [_dot(hn_ref[...],wa_ref[:, j * MXU_WIDTH:(j + 1) * MXU_WIDTH]) for j in range(3 * D_A // MXU_WIDTH)]
    tb = _dot(hn_ref[...],wb_ref[...])
    p = _dot(hn_ref[...],wg_ref[...]) + gpar_ref[0:1, :]
    zsplit = 2 * MXU_WIDTH
    z_ref[0, :, :zsplit] = _dot(hn_ref[...],wz_ref[:, :zsplit]).astype(z_ref.dtype)
    ssa = [_head_sum(t * t, bd) for t in ta[:4]]

    row = _iota2((tile + 2 * SUBLANES, 1), 0)
    valid = ((row >= SUBLANES) | (i > 0)) & ((row < tile + SUBLANES) | (i < n - 1))
    cext_ref[...] = jnp.where(valid, ce, 0.0)
    conv = cw_ref[0:1, :] * cext_ref[pl.ds(SUBLANES - CONV_W // 2, tile), :]
    for t in range(1, CONV_W):
        conv = conv + cw_ref[t:t + 1, :] * cext_ref[pl.ds(SUBLANES - CONV_W // 2 + t, tile), :]
    s = conv * _sigmoid(conv)
    tc = [s[:, j * MXU_WIDTH:(j + 1) * MXU_WIDTH] for j in range(3 * D_C // MXU_WIDTH)]

    ssc = [_head_sum(t * t, bd) for t in tc[:2]]
    z_ref[0, :, zsplit:] = _dot(hn_ref[...],wz_ref[:, zsplit:]).astype(z_ref.dtype)

    for j, t in enumerate(ta):
        if j < 4:
            t = t * lax.rsqrt(ssa[j] * (1.0 / HEAD_DIM) + EPS) * qkw_ref[j // 2:j // 2 + 1, :]
        if j < 2:
            a_ref[0, :, j * MXU_WIDTH:(j + 1) * MXU_WIDTH] = t
        else:
            for half in range(2):
                kvs_ref[2 * (j - 2) + half] = t[:, half * LANES:(half + 1) * LANES]
    for lt in range(kvs_ref.shape[0]):
        lanes = slice(lt * LANES, (lt + 1) * LANES)
        kv_refs[0][0, :, lanes] = kvs_ref[lt].astype(BF16)
        prev_dil, src = 1, [kvs_ref.at[lt]]
        for ci in range(1, len(DILATED_CFGS)):
            dil = DILATED_CFGS[ci][1]
            step = dil // prev_dil
            rows = tile // dil
            nxt = [None] * dil
            for r in range(dil):
                x = src[r % prev_dil][pl.ds(r // prev_dil, rows, stride=step), :]
                kv_refs[ci][0, r, :, lanes] = x.astype(BF16)
                if ci + 1 < len(DILATED_CFGS):
                    kvg_ref[lt, r] = x
                    nxt[r] = kvg_ref.at[lt, r]
            prev_dil, src = dil, nxt

    b_ref[0, :, 0:D_B] = (tb[:, 0:D_B] * HEAD_DIM ** -0.5).astype(b_ref.dtype)
    b_ref[0, :, D_B:3 * D_B] = tb[:, D_B:3 * D_B].astype(b_ref.dtype)

    lane = _iota2(p.shape, 1)
    first4 = (lane & 7) < 4
    is_b = lane < 16
    val_b = jnp.where(first4, p, -_softplus(-p)) * LOG2E
    val_c = jnp.where(first4, -jnp.exp(gpar_ref[1:2, :]) * _softplus(p) * LOG2E, _sigmoid(p))
    g_ref[0] = jnp.where(is_b, val_b, val_c)

    c_ref[0, :, 0:D_C] = (tc[0] * lax.rsqrt(ssc[0] + EPS) * HEAD_DIM ** -0.5).astype(c_ref.dtype)
    c_ref[0, :, D_C:2 * D_C] = (tc[1] * lax.rsqrt(ssc[1] + EPS)).astype(c_ref.dtype)
    c_ref[0, :, 2 * D_C:3 * D_C] = tc[2].astype(c_ref.dtype)


def _in_proj(x, nw, wa, wb, wc, wg, wz, qkw, gpar, cw):
    bsz, seq, d = x.shape
    tile = IN_TILE
    nt = seq // tile
    hb = tile // SUBLANES
    full = lambda a: pl.BlockSpec(a.shape, lambda b, i: (0,) * a.ndim)
    row_spec = lambda w: pl.BlockSpec((1, tile, w), lambda b, i: (b, i, 0))
    outs = [((bsz, seq, D_A), F32, row_spec(D_A))]
    for (_, dil) in DILATED_CFGS:
        if dil == 1:
            outs.append(((bsz, seq, 2 * D_A), BF16, row_spec(2 * D_A)))
        else:
            outs.append(((bsz, dil, seq // dil, 2 * D_A), BF16,
                         pl.BlockSpec((1, dil, tile // dil, 2 * D_A), lambda b, i: (b, 0, i, 0))))
    outs += [((bsz, seq, 3 * D_B), BF16, row_spec(3 * D_B)), ((bsz, seq, 3 * D_C), BF16, row_spec(3 * D_C)),
             ((bsz, seq, LANES), F32, row_spec(LANES)), ((bsz, seq, D_A + 3 * D_B), BF16, row_spec(D_A + 3 * D_B))]
    return pl.pallas_call(
        _in_proj_kernel,
        grid=(bsz, nt),
        in_specs=[
            row_spec(d),
            pl.BlockSpec((1, SUBLANES, d), lambda b, i: (b, jnp.maximum(i * hb - 1, 0), 0)),
            pl.BlockSpec((1, SUBLANES, d), lambda b, i: (b, jnp.minimum((i + 1) * hb, nt * hb - 1), 0)),
            full(nw), full(wa), full(wb), full(wc), full(wg), full(wz), full(qkw), full(gpar), full(cw),
        ],
        out_specs=[spec for _, _, spec in outs],
        out_shape=[jax.ShapeDtypeStruct(shape, dtype) for shape, dtype, _ in outs],
        scratch_shapes=[pltpu.VMEM((tile + 2 * SUBLANES, 3 * D_C), F32), pltpu.VMEM((2 * D_A // LANES, tile, LANES), F32),
                        pltpu.VMEM((tile, d), BF16),
                        pltpu.VMEM((2 * D_A // LANES, DILATED_CFGS[1][1], tile // DILATED_CFGS[1][1], LANES), F32)],
        compiler_params=pltpu.CompilerParams(
            dimension_semantics=("parallel", "arbitrary"), vmem_limit_bytes=VMEM_LIMIT),
        name="in_proj",
    )(x, x, x, nw, wa, wb, wc, wg, wz, qkw, gpar, cw)


def _t5_bucket(rel):
    half = NUM_BUCKETS // 2
    max_exact = half // 2
    n = np.abs(rel)
    large = max_exact + (np.log(np.maximum(n, 1) / max_exact) / np.log(REL_MAX_DIST / max_exact)
                         * (half - max_exact)).astype(np.int32)
    large = np.minimum(large, half - 1)
    return (rel > 0).astype(np.int32) * half + np.where(n < max_exact, n, large)


def _attn_bias_tiles(rel_bias):
    period = 640
    wide = K_WIN + 2 * N_SIDE
    lead = Q_BLK - 1 + N_SIDE
    cfg_tiles = []
    for (_, dil) in DILATED_CFGS:
        offs = dil * np.arange(-N_SIDE, N_SIDE + 1)
        bias = rel_bias[_t5_bucket(offs)].T.astype(F32) * LOG2E
        heads = bias.shape[0]
        vec = jnp.concatenate([jnp.full((heads, lead + N_SIDE), NEG, F32), bias,
                               jnp.full((heads, period - lead - 3 * N_SIDE - 1), NEG, F32)], axis=1)
        rows = jnp.tile(vec, (1, Q_BLK))[:, :Q_BLK * (period - 1)].reshape(heads, Q_BLK, period - 1)
        ext = rows[:, :, lead:lead + wide]
        variants = [ext[:, :, 2 * N_SIDE:2 * N_SIDE + K_WIN], ext[:, :, N_SIDE:N_SIDE + K_WIN],
                    ext[:, :, 0:K_WIN]]
        cfg_tiles.append(jnp.stack(variants, axis=1))
    return jnp.stack(cfg_tiles, axis=0)


def _attn_kernel(q_ref, *rest):
    ncfg = len(DILATED_CFGS)
    kv_refs = [(rest[2 * c], rest[2 * c + 1]) for c in range(ncfg)]
    bias_ref, o_ref, acc_ref, m_ref, l_ref = rest[2 * ncfg:]
    sb = pl.program_id(2)
    seq = kv_refs[0][0].shape[1]
    qsup = q_ref.shape[1]
    nblk = qsup // Q_BLK
    head0 = _iota2((Q_BLK, LANES), 1) < HEAD_DIM
    sel0 = (_iota2((Q_BLK, 2 * LANES), 1) & HEAD_DIM) == 0

    groups = [(pos, ci, g) for pos, ci in enumerate(reversed(range(len(DILATED_CFGS))))
              for g in range(nblk // ATTN_ILP)]

    def scores(pos, ci, g):
        dil = DILATED_CFGS[ci][1]
        n_idx = seq // dil
        blocks = []
        for u in range(ATTN_ILP):
            t = g * ATTN_ILP + u
            r, j = t % dil, t // dil
            qs = r + dil * (j * Q_BLK)
            i0 = sb * (qsup // dil) + j * Q_BLK
            ws = jnp.clip(i0 - N_SIDE, 0, n_idx - K_WIN)
            var = jnp.where(i0 == 0, 0, jnp.where(i0 == n_idx - Q_BLK, 2, 1))
            qrows = pl.ds(qs, Q_BLK) if dil == 1 else pl.ds(qs, Q_BLK, stride=dil)
            krows = pl.ds(pl.multiple_of(ws, N_SIDE), K_WIN)
            k_ref, v_ref = kv_refs[ci]
            if dil == 1:
                k2, v2 = k_ref[0, krows, :], v_ref[0, krows, :]
            else:
                k2, v2 = k_ref[0, r, krows, :], v_ref[0, r, krows, :]
            q2 = q_ref[0, qrows, :].astype(BF16)
            blocks.append(dict(qrows=qrows, var=var, k2=k2,
                               qh=[jnp.where(head0 if h == 0 else ~head0, q2, jnp.zeros_like(q2)) for h in range(2)],
                               vo=jnp.concatenate([v2, jnp.ones_like(v2)], axis=1)))
        chains = [(blk, h) for blk in blocks for h in range(2)]
        s = [_dot_nt(blk["qh"][h], blk["k2"]) + bias_ref[ci, h, blk["var"]] for blk, h in chains]
        mx = [jnp.max(sh, axis=-1, keepdims=True) for sh in s]
        p = [jnp.exp2(sh - m).astype(BF16) for sh, m in zip(s, mx)]
        return dict(pos=pos, blocks=blocks, chains=chains, mx=mx, p=p)

    def accumulate(st):
        pv = [_dot(ph, blk["vo"]) for ph, (blk, _) in zip(st["p"], st["chains"])]
        for u, blk in enumerate(st["blocks"]):
            qrows = blk["qrows"]
            tot = jnp.where(sel0, pv[2 * u], pv[2 * u + 1])
            num2 = tot[:, :LANES]
            l2 = tot[:, LANES:]
            m2 = jnp.where(head0, st["mx"][2 * u], st["mx"][2 * u + 1])
            if st["pos"] == 0:
                acc_ref[qrows, :] = num2
                m_ref[qrows, :] = m2
                l_ref[qrows, :] = l2
            else:
                mo = m_ref[qrows, :]
                mn = jnp.maximum(mo, m2)
                a = jnp.exp2(mo - mn)
                b = jnp.exp2(m2 - mn)
                acc_ref[qrows, :] = acc_ref[qrows, :] * a + num2 * b
                l_ref[qrows, :] = l_ref[qrows, :] * a + l2 * b
                m_ref[qrows, :] = mn

    in_flight = []
    for grp in groups:
        in_flight.append(scores(*grp))
        if len(in_flight) > ATTN_DEPTH:
            accumulate(in_flight.pop(0))
    for st in in_flight:
        accumulate(st)

    o_ref[0] = (acc_ref[...] / l_ref[...]).astype(o_ref.dtype)


def _attention(aq, akv, bias_tiles):
    bsz, seq, _ = aq.shape
    npair = D_A // LANES
    qsup = Q_SUPER
    assert seq % qsup == 0 and seq // DILATED_CFGS[-1][1] >= K_WIN
    in_specs = [pl.BlockSpec((1, qsup, LANES), lambda b, p, s: (b, s, p))]
    operands = [aq]
    for kv, (_, dil) in zip(akv, DILATED_CFGS):
        for part in range(2):
            if dil == 1:
                in_specs.append(pl.BlockSpec((1, seq, LANES), lambda b, p, s, part=part: (b, 0, part * npair + p)))
            else:
                in_specs.append(pl.BlockSpec((1, dil, seq // dil, LANES),
                                             lambda b, p, s, part=part: (b, 0, 0, part * npair + p)))
            operands.append(kv)
    in_specs.append(pl.BlockSpec((len(DILATED_CFGS), 2, 3, Q_BLK, K_WIN), lambda b, p, s: (0, p, 0, 0, 0)))
    return pl.pallas_call(
        _attn_kernel,
        grid=(bsz, npair, seq // qsup),
        in_specs=in_specs,
        out_specs=pl.BlockSpec((1, qsup, LANES), lambda b, p, s: (b, s, p)),
        out_shape=jax.ShapeDtypeStruct((bsz, seq, D_A), BF16),
        scratch_shapes=[pltpu.VMEM((qsup, LANES), F32)] * 3,
        compiler_params=pltpu.CompilerParams(
            dimension_semantics=("parallel", "parallel", "arbitrary"), vmem_limit_bytes=VMEM_LIMIT),
        name="dilated_attn",
    )(*operands, bias_tiles)


def _direction_masks(backward):
    row = _iota2((CHUNK, CHUNK), 0)
    col = _iota2((CHUNK, CHUNK), 1)
    if backward:
        return row <= col, row < col, row ^ col
    return row >= col, row > col, row ^ col


def _gate_cumsums(g, incl, incl_other):
    tri = incl.astype(BF16)
    tri_t = incl_other.astype(BF16)
    ghi, glo = _split(g)
    gb = _dot(tri, ghi) + _dot(tri, glo)
    gt = g.T
    thi, tlo = _split(gt)
    gbt = _dot(thi, tri_t) + _dot(tlo, tri_t)
    return gb, gt, gbt


def _gate_lane(mixer, pair, kind, backward, head):
    return 16 * mixer + 8 * pair + 4 * kind + 2 * int(backward) + head


def _lane_expander(lanes):
    n = len(lanes)
    row = _iota2((LANES, n * LANES), 0)
    blk = jnp.right_shift(_iota2((LANES, n * LANES), 1), LANES.bit_length() - 1)
    want = sum(jnp.where(blk == j, lane, 0) for j, lane in enumerate(lanes))
    return (row == want).astype(BF16)


def _expand_lanes(x, expander):
    hi, lo = _split(x)
    return _dot(hi, expander) + _dot(lo, expander)


def _mlstm_stages(qf, kf, vf, gf, qb, kb, vb, gb, hf_ref, hb_ref, s_ref, m_ref, fwd_masks, bwd_masks):
    npair = D_B // LANES
    head0 = _iota2((CHUNK, LANES), 1) < HEAD_DIM
    sel0 = (_iota2((CHUNK, 2 * LANES), 1) & HEAD_DIM) == 0
    bd = _pair_block_diag(LANES, 2 * LANES)
    wide = lambda x: jnp.concatenate([x, x], axis=1)

    groups = []
    for backward, (q_ref, k_ref, v_ref, g_ref), h_ref in ((False, (qf, kf, vf, gf), hf_ref),
                                                          (True, (qb, kb, vb, gb), hb_ref)):
        incl = (bwd_masks if backward else fwd_masks)[0]
        incl_other = (fwd_masks if backward else bwd_masks)[0]
        last = 0 if backward else CHUNK - 1
        g, gbc, gt, gbt = g_ref
        i_lanes = [_gate_lane(0, p, 0, backward, h) for p in range(npair) for h in range(2)]
        f_lanes = [_gate_lane(0, p, 1, backward, h) for p in range(npair) for h in range(2)]
        li_all = _expand_lanes(g, _lane_expander(i_lanes))
        bc_all = _expand_lanes(gbc, _lane_expander(f_lanes))
        yield
        for p in range(npair):
            idx = 2 * int(backward) + p
            heads = []
            for h in range(2):
                j = 2 * p + h
                li = li_all[:, j * LANES:(j + 1) * LANES]
                bc = bc_all[:, j * LANES:(j + 1) * LANES]
                rrow = gt[i_lanes[j]:i_lanes[j] + 1, :] - gbt[f_lanes[j]:f_lanes[j] + 1, :]
                mst = m_ref[idx, h:h + 1, :]
                dmat = jnp.where(incl, wide(bc) + rrow, -jnp.inf)
                inter = bc + mst
                mt = jnp.maximum(jnp.max(dmat, axis=-1, keepdims=True), inter)
                blast = bc[last:last + 1, :]
                wlog = blast - bc + li
                mn = jnp.maximum(blast + mst, jnp.max(wlog, axis=0, keepdims=True))
                heads.append(dict(decay=jnp.exp2(dmat - wide(mt)), iw=jnp.exp2(inter - mt), emt=jnp.exp2(-mt),
                                  ws=jnp.exp2(wlog - mn), dec=jnp.exp2(blast + mst - mn), mnew=mn))
            cols = slice(p * LANES, (p + 1) * LANES)
            v2 = v_ref[0, :, cols]
            groups.append(dict(q2=q_ref[0, :, cols], k2=k_ref[0, :, cols], heads=heads, state=s_ref[idx], idx=idx,
                               h_ref=h_ref, cols=cols,
                               vo=jnp.concatenate([v2, jnp.ones_like(v2)], axis=1)))
            yield

    chains = [(d, h) for d in groups for h in range(2)]
    q_state = [_dot(d["q2"], d["state"].astype(BF16)) for d in groups]
    yield
    qk = [_dot_nt(jnp.where(head0 if h == 0 else ~head0, d["q2"], jnp.zeros_like(d["q2"])), d["k2"])
          for d, h in chains]
    yield
    sc = [(qkh * d["heads"][h]["decay"]).astype(BF16) for (d, h), qkh in zip(chains, qk)]
    yield
    pv = [_dot(sch, d["vo"]) for (d, _), sch in zip(chains, sc)]
    yield
    upd = [_dot_tn(d["k2"], (jnp.where(sel0, wide(d["heads"][0]["ws"]), wide(d["heads"][1]["ws"]))
                             * d["vo"].astype(F32)).astype(BF16)) for d in groups]
    yield
    for i, d in enumerate(groups):
        h0, h1 = d["heads"]
        tot = jnp.where(sel0, wide(h0["iw"]), wide(h1["iw"])) * q_state[i] + jnp.where(sel0, pv[2 * i], pv[2 * i + 1])
        den = jnp.maximum(jnp.abs(tot[:, LANES:]), jnp.where(head0, h0["emt"], h1["emt"]))
        d["h_ref"][0, :, d["cols"]] = (tot[:, :LANES] / den).astype(d["h_ref"].dtype)
        dec2 = jnp.where(sel0[0:1, :], wide(h0["dec"]), wide(h1["dec"]))
        s_ref[d["idx"]] = dec2 * d["state"] + jnp.where(bd, upd[i], 0.0)
        m_ref[d["idx"], 0:1, :] = h0["mnew"]
        m_ref[d["idx"], 1:2, :] = h1["mnew"]


_INV_BASE = 32


def _unit_tri_inverses(ms, xor, tick, upto):
    diag = xor == 0
    one = jnp.ones((), BF16)
    zero = jnp.zeros((), BF16)
    base = xor < _INV_BASE
    pbs = [jnp.where(base, m, zero) for m in ms]
    ts = [jnp.where(diag, one, -pb) for pb in pbs]
    k = 1
    while True:
        pbs = [_dot(pb, pb).astype(BF16) for pb in pbs]
        tick()
        k *= 2
        ts = [_dot(t, jnp.where(diag, one, pb)).astype(BF16) for t, pb in zip(ts, pbs)]
        tick()
        if 2 * k >= _INV_BASE:
            break
    b = _INV_BASE
    while b < upto:
        level = (xor >= b) & (xor < 2 * b)
        xs = [_dot(jnp.where(level, m, zero), t).astype(BF16) for m, t in zip(ms, ts)]
        tick()
        ts = [_dot(t, jnp.where(diag, one, -x)).astype(BF16) for t, x in zip(ts, xs)]
        tick()
        b *= 2
    return ts


def _gdn_step(qf, kf, vf, gf, qb, kb, vb, gb, of_ref, ob_ref, s_ref, fwd_masks, bwd_masks, tick):
    npair = D_C // LANES
    head0 = _iota2((CHUNK, LANES), 1) < HEAD_DIM
    sel0 = (_iota2((CHUNK, 2 * LANES), 1) & HEAD_DIM) == 0
    bd = _pair_block_diag(LANES, LANES)
    xor = fwd_masks[2]

    dirs = []
    for backward, (q_ref, k_ref, v_ref, g_ref), o_ref in ((False, (qf, kf, vf, gf), of_ref),
                                                          (True, (qb, kb, vb, gb), ob_ref)):
        incl, strict, _ = bwd_masks if backward else fwd_masks
        incl_other = (fwd_masks if backward else bwd_masks)[0]
        last = 0 if backward else CHUNK - 1
        g, gbc, _, gbt = g_ref
        for p in range(npair):
            cols = slice(p * LANES, (p + 1) * LANES)
            q2, k2, v2 = q_ref[0, :, cols], k_ref[0, :, cols], v_ref[0, :, cols]
            a_lane = [_gate_lane(1, p, 0, backward, h) for h in range(2)]
            b_lane = [_gate_lane(1, p, 1, backward, h) for h in range(2)]
            gam = [gbc[:, a:a + 1] for a in a_lane]
            beta = [g[:, b:b + 1] for b in b_lane]
            glast2 = jnp.where(head0[0:1, :], gbc[last:last + 1, a_lane[0]:a_lane[0] + 1],
                               gbc[last:last + 1, a_lane[1]:a_lane[1] + 1])
            gam2 = jnp.where(head0, gam[0], gam[1])
            beta2 = jnp.where(head0, beta[0], beta[1])
            egam2 = jnp.exp2(gam2)
            kf32 = k2.astype(F32)
            rhs = jnp.concatenate([v2.astype(F32) * beta2, kf32 * (beta2 * egam2)], axis=1).astype(BF16)
            decay = [jnp.exp2(jnp.where(incl, gam[h] - gbt[a_lane[h]:a_lane[h] + 1, :], -jnp.inf)) for h in range(2)]
            idx = 2 * int(backward) + p
            state = s_ref[idx]
            dirs.append(dict(q2=q2, k2=k2, kf32=kf32, rhs=rhs, decay=decay, beta=beta, strict=strict, gam2=gam2,
                             egam2=egam2, glast2=glast2, state=state, state_b=state.astype(BF16), o_ref=o_ref,
                             cols=cols, idx=idx, backward=backward))

    chains = [(d, h) for d in dirs for h in range(2)]

    def head_lanes(x, h):
        return jnp.where(head0 if h == 0 else ~head0, x, 0).astype(BF16)

    kk = [_dot_nt(head_lanes(d["k2"], h), d["k2"]) for d, h in chains]
    tick()
    ms = [jnp.where(d["strict"], d["beta"][h] * kkh * d["decay"][h], 0.0).astype(BF16)
          for (d, h), kkh in zip(chains, kk)]
    qk = [_dot_nt(head_lanes(d["q2"], h), d["k2"]) for d, h in chains]
    tick()
    a_intra = [(qkh * d["decay"][h]).astype(BF16) for (d, h), qkh in zip(chains, qk)]
    half = CHUNK // 2
    t_half = _unit_tri_inverses(ms, xor, tick, upto=half)
    y = [_dot(t, d["rhs"]) for (d, _), t in zip(chains, t_half)]
    tick()
    early = [slice(half, CHUNK) if d["backward"] else slice(0, half) for d, _ in chains]
    late = [slice(0, half) if d["backward"] else slice(half, CHUNK) for d, _ in chains]
    z = [_dot(m[lt, el], yj[el].astype(BF16)).astype(BF16) for m, yj, el, lt in zip(ms, y, early, late)]
    tick()
    corr = [_dot(t[lt, lt], zj) for t, zj, lt in zip(t_half, z, late)]
    tick()
    uw = [jnp.concatenate([yj[el], yj[lt] - c][::-1] if d["backward"] else [yj[el], yj[lt] - c], axis=0)
          for (d, _), yj, c, el, lt in zip(chains, y, corr, early, late)]

    uw2 = [jnp.where(sel0, uw[2 * i], uw[2 * i + 1]) for i in range(len(dirs))]
    qg = [(d["q2"].astype(F32) * d["egam2"]).astype(BF16) for d in dirs]
    zeros = jnp.zeros((LANES, LANES), BF16)
    w_state, q_state = [None] * len(dirs), [None] * len(dirs)
    for i in range(0, len(dirs), npair):
        group = range(i, i + npair)
        state_bd = jnp.concatenate([jnp.concatenate([dirs[j]["state_b"] if j == k else zeros for k in group], axis=1)
                                    for j in group], axis=0)
        ws_cat = _dot(jnp.concatenate([uw2[j][:, LANES:].astype(BF16) for j in group], axis=1), state_bd)
        qs_cat = _dot(jnp.concatenate([qg[j] for j in group], axis=1), state_bd)
        for n, j in enumerate(group):
            w_state[j] = ws_cat[:, n * LANES:(n + 1) * LANES]
            q_state[j] = qs_cat[:, n * LANES:(n + 1) * LANES]
    tick()
    v_new = [(uw2[i][:, :LANES] - w_state[i]).astype(BF16) for i in range(len(dirs))]
    av = [_dot(a_intra[j], v_new[j // 2]) for j in range(len(chains))]
    tick()
    upd = [_dot_tn((d["kf32"] * jnp.exp2(d["glast2"] - d["gam2"])).astype(BF16), v_new[i])
           for i, d in enumerate(dirs)]
    for i, d in enumerate(dirs):
        d["o_ref"][0, :, d["cols"]] = (q_state[i] + jnp.where(head0, av[2 * i], av[2 * i + 1])).astype(d["o_ref"].dtype)
        s_ref[d["idx"]] = d["state"] * jnp.exp2(d["glast2"]) + jnp.where(bd, upd[i], 0.0)


def _recurrent_kernel(bqf, bkf, bvf, cqf, ckf, cvf, gf, bqb, bkb, bvb, cqb, ckb, cvb, gb,
                      hf_ref, hb_ref, of_ref, ob_ref, ms_ref, mm_ref, gs_ref):
    @pl.when(pl.program_id(1) == 0)
    def _():
        ms_ref[...] = jnp.zeros_like(ms_ref)
        mm_ref[...] = jnp.zeros_like(mm_ref)
        gs_ref[...] = jnp.zeros_like(gs_ref)

    fwd_masks = _direction_masks(False)
    bwd_masks = _direction_masks(True)
    gf = (gf[0],) + _gate_cumsums(gf[0], fwd_masks[0], bwd_masks[0])
    gb = (gb[0],) + _gate_cumsums(gb[0], bwd_masks[0], fwd_masks[0])
    mlstm = _mlstm_stages(bqf, bkf, bvf, gf, bqb, bkb, bvb, gb, hf_ref, hb_ref, ms_ref, mm_ref, fwd_masks, bwd_masks)
    _gdn_step(cqf, ckf, cvf, gf, cqb, ckb, cvb, gb, of_ref, ob_ref, gs_ref, fwd_masks, bwd_masks,
              lambda: next(mlstm, None))
    for _ in mlstm:
        pass


def _recurrent_mixers(bqkv, cqkv, gates):
    bsz, seq, _ = bqkv.shape
    nc = seq // CHUNK
    nstate = 2 * (D_B // LANES)
    in_specs = []
    for rev in (False, True):
        cidx = (lambda c: nc - 1 - c) if rev else (lambda c: c)
        for width in (D_B, D_C):
            for part in range(3):
                in_specs.append(pl.BlockSpec((1, CHUNK, width), lambda b, c, part=part, cidx=cidx: (b, cidx(c), part)))
        in_specs.append(pl.BlockSpec((1, CHUNK, LANES), lambda b, c, cidx=cidx: (b, cidx(c), 0)))
    out_b = jax.ShapeDtypeStruct((bsz, seq, D_B), BF16)
    out_c = jax.ShapeDtypeStruct((bsz, seq, D_C), BF16)
    fwd_spec = lambda w: pl.BlockSpec((1, CHUNK, w), lambda b, c: (b, c, 0))
    bwd_spec = lambda w: pl.BlockSpec((1, CHUNK, w), lambda b, c: (b, nc - 1 - c, 0))
    return pl.pallas_call(
        _recurrent_kernel,
        grid=(bsz, nc),
        in_specs=in_specs,
        out_specs=[fwd_spec(D_B), bwd_spec(D_B), fwd_spec(D_C), bwd_spec(D_C)],
        out_shape=(out_b, out_b, out_c, out_c),
        scratch_shapes=[pltpu.VMEM((nstate, LANES, 2 * LANES), F32), pltpu.VMEM((nstate, SUBLANES, LANES), F32),
                        pltpu.VMEM((nstate, LANES, LANES), F32)],
        compiler_params=pltpu.CompilerParams(
            dimension_semantics=("parallel", "arbitrary"), vmem_limit_bytes=VMEM_LIMIT),
        name="recurrent_mixers",
    )(bqkv, bqkv, bqkv, cqkv, cqkv, cqkv, gates, bqkv, bqkv, bqkv, cqkv, cqkv, cqkv, gates)


def _out_kernel(x_ref, oa_ref, hf_ref, hb_ref, cf_ref, cb_ref, z_ref, mw_ref, gw_ref, wo_ref, y_ref):
    bd = _head_block_diag(MXU_WIDTH).astype(BF16)
    z = z_ref[0].astype(F32)

    def silu(t):
        return t * _sigmoid(t)

    hsum = _sigmoid(z[:, D_A:D_A + D_B]) * (hf_ref[0].astype(F32) + hb_ref[0].astype(F32))
    csum = cf_ref[0].astype(F32) + cb_ref[0].astype(F32)
    ssb = _head_sum(hsum * hsum, bd)
    ssc = _head_sum(csum * csum, bd)
    ya = oa_ref[0].astype(F32) * silu(z[:, 0:D_A])
    yb = hsum * lax.rsqrt(ssb * (1.0 / HEAD_DIM) + EPS) * mw_ref[...] * silu(z[:, D_A + D_B:D_A + 2 * D_B])
    yc = csum * lax.rsqrt(ssc * (1.0 / HEAD_DIM) + EPS) * gw_ref[...] * silu(z[:, D_A + 2 * D_B:])
    y = jnp.concatenate([ya, yb, yc], axis=1).astype(BF16)
    y_ref[0] = x_ref[0] + _dot(y, wo_ref[...])


def _out_proj(x, oa, hf, hb, cf, cb, z, mw, gw, wo):
    bsz, seq, d = x.shape
    tile = OUT_TILE
    row_spec = lambda a: pl.BlockSpec((1, tile, a.shape[-1]), lambda b, i: (b, i, 0))
    full = lambda a: pl.BlockSpec(a.shape, lambda b, i: (0,) * a.ndim)
    return pl.pallas_call(
        _out_kernel,
        grid=(bsz, seq // tile),
        in_specs=[row_spec(a) for a in (x, oa, hf, hb, cf, cb, z)] + [full(mw), full(gw), full(wo)],
        out_specs=pl.BlockSpec((1, tile, d), lambda b, i: (b, i, 0)),
        out_shape=jax.ShapeDtypeStruct(x.shape, x.dtype),
        compiler_params=pltpu.CompilerParams(
            dimension_semantics=("parallel", "parallel"), vmem_limit_bytes=VMEM_LIMIT),
        name="out_proj",
    )(x, oa, hf, hb, cf, cb, z, mw, gw, wo)


def _to_gate_lanes(t):
    lead = t.shape[:-3]
    t = t.reshape(lead + (4, 2, 2))
    return jnp.swapaxes(t, -3, -2).reshape(lead + (16,))


def _gate_weights(w_in):
    nh = D_B // HEAD_DIM
    per_mixer = [_to_gate_lanes(w_in[:, off:off + 4 * nh].reshape(-1, 2, 2, nh)) for off in (_OFF_BIF, _OFF_CAB)]
    return jnp.pad(jnp.concatenate(per_mixer, axis=1), ((0, 0), (0, LANES - 2 * 16)))


def _gate_params(i_bias, f_bias, a_log, dt_bias):
    zeros = jnp.zeros_like(dt_bias)
    row0 = jnp.concatenate([_to_gate_lanes(jnp.stack([i_bias, f_bias])), _to_gate_lanes(jnp.stack([dt_bias, zeros]))])
    row1 = jnp.concatenate([jnp.zeros((16,), F32), _to_gate_lanes(jnp.stack([a_log, zeros]))])
    return jnp.pad(jnp.stack([row0, row1]), ((0, SUBLANES - 2), (0, LANES - 2 * 16)))


def _layer(x, bias_tiles, norm_w, w_in, w_out, qk_norm_w, i_bias, f_bias, m_norm_w, conv_w, a_log, dt_bias,
           g_norm_w):
    wa = w_in[:, _OFF_AQ:_OFF_AZ].astype(BF16)
    wb = w_in[:, _OFF_BQ:_OFF_BIF].astype(BF16)
    wc = w_in[:, _OFF_CQ:_OFF_CAB].astype(BF16)
    wz = jnp.concatenate([w_in[:, _OFF_AZ:_OFF_BQ], w_in[:, _OFF_BO:_OFF_CQ], w_in[:, _OFF_CZ:]],
                         axis=1).astype(BF16)
    wg = _gate_weights(w_in).astype(BF16)
    qkw = jnp.pad(jnp.tile(qk_norm_w, (1, MXU_WIDTH // HEAD_DIM)) * jnp.array([[HEAD_DIM ** -0.5 * LOG2E], [1.0]], F32),
                  ((0, SUBLANES - 2), (0, 0)))
    gpar = _gate_params(i_bias, f_bias, a_log, dt_bias)
    cw = jnp.pad(conv_w, ((0, SUBLANES - CONV_W), (0, 0)))

    aq, *akv, bqkv, cqkv, gates, z = _in_proj(x, norm_w[None, :], wa, wb, wc, wg, wz, qkw, gpar, cw)
    oa = _attention(aq, akv, bias_tiles)
    hf, hb, cf, cb = _recurrent_mixers(bqkv, cqkv, gates)
    mw = jnp.tile(m_norm_w, D_B // HEAD_DIM)[None, :]
    gw = jnp.tile(g_norm_w, D_C // HEAD_DIM)[None, :]
    return _out_proj(x, oa, hf, hb, cf, cb, z, mw, gw, w_out.astype(BF16))


def kernel(x, norm_w, w_in, w_out, qk_norm_w, rel_bias, mlstm_i_bias, mlstm_f_bias, mlstm_norm_w, gdn_conv_w,
           gdn_a_log, gdn_dt_bias, gdn_norm_w):
    bias_tiles = _attn_bias_tiles(rel_bias)
    for l in range(DEPTH):
        x = _layer(x, bias_tiles, norm_w[l], w_in[l], w_out[l], qk_norm_w[l], mlstm_i_bias[l], mlstm_f_bias[l],
                   mlstm_norm_w[l], gdn_conv_w[l], gdn_a_log[l], gdn_dt_bias[l], gdn_norm_w[l])
    return x
```

```python
import functools

import numpy as np
import jax
import jax.numpy as jnp
from jax import lax
from jax.experimental import pallas as pl
from jax.experimental.pallas import tpu as pltpu

F32 = jnp.float32
BF16 = jnp.bfloat16

D_MODEL = 1024
DEPTH = 2
EPS = 1e-6
HEAD_DIM = 64
LANES = 128
SUBLANES = 8
MXU_WIDTH = 256
D_A, D_B, D_C = 512, 256, 256
DILATED_CFGS = ((128, 1), (512, 4), (2048, 16))
N_SIDE = 64
NUM_BUCKETS = 32
REL_MAX_DIST = 1024
CONV_W = 5
CHUNK = 256
Q_SUPER = 2048
Q_BLK = 128
K_WIN = 256
ATTN_ILP = 2
ATTN_DEPTH = 1
IN_TILE = 512
OUT_TILE = 512
NEG = -1e30
LOG2E = 1.4426950408889634
VMEM_LIMIT = 56 * 1024 * 1024

_OFF_AQ, _OFF_AZ = 0, 1536
_OFF_BQ, _OFF_BIF, _OFF_BO, _OFF_BZ = 2048, 2816, 2832, 3088
_OFF_CQ, _OFF_CAB, _OFF_CZ = 3344, 4112, 4128


def _dot(a, b):
    return jnp.dot(a, b, preferred_element_type=F32)


def _dot_nt(a, b):
    return lax.dot_general(a, b, (((1,), (1,)), ((), ())), preferred_element_type=F32)


def _dot_tn(a, b):
    return lax.dot_general(a, b, (((0,), (0,)), ((), ())), preferred_element_type=F32)


def _split(a):
    hi = a.astype(BF16)
    lo = (a - hi.astype(F32)).astype(BF16)
    return hi, lo


def _iota2(shape, axis):
    return lax.broadcasted_iota(jnp.int32, shape, axis)


def _pair_block_diag(rows, cols):
    r = _iota2((rows, cols), 0)
    c = _iota2((rows, cols), 1)
    return (r < HEAD_DIM) == ((c & HEAD_DIM) == 0)


def _head_block_diag(n):
    return (_iota2((n, n), 0) ^ _iota2((n, n), 1)) < HEAD_DIM


def _head_sum(t, bd):
    return _dot(t.astype(BF16), bd)


def _softplus(y):
    return jnp.maximum(y, 0.0) + jnp.log1p(jnp.exp(-jnp.abs(y)))


def _sigmoid(y):
    return 1.0 / (1.0 + jnp.exp(-y))


def _in_proj_kernel(x_ref, xp_ref, xn_ref, nw_ref, wa_ref, wb_ref, wc_ref, wg_ref, wz_ref,
                    qkw_ref, gpar_ref, cw_ref,
                    a_ref, kv1_ref, kv4_ref, kv16_ref, b_ref, c_ref, g_ref, z_ref, cext_ref, kvs_ref, hn_ref, kvg_ref):
    kv_refs = (kv1_ref, kv4_ref, kv16_ref)
    i = pl.program_id(1)
    n = pl.num_programs(1)
    tile = x_ref.shape[1]
    bd = _head_block_diag(MXU_WIDTH).astype(BF16)

    xe = jnp.concatenate([xp_ref[0], x_ref[0], xn_ref[0]], axis=0)
    ms = jnp.mean(xe * xe, axis=-1, keepdims=True)
    hne = xe * lax.rsqrt(ms + EPS) * nw_ref[...]
    hn_ref[...] = hne[SUBLANES:SUBLANES + tile].astype(BF16)
    hne = hne.astype(BF16)

    ce = _dot(hne, wc_ref[...])
    ta = [_dot(hn_ref[...],wa_ref[:, j * MXU_WIDTH:(j + 1) * MXU_WIDTH]) for j in range(3 * D_A // MXU_WIDTH)]
    tb = _dot(hn_ref[...],wb_ref[...])
    p = _dot(hn_ref[...],wg_ref[...]) + gpar_ref[0:1, :]
    zsplit = 2 * MXU_WIDTH
    z_ref[0, :, :zsplit] = _dot(hn_ref[...],wz_ref[:, :zsplit]).astype(z_ref.dtype)
    ssa = [_head_sum(t * t, bd) for t in ta[:4]]

    row = _iota2((tile + 2 * SUBLANES, 1), 0)
    valid = ((row >= SUBLANES) | (i > 0)) & ((row < tile + SUBLANES) | (i < n - 1))
    cext_ref[...] = jnp.where(valid, ce, 0.0)
    conv = cw_ref[0:1, :] * cext_ref[pl.ds(SUBLANES - CONV_W // 2, tile), :]
    for t in range(1, CONV_W):
        conv = conv + cw_ref[t:t + 1, :] * cext_ref[pl.ds(SUBLANES - CONV_W // 2 + t, tile), :]
    s = conv * _sigmoid(conv)
    tc = [s[:, j * MXU_WIDTH:(j + 1) * MXU_WIDTH] for j in range(3 * D_C // MXU_WIDTH)]

    ssc = [_head_sum(t * t, bd) for t in tc[:2]]
    z_ref[0, :, zsplit:] = _dot(hn_ref[...],wz_ref[:, zsplit:]).astype(z_ref.dtype)

    for j, t in enumerate(ta):
        if j < 4:
            t = t * lax.rsqrt(ssa[j] * (1.0 / HEAD_DIM) + EPS) * qkw_ref[j // 2:j // 2 + 1, :]
        if j < 2:
            a_ref[0, :, j * MXU_WIDTH:(j + 1) * MXU_WIDTH] = t
        else:
            for half in range(2):
                kvs_ref[2 * (j - 2) + half] = t[:, half * LANES:(half + 1) * LANES]
    for lt in range(kvs_ref.shape[0]):
        lanes = slice(lt * LANES, (lt + 1) * LANES)
        kv_refs[0][0, :, lanes] = kvs_ref[lt].astype(BF16)
        prev_dil, src = 1, [kvs_ref.at[lt]]
        for ci in range(1, len(DILATED_CFGS)):
            dil = DILATED_CFGS[ci][1]
            step = dil // prev_dil
            rows = tile // dil
            nxt = [None] * dil
            for r in range(dil):
                x = src[r % prev_dil][pl.ds(r // prev_dil, rows, stride=step), :]
                kv_refs[ci][0, r, :, lanes] = x.astype(BF16)
                if ci + 1 < len(DILATED_CFGS):
                    kvg_ref[lt, r] = x
                    nxt[r] = kvg_ref.at[lt, r]
            prev_dil, src = dil, nxt

    b_ref[0, :, 0:D_B] = (tb[:, 0:D_B] * HEAD_DIM ** -0.5).astype(b_ref.dtype)
    b_ref[0, :, D_B:3 * D_B] = tb[:, D_B:3 * D_B].astype(b_ref.dtype)

    lane = _iota2(p.shape, 1)
    first4 = (lane & 7) < 4
    is_b = lane < 16
    val_b = jnp.where(first4, p, -_softplus(-p)) * LOG2E
    val_c = jnp.where(first4, -jnp.exp(gpar_ref[1:2, :]) * _softplus(p) * LOG2E, _sigmoid(p))
    g_ref[0] = jnp.where(is_b, val_b, val_c)

    c_ref[0, :, 0:D_C] = (tc[0] * lax.rsqrt(ssc[0] + EPS) * HEAD_DIM ** -0.5).astype(c_ref.dtype)
    c_ref[0, :, D_C:2 * D_C] = (tc[1] * lax.rsqrt(ssc[1] + EPS)).astype(c_ref.dtype)
    c_ref[0, :, 2 * D_C:3 * D_C] = tc[2].astype(c_ref.dtype)


def _in_proj(x, nw, wa, wb, wc, wg, wz, qkw, gpar, cw):
    bsz, seq, d = x.shape
    tile = IN_TILE
    nt = seq // tile
    hb = tile // SUBLANES
    full = lambda a: pl.BlockSpec(a.shape, lambda b, i: (0,) * a.ndim)
    row_spec = lambda w: pl.BlockSpec((1, tile, w), lambda b, i: (b, i, 0))
    outs = [((bsz, seq, D_A), F32, row_spec(D_A))]
    for (_, dil) in DILATED_CFGS:
        if dil == 1:
            outs.append(((bsz, seq, 2 * D_A), BF16, row_spec(2 * D_A)))
        else:
            outs.append(((bsz, dil, seq // dil, 2 * D_A), BF16,
                         pl.BlockSpec((1, dil, tile // dil, 2 * D_A), lambda b, i: (b, 0, i, 0))))
    outs += [((bsz, seq, 3 * D_B), BF16, row_spec(3 * D_B)), ((bsz, seq, 3 * D_C), BF16, row_spec(3 * D_C)),
             ((bsz, seq, LANES), F32, row_spec(LANES)), ((bsz, seq, D_A + 3 * D_B), BF16, row_spec(D_A + 3 * D_B))]
    return pl.pallas_call(
        _in_proj_kernel,
        grid=(bsz, nt),
        in_specs=[
            row_spec(d),
            pl.BlockSpec((1, SUBLANES, d), lambda b, i: (b, jnp.maximum(i * hb - 1, 0), 0)),
            pl.BlockSpec((1, SUBLANES, d), lambda b, i: (b, jnp.minimum((i + 1) * hb, nt * hb - 1), 0)),
            full(nw), full(wa), full(wb), full(wc), full(wg), full(wz), full(qkw), full(gpar), full(cw),
        ],
        out_specs=[spec for _, _, spec in outs],
        out_shape=[jax.ShapeDtypeStruct(shape, dtype) for shape, dtype, _ in outs],
        scratch_shapes=[pltpu.VMEM((tile + 2 * SUBLANES, 3 * D_C), F32), pltpu.VMEM((2 * D_A // LANES, tile, LANES), F32),
                        pltpu.VMEM((tile, d), BF16),
                        pltpu.VMEM((2 * D_A // LANES, DILATED_CFGS[1][1], tile // DILATED_CFGS[1][1], LANES), F32)],
        compiler_params=pltpu.CompilerParams(
            dimension_semantics=("parallel", "arbitrary"), vmem_limit_bytes=VMEM_LIMIT),
        name="in_proj",
    )(x, x, x, nw, wa, wb, wc, wg, wz, qkw, gpar, cw)


def _t5_bucket(rel):
    half = NUM_BUCKETS // 2
    max_exact = half // 2
    n = np.abs(rel)
    large = max_exact + (np.log(np.maximum(n, 1) / max_exact) / np.log(REL_MAX_DIST / max_exact)
                         * (half - max_exact)).astype(np.int32)
    large = np.minimum(large, half - 1)
    return (rel > 0).astype(np.int32) * half + np.where(n < max_exact, n, large)


def _attn_bias_tiles(rel_bias):
    period = 640
    wide = K_WIN + 2 * N_SIDE
    lead = Q_BLK - 1 + N_SIDE
    cfg_tiles = []
    for (_, dil) in DILATED_CFGS:
        offs = dil * np.arange(-N_SIDE, N_SIDE + 1)
        bias = rel_bias[_t5_bucket(offs)].T.astype(F32) * LOG2E
        heads = bias.shape[0]
        vec = jnp.concatenate([jnp.full((heads, lead + N_SIDE), NEG, F32), bias,
                               jnp.full((heads, period - lead - 3 * N_SIDE - 1), NEG, F32)], axis=1)
        rows = jnp.tile(vec, (1, Q_BLK))[:, :Q_BLK * (period - 1)].reshape(heads, Q_BLK, period - 1)
        ext = rows[:, :, lead:lead + wide]
        variants = [ext[:, :, 2 * N_SIDE:2 * N_SIDE + K_WIN], ext[:, :, N_SIDE:N_SIDE + K_WIN],
                    ext[:, :, 0:K_WIN]]
        cfg_tiles.append(jnp.stack(variants, axis=1))
    return jnp.stack(cfg_tiles, axis=0)


def _attn_kernel(q_ref, *rest):
    ncfg = len(DILATED_CFGS)
    kv_refs = [(rest[2 * c], rest[2 * c + 1]) for c in range(ncfg)]
    bias_ref, o_ref, acc_ref, m_ref, l_ref = rest[2 * ncfg:]
    sb = pl.program_id(2)
    seq = kv_refs[0][0].shape[1]
    qsup = q_ref.shape[1]
    nblk = qsup // Q_BLK
    head0 = _iota2((Q_BLK, LANES), 1) < HEAD_DIM
    sel0 = (_iota2((Q_BLK, 2 * LANES), 1) & HEAD_DIM) == 0

    groups = [(pos, ci, g) for pos, ci in enumerate(reversed(range(len(DILATED_CFGS))))
              for g in range(nblk // ATTN_ILP)]

    def scores(pos, ci, g):
        dil = DILATED_CFGS[ci][1]
        n_idx = seq // dil
        blocks = []
        for u in range(ATTN_ILP):
            t = g * ATTN_ILP + u
            r, j = t % dil, t // dil
            qs = r + dil * (j * Q_BLK)
            i0 = sb * (qsup // dil) + j * Q_BLK
            ws = jnp.clip(i0 - N_SIDE, 0, n_idx - K_WIN)
            var = jnp.where(i0 == 0, 0, jnp.where(i0 == n_idx - Q_BLK, 2, 1))
            qrows = pl.ds(qs, Q_BLK) if dil == 1 else pl.ds(qs, Q_BLK, stride=dil)
            krows = pl.ds(pl.multiple_of(ws, N_SIDE), K_WIN)
            k_ref, v_ref = kv_refs[ci]
            if dil == 1:
                k2, v2 = k_ref[0, krows, :], v_ref[0, krows, :]
            else:
                k2, v2 = k_ref[0, r, krows, :], v_ref[0, r, krows, :]
            q2 = q_ref[0, qrows, :].astype(BF16)
            blocks.append(dict(qrows=qrows, var=var, k2=k2,
                               qh=[jnp.where(head0 if h == 0 else ~head0, q2, jnp.zeros_like(q2)) for h in range(2)],
                               vo=jnp.concatenate([v2, jnp.ones_like(v2)], axis=1)))
        chains = [(blk, h) for blk in blocks for h in range(2)]
        s = [_dot_nt(blk["qh"][h], blk["k2"]) + bias_ref[ci, h, blk["var"]] for blk, h in chains]
        mx = [jnp.max(sh, axis=-1, keepdims=True) for sh in s]
        p = [jnp.exp2(sh - m).astype(BF16) for sh, m in zip(s, mx)]
        return dict(pos=pos, blocks=blocks, chains=chains, mx=mx, p=p)

    def accumulate(st):
        pv = [_dot(ph, blk["vo"]) for ph, (blk, _) in zip(st["p"], st["chains"])]
        for u, blk in enumerate(st["blocks"]):
            qrows = blk["qrows"]
            tot = jnp.where(sel0, pv[2 * u], pv[2 * u + 1])
            num2 = tot[:, :LANES]
            l2 = tot[:, LANES:]
            m2 = jnp.where(head0, st["mx"][2 * u], st["mx"][2 * u + 1])
            if st["pos"] == 0:
                acc_ref[qrows, :] = num2
                m_ref[qrows, :] = m2
                l_ref[qrows, :] = l2
            else:
                mo = m_ref[qrows, :]
                mn = jnp.maximum(mo, m2)
                a = jnp.exp2(mo - mn)
                b = jnp.exp2(m2 - mn)
                acc_ref[qrows, :] = acc_ref[qrows, :] * a + num2 * b
                l_ref[qrows, :] = l_ref[qrows, :] * a + l2 * b
                m_ref[qrows, :] = mn

    in_flight = []
    for grp in groups:
        in_flight.append(scores(*grp))
        if len(in_flight) > ATTN_DEPTH:
            accumulate(in_flight.pop(0))
    for st in in_flight:
        accumulate(st)

    o_ref[0] = (acc_ref[...] / l_ref[...]).astype(o_ref.dtype)


def _attention(aq, akv, bias_tiles):
    bsz, seq, _ = aq.shape
    npair = D_A // LANES
    qsup = Q_SUPER
    assert seq % qsup == 0 and seq // DILATED_CFGS[-1][1] >= K_WIN
    in_specs = [pl.BlockSpec((1, qsup, LANES), lambda b, p, s: (b, s, p))]
    operands = [aq]
    for kv, (_, dil) in zip(akv, DILATED_CFGS):
        for part in range(2):
            if dil == 1:
                in_specs.append(pl.BlockSpec((1, seq, LANES), lambda b, p, s, part=part: (b, 0, part * npair + p)))
            else:
                in_specs.append(pl.BlockSpec((1, dil, seq // dil, LANES),
                                             lambda b, p, s, part=part: (b, 0, 0, part * npair + p)))
            operands.append(kv)
    in_specs.append(pl.BlockSpec((len(DILATED_CFGS), 2, 3, Q_BLK, K_WIN), lambda b, p, s: (0, p, 0, 0, 0)))
    return pl.pallas_call(
        _attn_kernel,
        grid=(bsz, npair, seq // qsup),
        in_specs=in_specs,
        out_specs=pl.BlockSpec((1, qsup, LANES), lambda b, p, s: (b, s, p)),
        out_shape=jax.ShapeDtypeStruct((bsz, seq, D_A), BF16),
        scratch_shapes=[pltpu.VMEM((qsup, LANES), F32)] * 3,
        compiler_params=pltpu.CompilerParams(
            dimension_semantics=("parallel", "parallel", "arbitrary"), vmem_limit_bytes=VMEM_LIMIT),
        name="dilated_attn",
    )(*operands, bias_tiles)


def _direction_masks(backward):
    row = _iota2((CHUNK, CHUNK), 0)
    col = _iota2((CHUNK, CHUNK), 1)
    if backward:
        return row <= col, row < col, row ^ col
    return row >= col, row > col, row ^ col


def _gate_cumsums(g, incl, incl_other):
    tri = incl.astype(BF16)
    tri_t = incl_other.astype(BF16)
    both = _dot(tri, jnp.concatenate(_split(g), axis=1))
    gb = both[:, :LANES] + both[:, LANES:]
    gt = g.T
    thi, tlo = _split(gt)
    gbt = _dot(thi, tri_t) + _dot(tlo, tri_t)
    return gb, gt, gbt


def _gate_lane(mixer, pair, kind, backward, head):
    return 16 * mixer + 8 * pair + 4 * kind + 2 * int(backward) + head


def _mlstm_stages(qf, kf, vf, gf, qb, kb, vb, gb, hf_ref, hb_ref, s_ref, m_ref, fwd_masks, bwd_masks):
    npair = D_B // LANES
    head0 = _iota2((CHUNK, LANES), 1) < HEAD_DIM
    sel0 = (_iota2((CHUNK, 2 * LANES), 1) & HEAD_DIM) == 0
    bd = _pair_block_diag(LANES, 2 * LANES)
    wide = lambda x: jnp.concatenate([x, x], axis=1)

    groups = []
    for backward, (q_ref, k_ref, v_ref, g_ref), h_ref in ((False, (qf, kf, vf, gf), hf_ref),
                                                          (True, (qb, kb, vb, gb), hb_ref)):
        incl = (bwd_masks if backward else fwd_masks)[0]
        incl_other = (fwd_masks if backward else bwd_masks)[0]
        last = 0 if backward else CHUNK - 1
        g, gbc, gt, gbt = g_ref
        i_lanes = [_gate_lane(0, p, 0, backward, h) for p in range(npair) for h in range(2)]
        f_lanes = [_gate_lane(0, p, 1, backward, h) for p in range(npair) for h in range(2)]
        for p in range(npair):
            idx = 2 * int(backward) + p
            heads = []
            for h in range(2):
                j = 2 * p + h
                li = jnp.broadcast_to(g[:, i_lanes[j]:i_lanes[j] + 1], (CHUNK, LANES))
                bc = jnp.broadcast_to(gbc[:, f_lanes[j]:f_lanes[j] + 1], (CHUNK, LANES))
                rrow = gt[i_lanes[j]:i_lanes[j] + 1, :] - gbt[f_lanes[j]:f_lanes[j] + 1, :]
                mst = m_ref[idx, h:h + 1, :]
                dmat = jnp.where(incl, wide(bc) + rrow, -jnp.inf)
                inter = bc + mst
                mt = jnp.maximum(jnp.max(dmat, axis=-1, keepdims=True), inter)
                blast = bc[last:last + 1, :]
                wlog = blast - bc + li
                mn = jnp.maximum(blast + mst, jnp.max(wlog, axis=0, keepdims=True))
                heads.append(dict(decay=jnp.exp2(dmat - wide(mt)), iw=jnp.exp2(inter - mt), emt=jnp.exp2(-mt),
                                  ws=jnp.exp2(wlog - mn), dec=jnp.exp2(blast + mst - mn), mnew=mn))
            cols = slice(p * LANES, (p + 1) * LANES)
            v2 = v_ref[0, :, cols]
            groups.append(dict(q2=q_ref[0, :, cols], k2=k_ref[0, :, cols], heads=heads, state=s_ref[idx], idx=idx,
                               h_ref=h_ref, cols=cols,
                               vo=jnp.concatenate([v2, jnp.ones_like(v2)], axis=1)))
            yield

    chains = [(d, h) for d in groups for h in range(2)]
    q_state = [_dot(d["q2"], d["state"].astype(BF16)) for d in groups]
    yield
    qk = [_dot_nt(jnp.where(head0 if h == 0 else ~head0, d["q2"], jnp.zeros_like(d["q2"])), d["k2"])
          for d, h in chains]
    yield
    sc = [(qkh * d["heads"][h]["decay"]).astype(BF16) for (d, h), qkh in zip(chains, qk)]
    yield
    pv = [_dot(sch, d["vo"]) for (d, _), sch in zip(chains, sc)]
    yield
    upd = [_dot_tn(d["k2"], (jnp.where(sel0, wide(d["heads"][0]["ws"]), wide(d["heads"][1]["ws"]))
                             * d["vo"].astype(F32)).astype(BF16)) for d in groups]
    yield
    for i, d in enumerate(groups):
        h0, h1 = d["heads"]
        tot = jnp.where(sel0, wide(h0["iw"]), wide(h1["iw"])) * q_state[i] + jnp.where(sel0, pv[2 * i], pv[2 * i + 1])
        den = jnp.maximum(jnp.abs(tot[:, LANES:]), jnp.where(head0, h0["emt"], h1["emt"]))
        d["h_ref"][0, :, d["cols"]] = (tot[:, :LANES] / den).astype(d["h_ref"].dtype)
        dec2 = jnp.where(sel0[0:1, :], wide(h0["dec"]), wide(h1["dec"]))
        s_ref[d["idx"]] = dec2 * d["state"] + jnp.where(bd, upd[i], 0.0)
        m_ref[d["idx"], 0:1, :] = h0["mnew"]
        m_ref[d["idx"], 1:2, :] = h1["mnew"]


_INV_BASE = 32


def _unit_tri_inverses(ms, xor, tick, upto):
    diag = xor == 0
    one = jnp.ones((), BF16)
    zero = jnp.zeros((), BF16)
    base = xor < _INV_BASE
    pbs = [jnp.where(base, m, zero) for m in ms]
    ts = [jnp.where(diag, one, -pb) for pb in pbs]
    k = 1
    while True:
        pbs = [_dot(pb, pb).astype(BF16) for pb in pbs]
        tick()
        k *= 2
        ts = [_dot(t, jnp.where(diag, one, pb)).astype(BF16) for t, pb in zip(ts, pbs)]
        tick()
        if 2 * k >= _INV_BASE:
            break
    b = _INV_BASE
    while b < upto:
        level = (xor >= b) & (xor < 2 * b)
        xs = [_dot(jnp.where(level, m, zero), t).astype(BF16) for m, t in zip(ms, ts)]
        tick()
        ts = [_dot(t, jnp.where(diag, one, -x)).astype(BF16) for t, x in zip(ts, xs)]
        tick()
        b *= 2
    return ts


def _gdn_step(qf, kf, vf, gf, qb, kb, vb, gb, of_ref, ob_ref, s_ref, fwd_masks, bwd_masks, tick):
    npair = D_C // LANES
    head0 = _iota2((CHUNK, LANES), 1) < HEAD_DIM
    sel0 = (_iota2((CHUNK, 2 * LANES), 1) & HEAD_DIM) == 0
    bd = _pair_block_diag(LANES, LANES)
    xor = fwd_masks[2]

    dirs = []
    for backward, (q_ref, k_ref, v_ref, g_ref), o_ref in ((False, (qf, kf, vf, gf), of_ref),
                                                          (True, (qb, kb, vb, gb), ob_ref)):
        incl, strict, _ = bwd_masks if backward else fwd_masks
        incl_other = (fwd_masks if backward else bwd_masks)[0]
        last = 0 if backward else CHUNK - 1
        g, gbc, _, gbt = g_ref
        for p in range(npair):
            cols = slice(p * LANES, (p + 1) * LANES)
            q2, k2, v2 = q_ref[0, :, cols], k_ref[0, :, cols], v_ref[0, :, cols]
            a_lane = [_gate_lane(1, p, 0, backward, h) for h in range(2)]
            b_lane = [_gate_lane(1, p, 1, backward, h) for h in range(2)]
            gam = [gbc[:, a:a + 1] for a in a_lane]
            beta = [g[:, b:b + 1] for b in b_lane]
            glast2 = jnp.where(head0[0:1, :], gbc[last:last + 1, a_lane[0]:a_lane[0] + 1],
                               gbc[last:last + 1, a_lane[1]:a_lane[1] + 1])
            gam2 = jnp.where(head0, gam[0], gam[1])
            beta2 = jnp.where(head0, beta[0], beta[1])
            egam2 = jnp.exp2(gam2)
            kf32 = k2.astype(F32)
            rhs = jnp.concatenate([v2.astype(F32) * beta2, kf32 * (beta2 * egam2)], axis=1).astype(BF16)
            decay = [jnp.exp2(jnp.where(incl, gam[h] - gbt[a_lane[h]:a_lane[h] + 1, :], -jnp.inf)) for h in range(2)]
            idx = 2 * int(backward) + p
            state = s_ref[idx]
            dirs.append(dict(q2=q2, k2=k2, kf32=kf32, rhs=rhs, decay=decay, beta=beta, strict=strict, gam2=gam2,
                             egam2=egam2, glast2=glast2, state=state, state_b=state.astype(BF16), o_ref=o_ref,
                             cols=cols, idx=idx, backward=backward))

    chains = [(d, h) for d in dirs for h in range(2)]

    def head_lanes(x, h):
        return jnp.where(head0 if h == 0 else ~head0, x, 0).astype(BF16)

    kk = [_dot_nt(head_lanes(d["k2"], h), d["k2"]) for d, h in chains]
    tick()
    ms = [jnp.where(d["strict"], d["beta"][h] * kkh * d["decay"][h], 0.0).astype(BF16)
          for (d, h), kkh in zip(chains, kk)]
    qk = [_dot_nt(head_lanes(d["q2"], h), d["k2"]) for d, h in chains]
    tick()
    a_intra = [(qkh * d["decay"][h]).astype(BF16) for (d, h), qkh in zip(chains, qk)]
    half = CHUNK // 2
    t_half = _unit_tri_inverses(ms, xor, tick, upto=half)
    y = [_dot(t, d["rhs"]) for (d, _), t in zip(chains, t_half)]
    tick()
    early = [slice(half, CHUNK) if d["backward"] else slice(0, half) for d, _ in chains]
    late = [slice(0, half) if d["backward"] else slice(half, CHUNK) for d, _ in chains]
    z = [_dot(m[lt, el], yj[el].astype(BF16)).astype(BF16) for m, yj, el, lt in zip(ms, y, early, late)]
    tick()
    corr = [_dot(t[lt, lt], zj) for t, zj, lt in zip(t_half, z, late)]
    tick()
    uw = [jnp.concatenate([yj[el], yj[lt] - c][::-1] if d["backward"] else [yj[el], yj[lt] - c], axis=0)
          for (d, _), yj, c, el, lt in zip(chains, y, corr, early, late)]

    uw2 = [jnp.where(sel0, uw[2 * i], uw[2 * i + 1]) for i in range(len(dirs))]
    qg = [(d["q2"].astype(F32) * d["egam2"]).astype(BF16) for d in dirs]
    zeros = jnp.zeros((LANES, LANES), BF16)
    w_state, q_state = [None] * len(dirs), [None] * len(dirs)
    for i in range(0, len(dirs), npair):
        group = range(i, i + npair)
        state_bd = jnp.concatenate([jnp.concatenate([dirs[j]["state_b"] if j == k else zeros for k in group], axis=1)
                                    for j in group], axis=0)
        ws_cat = _dot(jnp.concatenate([uw2[j][:, LANES:].astype(BF16) for j in group], axis=1), state_bd)
        qs_cat = _dot(jnp.concatenate([qg[j] for j in group], axis=1), state_bd)
        for n, j in enumerate(group):
            w_state[j] = ws_cat[:, n * LANES:(n + 1) * LANES]
            q_state[j] = qs_cat[:, n * LANES:(n + 1) * LANES]
    tick()
    v_new = [(uw2[i][:, :LANES] - w_state[i]).astype(BF16) for i in range(len(dirs))]
    av = [_dot(a_intra[j], v_new[j // 2]) for j in range(len(chains))]
    tick()
    upd = [_dot_tn((d["kf32"] * jnp.exp2(d["glast2"] - d["gam2"])).astype(BF16), v_new[i])
           for i, d in enumerate(dirs)]
    for i, d in enumerate(dirs):
        d["o_ref"][0, :, d["cols"]] = (q_state[i] + jnp.where(head0, av[2 * i], av[2 * i + 1])).astype(d["o_ref"].dtype)
        s_ref[d["idx"]] = d["state"] * jnp.exp2(d["glast2"]) + jnp.where(bd, upd[i], 0.0)


def _recurrent_kernel(bqf, bkf, bvf, cqf, ckf, cvf, gf, bqb, bkb, bvb, cqb, ckb, cvb, gb,
                      hf_ref, hb_ref, of_ref, ob_ref, ms_ref, mm_ref, gs_ref):
    @pl.when(pl.program_id(1) == 0)
    def _():
        ms_ref[...] = jnp.zeros_like(ms_ref)
        mm_ref[...] = jnp.zeros_like(mm_ref)
        gs_ref[...] = jnp.zeros_like(gs_ref)

    fwd_masks = _direction_masks(False)
    bwd_masks = _direction_masks(True)
    gf = (gf[0],) + _gate_cumsums(gf[0], fwd_masks[0], bwd_masks[0])
    gb = (gb[0],) + _gate_cumsums(gb[0], bwd_masks[0], fwd_masks[0])
    mlstm = _mlstm_stages(bqf, bkf, bvf, gf, bqb, bkb, bvb, gb, hf_ref, hb_ref, ms_ref, mm_ref, fwd_masks, bwd_masks)
    _gdn_step(cqf, ckf, cvf, gf, cqb, ckb, cvb, gb, of_ref, ob_ref, gs_ref, fwd_masks, bwd_masks,
              lambda: next(mlstm, None))
    for _ in mlstm:
        pass


def _recurrent_mixers(bqkv, cqkv, gates):
    bsz, seq, _ = bqkv.shape
    nc = seq // CHUNK
    nstate = 2 * (D_B // LANES)
    in_specs = []
    for rev in (False, True):
        cidx = (lambda c: nc - 1 - c) if rev else (lambda c: c)
        for width in (D_B, D_C):
            for part in range(3):
                in_specs.append(pl.BlockSpec((1, CHUNK, width), lambda b, c, part=part, cidx=cidx: (b, cidx(c), part)))
        in_specs.append(pl.BlockSpec((1, CHUNK, LANES), lambda b, c, cidx=cidx: (b, cidx(c), 0)))
    out_b = jax.ShapeDtypeStruct((bsz, seq, D_B), BF16)
    out_c = jax.ShapeDtypeStruct((bsz, seq, D_C), BF16)
    fwd_spec = lambda w: pl.BlockSpec((1, CHUNK, w), lambda b, c: (b, c, 0))
    bwd_spec = lambda w: pl.BlockSpec((1, CHUNK, w), lambda b, c: (b, nc - 1 - c, 0))
    return pl.pallas_call(
        _recurrent_kernel,
        grid=(bsz, nc),
        in_specs=in_specs,
        out_specs=[fwd_spec(D_B), bwd_spec(D_B), fwd_spec(D_C), bwd_spec(D_C)],
        out_shape=(out_b, out_b, out_c, out_c),
        scratch_shapes=[pltpu.VMEM((nstate, LANES, 2 * LANES), F32), pltpu.VMEM((nstate, SUBLANES, LANES), F32),
                        pltpu.VMEM((nstate, LANES, LANES), F32)],
        compiler_params=pltpu.CompilerParams(
            dimension_semantics=("parallel", "arbitrary"), vmem_limit_bytes=VMEM_LIMIT),
        name="recurrent_mixers",
    )(bqkv, bqkv, bqkv, cqkv, cqkv, cqkv, gates, bqkv, bqkv, bqkv, cqkv, cqkv, cqkv, gates)


def _out_kernel(x_ref, oa_ref, hf_ref, hb_ref, cf_ref, cb_ref, z_ref, mw_ref, gw_ref, wo_ref, y_ref):
    bd = _head_block_diag(MXU_WIDTH).astype(BF16)
    z = z_ref[0].astype(F32)

    def silu(t):
        return t * _sigmoid(t)

    hsum = _sigmoid(z[:, D_A:D_A + D_B]) * (hf_ref[0].astype(F32) + hb_ref[0].astype(F32))
    csum = cf_ref[0].astype(F32) + cb_ref[0].astype(F32)
    ssb = _head_sum(hsum * hsum, bd)
    ssc = _head_sum(csum * csum, bd)
    ya = oa_ref[0].astype(F32) * silu(z[:, 0:D_A])
    yb = hsum * lax.rsqrt(ssb * (1.0 / HEAD_DIM) + EPS) * mw_ref[...] * silu(z[:, D_A + D_B:D_A + 2 * D_B])
    yc = csum * lax.rsqrt(ssc * (1.0 / HEAD_DIM) + EPS) * gw_ref[...] * silu(z[:, D_A + 2 * D_B:])
    y = jnp.concatenate([ya, yb, yc], axis=1).astype(BF16)
    y_ref[0] = x_ref[0] + _dot(y, wo_ref[...])


def _out_proj(x, oa, hf, hb, cf, cb, z, mw, gw, wo):
    bsz, seq, d = x.shape
    tile = OUT_TILE
    row_spec = lambda a: pl.BlockSpec((1, tile, a.shape[-1]), lambda b, i: (b, i, 0))
    full = lambda a: pl.BlockSpec(a.shape, lambda b, i: (0,) * a.ndim)
    return pl.pallas_call(
        _out_kernel,
        grid=(bsz, seq // tile),
        in_specs=[row_spec(a) for a in (x, oa, hf, hb, cf, cb, z)] + [full(mw), full(gw), full(wo)],
        out_specs=pl.BlockSpec((1, tile, d), lambda b, i: (b, i, 0)),
        out_shape=jax.ShapeDtypeStruct(x.shape, x.dtype),
        compiler_params=pltpu.CompilerParams(
            dimension_semantics=("parallel", "parallel"), vmem_limit_bytes=VMEM_LIMIT),
        name="out_proj",
    )(x, oa, hf, hb, cf, cb, z, mw, gw, wo)


def _to_gate_lanes(t):
    lead = t.shape[:-3]
    t = t.reshape(lead + (4, 2, 2))
    return jnp.swapaxes(t, -3, -2).reshape(lead + (16,))


def _gate_weights(w_in):
    nh = D_B // HEAD_DIM
    per_mixer = [_to_gate_lanes(w_in[:, off:off + 4 * nh].reshape(-1, 2, 2, nh)) for off in (_OFF_BIF, _OFF_CAB)]
    return jnp.pad(jnp.concatenate(per_mixer, axis=1), ((0, 0), (0, LANES - 2 * 16)))


def _gate_params(i_bias, f_bias, a_log, dt_bias):
    zeros = jnp.zeros_like(dt_bias)
    row0 = jnp.concatenate([_to_gate_lanes(jnp.stack([i_bias, f_bias])), _to_gate_lanes(jnp.stack([dt_bias, zeros]))])
    row1 = jnp.concatenate([jnp.zeros((16,), F32), _to_gate_lanes(jnp.stack([a_log, zeros]))])
    return jnp.pad(jnp.stack([row0, row1]), ((0, SUBLANES - 2), (0, LANES - 2 * 16)))


def _layer(x, bias_tiles, norm_w, w_in, w_out, qk_norm_w, i_bias, f_bias, m_norm_w, conv_w, a_log, dt_bias,
           g_norm_w):
    wa = w_in[:, _OFF_AQ:_OFF_AZ].astype(BF16)
    wb = w_in[:, _OFF_BQ:_OFF_BIF].astype(BF16)
    wc = w_in[:, _OFF_CQ:_OFF_CAB].astype(BF16)
    wz = jnp.concatenate([w_in[:, _OFF_AZ:_OFF_BQ], w_in[:, _OFF_BO:_OFF_CQ], w_in[:, _OFF_CZ:]],
                         axis=1).astype(BF16)
    wg = _gate_weights(w_in).astype(BF16)
    qkw = jnp.pad(jnp.tile(qk_norm_w, (1, MXU_WIDTH // HEAD_DIM)) * jnp.array([[HEAD_DIM ** -0.5 * LOG2E], [1.0]], F32),
                  ((0, SUBLANES - 2), (0, 0)))
    gpar = _gate_params(i_bias, f_bias, a_log, dt_bias)
    cw = jnp.pad(conv_w, ((0, SUBLANES - CONV_W), (0, 0)))

    aq, *akv, bqkv, cqkv, gates, z = _in_proj(x, norm_w[None, :], wa, wb, wc, wg, wz, qkw, gpar, cw)
    oa = _attention(aq, akv, bias_tiles)
    hf, hb, cf, cb = _recurrent_mixers(bqkv, cqkv, gates)
    mw = jnp.tile(m_norm_w, D_B // HEAD_DIM)[None, :]
    gw = jnp.tile(g_norm_w, D_C // HEAD_DIM)[None, :]
    return _out_proj(x, oa, hf, hb, cf, cb, z, mw, gw, w_out.astype(BF16))


def kernel(x, norm_w, w_in, w_out, qk_norm_w, rel_bias, mlstm_i_bias, mlstm_f_bias, mlstm_norm_w, gdn_conv_w,
           gdn_a_log, gdn_dt_bias, gdn_norm_w):
    bias_tiles = _attn_bias_tiles(rel_bias)
    for l in range(DEPTH):
        x = _layer(x, bias_tiles, norm_w[l], w_in[l], w_out[l], qk_norm_w[l], mlstm_i_bias[l], mlstm_f_bias[l],
                   mlstm_norm_w[l], gdn_conv_w[l], gdn_a_log[l], gdn_dt_bias[l], gdn_norm_w[l])
    return x
```

```python
import functools

import numpy as np
import jax
import jax.numpy as jnp
from jax import lax
from jax.experimental import pallas as pl
from jax.experimental.pallas import tpu as pltpu

F32 = jnp.float32
BF16 = jnp.bfloat16

D_MODEL = 1024
DEPTH = 2
EPS = 1e-6
HEAD_DIM = 64
LANES = 128
SUBLANES = 8
MXU_WIDTH = 256
D_A, D_B, D_C = 512, 256, 256
DILATED_CFGS = ((128, 1), (512, 4), (2048, 16))
N_SIDE = 64
NUM_BUCKETS = 32
REL_MAX_DIST = 1024
CONV_W = 5
CHUNK = 256
Q_SUPER = 2048
Q_BLK = 128
K_WIN = 256
ATTN_ILP = 2
ATTN_DEPTH = 1
IN_TILE = 512
OUT_TILE = 1024
NEG = -1e30
LOG2E = 1.4426950408889634
VMEM_LIMIT = 56 * 1024 * 1024

_OFF_AQ, _OFF_AZ = 0, 1536
_OFF_BQ, _OFF_BIF, _OFF_BO, _OFF_BZ = 2048, 2816, 2832, 3088
_OFF_CQ, _OFF_CAB, _OFF_CZ = 3344, 4112, 4128
_W_SPLITS = (0, 3 * D_A, 3 * D_A + 3 * D_B, 3 * D_A + 3 * D_B + 3 * D_C, 3 * D_A + 3 * D_B + 3 * D_C + LANES,
             3 * D_A + 3 * D_B + 3 * D_C + LANES + D_A + 3 * D_B)


def _dot(a, b):
    return jnp.dot(a, b, preferred_element_type=F32)


def _dot_nt(a, b):
    return lax.dot_general(a, b, (((1,), (1,)), ((), ())), preferred_element_type=F32)


def _dot_tn(a, b):
    return lax.dot_general(a, b, (((0,), (0,)), ((), ())), preferred_element_type=F32)


def _split(a):
    hi = a.astype(BF16)
    lo = (a - hi.astype(F32)).astype(BF16)
    return hi, lo


def _iota2(shape, axis):
    return lax.broadcasted_iota(jnp.int32, shape, axis)


def _pair_block_diag(rows, cols):
    r = _iota2((rows, cols), 0)
    c = _iota2((rows, cols), 1)
    return (r < HEAD_DIM) == ((c & HEAD_DIM) == 0)


def _head_block_diag(n):
    return (_iota2((n, n), 0) ^ _iota2((n, n), 1)) < HEAD_DIM


def _head_sum(t, bd):
    return _dot(t.astype(BF16), bd)


def _softplus(y):
    return jnp.maximum(y, 0.0) + jnp.log1p(jnp.exp(-jnp.abs(y)))


def _sigmoid(y):
    return 1.0 / (1.0 + jnp.exp(-y))


def _in_proj_kernel(x_ref, xp_ref, xn_ref, nw_ref, w_ref, qkw_ref, gpar_ref, cw_ref,
                    a_ref, kv1_ref, kv4_ref, kv16_ref, b_ref, c_ref, g_ref, z_ref, cext_ref, kvs_ref, hn_ref, kvg_ref):
    kv_refs = (kv1_ref, kv4_ref, kv16_ref)
    wa_ref, wb_ref, wc_ref, wg_ref, wz_ref = (w_ref.at[:, lo:hi] for lo, hi in zip(_W_SPLITS[:-1], _W_SPLITS[1:]))
    i = pl.program_id(1)
    n = pl.num_programs(1)
    tile = x_ref.shape[1]
    bd = _head_block_diag(MXU_WIDTH).astype(BF16)

    xe = jnp.concatenate([xp_ref[0], x_ref[0], xn_ref[0]], axis=0)
    ms = jnp.mean(xe * xe, axis=-1, keepdims=True)
    hne = xe * lax.rsqrt(ms + EPS) * nw_ref[...]
    hn_ref[...] = hne[SUBLANES:SUBLANES + tile].astype(BF16)
    hne = hne.astype(BF16)

    ce = _dot(hne, wc_ref[...])
    ta = [_dot(hn_ref[...],wa_ref[:, j * MXU_WIDTH:(j + 1) * MXU_WIDTH]) for j in range(3 * D_A // MXU_WIDTH)]
    tb = _dot(hn_ref[...],wb_ref[...])
    p = _dot(hn_ref[...],wg_ref[...]) + gpar_ref[0:1, :]
    zsplit = 2 * MXU_WIDTH
    z_ref[0, :, :zsplit] = _dot(hn_ref[...],wz_ref[:, :zsplit]).astype(z_ref.dtype)
    ssa = [_head_sum(t * t, bd) for t in ta[:4]]

    row = _iota2((tile + 2 * SUBLANES, 1), 0)
    valid = ((row >= SUBLANES) | (i > 0)) & ((row < tile + SUBLANES) | (i < n - 1))
    cext_ref[...] = jnp.where(valid, ce, 0.0)
    conv = cw_ref[0:1, :] * cext_ref[pl.ds(SUBLANES - CONV_W // 2, tile), :]
    for t in range(1, CONV_W):
        conv = conv + cw_ref[t:t + 1, :] * cext_ref[pl.ds(SUBLANES - CONV_W // 2 + t, tile), :]
    s = conv * _sigmoid(conv)
    tc = [s[:, j * MXU_WIDTH:(j + 1) * MXU_WIDTH] for j in range(3 * D_C // MXU_WIDTH)]

    ssc = [_head_sum(t * t, bd) for t in tc[:2]]
    z_ref[0, :, zsplit:] = _dot(hn_ref[...],wz_ref[:, zsplit:]).astype(z_ref.dtype)

    for j, t in enumerate(ta):
        if j < 4:
            t = t * lax.rsqrt(ssa[j] * (1.0 / HEAD_DIM) + EPS) * qkw_ref[j // 2:j // 2 + 1, :]
        if j < 2:
            a_ref[0, :, j * MXU_WIDTH:(j + 1) * MXU_WIDTH] = t
        else:
            for half in range(2):
                kvs_ref[2 * (j - 2) + half] = t[:, half * LANES:(half + 1) * LANES]
    for lt in range(kvs_ref.shape[0]):
        lanes = slice(lt * LANES, (lt + 1) * LANES)
        kv_refs[0][0, :, lanes] = kvs_ref[lt].astype(BF16)
        prev_dil, src = 1, [kvs_ref.at[lt]]
        for ci in range(1, len(DILATED_CFGS)):
            dil = DILATED_CFGS[ci][1]
            step = dil // prev_dil
            rows = tile // dil
            nxt = [None] * dil
            for r in range(dil):
                x = src[r % prev_dil][pl.ds(r // prev_dil, rows, stride=step), :]
                kv_refs[ci][0, r, :, lanes] = x.astype(BF16)
                if ci + 1 < len(DILATED_CFGS):
                    kvg_ref[lt, r] = x
                    nxt[r] = kvg_ref.at[lt, r]
            prev_dil, src = dil, nxt

    b_ref[0, :, 0:D_B] = (tb[:, 0:D_B] * HEAD_DIM ** -0.5).astype(b_ref.dtype)
    b_ref[0, :, D_B:3 * D_B] = tb[:, D_B:3 * D_B].astype(b_ref.dtype)

    lane = _iota2(p.shape, 1)
    first4 = (lane & 7) < 4
    is_b = lane < 16
    val_b = jnp.where(first4, p, -_softplus(-p)) * LOG2E
    val_c = jnp.where(first4, -jnp.exp(gpar_ref[1:2, :]) * _softplus(p) * LOG2E, _sigmoid(p))
    g_ref[0] = jnp.where(is_b, val_b, val_c)

    c_ref[0, :, 0:D_C] = (tc[0] * lax.rsqrt(ssc[0] + EPS) * HEAD_DIM ** -0.5).astype(c_ref.dtype)
    c_ref[0, :, D_C:2 * D_C] = (tc[1] * lax.rsqrt(ssc[1] + EPS)).astype(c_ref.dtype)
    c_ref[0, :, 2 * D_C:3 * D_C] = tc[2].astype(c_ref.dtype)


def _in_proj(x, nw, w_all, qkw, gpar, cw):
    bsz, seq, d = x.shape
    tile = IN_TILE
    nt = seq // tile
    hb = tile // SUBLANES
    full = lambda a: pl.BlockSpec(a.shape, lambda b, i: (0,) * a.ndim)
    row_spec = lambda w: pl.BlockSpec((1, tile, w), lambda b, i: (b, i, 0))
    outs = [((bsz, seq, D_A), F32, row_spec(D_A))]
    for (_, dil) in DILATED_CFGS:
        if dil == 1:
            outs.append(((bsz, seq, 2 * D_A), BF16, row_spec(2 * D_A)))
        else:
            outs.append(((bsz, dil, seq // dil, 2 * D_A), BF16,
                         pl.BlockSpec((1, dil, tile // dil, 2 * D_A), lambda b, i: (b, 0, i, 0))))
    outs += [((bsz, seq, 3 * D_B), BF16, row_spec(3 * D_B)), ((bsz, seq, 3 * D_C), BF16, row_spec(3 * D_C)),
             ((bsz, seq, LANES), F32, row_spec(LANES)), ((bsz, seq, D_A + 3 * D_B), BF16, row_spec(D_A + 3 * D_B))]
    return pl.pallas_call(
        _in_proj_kernel,
        grid=(bsz, nt),
        in_specs=[
            row_spec(d),
            pl.BlockSpec((1, SUBLANES, d), lambda b, i: (b, jnp.maximum(i * hb - 1, 0), 0)),
            pl.BlockSpec((1, SUBLANES, d), lambda b, i: (b, jnp.minimum((i + 1) * hb, nt * hb - 1), 0)),
            full(nw), full(w_all), full(qkw), full(gpar), full(cw),
        ],
        out_specs=[spec for _, _, spec in outs],
        out_shape=[jax.ShapeDtypeStruct(shape, dtype) for shape, dtype, _ in outs],
        scratch_shapes=[pltpu.VMEM((tile + 2 * SUBLANES, 3 * D_C), F32), pltpu.VMEM((2 * D_A // LANES, tile, LANES), F32),
                        pltpu.VMEM((tile, d), BF16),
                        pltpu.VMEM((2 * D_A // LANES, DILATED_CFGS[1][1], tile // DILATED_CFGS[1][1], LANES), F32)],
        compiler_params=pltpu.CompilerParams(
            dimension_semantics=("parallel", "arbitrary"), vmem_limit_bytes=VMEM_LIMIT),
        name="in_proj",
    )(x, x, x, nw, w_all, qkw, gpar, cw)


def _t5_bucket(rel):
    half = NUM_BUCKETS // 2
    max_exact = half // 2
    n = np.abs(rel)
    large = max_exact + (np.log(np.maximum(n, 1) / max_exact) / np.log(REL_MAX_DIST / max_exact)
                         * (half - max_exact)).astype(np.int32)
    large = np.minimum(large, half - 1)
    return (rel > 0).astype(np.int32) * half + np.where(n < max_exact, n, large)


def _attn_bias_tiles(rel_bias):
    period = 5 * LANES + 1
    wide = K_WIN + 2 * N_SIDE
    lead = Q_BLK - 1 + N_SIDE
    cfg_tiles = []
    for (_, dil) in DILATED_CFGS:
        offs = dil * np.arange(-N_SIDE, N_SIDE + 1)
        bias = rel_bias[_t5_bucket(offs)].T.astype(F32) * LOG2E
        heads = bias.shape[0]
        vec = jnp.concatenate([jnp.full((heads, lead + N_SIDE), NEG, F32), bias,
                               jnp.full((heads, period - lead - 3 * N_SIDE - 1), NEG, F32)], axis=1)
        rows = jnp.tile(vec, (1, Q_BLK))[:, :Q_BLK * (period - 1)].reshape(heads, Q_BLK, period - 1)
        ext = rows[:, :, lead:lead + wide]
        variants = [ext[:, :, 2 * N_SIDE:2 * N_SIDE + K_WIN], ext[:, :, N_SIDE:N_SIDE + K_WIN],
                    ext[:, :, 0:K_WIN]]
        cfg_tiles.append(jnp.stack(variants, axis=1))
    return jnp.stack(cfg_tiles, axis=0)


def _attn_kernel(q_ref, *rest):
    ncfg = len(DILATED_CFGS)
    kv_refs = [(rest[2 * c], rest[2 * c + 1]) for c in range(ncfg)]
    bias_ref, o_ref, acc_ref, m_ref, l_ref = rest[2 * ncfg:]
    sb = pl.program_id(2)
    seq = kv_refs[0][0].shape[1]
    qsup = q_ref.shape[1]
    nblk = qsup // Q_BLK
    head0 = _iota2((Q_BLK, LANES), 1) < HEAD_DIM
    sel0 = (_iota2((Q_BLK, 2 * LANES), 1) & HEAD_DIM) == 0

    groups = [(pos, ci, g) for pos, ci in enumerate(reversed(range(len(DILATED_CFGS))))
              for g in range(nblk // ATTN_ILP)]

    def scores(pos, ci, g):
        dil = DILATED_CFGS[ci][1]
        n_idx = seq // dil
        blocks = []
        for u in range(ATTN_ILP):
            t = g * ATTN_ILP + u
            r, j = t % dil, t // dil
            qs = r + dil * (j * Q_BLK)
            i0 = sb * (qsup // dil) + j * Q_BLK
            ws = jnp.clip(i0 - N_SIDE, 0, n_idx - K_WIN)
            var = jnp.where(i0 == 0, 0, jnp.where(i0 == n_idx - Q_BLK, 2, 1))
            qrows = pl.ds(qs, Q_BLK) if dil == 1 else pl.ds(qs, Q_BLK, stride=dil)
            krows = pl.ds(pl.multiple_of(ws, N_SIDE), K_WIN)
            k_ref, v_ref = kv_refs[ci]
            if dil == 1:
                k2, v2 = k_ref[0, krows, :], v_ref[0, krows, :]
            else:
                k2, v2 = k_ref[0, r, krows, :], v_ref[0, r, krows, :]
            q2 = q_ref[0, qrows, :].astype(BF16)
            blocks.append(dict(qrows=qrows, var=var, k2=k2,
                               qh=[jnp.where(head0 if h == 0 else ~head0, q2, jnp.zeros_like(q2)) for h in range(2)],
                               vo=jnp.concatenate([v2, jnp.ones_like(v2)], axis=1)))
        chains = [(blk, h) for blk in blocks for h in range(2)]
        s = [_dot_nt(blk["qh"][h], blk["k2"]) + bias_ref[ci, h, blk["var"]] for blk, h in chains]
        mx = [jnp.max(sh, axis=-1, keepdims=True) for sh in s]
        p = [jnp.exp2(sh - m).astype(BF16) for sh, m in zip(s, mx)]
        return dict(pos=pos, blocks=blocks, chains=chains, mx=mx, p=p)

    def accumulate(st):
        pv = [_dot(ph, blk["vo"]) for ph, (blk, _) in zip(st["p"], st["chains"])]
        for u, blk in enumerate(st["blocks"]):
            qrows = blk["qrows"]
            tot = jnp.where(sel0, pv[2 * u], pv[2 * u + 1])
            num2 = tot[:, :LANES]
            l2 = tot[:, LANES:]
            m2 = jnp.where(head0, st["mx"][2 * u], st["mx"][2 * u + 1])
            if st["pos"] == 0:
                acc_ref[qrows, :] = num2
                m_ref[qrows, :] = m2
                l_ref[qrows, :] = l2
            else:
                mo = m_ref[qrows, :]
                mn = jnp.maximum(mo, m2)
                a = jnp.exp2(mo - mn)
                b = jnp.exp2(m2 - mn)
                acc_ref[qrows, :] = acc_ref[qrows, :] * a + num2 * b
                l_ref[qrows, :] = l_ref[qrows, :] * a + l2 * b
                m_ref[qrows, :] = mn

    in_flight = []
    for grp in groups:
        in_flight.append(scores(*grp))
        if len(in_flight) > ATTN_DEPTH:
            accumulate(in_flight.pop(0))
    for st in in_flight:
        accumulate(st)

    o_ref[0] = (acc_ref[...] / l_ref[...]).astype(o_ref.dtype)


def _attention(aq, akv, bias_tiles):
    bsz, seq, _ = aq.shape
    npair = D_A // LANES
    qsup = Q_SUPER
    assert seq % qsup == 0 and seq // DILATED_CFGS[-1][1] >= K_WIN
    in_specs = [pl.BlockSpec((1, qsup, LANES), lambda b, p, s: (b, s, p))]
    operands = [aq]
    for kv, (_, dil) in zip(akv, DILATED_CFGS):
        for part in range(2):
            if dil == 1:
                in_specs.append(pl.BlockSpec((1, seq, LANES), lambda b, p, s, part=part: (b, 0, part * npair + p)))
            else:
                in_specs.append(pl.BlockSpec((1, dil, seq // dil, LANES),
                                             lambda b, p, s, part=part: (b, 0, 0, part * npair + p)))
            operands.append(kv)
    in_specs.append(pl.BlockSpec((len(DILATED_CFGS), 2, 3, Q_BLK, K_WIN), lambda b, p, s: (0, p, 0, 0, 0)))
    return pl.pallas_call(
        _attn_kernel,
        grid=(bsz, npair, seq // qsup),
        in_specs=in_specs,
        out_specs=pl.BlockSpec((1, qsup, LANES), lambda b, p, s: (b, s, p)),
        out_shape=jax.ShapeDtypeStruct((bsz, seq, D_A), BF16),
        scratch_shapes=[pltpu.VMEM((qsup, LANES), F32)] * 3,
        compiler_params=pltpu.CompilerParams(
            dimension_semantics=("parallel", "parallel", "arbitrary"), vmem_limit_bytes=VMEM_LIMIT),
        name="dilated_attn",
    )(*operands, bias_tiles)


def _direction_masks(backward):
    row = _iota2((CHUNK, CHUNK), 0)
    col = _iota2((CHUNK, CHUNK), 1)
    if backward:
        return row <= col, row < col, row ^ col
    return row >= col, row > col, row ^ col


def _gate_cumsums(g, incl, incl_other):
    tri = incl.astype(BF16)
    tri_t = incl_other.astype(BF16)
    both = _dot(tri, jnp.concatenate(_split(g), axis=1))
    gb = both[:, :LANES] + both[:, LANES:]
    gt = g.T
    thi, tlo = _split(gt)
    gbt = _dot(thi, tri_t) + _dot(tlo, tri_t)
    return gb, gt, gbt


def _gate_lane(mixer, pair, kind, backward, head):
    return 16 * mixer + 8 * pair + 4 * kind + 2 * int(backward) + head


def _mlstm_stages(qf, kf, vf, gf, qb, kb, vb, gb, hf_ref, hb_ref, s_ref, m_ref, fwd_masks, bwd_masks):
    npair = D_B // LANES
    head0 = _iota2((CHUNK, LANES), 1) < HEAD_DIM
    sel0 = (_iota2((CHUNK, 2 * LANES), 1) & HEAD_DIM) == 0
    bd = _pair_block_diag(LANES, 2 * LANES)
    wide = lambda x: jnp.concatenate([x, x], axis=1)

    groups = []
    for backward, (q_ref, k_ref, v_ref, g_ref), h_ref in ((False, (qf, kf, vf, gf), hf_ref),
                                                          (True, (qb, kb, vb, gb), hb_ref)):
        incl = (bwd_masks if backward else fwd_masks)[0]
        incl_other = (fwd_masks if backward else bwd_masks)[0]
        last = 0 if backward else CHUNK - 1
        g, gbc, gt, gbt = g_ref
        i_lanes = [_gate_lane(0, p, 0, backward, h) for p in range(npair) for h in range(2)]
        f_lanes = [_gate_lane(0, p, 1, backward, h) for p in range(npair) for h in range(2)]
        for p in range(npair):
            idx = 2 * int(backward) + p
            heads = []
            for h in range(2):
                j = 2 * p + h
                li = jnp.broadcast_to(g[:, i_lanes[j]:i_lanes[j] + 1], (CHUNK, LANES))
                bc = jnp.broadcast_to(gbc[:, f_lanes[j]:f_lanes[j] + 1], (CHUNK, LANES))
                rrow = gt[i_lanes[j]:i_lanes[j] + 1, :] - gbt[f_lanes[j]:f_lanes[j] + 1, :]
                mst = m_ref[idx, h:h + 1, :]
                dmat = jnp.where(incl, wide(bc) + rrow, -jnp.inf)
                inter = bc + mst
                mt = jnp.maximum(jnp.max(dmat, axis=-1, keepdims=True), inter)
                blast = bc[last:last + 1, :]
                wlog = blast - bc + li
                mn = jnp.maximum(blast + mst, jnp.max(wlog, axis=0, keepdims=True))
                heads.append(dict(decay=jnp.exp2(dmat - wide(mt)), iw=jnp.exp2(inter - mt), emt=jnp.exp2(-mt),
                                  ws=jnp.exp2(wlog - mn), dec=jnp.exp2(blast + mst - mn), mnew=mn))
            cols = slice(p * LANES, (p + 1) * LANES)
            v2 = v_ref[0, :, cols]
            groups.append(dict(q2=q_ref[0, :, cols], k2=k_ref[0, :, cols], heads=heads, state=s_ref[idx], idx=idx,
                               h_ref=h_ref, cols=cols,
                               vo=jnp.concatenate([v2, jnp.ones_like(v2)], axis=1)))
            yield

    chains = [(d, h) for d in groups for h in range(2)]
    q_state = [_dot(d["q2"], d["state"].astype(BF16)) for d in groups]
    yield
    qk = [_dot_nt(jnp.where(head0 if h == 0 else ~head0, d["q2"], jnp.zeros_like(d["q2"])), d["k2"])
          for d, h in chains]
    yield
    sc = [(qkh * d["heads"][h]["decay"]).astype(BF16) for (d, h), qkh in zip(chains, qk)]
    yield
    pv = [_dot(sch, d["vo"]) for (d, _), sch in zip(chains, sc)]
    yield
    upd = [_dot_tn(d["k2"], (jnp.where(sel0, wide(d["heads"][0]["ws"]), wide(d["heads"][1]["ws"]))
                             * d["vo"].astype(F32)).astype(BF16)) for d in groups]
    yield
    for i, d in enumerate(groups):
        h0, h1 = d["heads"]
        tot = jnp.where(sel0, wide(h0["iw"]), wide(h1["iw"])) * q_state[i] + jnp.where(sel0, pv[2 * i], pv[2 * i + 1])
        den = jnp.maximum(jnp.abs(tot[:, LANES:]), jnp.where(head0, h0["emt"], h1["emt"]))
        d["h_ref"][0, :, d["cols"]] = (tot[:, :LANES] / den).astype(d["h_ref"].dtype)
        dec2 = jnp.where(sel0[0:1, :], wide(h0["dec"]), wide(h1["dec"]))
        s_ref[d["idx"]] = dec2 * d["state"] + jnp.where(bd, upd[i], 0.0)
        m_ref[d["idx"], 0:1, :] = h0["mnew"]
        m_ref[d["idx"], 1:2, :] = h1["mnew"]


_INV_BASE = 32


def _unit_tri_inverses(ms, xor, tick, upto):
    diag = xor == 0
    one = jnp.ones((), BF16)
    zero = jnp.zeros((), BF16)
    base = xor < _INV_BASE
    pbs = [jnp.where(base, m, zero) for m in ms]
    ts = [jnp.where(diag, one, -pb) for pb in pbs]
    k = 1
    while True:
        pbs = [_dot(pb, pb).astype(BF16) for pb in pbs]
        tick()
        k *= 2
        ts = [_dot(t, jnp.where(diag, one, pb)).astype(BF16) for t, pb in zip(ts, pbs)]
        tick()
        if 2 * k >= _INV_BASE:
            break
    b = _INV_BASE
    while b < upto:
        level = (xor >= b) & (xor < 2 * b)
        xs = [_dot(jnp.where(level, m, zero), t).astype(BF16) for m, t in zip(ms, ts)]
        tick()
        ts = [_dot(t, jnp.where(diag, one, -x)).astype(BF16) for t, x in zip(ts, xs)]
        tick()
        b *= 2
    return ts


def _gdn_step(qf, kf, vf, gf, qb, kb, vb, gb, of_ref, ob_ref, s_ref, fwd_masks, bwd_masks, tick):
    npair = D_C // LANES
    head0 = _iota2((CHUNK, LANES), 1) < HEAD_DIM
    sel0 = (_iota2((CHUNK, 2 * LANES), 1) & HEAD_DIM) == 0
    bd = _pair_block_diag(LANES, LANES)
    xor = fwd_masks[2]

    dirs = []
    for backward, (q_ref, k_ref, v_ref, g_ref), o_ref in ((False, (qf, kf, vf, gf), of_ref),
                                                          (True, (qb, kb, vb, gb), ob_ref)):
        incl, strict, _ = bwd_masks if backward else fwd_masks
        incl_other = (fwd_masks if backward else bwd_masks)[0]
        last = 0 if backward else CHUNK - 1
        g, gbc, _, gbt = g_ref
        for p in range(npair):
            cols = slice(p * LANES, (p + 1) * LANES)
            q2, k2, v2 = q_ref[0, :, cols], k_ref[0, :, cols], v_ref[0, :, cols]
            a_lane = [_gate_lane(1, p, 0, backward, h) for h in range(2)]
            b_lane = [_gate_lane(1, p, 1, backward, h) for h in range(2)]
            gam = [gbc[:, a:a + 1] for a in a_lane]
            beta = [g[:, b:b + 1] for b in b_lane]
            glast2 = jnp.where(head0[0:1, :], gbc[last:last + 1, a_lane[0]:a_lane[0] + 1],
                               gbc[last:last + 1, a_lane[1]:a_lane[1] + 1])
            gam2 = jnp.where(head0, gam[0], gam[1])
            beta2 = jnp.where(head0, beta[0], beta[1])
            egam2 = jnp.exp2(gam2)
            kf32 = k2.astype(F32)
            rhs = jnp.concatenate([v2.astype(F32) * beta2, kf32 * (beta2 * egam2)], axis=1).astype(BF16)
            decay = [jnp.exp2(jnp.where(incl, gam[h] - gbt[a_lane[h]:a_lane[h] + 1, :], -jnp.inf)) for h in range(2)]
            idx = 2 * int(backward) + p
            state = s_ref[idx]
            dirs.append(dict(q2=q2, k2=k2, kf32=kf32, rhs=rhs, decay=decay, beta=beta, strict=strict, gam2=gam2,
                             egam2=egam2, glast2=glast2, state=state, state_b=state.astype(BF16), o_ref=o_ref,
                             cols=cols, idx=idx, backward=backward))

    chains = [(d, h) for d in dirs for h in range(2)]

    def head_lanes(x, h):
        return jnp.where(head0 if h == 0 else ~head0, x, 0).astype(BF16)

    kk = [_dot_nt(head_lanes(d["k2"], h), d["k2"]) for d, h in chains]
    tick()
    ms = [jnp.where(d["strict"], d["beta"][h] * kkh * d["decay"][h], 0.0).astype(BF16)
          for (d, h), kkh in zip(chains, kk)]
    qk = [_dot_nt(head_lanes(d["q2"], h), d["k2"]) for d, h in chains]
    tick()
    a_intra = [(qkh * d["decay"][h]).astype(BF16) for (d, h), qkh in zip(chains, qk)]
    half = CHUNK // 2
    t_half = _unit_tri_inverses(ms, xor, tick, upto=half)
    y = [_dot(t, d["rhs"]) for (d, _), t in zip(chains, t_half)]
    tick()
    early = [slice(half, CHUNK) if d["backward"] else slice(0, half) for d, _ in chains]
    late = [slice(0, half) if d["backward"] else slice(half, CHUNK) for d, _ in chains]
    z = [_dot(m[lt, el], yj[el].astype(BF16)).astype(BF16) for m, yj, el, lt in zip(ms, y, early, late)]
    tick()
    corr = [_dot(t[lt, lt], zj) for t, zj, lt in zip(t_half, z, late)]
    tick()
    uw = [jnp.concatenate([yj[el], yj[lt] - c][::-1] if d["backward"] else [yj[el], yj[lt] - c], axis=0)
          for (d, _), yj, c, el, lt in zip(chains, y, corr, early, late)]

    uw2 = [jnp.where(sel0, uw[2 * i], uw[2 * i + 1]) for i in range(len(dirs))]
    qg = [(d["q2"].astype(F32) * d["egam2"]).astype(BF16) for d in dirs]
    zeros = jnp.zeros((LANES, LANES), BF16)
    w_state, q_state = [None] * len(dirs), [None] * len(dirs)
    for i in range(0, len(dirs), npair):
        group = range(i, i + npair)
        state_bd = jnp.concatenate([jnp.concatenate([dirs[j]["state_b"] if j == k else zeros for k in group], axis=1)
                                    for j in group], axis=0)
        ws_cat = _dot(jnp.concatenate([uw2[j][:, LANES:].astype(BF16) for j in group], axis=1), state_bd)
        qs_cat = _dot(jnp.concatenate([qg[j] for j in group], axis=1), state_bd)
        for n, j in enumerate(group):
            w_state[j] = ws_cat[:, n * LANES:(n + 1) * LANES]
            q_state[j] = qs_cat[:, n * LANES:(n + 1) * LANES]
    tick()
    v_new = [(uw2[i][:, :LANES] - w_state[i]).astype(BF16) for i in range(len(dirs))]
    av = [_dot(a_intra[j], v_new[j // 2]) for j in range(len(chains))]
    tick()
    upd = [_dot_tn((d["kf32"] * jnp.exp2(d["glast2"] - d["gam2"])).astype(BF16), v_new[i])
           for i, d in enumerate(dirs)]
    for i, d in enumerate(dirs):
        d["o_ref"][0, :, d["cols"]] = (q_state[i] + jnp.where(head0, av[2 * i], av[2 * i + 1])).astype(d["o_ref"].dtype)
        s_ref[d["idx"]] = d["state"] * jnp.exp2(d["glast2"]) + jnp.where(bd, upd[i], 0.0)


def _recurrent_kernel(bqf, bkf, bvf, cqf, ckf, cvf, gf, bqb, bkb, bvb, cqb, ckb, cvb, gb,
                      hf_ref, hb_ref, of_ref, ob_ref, ms_ref, mm_ref, gs_ref):
    @pl.when(pl.program_id(1) == 0)
    def _():
        ms_ref[...] = jnp.zeros_like(ms_ref)
        mm_ref[...] = jnp.zeros_like(mm_ref)
        gs_ref[...] = jnp.zeros_like(gs_ref)

    fwd_masks = _direction_masks(False)
    bwd_masks = _direction_masks(True)
    gf = (gf[0],) + _gate_cumsums(gf[0], fwd_masks[0], bwd_masks[0])
    gb = (gb[0],) + _gate_cumsums(gb[0], bwd_masks[0], fwd_masks[0])
    mlstm = _mlstm_stages(bqf, bkf, bvf, gf, bqb, bkb, bvb, gb, hf_ref, hb_ref, ms_ref, mm_ref, fwd_masks, bwd_masks)
    _gdn_step(cqf, ckf, cvf, gf, cqb, ckb, cvb, gb, of_ref, ob_ref, gs_ref, fwd_masks, bwd_masks,
              lambda: next(mlstm, None))
    for _ in mlstm:
        pass


def _recurrent_mixers(bqkv, cqkv, gates):
    bsz, seq, _ = bqkv.shape
    nc = seq // CHUNK
    nstate = 2 * (D_B // LANES)
    in_specs = []
    for rev in (False, True):
        cidx = (lambda c: nc - 1 - c) if rev else (lambda c: c)
        for width in (D_B, D_C):
            for part in range(3):
                in_specs.append(pl.BlockSpec((1, CHUNK, width), lambda b, c, part=part, cidx=cidx: (b, cidx(c), part)))
        in_specs.append(pl.BlockSpec((1, CHUNK, LANES), lambda b, c, cidx=cidx: (b, cidx(c), 0)))
    out_b = jax.ShapeDtypeStruct((bsz, seq, D_B), BF16)
    out_c = jax.ShapeDtypeStruct((bsz, seq, D_C), BF16)
    fwd_spec = lambda w: pl.BlockSpec((1, CHUNK, w), lambda b, c: (b, c, 0))
    bwd_spec = lambda w: pl.BlockSpec((1, CHUNK, w), lambda b, c: (b, nc - 1 - c, 0))
    return pl.pallas_call(
        _recurrent_kernel,
        grid=(bsz, nc),
        in_specs=in_specs,
        out_specs=[fwd_spec(D_B), bwd_spec(D_B), fwd_spec(D_C), bwd_spec(D_C)],
        out_shape=(out_b, out_b, out_c, out_c),
        scratch_shapes=[pltpu.VMEM((nstate, LANES, 2 * LANES), F32), pltpu.VMEM((nstate, SUBLANES, LANES), F32),
                        pltpu.VMEM((nstate, LANES, LANES), F32)],
        compiler_params=pltpu.CompilerParams(
            dimension_semantics=("parallel", "arbitrary"), vmem_limit_bytes=VMEM_LIMIT),
        name="recurrent_mixers",
    )(bqkv, bqkv, bqkv, cqkv, cqkv, cqkv, gates, bqkv, bqkv, bqkv, cqkv, cqkv, cqkv, gates)


def _out_kernel(x_ref, oa_ref, hf_ref, hb_ref, cf_ref, cb_ref, z_ref, mw_ref, gw_ref, wo_ref, y_ref):
    bd = _head_block_diag(MXU_WIDTH).astype(BF16)
    z = z_ref[0].astype(F32)

    def silu(t):
        return t * _sigmoid(t)

    hsum = _sigmoid(z[:, D_A:D_A + D_B]) * (hf_ref[0].astype(F32) + hb_ref[0].astype(F32))
    csum = cf_ref[0].astype(F32) + cb_ref[0].astype(F32)
    ssb = _head_sum(hsum * hsum, bd)
    ssc = _head_sum(csum * csum, bd)
    ya = oa_ref[0].astype(F32) * silu(z[:, 0:D_A])
    yb = hsum * lax.rsqrt(ssb * (1.0 / HEAD_DIM) + EPS) * mw_ref[...] * silu(z[:, D_A + D_B:D_A + 2 * D_B])
    yc = csum * lax.rsqrt(ssc * (1.0 / HEAD_DIM) + EPS) * gw_ref[...] * silu(z[:, D_A + 2 * D_B:])
    y = jnp.concatenate([ya, yb, yc], axis=1).astype(BF16)
    y_ref[0] = x_ref[0] + _dot(y, wo_ref[...])


def _out_proj(x, oa, hf, hb, cf, cb, z, mw, gw, wo):
    bsz, seq, d = x.shape
    tile = OUT_TILE
    row_spec = lambda a: pl.BlockSpec((1, tile, a.shape[-1]), lambda b, i: (b, i, 0))
    full = lambda a: pl.BlockSpec(a.shape, lambda b, i: (0,) * a.ndim)
    return pl.pallas_call(
        _out_kernel,
        grid=(bsz, seq // tile),
        in_specs=[row_spec(a) for a in (x, oa, hf, hb, cf, cb, z)] + [full(mw), full(gw), full(wo)],
        out_specs=pl.BlockSpec((1, tile, d), lambda b, i: (b, i, 0)),
        out_shape=jax.ShapeDtypeStruct(x.shape, x.dtype),
        compiler_params=pltpu.CompilerParams(
            dimension_semantics=("parallel", "parallel"), vmem_limit_bytes=VMEM_LIMIT),
        name="out_proj",
    )(x, oa, hf, hb, cf, cb, z, mw, gw, wo)


def _to_gate_lanes(t):
    lead = t.shape[:-3]
    t = t.reshape(lead + (4, 2, 2))
    return jnp.swapaxes(t, -3, -2).reshape(lead + (16,))


def _gate_weights(w_in):
    nh = D_B // HEAD_DIM
    per_mixer = [_to_gate_lanes(w_in[:, off:off + 4 * nh].reshape(-1, 2, 2, nh)) for off in (_OFF_BIF, _OFF_CAB)]
    return jnp.pad(jnp.concatenate(per_mixer, axis=1), ((0, 0), (0, LANES - 2 * 16)))


def _gate_params(i_bias, f_bias, a_log, dt_bias):
    zeros = jnp.zeros_like(dt_bias)
    row0 = jnp.concatenate([_to_gate_lanes(jnp.stack([i_bias, f_bias])), _to_gate_lanes(jnp.stack([dt_bias, zeros]))])
    row1 = jnp.concatenate([jnp.zeros((16,), F32), _to_gate_lanes(jnp.stack([a_log, zeros]))])
    return jnp.pad(jnp.stack([row0, row1]), ((0, SUBLANES - 2), (0, LANES - 2 * 16)))


def _layer(x, bias_tiles, norm_w, w_in, w_out, qk_norm_w, i_bias, f_bias, m_norm_w, conv_w, a_log, dt_bias,
           g_norm_w):
    w_all = jnp.concatenate([w_in[:, _OFF_AQ:_OFF_AZ], w_in[:, _OFF_BQ:_OFF_BIF], w_in[:, _OFF_CQ:_OFF_CAB],
                             _gate_weights(w_in), w_in[:, _OFF_AZ:_OFF_BQ], w_in[:, _OFF_BO:_OFF_CQ],
                             w_in[:, _OFF_CZ:]], axis=1).astype(BF16)
    qkw = jnp.pad(jnp.tile(qk_norm_w, (1, MXU_WIDTH // HEAD_DIM)) * jnp.array([[HEAD_DIM ** -0.5 * LOG2E], [1.0]], F32),
                  ((0, SUBLANES - 2), (0, 0)))
    gpar = _gate_params(i_bias, f_bias, a_log, dt_bias)
    cw = jnp.pad(conv_w, ((0, SUBLANES - CONV_W), (0, 0)))

    aq, *akv, bqkv, cqkv, gates, z = _in_proj(x, norm_w[None, :], w_all, qkw, gpar, cw)
    oa = _attention(aq, akv, bias_tiles)
    hf, hb, cf, cb = _recurrent_mixers(bqkv, cqkv, gates)
    mw = jnp.tile(m_norm_w, D_B // HEAD_DIM)[None, :]
    gw = jnp.tile(g_norm_w, D_C // HEAD_DIM)[None, :]
    return _out_proj(x, oa, hf, hb, cf, cb, z, mw, gw, w_out.astype(BF16))


def kernel(x, norm_w, w_in, w_out, qk_norm_w, rel_bias, mlstm_i_bias, mlstm_f_bias, mlstm_norm_w, gdn_conv_w,
           gdn_a_log, gdn_dt_bias, gdn_norm_w):
    bias_tiles = _attn_bias_tiles(rel_bias)
    for l in range(DEPTH):
        x = _layer(x, bias_tiles, norm_w[l], w_in[l], w_out[l], qk_norm_w[l], mlstm_i_bias[l], mlstm_f_bias[l],
                   mlstm_norm_w[l], gdn_conv_w[l], gdn_a_log[l], gdn_dt_bias[l], gdn_norm_w[l])
    return x
```

```python
import functools

import numpy as np
import jax
import jax.numpy as jnp
from jax import lax
from jax.experimental import pallas as pl
from jax.experimental.pallas import tpu as pltpu

F32 = jnp.float32
BF16 = jnp.bfloat16

D_MODEL = 1024
DEPTH = 2
EPS = 1e-6
HEAD_DIM = 64
LANES = 128
SUBLANES = 8
MXU_WIDTH = 256
D_A, D_B, D_C = 512, 256, 256
DILATED_CFGS = ((128, 1), (512, 4), (2048, 16))
N_SIDE = 64
NUM_BUCKETS = 32
REL_MAX_DIST = 1024
CONV_W = 5
CHUNK = 256
Q_SUPER = 2048
Q_BLK = 128
K_WIN = 256
ATTN_ILP = 2
ATTN_DEPTH = 1
IN_TILE = 512
OUT_TILE = 1024
NEG = -1e30
LOG2E = 1.4426950408889634
VMEM_LIMIT = 56 * 1024 * 1024

_OFF_AQ, _OFF_AZ = 0, 1536
_OFF_BQ, _OFF_BIF, _OFF_BO, _OFF_BZ = 2048, 2816, 2832, 3088
_OFF_CQ, _OFF_CAB, _OFF_CZ = 3344, 4112, 4128
_W_SPLITS = (0, 3 * D_A, 3 * D_A + 3 * D_B, 3 * D_A + 3 * D_B + 3 * D_C, 3 * D_A + 3 * D_B + 3 * D_C + LANES)


def _dot(a, b):
    return jnp.dot(a, b, preferred_element_type=F32)


def _dot_nt(a, b):
    return lax.dot_general(a, b, (((1,), (1,)), ((), ())), preferred_element_type=F32)


def _dot_tn(a, b):
    return lax.dot_general(a, b, (((0,), (0,)), ((), ())), preferred_element_type=F32)


def _split(a):
    hi = a.astype(BF16)
    lo = (a - hi.astype(F32)).astype(BF16)
    return hi, lo


def _iota2(shape, axis):
    return lax.broadcasted_iota(jnp.int32, shape, axis)


def _pair_block_diag(rows, cols):
    r = _iota2((rows, cols), 0)
    c = _iota2((rows, cols), 1)
    return (r < HEAD_DIM) == ((c & HEAD_DIM) == 0)


def _head_block_diag(n):
    return (_iota2((n, n), 0) ^ _iota2((n, n), 1)) < HEAD_DIM


def _head_sum(t, bd):
    return _dot(t.astype(BF16), bd)


def _softplus(y):
    return jnp.maximum(y, 0.0) + jnp.log1p(jnp.exp(-jnp.abs(y)))


def _sigmoid(y):
    return 1.0 / (1.0 + jnp.exp(-y))


def _in_proj_kernel(x_ref, xp_ref, xn_ref, nw_ref, w_ref, qkw_ref, gpar_ref, cw_ref,
                    a_ref, kv1_ref, kv4_ref, kv16_ref, b_ref, c_ref, g_ref, cext_ref, kvs_ref, hn_ref, kvg_ref):
    kv_refs = (kv1_ref, kv4_ref, kv16_ref)
    wa_ref, wb_ref, wc_ref, wg_ref = (w_ref.at[:, lo:hi] for lo, hi in zip(_W_SPLITS[:-1], _W_SPLITS[1:]))
    i = pl.program_id(1)
    n = pl.num_programs(1)
    tile = x_ref.shape[1]
    bd = _head_block_diag(MXU_WIDTH).astype(BF16)

    xe = jnp.concatenate([xp_ref[0], x_ref[0], xn_ref[0]], axis=0)
    ms = jnp.mean(xe * xe, axis=-1, keepdims=True)
    hne = xe * lax.rsqrt(ms + EPS) * nw_ref[...]
    hn_ref[...] = hne[SUBLANES:SUBLANES + tile].astype(BF16)
    hne = hne.astype(BF16)

    ce = _dot(hne, wc_ref[...])
    ta = [_dot(hn_ref[...], wa_ref[:, j * MXU_WIDTH:(j + 1) * MXU_WIDTH]) for j in range(3 * D_A // MXU_WIDTH)]
    p = _dot(hn_ref[...], wg_ref[...]) + gpar_ref[0:1, :]
    ssa = [_head_sum(t * t, bd) for t in ta[:4]]

    row = _iota2((tile + 2 * SUBLANES, 1), 0)
    valid = ((row >= SUBLANES) | (i > 0)) & ((row < tile + SUBLANES) | (i < n - 1))
    cext_ref[...] = jnp.where(valid, ce, 0.0)
    conv = cw_ref[0:1, :] * cext_ref[pl.ds(SUBLANES - CONV_W // 2, tile), :]
    for t in range(1, CONV_W):
        conv = conv + cw_ref[t:t + 1, :] * cext_ref[pl.ds(SUBLANES - CONV_W // 2 + t, tile), :]
    s = conv * _sigmoid(conv)
    tc = [s[:, j * MXU_WIDTH:(j + 1) * MXU_WIDTH] for j in range(3 * D_C // MXU_WIDTH)]

    ssc = [_head_sum(t * t, bd) for t in tc[:2]]
    tb = _dot(hn_ref[...], wb_ref[...])

    for j, t in enumerate(ta):
        if j < 4:
            t = t * lax.rsqrt(ssa[j] * (1.0 / HEAD_DIM) + EPS) * qkw_ref[j // 2:j // 2 + 1, :]
        if j < 2:
            a_ref[0, :, j * MXU_WIDTH:(j + 1) * MXU_WIDTH] = t
        else:
            for half in range(2):
                kvs_ref[2 * (j - 2) + half] = t[:, half * LANES:(half + 1) * LANES]
    for lt in range(kvs_ref.shape[0]):
        lanes = slice(lt * LANES, (lt + 1) * LANES)
        kv_refs[0][0, :, lanes] = kvs_ref[lt].astype(BF16)
        prev_dil, src = 1, [kvs_ref.at[lt]]
        for ci in range(1, len(DILATED_CFGS)):
            dil = DILATED_CFGS[ci][1]
            step = dil // prev_dil
            rows = tile // dil
            nxt = [None] * dil
            for r in range(dil):
                x = src[r % prev_dil][pl.ds(r // prev_dil, rows, stride=step), :]
                kv_refs[ci][0, r, :, lanes] = x.astype(BF16)
                if ci + 1 < len(DILATED_CFGS):
                    kvg_ref[lt, r] = x
                    nxt[r] = kvg_ref.at[lt, r]
            prev_dil, src = dil, nxt

    b_ref[0, :, 0:D_B] = (tb[:, 0:D_B] * HEAD_DIM ** -0.5).astype(b_ref.dtype)
    b_ref[0, :, D_B:3 * D_B] = tb[:, D_B:3 * D_B].astype(b_ref.dtype)

    lane = _iota2(p.shape, 1)
    first4 = (lane & 7) < 4
    is_b = lane < 16
    val_b = jnp.where(first4, p, -_softplus(-p)) * LOG2E
    val_c = jnp.where(first4, -jnp.exp(gpar_ref[1:2, :]) * _softplus(p) * LOG2E, _sigmoid(p))
    g_ref[0] = jnp.where(is_b, val_b, val_c)

    c_ref[0, :, 0:D_C] = (tc[0] * lax.rsqrt(ssc[0] + EPS) * HEAD_DIM ** -0.5).astype(c_ref.dtype)
    c_ref[0, :, D_C:2 * D_C] = (tc[1] * lax.rsqrt(ssc[1] + EPS)).astype(c_ref.dtype)
    c_ref[0, :, 2 * D_C:3 * D_C] = tc[2].astype(c_ref.dtype)


def _in_proj(x, nw, w_all, qkw, gpar, cw):
    bsz, seq, d = x.shape
    tile = IN_TILE
    nt = seq // tile
    hb = tile // SUBLANES
    full = lambda a: pl.BlockSpec(a.shape, lambda b, i: (0,) * a.ndim)
    row_spec = lambda w: pl.BlockSpec((1, tile, w), lambda b, i: (b, i, 0))
    outs = [((bsz, seq, D_A), F32, row_spec(D_A))]
    for (_, dil) in DILATED_CFGS:
        if dil == 1:
            outs.append(((bsz, seq, 2 * D_A), BF16, row_spec(2 * D_A)))
        else:
            outs.append(((bsz, dil, seq // dil, 2 * D_A), BF16,
                         pl.BlockSpec((1, dil, tile // dil, 2 * D_A), lambda b, i: (b, 0, i, 0))))
    outs += [((bsz, seq, 3 * D_B), BF16, row_spec(3 * D_B)), ((bsz, seq, 3 * D_C), BF16, row_spec(3 * D_C)),
             ((bsz, seq, LANES), F32, row_spec(LANES))]
    return pl.pallas_call(
        _in_proj_kernel,
        grid=(bsz, nt),
        in_specs=[
            row_spec(d),
            pl.BlockSpec((1, SUBLANES, d), lambda b, i: (b, jnp.maximum(i * hb - 1, 0), 0)),
            pl.BlockSpec((1, SUBLANES, d), lambda b, i: (b, jnp.minimum((i + 1) * hb, nt * hb - 1), 0)),
            full(nw), full(w_all), full(qkw), full(gpar), full(cw),
        ],
        out_specs=[spec for _, _, spec in outs],
        out_shape=[jax.ShapeDtypeStruct(shape, dtype) for shape, dtype, _ in outs],
        scratch_shapes=[pltpu.VMEM((tile + 2 * SUBLANES, 3 * D_C), F32), pltpu.VMEM((2 * D_A // LANES, tile, LANES), F32),
                        pltpu.VMEM((tile, d), BF16),
                        pltpu.VMEM((2 * D_A // LANES, DILATED_CFGS[1][1], tile // DILATED_CFGS[1][1], LANES), F32)],
        compiler_params=pltpu.CompilerParams(
            dimension_semantics=("parallel", "arbitrary"), vmem_limit_bytes=VMEM_LIMIT),
        name="in_proj",
    )(x, x, x, nw, w_all, qkw, gpar, cw)


def _t5_bucket(rel):
    half = NUM_BUCKETS // 2
    max_exact = half // 2
    n = np.abs(rel)
    large = max_exact + (np.log(np.maximum(n, 1) / max_exact) / np.log(REL_MAX_DIST / max_exact)
                         * (half - max_exact)).astype(np.int32)
    large = np.minimum(large, half - 1)
    return (rel > 0).astype(np.int32) * half + np.where(n < max_exact, n, large)


def _attn_bias_tiles(rel_bias):
    period = 5 * LANES + 1
    wide = K_WIN + 2 * N_SIDE
    lead = Q_BLK - 1 + N_SIDE
    cfg_tiles = []
    for (_, dil) in DILATED_CFGS:
        offs = dil * np.arange(-N_SIDE, N_SIDE + 1)
        bias = rel_bias[_t5_bucket(offs)].T.astype(F32) * LOG2E
        heads = bias.shape[0]
        vec = jnp.concatenate([jnp.full((heads, lead + N_SIDE), NEG, F32), bias,
                               jnp.full((heads, period - lead - 3 * N_SIDE - 1), NEG, F32)], axis=1)
        rows = jnp.tile(vec, (1, Q_BLK))[:, :Q_BLK * (period - 1)].reshape(heads, Q_BLK, period - 1)
        ext = rows[:, :, lead:lead + wide]
        variants = [ext[:, :, 2 * N_SIDE:2 * N_SIDE + K_WIN], ext[:, :, N_SIDE:N_SIDE + K_WIN],
                    ext[:, :, 0:K_WIN]]
        cfg_tiles.append(jnp.stack(variants, axis=1))
    return jnp.stack(cfg_tiles, axis=0)


def _attn_kernel(q_ref, *rest):
    ncfg = len(DILATED_CFGS)
    kv_refs = [(rest[2 * c], rest[2 * c + 1]) for c in range(ncfg)]
    bias_ref, o_ref, acc_ref, m_ref, l_ref = rest[2 * ncfg:]
    sb = pl.program_id(2)
    seq = kv_refs[0][0].shape[1]
    qsup = q_ref.shape[1]
    nblk = qsup // Q_BLK
    head0 = _iota2((Q_BLK, LANES), 1) < HEAD_DIM
    sel0 = (_iota2((Q_BLK, 2 * LANES), 1) & HEAD_DIM) == 0

    groups = [(pos, ci, g) for pos, ci in enumerate(reversed(range(len(DILATED_CFGS))))
              for g in range(nblk // ATTN_ILP)]

    def scores(pos, ci, g):
        dil = DILATED_CFGS[ci][1]
        n_idx = seq // dil
        blocks = []
        for u in range(ATTN_ILP):
            t = g * ATTN_ILP + u
            r, j = t % dil, t // dil
            qs = r + dil * (j * Q_BLK)
            i0 = sb * (qsup // dil) + j * Q_BLK
            ws = jnp.clip(i0 - N_SIDE, 0, n_idx - K_WIN)
            var = jnp.where(i0 == 0, 0, jnp.where(i0 == n_idx - Q_BLK, 2, 1))
            qrows = pl.ds(qs, Q_BLK) if dil == 1 else pl.ds(qs, Q_BLK, stride=dil)
            krows = pl.ds(pl.multiple_of(ws, N_SIDE), K_WIN)
            k_ref, v_ref = kv_refs[ci]
            if dil == 1:
                k2, v2 = k_ref[0, krows, :], v_ref[0, krows, :]
            else:
                k2, v2 = k_ref[0, r, krows, :], v_ref[0, r, krows, :]
            q2 = q_ref[0, qrows, :].astype(BF16)
            blocks.append(dict(qrows=qrows, var=var, k2=k2,
                               qh=[jnp.where(head0 if h == 0 else ~head0, q2, jnp.zeros_like(q2)) for h in range(2)],
                               vo=jnp.concatenate([v2, jnp.ones_like(v2)], axis=1)))
        chains = [(blk, h) for blk in blocks for h in range(2)]
        s = [_dot_nt(blk["qh"][h], blk["k2"]) + bias_ref[ci, h, blk["var"]] for blk, h in chains]
        mx = [jnp.max(sh, axis=-1, keepdims=True) for sh in s]
        p = [jnp.exp2(sh - m).astype(BF16) for sh, m in zip(s, mx)]
        return dict(pos=pos, blocks=blocks, chains=chains, mx=mx, p=p)

    def accumulate(st):
        pv = [_dot(ph, blk["vo"]) for ph, (blk, _) in zip(st["p"], st["chains"])]
        for u, blk in enumerate(st["blocks"]):
            qrows = blk["qrows"]
            tot = jnp.where(sel0, pv[2 * u], pv[2 * u + 1])
            num2 = tot[:, :LANES]
            l2 = tot[:, LANES:]
            m2 = jnp.where(head0, st["mx"][2 * u], st["mx"][2 * u + 1])
            if st["pos"] == 0:
                acc_ref[qrows, :] = num2
                m_ref[qrows, :] = m2
                l_ref[qrows, :] = l2
            else:
                mo = m_ref[qrows, :]
                mn = jnp.maximum(mo, m2)
                a = jnp.exp2(mo - mn)
                b = jnp.exp2(m2 - mn)
                acc_ref[qrows, :] = acc_ref[qrows, :] * a + num2 * b
                l_ref[qrows, :] = l_ref[qrows, :] * a + l2 * b
                m_ref[qrows, :] = mn

    in_flight = []
    for grp in groups:
        in_flight.append(scores(*grp))
        if len(in_flight) > ATTN_DEPTH:
            accumulate(in_flight.pop(0))
    for st in in_flight:
        accumulate(st)

    o_ref[0] = (acc_ref[...] / l_ref[...]).astype(o_ref.dtype)


def _attention(aq, akv, bias_tiles):
    bsz, seq, _ = aq.shape
    npair = D_A // LANES
    qsup = Q_SUPER
    assert seq % qsup == 0 and seq // DILATED_CFGS[-1][1] >= K_WIN
    in_specs = [pl.BlockSpec((1, qsup, LANES), lambda b, p, s: (b, s, p))]
    operands = [aq]
    for kv, (_, dil) in zip(akv, DILATED_CFGS):
        for part in range(2):
            if dil == 1:
                in_specs.append(pl.BlockSpec((1, seq, LANES), lambda b, p, s, part=part: (b, 0, part * npair + p)))
            else:
                in_specs.append(pl.BlockSpec((1, dil, seq // dil, LANES),
                                             lambda b, p, s, part=part: (b, 0, 0, part * npair + p)))
            operands.append(kv)
    in_specs.append(pl.BlockSpec((len(DILATED_CFGS), 2, 3, Q_BLK, K_WIN), lambda b, p, s: (0, p, 0, 0, 0)))
    return pl.pallas_call(
        _attn_kernel,
        grid=(bsz, npair, seq // qsup),
        in_specs=in_specs,
        out_specs=pl.BlockSpec((1, qsup, LANES), lambda b, p, s: (b, s, p)),
        out_shape=jax.ShapeDtypeStruct((bsz, seq, D_A), BF16),
        scratch_shapes=[pltpu.VMEM((qsup, LANES), F32)] * 3,
        compiler_params=pltpu.CompilerParams(
            dimension_semantics=("parallel", "parallel", "arbitrary"), vmem_limit_bytes=VMEM_LIMIT),
        name="dilated_attn",
    )(*operands, bias_tiles)


def _direction_masks(backward):
    row = _iota2((CHUNK, CHUNK), 0)
    col = _iota2((CHUNK, CHUNK), 1)
    if backward:
        return row <= col, row < col, row ^ col
    return row >= col, row > col, row ^ col


def _gate_cumsums(g, incl, incl_other):
    tri = incl.astype(BF16)
    tri_t = incl_other.astype(BF16)
    both = _dot(tri, jnp.concatenate(_split(g), axis=1))
    gb = both[:, :LANES] + both[:, LANES:]
    gt = g.T
    thi, tlo = _split(gt)
    gbt = _dot(thi, tri_t) + _dot(tlo, tri_t)
    return gb, gt, gbt


def _gate_lane(mixer, pair, kind, backward, head):
    return 16 * mixer + 8 * pair + 4 * kind + 2 * int(backward) + head


def _mlstm_stages(qf, kf, vf, gf, qb, kb, vb, gb, hf_ref, hb_ref, s_ref, m_ref, fwd_masks, bwd_masks):
    npair = D_B // LANES
    head0 = _iota2((CHUNK, LANES), 1) < HEAD_DIM
    sel0 = (_iota2((CHUNK, 2 * LANES), 1) & HEAD_DIM) == 0
    bd = _pair_block_diag(LANES, 2 * LANES)
    wide = lambda x: jnp.concatenate([x, x], axis=1)

    groups = []
    for backward, (q_ref, k_ref, v_ref, g_ref), h_ref in ((False, (qf, kf, vf, gf), hf_ref),
                                                          (True, (qb, kb, vb, gb), hb_ref)):
        incl = (bwd_masks if backward else fwd_masks)[0]
        incl_other = (fwd_masks if backward else bwd_masks)[0]
        last = 0 if backward else CHUNK - 1
        g, gbc, gt, gbt = g_ref
        i_lanes = [_gate_lane(0, p, 0, backward, h) for p in range(npair) for h in range(2)]
        f_lanes = [_gate_lane(0, p, 1, backward, h) for p in range(npair) for h in range(2)]
        for p in range(npair):
            idx = 2 * int(backward) + p
            heads = []
            for h in range(2):
                j = 2 * p + h
                li = jnp.broadcast_to(g[:, i_lanes[j]:i_lanes[j] + 1], (CHUNK, LANES))
                bc = jnp.broadcast_to(gbc[:, f_lanes[j]:f_lanes[j] + 1], (CHUNK, LANES))
                rrow = gt[i_lanes[j]:i_lanes[j] + 1, :] - gbt[f_lanes[j]:f_lanes[j] + 1, :]
                mst = m_ref[idx, h:h + 1, :]
                dmat = jnp.where(incl, wide(bc) + rrow, -jnp.inf)
                inter = bc + mst
                mt = jnp.maximum(jnp.max(dmat, axis=-1, keepdims=True), inter)
                blast = bc[last:last + 1, :]
                wlog = blast - bc + li
                mn = jnp.maximum(blast + mst, jnp.max(wlog, axis=0, keepdims=True))
                heads.append(dict(decay=jnp.exp2(dmat - wide(mt)), iw=jnp.exp2(inter - mt), emt=jnp.exp2(-mt),
                                  ws=jnp.exp2(wlog - mn), dec=jnp.exp2(blast + mst - mn), mnew=mn))
            cols = slice(p * LANES, (p + 1) * LANES)
            v2 = v_ref[0, :, cols]
            groups.append(dict(q2=q_ref[0, :, cols], k2=k_ref[0, :, cols], heads=heads, state=s_ref[idx], idx=idx,
                               h_ref=h_ref, cols=cols,
                               vo=jnp.concatenate([v2, jnp.ones_like(v2)], axis=1)))
            yield

    chains = [(d, h) for d in groups for h in range(2)]
    q_state = [_dot(d["q2"], d["state"].astype(BF16)) for d in groups]
    yield
    qk = [_dot_nt(jnp.where(head0 if h == 0 else ~head0, d["q2"], jnp.zeros_like(d["q2"])), d["k2"])
          for d, h in chains]
    yield
    sc = [(qkh * d["heads"][h]["decay"]).astype(BF16) for (d, h), qkh in zip(chains, qk)]
    yield
    pv = [_dot(sch, d["vo"]) for (d, _), sch in zip(chains, sc)]
    yield
    upd = [_dot_tn(d["k2"], (jnp.where(sel0, wide(d["heads"][0]["ws"]), wide(d["heads"][1]["ws"]))
                             * d["vo"].astype(F32)).astype(BF16)) for d in groups]
    yield
    for i, d in enumerate(groups):
        h0, h1 = d["heads"]
        tot = jnp.where(sel0, wide(h0["iw"]), wide(h1["iw"])) * q_state[i] + jnp.where(sel0, pv[2 * i], pv[2 * i + 1])
        den = jnp.maximum(jnp.abs(tot[:, LANES:]), jnp.where(head0, h0["emt"], h1["emt"]))
        d["h_ref"][0, :, d["cols"]] = (tot[:, :LANES] / den).astype(d["h_ref"].dtype)
        dec2 = jnp.where(sel0[0:1, :], wide(h0["dec"]), wide(h1["dec"]))
        s_ref[d["idx"]] = dec2 * d["state"] + jnp.where(bd, upd[i], 0.0)
        m_ref[d["idx"], 0:1, :] = h0["mnew"]
        m_ref[d["idx"], 1:2, :] = h1["mnew"]


_INV_BASE = 32


def _unit_tri_inverses(ms, xor, tick, upto):
    diag = xor == 0
    one = jnp.ones((), BF16)
    zero = jnp.zeros((), BF16)
    base = xor < _INV_BASE
    pbs = [jnp.where(base, m, zero) for m in ms]
    ts = [jnp.where(diag, one, -pb) for pb in pbs]
    k = 1
    while True:
        pbs = [_dot(pb, pb).astype(BF16) for pb in pbs]
        tick()
        k *= 2
        ts = [_dot(t, jnp.where(diag, one, pb)).astype(BF16) for t, pb in zip(ts, pbs)]
        tick()
        if 2 * k >= _INV_BASE:
            break
    b = _INV_BASE
    while b < upto:
        level = (xor >= b) & (xor < 2 * b)
        xs = [_dot(jnp.where(level, m, zero), t).astype(BF16) for m, t in zip(ms, ts)]
        tick()
        ts = [_dot(t, jnp.where(diag, one, -x)).astype(BF16) for t, x in zip(ts, xs)]
        tick()
        b *= 2
    return ts


def _gdn_step(qf, kf, vf, gf, qb, kb, vb, gb, of_ref, ob_ref, s_ref, fwd_masks, bwd_masks, tick):
    npair = D_C // LANES
    head0 = _iota2((CHUNK, LANES), 1) < HEAD_DIM
    sel0 = (_iota2((CHUNK, 2 * LANES), 1) & HEAD_DIM) == 0
    bd = _pair_block_diag(LANES, LANES)
    xor = fwd_masks[2]

    dirs = []
    for backward, (q_ref, k_ref, v_ref, g_ref), o_ref in ((False, (qf, kf, vf, gf), of_ref),
                                                          (True, (qb, kb, vb, gb), ob_ref)):
        incl, strict, _ = bwd_masks if backward else fwd_masks
        incl_other = (fwd_masks if backward else bwd_masks)[0]
        last = 0 if backward else CHUNK - 1
        g, gbc, _, gbt = g_ref
        for p in range(npair):
            cols = slice(p * LANES, (p + 1) * LANES)
            q2, k2, v2 = q_ref[0, :, cols], k_ref[0, :, cols], v_ref[0, :, cols]
            a_lane = [_gate_lane(1, p, 0, backward, h) for h in range(2)]
            b_lane = [_gate_lane(1, p, 1, backward, h) for h in range(2)]
            gam = [gbc[:, a:a + 1] for a in a_lane]
            beta = [g[:, b:b + 1] for b in b_lane]
            glast2 = jnp.where(head0[0:1, :], gbc[last:last + 1, a_lane[0]:a_lane[0] + 1],
                               gbc[last:last + 1, a_lane[1]:a_lane[1] + 1])
            gam2 = jnp.where(head0, gam[0], gam[1])
            beta2 = jnp.where(head0, beta[0], beta[1])
            egam2 = jnp.exp2(gam2)
            kf32 = k2.astype(F32)
            rhs = jnp.concatenate([v2.astype(F32) * beta2, kf32 * (beta2 * egam2)], axis=1).astype(BF16)
            decay = [jnp.exp2(jnp.where(incl, gam[h] - gbt[a_lane[h]:a_lane[h] + 1, :], -jnp.inf)) for h in range(2)]
            idx = 2 * int(backward) + p
            state = s_ref[idx]
            dirs.append(dict(q2=q2, k2=k2, kf32=kf32, rhs=rhs, decay=decay, beta=beta, strict=strict, gam2=gam2,
                             egam2=egam2, glast2=glast2, state=state, state_b=state.astype(BF16), o_ref=o_ref,
                             cols=cols, idx=idx, backward=backward))

    chains = [(d, h) for d in dirs for h in range(2)]

    def head_lanes(x, h):
        return jnp.where(head0 if h == 0 else ~head0, x, 0).astype(BF16)

    kk = [_dot_nt(head_lanes(d["k2"], h), d["k2"]) for d, h in chains]
    tick()
    ms = [jnp.where(d["strict"], d["beta"][h] * kkh * d["decay"][h], 0.0).astype(BF16)
          for (d, h), kkh in zip(chains, kk)]
    qk = [_dot_nt(head_lanes(d["q2"], h), d["k2"]) for d, h in chains]
    tick()
    a_intra = [(qkh * d["decay"][h]).astype(BF16) for (d, h), qkh in zip(chains, qk)]
    half = CHUNK // 2
    t_half = _unit_tri_inverses(ms, xor, tick, upto=half)
    y = [_dot(t, d["rhs"]) for (d, _), t in zip(chains, t_half)]
    tick()
    early = [slice(half, CHUNK) if d["backward"] else slice(0, half) for d, _ in chains]
    late = [slice(0, half) if d["backward"] else slice(half, CHUNK) for d, _ in chains]
    z = [_dot(m[lt, el], yj[el].astype(BF16)).astype(BF16) for m, yj, el, lt in zip(ms, y, early, late)]
    tick()
    corr = [_dot(t[lt, lt], zj) for t, zj, lt in zip(t_half, z, late)]
    tick()
    uw = [jnp.concatenate([yj[el], yj[lt] - c][::-1] if d["backward"] else [yj[el], yj[lt] - c], axis=0)
          for (d, _), yj, c, el, lt in zip(chains, y, corr, early, late)]

    uw2 = [jnp.where(sel0, uw[2 * i], uw[2 * i + 1]) for i in range(len(dirs))]
    qg = [(d["q2"].astype(F32) * d["egam2"]).astype(BF16) for d in dirs]
    zeros = jnp.zeros((LANES, LANES), BF16)
    w_state, q_state = [None] * len(dirs), [None] * len(dirs)
    for i in range(0, len(dirs), npair):
        group = range(i, i + npair)
        state_bd = jnp.concatenate([jnp.concatenate([dirs[j]["state_b"] if j == k else zeros for k in group], axis=1)
                                    for j in group], axis=0)
        ws_cat = _dot(jnp.concatenate([uw2[j][:, LANES:].astype(BF16) for j in group], axis=1), state_bd)
        qs_cat = _dot(jnp.concatenate([qg[j] for j in group], axis=1), state_bd)
        for n, j in enumerate(group):
            w_state[j] = ws_cat[:, n * LANES:(n + 1) * LANES]
            q_state[j] = qs_cat[:, n * LANES:(n + 1) * LANES]
    tick()
    v_new = [(uw2[i][:, :LANES] - w_state[i]).astype(BF16) for i in range(len(dirs))]
    av = [_dot(a_intra[j], v_new[j // 2]) for j in range(len(chains))]
    tick()
    upd = [_dot_tn((d["kf32"] * jnp.exp2(d["glast2"] - d["gam2"])).astype(BF16), v_new[i])
           for i, d in enumerate(dirs)]
    for i, d in enumerate(dirs):
        d["o_ref"][0, :, d["cols"]] = (q_state[i] + jnp.where(head0, av[2 * i], av[2 * i + 1])).astype(d["o_ref"].dtype)
        s_ref[d["idx"]] = d["state"] * jnp.exp2(d["glast2"]) + jnp.where(bd, upd[i], 0.0)


def _recurrent_kernel(bqf, bkf, bvf, cqf, ckf, cvf, gf, bqb, bkb, bvb, cqb, ckb, cvb, gb,
                      hf_ref, hb_ref, of_ref, ob_ref, ms_ref, mm_ref, gs_ref):
    @pl.when(pl.program_id(1) == 0)
    def _():
        ms_ref[...] = jnp.zeros_like(ms_ref)
        mm_ref[...] = jnp.zeros_like(mm_ref)
        gs_ref[...] = jnp.zeros_like(gs_ref)

    fwd_masks = _direction_masks(False)
    bwd_masks = _direction_masks(True)
    gf = (gf[0],) + _gate_cumsums(gf[0], fwd_masks[0], bwd_masks[0])
    gb = (gb[0],) + _gate_cumsums(gb[0], bwd_masks[0], fwd_masks[0])
    mlstm = _mlstm_stages(bqf, bkf, bvf, gf, bqb, bkb, bvb, gb, hf_ref, hb_ref, ms_ref, mm_ref, fwd_masks, bwd_masks)
    _gdn_step(cqf, ckf, cvf, gf, cqb, ckb, cvb, gb, of_ref, ob_ref, gs_ref, fwd_masks, bwd_masks,
              lambda: next(mlstm, None))
    for _ in mlstm:
        pass


def _recurrent_mixers(bqkv, cqkv, gates):
    bsz, seq, _ = bqkv.shape
    nc = seq // CHUNK
    nstate = 2 * (D_B // LANES)
    in_specs = []
    for rev in (False, True):
        cidx = (lambda c: nc - 1 - c) if rev else (lambda c: c)
        for width in (D_B, D_C):
            for part in range(3):
                in_specs.append(pl.BlockSpec((1, CHUNK, width), lambda b, c, part=part, cidx=cidx: (b, cidx(c), part)))
        in_specs.append(pl.BlockSpec((1, CHUNK, LANES), lambda b, c, cidx=cidx: (b, cidx(c), 0)))
    out_b = jax.ShapeDtypeStruct((bsz, seq, D_B), BF16)
    out_c = jax.ShapeDtypeStruct((bsz, seq, D_C), BF16)
    fwd_spec = lambda w: pl.BlockSpec((1, CHUNK, w), lambda b, c: (b, c, 0))
    bwd_spec = lambda w: pl.BlockSpec((1, CHUNK, w), lambda b, c: (b, nc - 1 - c, 0))
    return pl.pallas_call(
        _recurrent_kernel,
        grid=(bsz, nc),
        in_specs=in_specs,
        out_specs=[fwd_spec(D_B), bwd_spec(D_B), fwd_spec(D_C), bwd_spec(D_C)],
        out_shape=(out_b, out_b, out_c, out_c),
        scratch_shapes=[pltpu.VMEM((nstate, LANES, 2 * LANES), F32), pltpu.VMEM((nstate, SUBLANES, LANES), F32),
                        pltpu.VMEM((nstate, LANES, LANES), F32)],
        compiler_params=pltpu.CompilerParams(
            dimension_semantics=("parallel", "arbitrary"), vmem_limit_bytes=VMEM_LIMIT),
        name="recurrent_mixers",
    )(bqkv, bqkv, bqkv, cqkv, cqkv, cqkv, gates, bqkv, bqkv, bqkv, cqkv, cqkv, cqkv, gates)


def _out_kernel(x_ref, oa_ref, hf_ref, hb_ref, cf_ref, cb_ref, nw_ref, wz_ref, mw_ref, gw_ref, wo_ref, y_ref):
    bd = _head_block_diag(MXU_WIDTH).astype(BF16)
    x = x_ref[0]
    hn = (x * lax.rsqrt(jnp.mean(x * x, axis=-1, keepdims=True) + EPS) * nw_ref[...]).astype(BF16)
    z = _dot(hn, wz_ref[...])

    def silu(t):
        return t * _sigmoid(t)

    hsum = _sigmoid(z[:, D_A:D_A + D_B]) * (hf_ref[0].astype(F32) + hb_ref[0].astype(F32))
    csum = cf_ref[0].astype(F32) + cb_ref[0].astype(F32)
    ssb = _head_sum(hsum * hsum, bd)
    ssc = _head_sum(csum * csum, bd)
    ya = oa_ref[0].astype(F32) * silu(z[:, 0:D_A])
    yb = hsum * lax.rsqrt(ssb * (1.0 / HEAD_DIM) + EPS) * mw_ref[...] * silu(z[:, D_A + D_B:D_A + 2 * D_B])
    yc = csum * lax.rsqrt(ssc * (1.0 / HEAD_DIM) + EPS) * gw_ref[...] * silu(z[:, D_A + 2 * D_B:])
    y = jnp.concatenate([ya, yb, yc], axis=1).astype(BF16)
    y_ref[0] = x + _dot(y, wo_ref[...])


def _out_proj(x, oa, hf, hb, cf, cb, nw, wz, mw, gw, wo):
    bsz, seq, d = x.shape
    tile = OUT_TILE
    row_spec = lambda a: pl.BlockSpec((1, tile, a.shape[-1]), lambda b, i: (b, i, 0))
    full = lambda a: pl.BlockSpec(a.shape, lambda b, i: (0,) * a.ndim)
    return pl.pallas_call(
        _out_kernel,
        grid=(bsz, seq // tile),
        in_specs=[row_spec(a) for a in (x, oa, hf, hb, cf, cb)] + [full(a) for a in (nw, wz, mw, gw, wo)],
        out_specs=pl.BlockSpec((1, tile, d), lambda b, i: (b, i, 0)),
        out_shape=jax.ShapeDtypeStruct(x.shape, x.dtype),
        compiler_params=pltpu.CompilerParams(
            dimension_semantics=("parallel", "parallel"), vmem_limit_bytes=VMEM_LIMIT),
        name="out_proj",
    )(x, oa, hf, hb, cf, cb, nw, wz, mw, gw, wo)


def _to_gate_lanes(t):
    lead = t.shape[:-3]
    t = t.reshape(lead + (4, 2, 2))
    return jnp.swapaxes(t, -3, -2).reshape(lead + (16,))


def _gate_weights(w_in):
    nh = D_B // HEAD_DIM
    per_mixer = [_to_gate_lanes(w_in[:, off:off + 4 * nh].reshape(-1, 2, 2, nh)) for off in (_OFF_BIF, _OFF_CAB)]
    return jnp.pad(jnp.concatenate(per_mixer, axis=1), ((0, 0), (0, LANES - 2 * 16)))


def _gate_params(i_bias, f_bias, a_log, dt_bias):
    zeros = jnp.zeros_like(dt_bias)
    row0 = jnp.concatenate([_to_gate_lanes(jnp.stack([i_bias, f_bias])), _to_gate_lanes(jnp.stack([dt_bias, zeros]))])
    row1 = jnp.concatenate([jnp.zeros((16,), F32), _to_gate_lanes(jnp.stack([a_log, zeros]))])
    return jnp.pad(jnp.stack([row0, row1]), ((0, SUBLANES - 2), (0, LANES - 2 * 16)))


def _layer(x, bias_tiles, norm_w, w_in, w_out, qk_norm_w, i_bias, f_bias, m_norm_w, conv_w, a_log, dt_bias,
           g_norm_w):
    w_all = jnp.concatenate([w_in[:, _OFF_AQ:_OFF_AZ], w_in[:, _OFF_BQ:_OFF_BIF], w_in[:, _OFF_CQ:_OFF_CAB],
                             _gate_weights(w_in)], axis=1).astype(BF16)
    wz = jnp.concatenate([w_in[:, _OFF_AZ:_OFF_BQ], w_in[:, _OFF_BO:_OFF_CQ], w_in[:, _OFF_CZ:]],
                         axis=1).astype(BF16)
    qkw = jnp.pad(jnp.tile(qk_norm_w, (1, MXU_WIDTH // HEAD_DIM)) * jnp.array([[HEAD_DIM ** -0.5 * LOG2E], [1.0]], F32),
                  ((0, SUBLANES - 2), (0, 0)))
    gpar = _gate_params(i_bias, f_bias, a_log, dt_bias)
    cw = jnp.pad(conv_w, ((0, SUBLANES - CONV_W), (0, 0)))

    aq, *akv, bqkv, cqkv, gates = _in_proj(x, norm_w[None, :], w_all, qkw, gpar, cw)
    oa = _attention(aq, akv, bias_tiles)
    hf, hb, cf, cb = _recurrent_mixers(bqkv, cqkv, gates)
    mw = jnp.tile(m_norm_w, D_B // HEAD_DIM)[None, :]
    gw = jnp.tile(g_norm_w, D_C // HEAD_DIM)[None, :]
    return _out_proj(x, oa, hf, hb, cf, cb, norm_w[None, :], wz, mw, gw, w_out.astype(BF16))


def kernel(x, norm_w, w_in, w_out, qk_norm_w, rel_bias, mlstm_i_bias, mlstm_f_bias, mlstm_norm_w, gdn_conv_w,
           gdn_a_log, gdn_dt_bias, gdn_norm_w):
    bias_tiles = _attn_bias_tiles(rel_bias)
    for l in range(DEPTH):
        x = _layer(x, bias_tiles, norm_w[l], w_in[l], w_out[l], qk_norm_w[l], mlstm_i_bias[l], mlstm_f_bias[l],
                   mlstm_norm_w[l], gdn_conv_w[l], gdn_a_log[l], gdn_dt_bias[l], gdn_norm_w[l])
    return x
```

```python
import functools

import numpy as np
import jax
import jax.numpy as jnp
from jax import lax
from jax.experimental import pallas as pl
from jax.experimental.pallas import tpu as pltpu

F32 = jnp.float32
BF16 = jnp.bfloat16

D_MODEL = 1024
DEPTH = 2
EPS = 1e-6
HEAD_DIM = 64
LANES = 128
SUBLANES = 8
MXU_WIDTH = 256
D_A, D_B, D_C = 512, 256, 256
DILATED_CFGS = ((128, 1), (512, 4), (2048, 16))
N_SIDE = 64
NUM_BUCKETS = 32
REL_MAX_DIST = 1024
CONV_W = 5
CHUNK = 256
Q_SUPER = 2048
Q_BLK = 128
K_WIN = 256
ATTN_ILP = 2
ATTN_DEPTH = 1
IN_TILE = 512
OUT_TILE = 1024
NEG = -1e30
LOG2E = 1.4426950408889634
VMEM_LIMIT = 56 * 1024 * 1024

_OFF_AQ, _OFF_AZ = 0, 1536
_OFF_BQ, _OFF_BIF, _OFF_BO, _OFF_BZ = 2048, 2816, 2832, 3088
_OFF_CQ, _OFF_CAB, _OFF_CZ = 3344, 4112, 4128
_W_SPLITS = (0, 3 * D_A, 3 * D_A + 3 * D_B, 3 * D_A + 3 * D_B + 3 * D_C, 3 * D_A + 3 * D_B + 3 * D_C + LANES)


def _dot(a, b):
    return jnp.dot(a, b, preferred_element_type=F32)


def _dot_nt(a, b):
    return lax.dot_general(a, b, (((1,), (1,)), ((), ())), preferred_element_type=F32)


def _dot_tn(a, b):
    return lax.dot_general(a, b, (((0,), (0,)), ((), ())), preferred_element_type=F32)


def _split(a):
    hi = a.astype(BF16)
    lo = (a - hi.astype(F32)).astype(BF16)
    return hi, lo


def _iota2(shape, axis):
    return lax.broadcasted_iota(jnp.int32, shape, axis)


def _pair_block_diag(rows, cols):
    r = _iota2((rows, cols), 0)
    c = _iota2((rows, cols), 1)
    return (r < HEAD_DIM) == ((c & HEAD_DIM) == 0)


def _head_block_diag(n):
    return (_iota2((n, n), 0) ^ _iota2((n, n), 1)) < HEAD_DIM


def _head_sum(t, bd):
    return _dot(t.astype(BF16), bd)


def _softplus(y):
    return jnp.maximum(y, 0.0) + jnp.log1p(jnp.exp(-jnp.abs(y)))


def _sigmoid(y):
    return 1.0 / (1.0 + jnp.exp(-y))


def _in_proj_kernel(x_ref, xp_ref, xn_ref, nw_ref, w_ref, qkw_ref, gpar_ref, cw_ref,
                    a_ref, kv1_ref, kv4_ref, kv16_ref, b_ref, c_ref, g_ref, cext_ref, kvs_ref, hn_ref, kvg_ref):
    kv_refs = (kv1_ref, kv4_ref, kv16_ref)
    wa_ref, wb_ref, wc_ref, wg_ref = (w_ref.at[:, lo:hi] for lo, hi in zip(_W_SPLITS[:-1], _W_SPLITS[1:]))
    i = pl.program_id(1)
    n = pl.num_programs(1)
    tile = x_ref.shape[1]
    bd = _head_block_diag(MXU_WIDTH).astype(BF16)

    xe = jnp.concatenate([xp_ref[0], x_ref[0], xn_ref[0]], axis=0)
    ms = jnp.mean(xe * xe, axis=-1, keepdims=True)
    hne = xe * lax.rsqrt(ms + EPS) * nw_ref[...]
    hn_ref[...] = hne[SUBLANES:SUBLANES + tile].astype(BF16)
    hne = hne.astype(BF16)

    ce = _dot(hne, wc_ref[...])
    ta = [_dot(hn_ref[...], wa_ref[:, j * MXU_WIDTH:(j + 1) * MXU_WIDTH]) for j in range(3 * D_A // MXU_WIDTH)]
    p = _dot(hn_ref[...], wg_ref[...]) + gpar_ref[0:1, :]
    ssa = [_head_sum(t * t, bd) for t in ta[:4]]

    row = _iota2((tile + 2 * SUBLANES, 1), 0)
    valid = ((row >= SUBLANES) | (i > 0)) & ((row < tile + SUBLANES) | (i < n - 1))
    cext_ref[...] = jnp.where(valid, ce, 0.0)
    conv = cw_ref[0:1, :] * cext_ref[pl.ds(SUBLANES - CONV_W // 2, tile), :]
    for t in range(1, CONV_W):
        conv = conv + cw_ref[t:t + 1, :] * cext_ref[pl.ds(SUBLANES - CONV_W // 2 + t, tile), :]
    s = conv * _sigmoid(conv)
    tc = [s[:, j * MXU_WIDTH:(j + 1) * MXU_WIDTH] for j in range(3 * D_C // MXU_WIDTH)]

    ssc = [_head_sum(t * t, bd) for t in tc[:2]]
    tb = _dot(hn_ref[...], wb_ref[...])

    for j, t in enumerate(ta):
        if j < 4:
            t = t * lax.rsqrt(ssa[j] * (1.0 / HEAD_DIM) + EPS) * qkw_ref[j // 2:j // 2 + 1, :]
        if j < 2:
            a_ref[0, :, j * MXU_WIDTH:(j + 1) * MXU_WIDTH] = t
        else:
            for half in range(2):
                kvs_ref[2 * (j - 2) + half] = t[:, half * LANES:(half + 1) * LANES]
    for lt in range(kvs_ref.shape[0]):
        lanes = slice(lt * LANES, (lt + 1) * LANES)
        kv_refs[0][0, :, lanes] = kvs_ref[lt].astype(BF16)
        prev_dil, src = 1, [kvs_ref.at[lt]]
        for ci in range(1, len(DILATED_CFGS)):
            dil = DILATED_CFGS[ci][1]
            step = dil // prev_dil
            rows = tile // dil
            nxt = [None] * dil
            for r in range(dil):
                x = src[r % prev_dil][pl.ds(r // prev_dil, rows, stride=step), :]
                kv_refs[ci][0, r, :, lanes] = x.astype(BF16)
                if ci + 1 < len(DILATED_CFGS):
                    kvg_ref[lt, r] = x
                    nxt[r] = kvg_ref.at[lt, r]
            prev_dil, src = dil, nxt

    b_ref[0, :, 0:D_B] = (tb[:, 0:D_B] * HEAD_DIM ** -0.5).astype(b_ref.dtype)
    b_ref[0, :, D_B:3 * D_B] = tb[:, D_B:3 * D_B].astype(b_ref.dtype)

    lane = _iota2(p.shape, 1)
    first4 = (lane & 7) < 4
    is_b = lane < 16
    val_b = jnp.where(first4, p, -_softplus(-p)) * LOG2E
    val_c = jnp.where(first4, -jnp.exp(gpar_ref[1:2, :]) * _softplus(p) * LOG2E, _sigmoid(p))
    g_ref[0] = jnp.where(is_b, val_b, val_c)

    c_ref[0, :, 0:D_C] = (tc[0] * lax.rsqrt(ssc[0] + EPS) * HEAD_DIM ** -0.5).astype(c_ref.dtype)
    c_ref[0, :, D_C:2 * D_C] = (tc[1] * lax.rsqrt(ssc[1] + EPS)).astype(c_ref.dtype)
    c_ref[0, :, 2 * D_C:3 * D_C] = tc[2].astype(c_ref.dtype)


def _in_proj(x, nw, w_all, qkw, gpar, cw):
    bsz, seq, d = x.shape
    tile = IN_TILE
    nt = seq // tile
    hb = tile // SUBLANES
    full = lambda a: pl.BlockSpec(a.shape, lambda b, i: (0,) * a.ndim)
    row_spec = lambda w: pl.BlockSpec((1, tile, w), lambda b, i: (b, i, 0))
    outs = [((bsz, seq, D_A), F32, row_spec(D_A))]
    for (_, dil) in DILATED_CFGS:
        if dil == 1:
            outs.append(((bsz, seq, 2 * D_A), BF16, row_spec(2 * D_A)))
        else:
            outs.append(((bsz, dil, seq // dil, 2 * D_A), BF16,
                         pl.BlockSpec((1, dil, tile // dil, 2 * D_A), lambda b, i: (b, 0, i, 0))))
    outs += [((bsz, seq, 3 * D_B), BF16, row_spec(3 * D_B)), ((bsz, seq, 3 * D_C), BF16, row_spec(3 * D_C)),
             ((bsz, seq, LANES), F32, row_spec(LANES))]
    return pl.pallas_call(
        _in_proj_kernel,
        grid=(bsz, nt),
        in_specs=[
            row_spec(d),
            pl.BlockSpec((1, SUBLANES, d), lambda b, i: (b, jnp.maximum(i * hb - 1, 0), 0)),
            pl.BlockSpec((1, SUBLANES, d), lambda b, i: (b, jnp.minimum((i + 1) * hb, nt * hb - 1), 0)),
            full(nw), full(w_all), full(qkw), full(gpar), full(cw),
        ],
        out_specs=[spec for _, _, spec in outs],
        out_shape=[jax.ShapeDtypeStruct(shape, dtype) for shape, dtype, _ in outs],
        scratch_shapes=[pltpu.VMEM((tile + 2 * SUBLANES, 3 * D_C), F32), pltpu.VMEM((2 * D_A // LANES, tile, LANES), F32),
                        pltpu.VMEM((tile, d), BF16),
                        pltpu.VMEM((2 * D_A // LANES, DILATED_CFGS[1][1], tile // DILATED_CFGS[1][1], LANES), F32)],
        compiler_params=pltpu.CompilerParams(
            dimension_semantics=("parallel", "arbitrary"), vmem_limit_bytes=VMEM_LIMIT),
        name="in_proj",
    )(x, x, x, nw, w_all, qkw, gpar, cw)


def _t5_bucket(rel):
    half = NUM_BUCKETS // 2
    max_exact = half // 2
    n = np.abs(rel)
    large = max_exact + (np.log(np.maximum(n, 1) / max_exact) / np.log(REL_MAX_DIST / max_exact)
                         * (half - max_exact)).astype(np.int32)
    large = np.minimum(large, half - 1)
    return (rel > 0).astype(np.int32) * half + np.where(n < max_exact, n, large)


def _attn_bias_tiles(rel_bias):
    period = 5 * LANES + 1
    wide = K_WIN + 2 * N_SIDE
    lead = Q_BLK - 1 + N_SIDE
    cfg_tiles = []
    for (_, dil) in DILATED_CFGS:
        offs = dil * np.arange(-N_SIDE, N_SIDE + 1)
        bias = rel_bias[_t5_bucket(offs)].T.astype(F32) * LOG2E
        heads = bias.shape[0]
        vec = jnp.concatenate([jnp.full((heads, lead + N_SIDE), NEG, F32), bias,
                               jnp.full((heads, period - lead - 3 * N_SIDE - 1), NEG, F32)], axis=1)
        rows = jnp.tile(vec, (1, Q_BLK))[:, :Q_BLK * (period - 1)].reshape(heads, Q_BLK, period - 1)
        ext = rows[:, :, lead:lead + wide]
        variants = [ext[:, :, 2 * N_SIDE:2 * N_SIDE + K_WIN], ext[:, :, N_SIDE:N_SIDE + K_WIN],
                    ext[:, :, 0:K_WIN]]
        cfg_tiles.append(jnp.stack(variants, axis=1))
    return jnp.stack(cfg_tiles, axis=0)


def _attn_kernel(q_ref, *rest):
    ncfg = len(DILATED_CFGS)
    kv_refs = [(rest[2 * c], rest[2 * c + 1]) for c in range(ncfg)]
    bias_ref, o_ref, acc_ref, m_ref, l_ref = rest[2 * ncfg:]
    sb = pl.program_id(2)
    seq = kv_refs[0][0].shape[1]
    qsup = q_ref.shape[1]
    nblk = qsup // Q_BLK
    head0 = _iota2((Q_BLK, LANES), 1) < HEAD_DIM
    sel0 = (_iota2((Q_BLK, 2 * LANES), 1) & HEAD_DIM) == 0

    groups = [(pos, ci, g) for pos, ci in enumerate(reversed(range(len(DILATED_CFGS))))
              for g in range(nblk // ATTN_ILP)]

    def scores(pos, ci, g):
        dil = DILATED_CFGS[ci][1]
        n_idx = seq // dil
        blocks = []
        for u in range(ATTN_ILP):
            t = g * ATTN_ILP + u
            r, j = t % dil, t // dil
            qs = r + dil * (j * Q_BLK)
            i0 = sb * (qsup // dil) + j * Q_BLK
            ws = jnp.clip(i0 - N_SIDE, 0, n_idx - K_WIN)
            var = jnp.where(i0 == 0, 0, jnp.where(i0 == n_idx - Q_BLK, 2, 1))
            qrows = pl.ds(qs, Q_BLK) if dil == 1 else pl.ds(qs, Q_BLK, stride=dil)
            krows = pl.ds(pl.multiple_of(ws, N_SIDE), K_WIN)
            k_ref, v_ref = kv_refs[ci]
            if dil == 1:
                k2, v2 = k_ref[0, krows, :], v_ref[0, krows, :]
            else:
                k2, v2 = k_ref[0, r, krows, :], v_ref[0, r, krows, :]
            q2 = q_ref[0, qrows, :].astype(BF16)
            blocks.append(dict(qrows=qrows, var=var, k2=k2,
                               qh=[jnp.where(head0 if h == 0 else ~head0, q2, jnp.zeros_like(q2)) for h in range(2)],
                               vo=jnp.concatenate([v2, jnp.ones_like(v2)], axis=1)))
        chains = [(blk, h) for blk in blocks for h in range(2)]
        s = [_dot_nt(blk["qh"][h], blk["k2"]) + bias_ref[ci, h, blk["var"]] for blk, h in chains]
        mx = [jnp.max(sh, axis=-1, keepdims=True) for sh in s]
        p = [jnp.exp2(sh - m).astype(BF16) for sh, m in zip(s, mx)]
        return dict(pos=pos, blocks=blocks, chains=chains, mx=mx, p=p)

    def accumulate(st):
        pv = [_dot(ph, blk["vo"]) for ph, (blk, _) in zip(st["p"], st["chains"])]
        for u, blk in enumerate(st["blocks"]):
            qrows = blk["qrows"]
            tot = jnp.where(sel0, pv[2 * u], pv[2 * u + 1])
            num2 = tot[:, :LANES]
            l2 = tot[:, LANES:]
            m2 = jnp.where(head0, st["mx"][2 * u], st["mx"][2 * u + 1])
            if st["pos"] == 0:
                acc_ref[qrows, :] = num2
                m_ref[qrows, :] = m2
                l_ref[qrows, :] = l2
            else:
                mo = m_ref[qrows, :]
                mn = jnp.maximum(mo, m2)
                a = jnp.exp2(mo - mn)
                b = jnp.exp2(m2 - mn)
                acc_ref[qrows, :] = acc_ref[qrows, :] * a + num2 * b
                l_ref[qrows, :] = l_ref[qrows, :] * a + l2 * b
                m_ref[qrows, :] = mn

    in_flight = []
    for grp in groups:
        in_flight.append(scores(*grp))
        if len(in_flight) > ATTN_DEPTH:
            accumulate(in_flight.pop(0))
    for st in in_flight:
        accumulate(st)

    o_ref[0] = (acc_ref[...] / l_ref[...]).astype(o_ref.dtype)


def _attention(aq, akv, bias_tiles):
    bsz, seq, _ = aq.shape
    npair = D_A // LANES
    qsup = Q_SUPER
    assert seq % qsup == 0 and seq // DILATED_CFGS[-1][1] >= K_WIN
    in_specs = [pl.BlockSpec((1, qsup, LANES), lambda b, p, s: (b, s, p))]
    operands = [aq]
    for kv, (_, dil) in zip(akv, DILATED_CFGS):
        for part in range(2):
            if dil == 1:
                in_specs.append(pl.BlockSpec((1, seq, LANES), lambda b, p, s, part=part: (b, 0, part * npair + p)))
            else:
                in_specs.append(pl.BlockSpec((1, dil, seq // dil, LANES),
                                             lambda b, p, s, part=part: (b, 0, 0, part * npair + p)))
            operands.append(kv)
    in_specs.append(pl.BlockSpec((len(DILATED_CFGS), 2, 3, Q_BLK, K_WIN), lambda b, p, s: (0, p, 0, 0, 0)))
    return pl.pallas_call(
        _attn_kernel,
        grid=(bsz, npair, seq // qsup),
        in_specs=in_specs,
        out_specs=pl.BlockSpec((1, qsup, LANES), lambda b, p, s: (b, s, p)),
        out_shape=jax.ShapeDtypeStruct((bsz, seq, D_A), BF16),
        scratch_shapes=[pltpu.VMEM((qsup, LANES), F32)] * 3,
        compiler_params=pltpu.CompilerParams(
            dimension_semantics=("parallel", "parallel", "arbitrary"), vmem_limit_bytes=VMEM_LIMIT),
        name="dilated_attn",
    )(*operands, bias_tiles)


def _direction_masks(backward):
    row = _iota2((CHUNK, CHUNK), 0)
    col = _iota2((CHUNK, CHUNK), 1)
    if backward:
        return row <= col, row < col, row ^ col
    return row >= col, row > col, row ^ col


def _gate_cumsums(g, incl, incl_other):
    tri = incl.astype(BF16)
    tri_t = incl_other.astype(BF16)
    both = _dot(tri, jnp.concatenate(_split(g), axis=1))
    gb = both[:, :LANES] + both[:, LANES:]
    gt = g.T
    thi, tlo = _split(gt)
    gbt = _dot(thi, tri_t) + _dot(tlo, tri_t)
    return gb, gt, gbt


def _gate_lane(mixer, pair, kind, backward, head):
    return 16 * mixer + 8 * pair + 4 * kind + 2 * int(backward) + head


def _mlstm_stages(qf, kf, vf, gf, qb, kb, vb, gb, hf_ref, hb_ref, s_ref, m_ref, fwd_masks, bwd_masks):
    npair = D_B // LANES
    head0 = _iota2((CHUNK, LANES), 1) < HEAD_DIM
    sel0 = (_iota2((CHUNK, 2 * LANES), 1) & HEAD_DIM) == 0
    bd = _pair_block_diag(LANES, 2 * LANES)
    wide = lambda x: jnp.concatenate([x, x], axis=1)

    groups = []
    for backward, (q_ref, k_ref, v_ref, g_ref), h_ref in ((False, (qf, kf, vf, gf), hf_ref),
                                                          (True, (qb, kb, vb, gb), hb_ref)):
        incl = (bwd_masks if backward else fwd_masks)[0]
        incl_other = (fwd_masks if backward else bwd_masks)[0]
        last = 0 if backward else CHUNK - 1
        g, gbc, gt, gbt = g_ref
        i_lanes = [_gate_lane(0, p, 0, backward, h) for p in range(npair) for h in range(2)]
        f_lanes = [_gate_lane(0, p, 1, backward, h) for p in range(npair) for h in range(2)]
        for p in range(npair):
            idx = 2 * int(backward) + p
            heads = []
            for h in range(2):
                j = 2 * p + h
                li = jnp.broadcast_to(g[:, i_lanes[j]:i_lanes[j] + 1], (CHUNK, LANES))
                bc = jnp.broadcast_to(gbc[:, f_lanes[j]:f_lanes[j] + 1], (CHUNK, LANES))
                rrow = gt[i_lanes[j]:i_lanes[j] + 1, :] - gbt[f_lanes[j]:f_lanes[j] + 1, :]
                mst = m_ref[idx, h:h + 1, :]
                dmat = jnp.where(incl, wide(bc) + rrow, -jnp.inf)
                inter = bc + mst
                mt = jnp.maximum(jnp.max(dmat, axis=-1, keepdims=True), inter)
                blast = bc[last:last + 1, :]
                wlog = blast - bc + li
                mn = jnp.maximum(blast + mst, jnp.max(wlog, axis=0, keepdims=True))
                heads.append(dict(decay=jnp.exp2(dmat - wide(mt)), iw=jnp.exp2(inter - mt), emt=jnp.exp2(-mt),
                                  ws=jnp.exp2(wlog - mn), dec=jnp.exp2(blast + mst - mn), mnew=mn))
            cols = slice(p * LANES, (p + 1) * LANES)
            v2 = v_ref[0, :, cols]
            groups.append(dict(q2=q_ref[0, :, cols], k2=k_ref[0, :, cols], heads=heads, state=s_ref[idx], idx=idx,
                               h_ref=h_ref, cols=cols,
                               vo=jnp.concatenate([v2, jnp.ones_like(v2)], axis=1)))
            yield

    chains = [(d, h) for d in groups for h in range(2)]
    q_state = [_dot(d["q2"], d["state"].astype(BF16)) for d in groups]
    yield
    qk = [_dot_nt(jnp.where(head0 if h == 0 else ~head0, d["q2"], jnp.zeros_like(d["q2"])), d["k2"])
          for d, h in chains]
    yield
    sc = [(qkh * d["heads"][h]["decay"]).astype(BF16) for (d, h), qkh in zip(chains, qk)]
    yield
    pv = [_dot(sch, d["vo"]) for (d, _), sch in zip(chains, sc)]
    yield
    upd = [_dot_tn(d["k2"], (jnp.where(sel0, wide(d["heads"][0]["ws"]), wide(d["heads"][1]["ws"]))
                             * d["vo"].astype(F32)).astype(BF16)) for d in groups]
    yield
    for i, d in enumerate(groups):
        h0, h1 = d["heads"]
        tot = jnp.where(sel0, wide(h0["iw"]), wide(h1["iw"])) * q_state[i] + jnp.where(sel0, pv[2 * i], pv[2 * i + 1])
        den = jnp.maximum(jnp.abs(tot[:, LANES:]), jnp.where(head0, h0["emt"], h1["emt"]))
        d["h_ref"][0, :, d["cols"]] = (tot[:, :LANES] / den).astype(d["h_ref"].dtype)
        dec2 = jnp.where(sel0[0:1, :], wide(h0["dec"]), wide(h1["dec"]))
        s_ref[d["idx"]] = dec2 * d["state"] + jnp.where(bd, upd[i], 0.0)
        m_ref[d["idx"], 0:1, :] = h0["mnew"]
        m_ref[d["idx"], 1:2, :] = h1["mnew"]


_INV_BASE = 32


def _unit_tri_inverses(ms, xor, tick, upto):
    diag = xor == 0
    one = jnp.ones((), BF16)
    zero = jnp.zeros((), BF16)
    base = xor < _INV_BASE
    pbs = [jnp.where(base, m, zero) for m in ms]
    ts = [jnp.where(diag, one, -pb) for pb in pbs]
    k = 1
    while True:
        pbs = [_dot(pb, pb).astype(BF16) for pb in pbs]
        tick()
        k *= 2
        ts = [_dot(t, jnp.where(diag, one, pb)).astype(BF16) for t, pb in zip(ts, pbs)]
        tick()
        if 2 * k >= _INV_BASE:
            break
    b = _INV_BASE
    while b < upto:
        level = (xor >= b) & (xor < 2 * b)
        xs = [_dot(jnp.where(level, m, zero), t).astype(BF16) for m, t in zip(ms, ts)]
        tick()
        ts = [_dot(t, jnp.where(diag, one, -x)).astype(BF16) for t, x in zip(ts, xs)]
        tick()
        b *= 2
    return ts


def _gdn_step(qf, kf, vf, gf, qb, kb, vb, gb, of_ref, ob_ref, s_ref, fwd_masks, bwd_masks, tick):
    npair = D_C // LANES
    head0 = _iota2((CHUNK, LANES), 1) < HEAD_DIM
    sel0 = (_iota2((CHUNK, 2 * LANES), 1) & HEAD_DIM) == 0
    bd = _pair_block_diag(LANES, LANES)
    xor = fwd_masks[2]

    dirs = []
    for backward, (q_ref, k_ref, v_ref, g_ref), o_ref in ((False, (qf, kf, vf, gf), of_ref),
                                                          (True, (qb, kb, vb, gb), ob_ref)):
        incl, strict, _ = bwd_masks if backward else fwd_masks
        incl_other = (fwd_masks if backward else bwd_masks)[0]
        last = 0 if backward else CHUNK - 1
        g, gbc, _, gbt = g_ref
        for p in range(npair):
            cols = slice(p * LANES, (p + 1) * LANES)
            q2, k2, v2 = q_ref[0, :, cols], k_ref[0, :, cols], v_ref[0, :, cols]
            a_lane = [_gate_lane(1, p, 0, backward, h) for h in range(2)]
            b_lane = [_gate_lane(1, p, 1, backward, h) for h in range(2)]
            gam = [gbc[:, a:a + 1] for a in a_lane]
            beta = [g[:, b:b + 1] for b in b_lane]
            glast2 = jnp.where(head0[0:1, :], gbc[last:last + 1, a_lane[0]:a_lane[0] + 1],
                               gbc[last:last + 1, a_lane[1]:a_lane[1] + 1])
            gam2 = jnp.where(head0, gam[0], gam[1])
            beta2 = jnp.where(head0, beta[0], beta[1])
            egam2 = jnp.exp2(gam2)
            kf32 = k2.astype(F32)
            rhs = jnp.concatenate([v2.astype(F32) * beta2, kf32 * (beta2 * egam2)], axis=1).astype(BF16)
            decay = [jnp.exp2(jnp.where(incl, gam[h] - gbt[a_lane[h]:a_lane[h] + 1, :], -jnp.inf)) for h in range(2)]
            idx = 2 * int(backward) + p
            state = s_ref[idx]
            dirs.append(dict(q2=q2, k2=k2, kf32=kf32, rhs=rhs, decay=decay, beta=beta, strict=strict, gam2=gam2,
                             egam2=egam2, glast2=glast2, state=state, state_b=state.astype(BF16), o_ref=o_ref,
                             cols=cols, idx=idx, backward=backward))

    chains = [(d, h) for d in dirs for h in range(2)]

    def head_lanes(x, h):
        return jnp.where(head0 if h == 0 else ~head0, x, 0).astype(BF16)

    kk = [_dot_nt(head_lanes(d["k2"], h), d["k2"]) for d, h in chains]
    tick()
    ms = [jnp.where(d["strict"], d["beta"][h] * kkh * d["decay"][h], 0.0).astype(BF16)
          for (d, h), kkh in zip(chains, kk)]
    qk = [_dot_nt(head_lanes(d["q2"], h), d["k2"]) for d, h in chains]
    tick()
    a_intra = [(qkh * d["decay"][h]).astype(BF16) for (d, h), qkh in zip(chains, qk)]
    half = CHUNK // 2
    t_half = _unit_tri_inverses(ms, xor, tick, upto=half)
    y = [_dot(t, d["rhs"]) for (d, _), t in zip(chains, t_half)]
    tick()
    early = [slice(half, CHUNK) if d["backward"] else slice(0, half) for d, _ in chains]
    late = [slice(0, half) if d["backward"] else slice(half, CHUNK) for d, _ in chains]
    z = [_dot(m[lt, el], yj[el].astype(BF16)).astype(BF16) for m, yj, el, lt in zip(ms, y, early, late)]
    tick()
    corr = [_dot(t[lt, lt], zj) for t, zj, lt in zip(t_half, z, late)]
    tick()
    uw = [jnp.concatenate([yj[el], yj[lt] - c][::-1] if d["backward"] else [yj[el], yj[lt] - c], axis=0)
          for (d, _), yj, c, el, lt in zip(chains, y, corr, early, late)]

    uw2 = [jnp.where(sel0, uw[2 * i], uw[2 * i + 1]) for i in range(len(dirs))]
    qg = [(d["q2"].astype(F32) * d["egam2"]).astype(BF16) for d in dirs]
    zeros = jnp.zeros((LANES, LANES), BF16)
    w_state, q_state = [None] * len(dirs), [None] * len(dirs)
    for i in range(0, len(dirs), npair):
        group = range(i, i + npair)
        state_bd = jnp.concatenate([jnp.concatenate([dirs[j]["state_b"] if j == k else zeros for k in group], axis=1)
                                    for j in group], axis=0)
        ws_cat = _dot(jnp.concatenate([uw2[j][:, LANES:].astype(BF16) for j in group], axis=1), state_bd)
        qs_cat = _dot(jnp.concatenate([qg[j] for j in group], axis=1), state_bd)
        for n, j in enumerate(group):
            w_state[j] = ws_cat[:, n * LANES:(n + 1) * LANES]
            q_state[j] = qs_cat[:, n * LANES:(n + 1) * LANES]
    tick()
    v_new = [(uw2[i][:, :LANES] - w_state[i]).astype(BF16) for i in range(len(dirs))]
    av = [_dot(a_intra[j], v_new[j // 2]) for j in range(len(chains))]
    tick()
    upd = [_dot_tn((d["kf32"] * jnp.exp2(d["glast2"] - d["gam2"])).astype(BF16), v_new[i])
           for i, d in enumerate(dirs)]
    for i, d in enumerate(dirs):
        d["o_ref"][0, :, d["cols"]] = (q_state[i] + jnp.where(head0, av[2 * i], av[2 * i + 1])).astype(d["o_ref"].dtype)
        s_ref[d["idx"]] = d["state"] * jnp.exp2(d["glast2"]) + jnp.where(bd, upd[i], 0.0)


def _recurrent_kernel(bqf, bkf, bvf, cqf, ckf, cvf, gf, bqb, bkb, bvb, cqb, ckb, cvb, gb,
                      hf_ref, hb_ref, of_ref, ob_ref, ms_ref, mm_ref, gs_ref):
    @pl.when(pl.program_id(1) == 0)
    def _():
        ms_ref[...] = jnp.zeros_like(ms_ref)
        mm_ref[...] = jnp.zeros_like(mm_ref)
        gs_ref[...] = jnp.zeros_like(gs_ref)

    fwd_masks = _direction_masks(False)
    bwd_masks = _direction_masks(True)
    gf = (gf[0],) + _gate_cumsums(gf[0], fwd_masks[0], bwd_masks[0])
    gb = (gb[0],) + _gate_cumsums(gb[0], bwd_masks[0], fwd_masks[0])
    mlstm = _mlstm_stages(bqf, bkf, bvf, gf, bqb, bkb, bvb, gb, hf_ref, hb_ref, ms_ref, mm_ref, fwd_masks, bwd_masks)
    _gdn_step(cqf, ckf, cvf, gf, cqb, ckb, cvb, gb, of_ref, ob_ref, gs_ref, fwd_masks, bwd_masks,
              lambda: next(mlstm, None))
    for _ in mlstm:
        pass


def _recurrent_mixers(bqkv, cqkv, gates):
    bsz, seq, _ = bqkv.shape
    nc = seq // CHUNK
    nstate = 2 * (D_B // LANES)
    in_specs = []
    for rev in (False, True):
        cidx = (lambda c: nc - 1 - c) if rev else (lambda c: c)
        for width in (D_B, D_C):
            for part in range(3):
                in_specs.append(pl.BlockSpec((1, CHUNK, width), lambda b, c, part=part, cidx=cidx: (b, cidx(c), part)))
        in_specs.append(pl.BlockSpec((1, CHUNK, LANES), lambda b, c, cidx=cidx: (b, cidx(c), 0)))
    out_b = jax.ShapeDtypeStruct((bsz, seq, D_B), BF16)
    out_c = jax.ShapeDtypeStruct((bsz, seq, D_C), BF16)
    fwd_spec = lambda w: pl.BlockSpec((1, CHUNK, w), lambda b, c: (b, c, 0))
    bwd_spec = lambda w: pl.BlockSpec((1, CHUNK, w), lambda b, c: (b, nc - 1 - c, 0))
    return pl.pallas_call(
        _recurrent_kernel,
        grid=(bsz, nc),
        in_specs=in_specs,
        out_specs=[fwd_spec(D_B), bwd_spec(D_B), fwd_spec(D_C), bwd_spec(D_C)],
        out_shape=(out_b, out_b, out_c, out_c),
        scratch_shapes=[pltpu.VMEM((nstate, LANES, 2 * LANES), F32), pltpu.VMEM((nstate, SUBLANES, LANES), F32),
                        pltpu.VMEM((nstate, LANES, LANES), F32)],
        compiler_params=pltpu.CompilerParams(
            dimension_semantics=("parallel", "arbitrary"), vmem_limit_bytes=VMEM_LIMIT),
        name="recurrent_mixers",
    )(bqkv, bqkv, bqkv, cqkv, cqkv, cqkv, gates, bqkv, bqkv, bqkv, cqkv, cqkv, cqkv, gates)


def _out_kernel(x_ref, oa_ref, hf_ref, hb_ref, cf_ref, cb_ref, nw_ref, wz_ref, mw_ref, gw_ref, wo_ref, y_ref):
    bd = _head_block_diag(MXU_WIDTH).astype(BF16)
    x = x_ref[0]
    hn = (x * lax.rsqrt(jnp.mean(x * x, axis=-1, keepdims=True) + EPS) * nw_ref[...]).astype(BF16)
    z = _dot(hn, wz_ref[...])

    def silu(t):
        return t * _sigmoid(t)

    hsum = _sigmoid(z[:, D_A:D_A + D_B]) * (hf_ref[0].astype(F32) + hb_ref[0].astype(F32))
    csum = cf_ref[0].astype(F32) + cb_ref[0].astype(F32)
    ssb = _head_sum(hsum * hsum, bd)
    ssc = _head_sum(csum * csum, bd)
    ya = oa_ref[0].astype(F32) * silu(z[:, 0:D_A])
    yb = hsum * lax.rsqrt(ssb * (1.0 / HEAD_DIM) + EPS) * mw_ref[...] * silu(z[:, D_A + D_B:D_A + 2 * D_B])
    yc = csum * lax.rsqrt(ssc * (1.0 / HEAD_DIM) + EPS) * gw_ref[...] * silu(z[:, D_A + 2 * D_B:])
    y = jnp.concatenate([ya, yb, yc], axis=1).astype(BF16)
    y_ref[0] = x + _dot(y, wo_ref[...])


def _out_proj(x, oa, hf, hb, cf, cb, nw, wz, mw, gw, wo):
    bsz, seq, d = x.shape
    tile = OUT_TILE
    row_spec = lambda a: pl.BlockSpec((1, tile, a.shape[-1]), lambda b, i: (b, i, 0))
    full = lambda a: pl.BlockSpec(a.shape, lambda b, i: (0,) * a.ndim)
    return pl.pallas_call(
        _out_kernel,
        grid=(bsz, seq // tile),
        in_specs=[row_spec(a) for a in (x, oa, hf, hb, cf, cb)] + [full(a) for a in (nw, wz, mw, gw, wo)],
        out_specs=pl.BlockSpec((1, tile, d), lambda b, i: (b, i, 0)),
        out_shape=jax.ShapeDtypeStruct(x.shape, x.dtype),
        compiler_params=pltpu.CompilerParams(
            dimension_semantics=("parallel", "parallel"), vmem_limit_bytes=VMEM_LIMIT),
        name="out_proj",
    )(x, oa, hf, hb, cf, cb, nw, wz, mw, gw, wo)


def _to_gate_lanes(t):
    lead = t.shape[:-3]
    t = t.reshape(lead + (4, 2, 2))
    return jnp.swapaxes(t, -3, -2).reshape(lead + (16,))


def _gate_weights(w_in):
    nh = D_B // HEAD_DIM
    per_mixer = [_to_gate_lanes(w_in[:, off:off + 4 * nh].reshape(-1, 2, 2, nh)) for off in (_OFF_BIF, _OFF_CAB)]
    return jnp.pad(jnp.concatenate(per_mixer, axis=1), ((0, 0), (0, LANES - 2 * 16)))


def _gate_params(i_bias, f_bias, a_log, dt_bias):
    zeros = jnp.zeros_like(dt_bias)
    row0 = jnp.concatenate([_to_gate_lanes(jnp.stack([i_bias, f_bias])), _to_gate_lanes(jnp.stack([dt_bias, zeros]))])
    row1 = jnp.concatenate([jnp.zeros((16,), F32), _to_gate_lanes(jnp.stack([a_log, zeros]))])
    return jnp.pad(jnp.stack([row0, row1]), ((0, SUBLANES - 2), (0, LANES - 2 * 16)))


def _pack_params(norm_w, w_in, w_out, qk_norm_w, i_bias, f_bias, m_norm_w, conv_w, a_log, dt_bias, g_norm_w):
    w_all = jnp.concatenate([w_in[:, _OFF_AQ:_OFF_AZ], w_in[:, _OFF_BQ:_OFF_BIF], w_in[:, _OFF_CQ:_OFF_CAB],
                             _gate_weights(w_in)], axis=1).astype(BF16)
    wz = jnp.concatenate([w_in[:, _OFF_AZ:_OFF_BQ], w_in[:, _OFF_BO:_OFF_CQ], w_in[:, _OFF_CZ:]],
                         axis=1).astype(BF16)
    qkw = jnp.pad(jnp.tile(qk_norm_w, (1, MXU_WIDTH // HEAD_DIM)) * jnp.array([[HEAD_DIM ** -0.5 * LOG2E], [1.0]], F32),
                  ((0, SUBLANES - 2), (0, 0)))
    return dict(nw=norm_w[None, :], w_all=w_all, wz=wz, wo=w_out.astype(BF16), qkw=qkw,
                gpar=_gate_params(i_bias, f_bias, a_log, dt_bias),
                cw=jnp.pad(conv_w, ((0, SUBLANES - CONV_W), (0, 0))),
                mw=jnp.tile(m_norm_w, D_B // HEAD_DIM)[None, :], gw=jnp.tile(g_norm_w, D_C // HEAD_DIM)[None, :])


def _layer(x, bias_tiles, p):
    aq, *akv, bqkv, cqkv, gates = _in_proj(x, p["nw"], p["w_all"], p["qkw"], p["gpar"], p["cw"])
    oa = _attention(aq, akv, bias_tiles)
    hf, hb, cf, cb = _recurrent_mixers(bqkv, cqkv, gates)
    return _out_proj(x, oa, hf, hb, cf, cb, p["nw"], p["wz"], p["mw"], p["gw"], p["wo"])


def kernel(x, norm_w, w_in, w_out, qk_norm_w, rel_bias, mlstm_i_bias, mlstm_f_bias, mlstm_norm_w, gdn_conv_w,
           gdn_a_log, gdn_dt_bias, gdn_norm_w):
    bias_tiles = _attn_bias_tiles(rel_bias)
    params = jax.vmap(_pack_params)(norm_w, w_in, w_out, qk_norm_w, mlstm_i_bias, mlstm_f_bias, mlstm_norm_w,
                                    gdn_conv_w, gdn_a_log, gdn_dt_bias, gdn_norm_w)
    for l in range(DEPTH):
        x = _layer(x, bias_tiles, {name: a[l] for name, a in params.items()})
    return x
```

```python
import functools

import numpy as np
import jax
import jax.numpy as jnp
from jax import lax
from jax.experimental import pallas as pl
from jax.experimental.pallas import tpu as pltpu

F32 = jnp.float32
BF16 = jnp.bfloat16

D_MODEL = 1024
DEPTH = 2
EPS = 1e-6
HEAD_DIM = 64
LANES = 128
SUBLANES = 8
MXU_WIDTH = 256
D_A, D_B, D_C = 512, 256, 256
DILATED_CFGS = ((128, 1), (512, 4), (2048, 16))
N_SIDE = 64
NUM_BUCKETS = 32
REL_MAX_DIST = 1024
CONV_W = 5
CHUNK = 256
Q_SUPER = 4096
Q_BLK = 128
K_WIN = 256
ATTN_ILP = 2
ATTN_DEPTH = 1
IN_TILE = 512
OUT_TILE = 1024
NEG = -1e30
LOG2E = 1.4426950408889634
VMEM_LIMIT = 56 * 1024 * 1024

_OFF_AQ, _OFF_AZ = 0, 1536
_OFF_BQ, _OFF_BIF, _OFF_BO, _OFF_BZ = 2048, 2816, 2832, 3088
_OFF_CQ, _OFF_CAB, _OFF_CZ = 3344, 4112, 4128
_W_SPLITS = (0, 3 * D_A, 3 * D_A + 3 * D_B, 3 * D_A + 3 * D_B + 3 * D_C, 3 * D_A + 3 * D_B + 3 * D_C + LANES)


def _dot(a, b):
    return jnp.dot(a, b, preferred_element_type=F32)


def _dot_nt(a, b):
    return lax.dot_general(a, b, (((1,), (1,)), ((), ())), preferred_element_type=F32)


def _dot_tn(a, b):
    return lax.dot_general(a, b, (((0,), (0,)), ((), ())), preferred_element_type=F32)


def _split(a):
    hi = a.astype(BF16)
    lo = (a - hi.astype(F32)).astype(BF16)
    return hi, lo


def _iota2(shape, axis):
    return lax.broadcasted_iota(jnp.int32, shape, axis)


def _pair_block_diag(rows, cols):
    r = _iota2((rows, cols), 0)
    c = _iota2((rows, cols), 1)
    return (r < HEAD_DIM) == ((c & HEAD_DIM) == 0)


def _head_block_diag(n):
    return (_iota2((n, n), 0) ^ _iota2((n, n), 1)) < HEAD_DIM


def _head_sum(t, bd):
    return _dot(t.astype(BF16), bd)


def _softplus(y):
    return jnp.maximum(y, 0.0) + jnp.log1p(jnp.exp(-jnp.abs(y)))


def _sigmoid(y):
    return 1.0 / (1.0 + jnp.exp(-y))


def _in_proj_kernel(x_ref, xp_ref, xn_ref, nw_ref, w_ref, qkw_ref, gpar_ref, cw_ref,
                    a_ref, kv1_ref, kv4_ref, kv16_ref, b_ref, c_ref, g_ref, cext_ref, kvs_ref, hn_ref, kvg_ref):
    kv_refs = (kv1_ref, kv4_ref, kv16_ref)
    wa_ref, wb_ref, wc_ref, wg_ref = (w_ref.at[:, lo:hi] for lo, hi in zip(_W_SPLITS[:-1], _W_SPLITS[1:]))
    i = pl.program_id(1)
    n = pl.num_programs(1)
    tile = x_ref.shape[1]
    bd = _head_block_diag(MXU_WIDTH).astype(BF16)

    xe = jnp.concatenate([xp_ref[0], x_ref[0], xn_ref[0]], axis=0)
    ms = jnp.mean(xe * xe, axis=-1, keepdims=True)
    hne = xe * lax.rsqrt(ms + EPS) * nw_ref[...]
    hn_ref[...] = hne[SUBLANES:SUBLANES + tile].astype(BF16)
    hne = hne.astype(BF16)

    ce = _dot(hne, wc_ref[...])
    ta = [_dot(hn_ref[...], wa_ref[:, j * MXU_WIDTH:(j + 1) * MXU_WIDTH]) for j in range(3 * D_A // MXU_WIDTH)]
    p = _dot(hn_ref[...], wg_ref[...]) + gpar_ref[0:1, :]
    ssa = [_head_sum(t * t, bd) for t in ta[:4]]

    row = _iota2((tile + 2 * SUBLANES, 1), 0)
    valid = ((row >= SUBLANES) | (i > 0)) & ((row < tile + SUBLANES) | (i < n - 1))
    cext_ref[...] = jnp.where(valid, ce, 0.0)
    conv = cw_ref[0:1, :] * cext_ref[pl.ds(SUBLANES - CONV_W // 2, tile), :]
    for t in range(1, CONV_W):
        conv = conv + cw_ref[t:t + 1, :] * cext_ref[pl.ds(SUBLANES - CONV_W // 2 + t, tile), :]
    s = conv * _sigmoid(conv)
    tc = [s[:, j * MXU_WIDTH:(j + 1) * MXU_WIDTH] for j in range(3 * D_C // MXU_WIDTH)]

    ssc = [_head_sum(t * t, bd) for t in tc[:2]]
    tb = _dot(hn_ref[...], wb_ref[...])

    for j, t in enumerate(ta):
        if j < 4:
            t = t * lax.rsqrt(ssa[j] * (1.0 / HEAD_DIM) + EPS) * qkw_ref[j // 2:j // 2 + 1, :]
        if j < 2:
            a_ref[0, :, j * MXU_WIDTH:(j + 1) * MXU_WIDTH] = t
        else:
            for half in range(2):
                kvs_ref[2 * (j - 2) + half] = t[:, half * LANES:(half + 1) * LANES]
    for lt in range(kvs_ref.shape[0]):
        lanes = slice(lt * LANES, (lt + 1) * LANES)
        kv_refs[0][0, :, lanes] = kvs_ref[lt].astype(BF16)
        prev_dil, src = 1, [kvs_ref.at[lt]]
        for ci in range(1, len(DILATED_CFGS)):
            dil = DILATED_CFGS[ci][1]
            step = dil // prev_dil
            rows = tile // dil
            nxt = [None] * dil
            for r in range(dil):
                x = src[r % prev_dil][pl.ds(r // prev_dil, rows, stride=step), :]
                kv_refs[ci][0, r, :, lanes] = x.astype(BF16)
                if ci + 1 < len(DILATED_CFGS):
                    kvg_ref[lt, r] = x
                    nxt[r] = kvg_ref.at[lt, r]
            prev_dil, src = dil, nxt

    b_ref[0, :, 0:D_B] = (tb[:, 0:D_B] * HEAD_DIM ** -0.5).astype(b_ref.dtype)
    b_ref[0, :, D_B:3 * D_B] = tb[:, D_B:3 * D_B].astype(b_ref.dtype)

    lane = _iota2(p.shape, 1)
    first4 = (lane & 7) < 4
    is_b = lane < 16
    val_b = jnp.where(first4, p, -_softplus(-p)) * LOG2E
    val_c = jnp.where(first4, -jnp.exp(gpar_ref[1:2, :]) * _softplus(p) * LOG2E, _sigmoid(p))
    g_ref[0] = jnp.where(is_b, val_b, val_c)

    c_ref[0, :, 0:D_C] = (tc[0] * lax.rsqrt(ssc[0] + EPS) * HEAD_DIM ** -0.5).astype(c_ref.dtype)
    c_ref[0, :, D_C:2 * D_C] = (tc[1] * lax.rsqrt(ssc[1] + EPS)).astype(c_ref.dtype)
    c_ref[0, :, 2 * D_C:3 * D_C] = tc[2].astype(c_ref.dtype)


def _layer_spec(a, layer):
    return pl.BlockSpec((None,) + a.shape[1:], lambda b, i: (layer,) + (0,) * (a.ndim - 1))


def _in_proj(x, layer, nw, w_all, qkw, gpar, cw):
    bsz, seq, d = x.shape
    tile = IN_TILE
    nt = seq // tile
    hb = tile // SUBLANES
    full = lambda a: _layer_spec(a, layer)
    row_spec = lambda w: pl.BlockSpec((1, tile, w), lambda b, i: (b, i, 0))
    outs = [((bsz, seq, D_A), F32, row_spec(D_A))]
    for (_, dil) in DILATED_CFGS:
        if dil == 1:
            outs.append(((bsz, seq, 2 * D_A), BF16, row_spec(2 * D_A)))
        else:
            outs.append(((bsz, dil, seq // dil, 2 * D_A), BF16,
                         pl.BlockSpec((1, dil, tile // dil, 2 * D_A), lambda b, i: (b, 0, i, 0))))
    outs += [((bsz, seq, 3 * D_B), BF16, row_spec(3 * D_B)), ((bsz, seq, 3 * D_C), BF16, row_spec(3 * D_C)),
             ((bsz, seq, LANES), F32, row_spec(LANES))]
    return pl.pallas_call(
        _in_proj_kernel,
        grid=(bsz, nt),
        in_specs=[
            row_spec(d),
            pl.BlockSpec((1, SUBLANES, d), lambda b, i: (b, jnp.maximum(i * hb - 1, 0), 0)),
            pl.BlockSpec((1, SUBLANES, d), lambda b, i: (b, jnp.minimum((i + 1) * hb, nt * hb - 1), 0)),
            full(nw), full(w_all), full(qkw), full(gpar), full(cw),
        ],
        out_specs=[spec for _, _, spec in outs],
        out_shape=[jax.ShapeDtypeStruct(shape, dtype) for shape, dtype, _ in outs],
        scratch_shapes=[pltpu.VMEM((tile + 2 * SUBLANES, 3 * D_C), F32), pltpu.VMEM((2 * D_A // LANES, tile, LANES), F32),
                        pltpu.VMEM((tile, d), BF16),
                        pltpu.VMEM((2 * D_A // LANES, DILATED_CFGS[1][1], tile // DILATED_CFGS[1][1], LANES), F32)],
        compiler_params=pltpu.CompilerParams(
            dimension_semantics=("parallel", "arbitrary"), vmem_limit_bytes=VMEM_LIMIT),
        name="in_proj",
    )(x, x, x, nw, w_all, qkw, gpar, cw)


def _t5_bucket(rel):
    half = NUM_BUCKETS // 2
    max_exact = half // 2
    n = np.abs(rel)
    large = max_exact + (np.log(np.maximum(n, 1) / max_exact) / np.log(REL_MAX_DIST / max_exact)
                         * (half - max_exact)).astype(np.int32)
    large = np.minimum(large, half - 1)
    return (rel > 0).astype(np.int32) * half + np.where(n < max_exact, n, large)


def _attn_bias_tiles(rel_bias):
    period = 5 * LANES + 1
    wide = K_WIN + 2 * N_SIDE
    lead = Q_BLK - 1 + N_SIDE
    cfg_tiles = []
    for (_, dil) in DILATED_CFGS:
        offs = dil * np.arange(-N_SIDE, N_SIDE + 1)
        bias = rel_bias[_t5_bucket(offs)].T.astype(F32) * LOG2E
        heads = bias.shape[0]
        vec = jnp.concatenate([jnp.full((heads, lead + N_SIDE), NEG, F32), bias,
                               jnp.full((heads, period - lead - 3 * N_SIDE - 1), NEG, F32)], axis=1)
        rows = jnp.tile(vec, (1, Q_BLK))[:, :Q_BLK * (period - 1)].reshape(heads, Q_BLK, period - 1)
        ext = rows[:, :, lead:lead + wide]
        variants = [ext[:, :, 2 * N_SIDE:2 * N_SIDE + K_WIN], ext[:, :, N_SIDE:N_SIDE + K_WIN],
                    ext[:, :, 0:K_WIN]]
        cfg_tiles.append(jnp.stack(variants, axis=1))
    return jnp.stack(cfg_tiles, axis=0)


def _attn_kernel(q_ref, *rest):
    ncfg = len(DILATED_CFGS)
    kv_refs = [(rest[2 * c], rest[2 * c + 1]) for c in range(ncfg)]
    bias_ref, o_ref, acc_ref, m_ref, l_ref = rest[2 * ncfg:]
    sb = pl.program_id(2)
    seq = kv_refs[0][0].shape[1]
    qsup = q_ref.shape[1]
    nblk = qsup // Q_BLK
    head0 = _iota2((Q_BLK, LANES), 1) < HEAD_DIM
    sel0 = (_iota2((Q_BLK, 2 * LANES), 1) & HEAD_DIM) == 0

    groups = [(pos, ci, g) for pos, ci in enumerate(reversed(range(len(DILATED_CFGS))))
              for g in range(nblk // ATTN_ILP)]

    def scores(pos, ci, g):
        dil = DILATED_CFGS[ci][1]
        n_idx = seq // dil
        blocks = []
        for u in range(ATTN_ILP):
            t = g * ATTN_ILP + u
            r, j = t % dil, t // dil
            qs = r + dil * (j * Q_BLK)
            i0 = sb * (qsup // dil) + j * Q_BLK
            ws = jnp.clip(i0 - N_SIDE, 0, n_idx - K_WIN)
            var = jnp.where(i0 == 0, 0, jnp.where(i0 == n_idx - Q_BLK, 2, 1))
            qrows = pl.ds(qs, Q_BLK) if dil == 1 else pl.ds(qs, Q_BLK, stride=dil)
            krows = pl.ds(pl.multiple_of(ws, N_SIDE), K_WIN)
            k_ref, v_ref = kv_refs[ci]
            if dil == 1:
                k2, v2 = k_ref[0, krows, :], v_ref[0, krows, :]
            else:
                k2, v2 = k_ref[0, r, krows, :], v_ref[0, r, krows, :]
            q2 = q_ref[0, qrows, :].astype(BF16)
            blocks.append(dict(qrows=qrows, var=var, k2=k2,
                               qh=[jnp.where(head0 if h == 0 else ~head0, q2, jnp.zeros_like(q2)) for h in range(2)],
                               vo=jnp.concatenate([v2, jnp.ones_like(v2)], axis=1)))
        chains = [(blk, h) for blk in blocks for h in range(2)]
        s = [_dot_nt(blk["qh"][h], blk["k2"]) + bias_ref[ci, h, blk["var"]] for blk, h in chains]
        mx = [jnp.max(sh, axis=-1, keepdims=True) for sh in s]
        p = [jnp.exp2(sh - m).astype(BF16) for sh, m in zip(s, mx)]
        return dict(pos=pos, blocks=blocks, chains=chains, mx=mx, p=p)

    def accumulate(st):
        pv = [_dot(ph, blk["vo"]) for ph, (blk, _) in zip(st["p"], st["chains"])]
        for u, blk in enumerate(st["blocks"]):
            qrows = blk["qrows"]
            tot = jnp.where(sel0, pv[2 * u], pv[2 * u + 1])
            num2 = tot[:, :LANES]
            l2 = tot[:, LANES:]
            m2 = jnp.where(head0, st["mx"][2 * u], st["mx"][2 * u + 1])
            if st["pos"] == 0:
                acc_ref[qrows, :] = num2
                m_ref[qrows, :] = m2
                l_ref[qrows, :] = l2
            else:
                mo = m_ref[qrows, :]
                mn = jnp.maximum(mo, m2)
                a = jnp.exp2(mo - mn)
                b = jnp.exp2(m2 - mn)
                acc_ref[qrows, :] = acc_ref[qrows, :] * a + num2 * b
                l_ref[qrows, :] = l_ref[qrows, :] * a + l2 * b
                m_ref[qrows, :] = mn

    in_flight = []
    for grp in groups:
        in_flight.append(scores(*grp))
        if len(in_flight) > ATTN_DEPTH:
            accumulate(in_flight.pop(0))
    for st in in_flight:
        accumulate(st)

    o_ref[0] = (acc_ref[...] / l_ref[...]).astype(o_ref.dtype)


def _attention(aq, akv, bias_tiles):
    bsz, seq, _ = aq.shape
    npair = D_A // LANES
    qsup = Q_SUPER
    assert seq % qsup == 0 and seq // DILATED_CFGS[-1][1] >= K_WIN
    in_specs = [pl.BlockSpec((1, qsup, LANES), lambda b, p, s: (b, s, p))]
    operands = [aq]
    for kv, (_, dil) in zip(akv, DILATED_CFGS):
        for part in range(2):
            if dil == 1:
                in_specs.append(pl.BlockSpec((1, seq, LANES), lambda b, p, s, part=part: (b, 0, part * npair + p)))
            else:
                in_specs.append(pl.BlockSpec((1, dil, seq // dil, LANES),
                                             lambda b, p, s, part=part: (b, 0, 0, part * npair + p)))
            operands.append(kv)
    in_specs.append(pl.BlockSpec((len(DILATED_CFGS), 2, 3, Q_BLK, K_WIN), lambda b, p, s: (0, p, 0, 0, 0)))
    return pl.pallas_call(
        _attn_kernel,
        grid=(bsz, npair, seq // qsup),
        in_specs=in_specs,
        out_specs=pl.BlockSpec((1, qsup, LANES), lambda b, p, s: (b, s, p)),
        out_shape=jax.ShapeDtypeStruct((bsz, seq, D_A), BF16),
        scratch_shapes=[pltpu.VMEM((qsup, LANES), F32)] * 3,
        compiler_params=pltpu.CompilerParams(
            dimension_semantics=("parallel", "parallel", "arbitrary"), vmem_limit_bytes=VMEM_LIMIT),
        name="dilated_attn",
    )(*operands, bias_tiles)


def _direction_masks(backward):
    row = _iota2((CHUNK, CHUNK), 0)
    col = _iota2((CHUNK, CHUNK), 1)
    if backward:
        return row <= col, row < col, row ^ col
    return row >= col, row > col, row ^ col


def _gate_cumsums(g, incl, incl_other):
    tri = incl.astype(BF16)
    tri_t = incl_other.astype(BF16)
    both = _dot(tri, jnp.concatenate(_split(g), axis=1))
    gb = both[:, :LANES] + both[:, LANES:]
    gt = g.T
    thi, tlo = _split(gt)
    gbt = _dot(thi, tri_t) + _dot(tlo, tri_t)
    return gb, gt, gbt


def _gate_lane(mixer, pair, kind, backward, head):
    return 16 * mixer + 8 * pair + 4 * kind + 2 * int(backward) + head


def _mlstm_stages(qf, kf, vf, gf, qb, kb, vb, gb, hf_ref, hb_ref, s_ref, m_ref, fwd_masks, bwd_masks):
    npair = D_B // LANES
    head0 = _iota2((CHUNK, LANES), 1) < HEAD_DIM
    sel0 = (_iota2((CHUNK, 2 * LANES), 1) & HEAD_DIM) == 0
    bd = _pair_block_diag(LANES, 2 * LANES)
    wide = lambda x: jnp.concatenate([x, x], axis=1)

    groups = []
    for backward, (q_ref, k_ref, v_ref, g_ref), h_ref in ((False, (qf, kf, vf, gf), hf_ref),
                                                          (True, (qb, kb, vb, gb), hb_ref)):
        incl = (bwd_masks if backward else fwd_masks)[0]
        incl_other = (fwd_masks if backward else bwd_masks)[0]
        last = 0 if backward else CHUNK - 1
        g, gbc, gt, gbt = g_ref
        i_lanes = [_gate_lane(0, p, 0, backward, h) for p in range(npair) for h in range(2)]
        f_lanes = [_gate_lane(0, p, 1, backward, h) for p in range(npair) for h in range(2)]
        for p in range(npair):
            idx = 2 * int(backward) + p
            heads = []
            for h in range(2):
                j = 2 * p + h
                li = jnp.broadcast_to(g[:, i_lanes[j]:i_lanes[j] + 1], (CHUNK, LANES))
                bc = jnp.broadcast_to(gbc[:, f_lanes[j]:f_lanes[j] + 1], (CHUNK, LANES))
                rrow = gt[i_lanes[j]:i_lanes[j] + 1, :] - gbt[f_lanes[j]:f_lanes[j] + 1, :]
                mst = m_ref[idx, h:h + 1, :]
                dmat = jnp.where(incl, wide(bc) + rrow, -jnp.inf)
                inter = bc + mst
                mt = jnp.maximum(jnp.max(dmat, axis=-1, keepdims=True), inter)
                blast = bc[last:last + 1, :]
                wlog = blast - bc + li
                mn = jnp.maximum(blast + mst, jnp.max(wlog, axis=0, keepdims=True))
                heads.append(dict(decay=jnp.exp2(dmat - wide(mt)), iw=jnp.exp2(inter - mt), emt=jnp.exp2(-mt),
                                  ws=jnp.exp2(wlog - mn), dec=jnp.exp2(blast + mst - mn), mnew=mn))
            cols = slice(p * LANES, (p + 1) * LANES)
            v2 = v_ref[0, :, cols]
            groups.append(dict(q2=q_ref[0, :, cols], k2=k_ref[0, :, cols], heads=heads, state=s_ref[idx], idx=idx,
                               h_ref=h_ref, cols=cols,
                               vo=jnp.concatenate([v2, jnp.ones_like(v2)], axis=1)))
            yield

    chains = [(d, h) for d in groups for h in range(2)]
    q_state = [_dot(d["q2"], d["state"].astype(BF16)) for d in groups]
    yield
    qk = [_dot_nt(jnp.where(head0 if h == 0 else ~head0, d["q2"], jnp.zeros_like(d["q2"])), d["k2"])
          for d, h in chains]
    yield
    sc = [(qkh * d["heads"][h]["decay"]).astype(BF16) for (d, h), qkh in zip(chains, qk)]
    yield
    pv = [_dot(sch, d["vo"]) for (d, _), sch in zip(chains, sc)]
    yield
    upd = [_dot_tn(d["k2"], (jnp.where(sel0, wide(d["heads"][0]["ws"]), wide(d["heads"][1]["ws"]))
                             * d["vo"].astype(F32)).astype(BF16)) for d in groups]
    yield
    for i, d in enumerate(groups):
        h0, h1 = d["heads"]
        tot = jnp.where(sel0, wide(h0["iw"]), wide(h1["iw"])) * q_state[i] + jnp.where(sel0, pv[2 * i], pv[2 * i + 1])
        den = jnp.maximum(jnp.abs(tot[:, LANES:]), jnp.where(head0, h0["emt"], h1["emt"]))
        d["h_ref"][0, :, d["cols"]] = (tot[:, :LANES] / den).astype(d["h_ref"].dtype)
        dec2 = jnp.where(sel0[0:1, :], wide(h0["dec"]), wide(h1["dec"]))
        s_ref[d["idx"]] = dec2 * d["state"] + jnp.where(bd, upd[i], 0.0)
        m_ref[d["idx"], 0:1, :] = h0["mnew"]
        m_ref[d["idx"], 1:2, :] = h1["mnew"]


_INV_BASE = 32


def _unit_tri_inverses(ms, xor, tick, upto):
    diag = xor == 0
    one = jnp.ones((), BF16)
    zero = jnp.zeros((), BF16)
    base = xor < _INV_BASE
    pbs = [jnp.where(base, m, zero) for m in ms]
    ts = [jnp.where(diag, one, -pb) for pb in pbs]
    k = 1
    while True:
        pbs = [_dot(pb, pb).astype(BF16) for pb in pbs]
        tick()
        k *= 2
        ts = [_dot(t, jnp.where(diag, one, pb)).astype(BF16) for t, pb in zip(ts, pbs)]
        tick()
        if 2 * k >= _INV_BASE:
            break
    b = _INV_BASE
    while b < upto:
        level = (xor >= b) & (xor < 2 * b)
        xs = [_dot(jnp.where(level, m, zero), t).astype(BF16) for m, t in zip(ms, ts)]
        tick()
        ts = [_dot(t, jnp.where(diag, one, -x)).astype(BF16) for t, x in zip(ts, xs)]
        tick()
        b *= 2
    return ts


def _gdn_step(qf, kf, vf, gf, qb, kb, vb, gb, of_ref, ob_ref, s_ref, fwd_masks, bwd_masks, tick):
    npair = D_C // LANES
    head0 = _iota2((CHUNK, LANES), 1) < HEAD_DIM
    sel0 = (_iota2((CHUNK, 2 * LANES), 1) & HEAD_DIM) == 0
    bd = _pair_block_diag(LANES, LANES)
    xor = fwd_masks[2]

    dirs = []
    for backward, (q_ref, k_ref, v_ref, g_ref), o_ref in ((False, (qf, kf, vf, gf), of_ref),
                                                          (True, (qb, kb, vb, gb), ob_ref)):
        incl, strict, _ = bwd_masks if backward else fwd_masks
        incl_other = (fwd_masks if backward else bwd_masks)[0]
        last = 0 if backward else CHUNK - 1
        g, gbc, _, gbt = g_ref
        for p in range(npair):
            cols = slice(p * LANES, (p + 1) * LANES)
            q2, k2, v2 = q_ref[0, :, cols], k_ref[0, :, cols], v_ref[0, :, cols]
            a_lane = [_gate_lane(1, p, 0, backward, h) for h in range(2)]
            b_lane = [_gate_lane(1, p, 1, backward, h) for h in range(2)]
            gam = [gbc[:, a:a + 1] for a in a_lane]
            beta = [g[:, b:b + 1] for b in b_lane]
            glast2 = jnp.where(head0[0:1, :], gbc[last:last + 1, a_lane[0]:a_lane[0] + 1],
                               gbc[last:last + 1, a_lane[1]:a_lane[1] + 1])
            gam2 = jnp.where(head0, gam[0], gam[1])
            beta2 = jnp.where(head0, beta[0], beta[1])
            egam2 = jnp.exp2(gam2)
            kf32 = k2.astype(F32)
            rhs = jnp.concatenate([v2.astype(F32) * beta2, kf32 * (beta2 * egam2)], axis=1).astype(BF16)
            decay = [jnp.exp2(jnp.where(incl, gam[h] - gbt[a_lane[h]:a_lane[h] + 1, :], -jnp.inf)) for h in range(2)]
            idx = 2 * int(backward) + p
            state = s_ref[idx]
            dirs.append(dict(q2=q2, k2=k2, kf32=kf32, rhs=rhs, decay=decay, beta=beta, strict=strict, gam2=gam2,
                             egam2=egam2, glast2=glast2, state=state, state_b=state.astype(BF16), o_ref=o_ref,
                             cols=cols, idx=idx, backward=backward))

    chains = [(d, h) for d in dirs for h in range(2)]

    def head_lanes(x, h):
        return jnp.where(head0 if h == 0 else ~head0, x, 0).astype(BF16)

    kk = [_dot_nt(head_lanes(d["k2"], h), d["k2"]) for d, h in chains]
    tick()
    ms = [jnp.where(d["strict"], d["beta"][h] * kkh * d["decay"][h], 0.0).astype(BF16)
          for (d, h), kkh in zip(chains, kk)]
    qk = [_dot_nt(head_lanes(d["q2"], h), d["k2"]) for d, h in chains]
    tick()
    a_intra = [(qkh * d["decay"][h]).astype(BF16) for (d, h), qkh in zip(chains, qk)]
    half = CHUNK // 2
    t_half = _unit_tri_inverses(ms, xor, tick, upto=half)
    y = [_dot(t, d["rhs"]) for (d, _), t in zip(chains, t_half)]
    tick()
    early = [slice(half, CHUNK) if d["backward"] else slice(0, half) for d, _ in chains]
    late = [slice(0, half) if d["backward"] else slice(half, CHUNK) for d, _ in chains]
    z = [_dot(m[lt, el], yj[el].astype(BF16)).astype(BF16) for m, yj, el, lt in zip(ms, y, early, late)]
    tick()
    corr = [_dot(t[lt, lt], zj) for t, zj, lt in zip(t_half, z, late)]
    tick()
    uw = [jnp.concatenate([yj[el], yj[lt] - c][::-1] if d["backward"] else [yj[el], yj[lt] - c], axis=0)
          for (d, _), yj, c, el, lt in zip(chains, y, corr, early, late)]

    uw2 = [jnp.where(sel0, uw[2 * i], uw[2 * i + 1]) for i in range(len(dirs))]
    qg = [(d["q2"].astype(F32) * d["egam2"]).astype(BF16) for d in dirs]
    zeros = jnp.zeros((LANES, LANES), BF16)
    w_state, q_state = [None] * len(dirs), [None] * len(dirs)
    for i in range(0, len(dirs), npair):
        group = range(i, i + npair)
        state_bd = jnp.concatenate([jnp.concatenate([dirs[j]["state_b"] if j == k else zeros for k in group], axis=1)
                                    for j in group], axis=0)
        ws_cat = _dot(jnp.concatenate([uw2[j][:, LANES:].astype(BF16) for j in group], axis=1), state_bd)
        qs_cat = _dot(jnp.concatenate([qg[j] for j in group], axis=1), state_bd)
        for n, j in enumerate(group):
            w_state[j] = ws_cat[:, n * LANES:(n + 1) * LANES]
            q_state[j] = qs_cat[:, n * LANES:(n + 1) * LANES]
    tick()
    v_new = [(uw2[i][:, :LANES] - w_state[i]).astype(BF16) for i in range(len(dirs))]
    av = [_dot(a_intra[j], v_new[j // 2]) for j in range(len(chains))]
    tick()
    upd = [_dot_tn((d["kf32"] * jnp.exp2(d["glast2"] - d["gam2"])).astype(BF16), v_new[i])
           for i, d in enumerate(dirs)]
    for i, d in enumerate(dirs):
        d["o_ref"][0, :, d["cols"]] = (q_state[i] + jnp.where(head0, av[2 * i], av[2 * i + 1])).astype(d["o_ref"].dtype)
        s_ref[d["idx"]] = d["state"] * jnp.exp2(d["glast2"]) + jnp.where(bd, upd[i], 0.0)


def _recurrent_kernel(bqf, bkf, bvf, cqf, ckf, cvf, gf, bqb, bkb, bvb, cqb, ckb, cvb, gb,
                      hf_ref, hb_ref, of_ref, ob_ref, ms_ref, mm_ref, gs_ref):
    @pl.when(pl.program_id(1) == 0)
    def _():
        ms_ref[...] = jnp.zeros_like(ms_ref)
        mm_ref[...] = jnp.zeros_like(mm_ref)
        gs_ref[...] = jnp.zeros_like(gs_ref)

    fwd_masks = _direction_masks(False)
    bwd_masks = _direction_masks(True)
    gf = (gf[0],) + _gate_cumsums(gf[0], fwd_masks[0], bwd_masks[0])
    gb = (gb[0],) + _gate_cumsums(gb[0], bwd_masks[0], fwd_masks[0])
    mlstm = _mlstm_stages(bqf, bkf, bvf, gf, bqb, bkb, bvb, gb, hf_ref, hb_ref, ms_ref, mm_ref, fwd_masks, bwd_masks)
    _gdn_step(cqf, ckf, cvf, gf, cqb, ckb, cvb, gb, of_ref, ob_ref, gs_ref, fwd_masks, bwd_masks,
              lambda: next(mlstm, None))
    for _ in mlstm:
        pass


def _recurrent_mixers(bqkv, cqkv, gates):
    bsz, seq, _ = bqkv.shape
    nc = seq // CHUNK
    nstate = 2 * (D_B // LANES)
    in_specs = []
    for rev in (False, True):
        cidx = (lambda c: nc - 1 - c) if rev else (lambda c: c)
        for width in (D_B, D_C):
            for part in range(3):
                in_specs.append(pl.BlockSpec((1, CHUNK, width), lambda b, c, part=part, cidx=cidx: (b, cidx(c), part)))
        in_specs.append(pl.BlockSpec((1, CHUNK, LANES), lambda b, c, cidx=cidx: (b, cidx(c), 0)))
    out_b = jax.ShapeDtypeStruct((bsz, seq, D_B), BF16)
    out_c = jax.ShapeDtypeStruct((bsz, seq, D_C), BF16)
    fwd_spec = lambda w: pl.BlockSpec((1, CHUNK, w), lambda b, c: (b, c, 0))
    bwd_spec = lambda w: pl.BlockSpec((1, CHUNK, w), lambda b, c: (b, nc - 1 - c, 0))
    return pl.pallas_call(
        _recurrent_kernel,
        grid=(bsz, nc),
        in_specs=in_specs,
        out_specs=[fwd_spec(D_B), bwd_spec(D_B), fwd_spec(D_C), bwd_spec(D_C)],
        out_shape=(out_b, out_b, out_c, out_c),
        scratch_shapes=[pltpu.VMEM((nstate, LANES, 2 * LANES), F32), pltpu.VMEM((nstate, SUBLANES, LANES), F32),
                        pltpu.VMEM((nstate, LANES, LANES), F32)],
        compiler_params=pltpu.CompilerParams(
            dimension_semantics=("parallel", "arbitrary"), vmem_limit_bytes=VMEM_LIMIT),
        name="recurrent_mixers",
    )(bqkv, bqkv, bqkv, cqkv, cqkv, cqkv, gates, bqkv, bqkv, bqkv, cqkv, cqkv, cqkv, gates)


def _out_kernel(x_ref, oa_ref, hf_ref, hb_ref, cf_ref, cb_ref, nw_ref, wz_ref, mw_ref, gw_ref, wo_ref, y_ref):
    bd = _head_block_diag(MXU_WIDTH).astype(BF16)
    x = x_ref[0]
    hn = (x * lax.rsqrt(jnp.mean(x * x, axis=-1, keepdims=True) + EPS) * nw_ref[...]).astype(BF16)
    z = _dot(hn, wz_ref[...])

    def silu(t):
        return t * _sigmoid(t)

    hsum = _sigmoid(z[:, D_A:D_A + D_B]) * (hf_ref[0].astype(F32) + hb_ref[0].astype(F32))
    csum = cf_ref[0].astype(F32) + cb_ref[0].astype(F32)
    ssb = _head_sum(hsum * hsum, bd)
    ssc = _head_sum(csum * csum, bd)
    ya = oa_ref[0].astype(F32) * silu(z[:, 0:D_A])
    yb = hsum * lax.rsqrt(ssb * (1.0 / HEAD_DIM) + EPS) * mw_ref[...] * silu(z[:, D_A + D_B:D_A + 2 * D_B])
    yc = csum * lax.rsqrt(ssc * (1.0 / HEAD_DIM) + EPS) * gw_ref[...] * silu(z[:, D_A + 2 * D_B:])
    y = jnp.concatenate([ya, yb, yc], axis=1).astype(BF16)
    y_ref[0] = x + _dot(y, wo_ref[...])


def _out_proj(x, layer, oa, hf, hb, cf, cb, nw, wz, mw, gw, wo):
    bsz, seq, d = x.shape
    tile = OUT_TILE
    row_spec = lambda a: pl.BlockSpec((1, tile, a.shape[-1]), lambda b, i: (b, i, 0))
    full = lambda a: _layer_spec(a, layer)
    return pl.pallas_call(
        _out_kernel,
        grid=(bsz, seq // tile),
        in_specs=[row_spec(a) for a in (x, oa, hf, hb, cf, cb)] + [full(a) for a in (nw, wz, mw, gw, wo)],
        out_specs=pl.BlockSpec((1, tile, d), lambda b, i: (b, i, 0)),
        out_shape=jax.ShapeDtypeStruct(x.shape, x.dtype),
        compiler_params=pltpu.CompilerParams(
            dimension_semantics=("parallel", "parallel"), vmem_limit_bytes=VMEM_LIMIT),
        name="out_proj",
    )(x, oa, hf, hb, cf, cb, nw, wz, mw, gw, wo)


def _to_gate_lanes(t):
    lead = t.shape[:-3]
    t = t.reshape(lead + (4, 2, 2))
    return jnp.swapaxes(t, -3, -2).reshape(lead + (16,))


def _gate_weights(w_in):
    nh = D_B // HEAD_DIM
    per_mixer = [_to_gate_lanes(w_in[:, off:off + 4 * nh].reshape(-1, 2, 2, nh)) for off in (_OFF_BIF, _OFF_CAB)]
    return jnp.pad(jnp.concatenate(per_mixer, axis=1), ((0, 0), (0, LANES - 2 * 16)))


def _gate_params(i_bias, f_bias, a_log, dt_bias):
    zeros = jnp.zeros_like(dt_bias)
    row0 = jnp.concatenate([_to_gate_lanes(jnp.stack([i_bias, f_bias])), _to_gate_lanes(jnp.stack([dt_bias, zeros]))])
    row1 = jnp.concatenate([jnp.zeros((16,), F32), _to_gate_lanes(jnp.stack([a_log, zeros]))])
    return jnp.pad(jnp.stack([row0, row1]), ((0, SUBLANES - 2), (0, LANES - 2 * 16)))


def _pack_params(norm_w, w_in, w_out, qk_norm_w, i_bias, f_bias, m_norm_w, conv_w, a_log, dt_bias, g_norm_w):
    w_all = jnp.concatenate([w_in[:, _OFF_AQ:_OFF_AZ], w_in[:, _OFF_BQ:_OFF_BIF], w_in[:, _OFF_CQ:_OFF_CAB],
                             _gate_weights(w_in)], axis=1).astype(BF16)
    wz = jnp.concatenate([w_in[:, _OFF_AZ:_OFF_BQ], w_in[:, _OFF_BO:_OFF_CQ], w_in[:, _OFF_CZ:]],
                         axis=1).astype(BF16)
    qkw = jnp.pad(jnp.tile(qk_norm_w, (1, MXU_WIDTH // HEAD_DIM)) * jnp.array([[HEAD_DIM ** -0.5 * LOG2E], [1.0]], F32),
                  ((0, SUBLANES - 2), (0, 0)))
    return dict(nw=norm_w[None, :], w_all=w_all, wz=wz, wo=w_out.astype(BF16), qkw=qkw,
                gpar=_gate_params(i_bias, f_bias, a_log, dt_bias),
                cw=jnp.pad(conv_w, ((0, SUBLANES - CONV_W), (0, 0))),
                mw=jnp.tile(m_norm_w, D_B // HEAD_DIM)[None, :], gw=jnp.tile(g_norm_w, D_C // HEAD_DIM)[None, :])


def _layer(x, bias_tiles, p, layer):
    aq, *akv, bqkv, cqkv, gates = _in_proj(x, layer, p["nw"], p["w_all"], p["qkw"], p["gpar"], p["cw"])
    oa = _attention(aq, akv, bias_tiles)
    hf, hb, cf, cb = _recurrent_mixers(bqkv, cqkv, gates)
    return _out_proj(x, layer, oa, hf, hb, cf, cb, p["nw"], p["wz"], p["mw"], p["gw"], p["wo"])


def kernel(x, norm_w, w_in, w_out, qk_norm_w, rel_bias, mlstm_i_bias, mlstm_f_bias, mlstm_norm_w, gdn_conv_w,
           gdn_a_log, gdn_dt_bias, gdn_norm_w):
    bias_tiles = _attn_bias_tiles(rel_bias)
    params = jax.vmap(_pack_params)(norm_w, w_in, w_out, qk_norm_w, mlstm_i_bias, mlstm_f_bias, mlstm_norm_w,
                                    gdn_conv_w, gdn_a_log, gdn_dt_bias, gdn_norm_w)
    for l in range(DEPTH):
        x = _layer(x, bias_tiles, params, l)
    return x
```

```python
import functools

import numpy as np
import jax
import jax.numpy as jnp
from jax import lax
from jax.experimental import pallas as pl
from jax.experimental.pallas import tpu as pltpu

F32 = jnp.float32
BF16 = jnp.bfloat16

D_MODEL = 1024
DEPTH = 2
EPS = 1e-6
HEAD_DIM = 64
LANES = 128
SUBLANES = 8
MXU_WIDTH = 256
D_A, D_B, D_C = 512, 256, 256
DILATED_CFGS = ((128, 1), (512, 4), (2048, 16))
N_SIDE = 64
NUM_BUCKETS = 32
REL_MAX_DIST = 1024
CONV_W = 5
CHUNK = 256
Q_SUPER = 4096
Q_BLK = 128
K_WIN = 256
ATTN_ILP = 2
ATTN_DEPTH = 1
IN_TILE = 512
OUT_TILE = 1024
NEG = -1e30
LOG2E = 1.4426950408889634
VMEM_LIMIT = 56 * 1024 * 1024

_OFF_AQ, _OFF_AZ = 0, 1536
_OFF_BQ, _OFF_BIF, _OFF_BO, _OFF_BZ = 2048, 2816, 2832, 3088
_OFF_CQ, _OFF_CAB, _OFF_CZ = 3344, 4112, 4128
_W_SPLITS = (0, 3 * D_A, 3 * D_A + 3 * D_B, 3 * D_A + 3 * D_B + 3 * D_C, 3 * D_A + 3 * D_B + 3 * D_C + LANES)


def _dot(a, b):
    return jnp.dot(a, b, preferred_element_type=F32)


def _dot_nt(a, b):
    return lax.dot_general(a, b, (((1,), (1,)), ((), ())), preferred_element_type=F32)


def _dot_tn(a, b):
    return lax.dot_general(a, b, (((0,), (0,)), ((), ())), preferred_element_type=F32)


def _split(a):
    hi = a.astype(BF16)
    lo = (a - hi.astype(F32)).astype(BF16)
    return hi, lo


def _iota2(shape, axis):
    return lax.broadcasted_iota(jnp.int32, shape, axis)


def _pair_block_diag(rows, cols):
    r = _iota2((rows, cols), 0)
    c = _iota2((rows, cols), 1)
    return (r < HEAD_DIM) == ((c & HEAD_DIM) == 0)


def _head_block_diag(n):
    return (_iota2((n, n), 0) ^ _iota2((n, n), 1)) < HEAD_DIM


def _head_sum(t, bd):
    return _dot(t.astype(BF16), bd)


def _softplus(y):
    return jnp.maximum(y, 0.0) + jnp.log1p(jnp.exp(-jnp.abs(y)))


def _sigmoid(y):
    return 1.0 / (1.0 + jnp.exp(-y))


def _in_proj_kernel(x_ref, xp_ref, xn_ref, nw_ref, w_ref, qkw_ref, gpar_ref, cw_ref,
                    a_ref, kv1_ref, kv4_ref, kv16_ref, b_ref, c_ref, g_ref, cext_ref, kvs_ref, hn_ref, kvg_ref):
    kv_refs = (kv1_ref, kv4_ref, kv16_ref)
    wa_ref, wb_ref, wc_ref, wg_ref = (w_ref.at[:, lo:hi] for lo, hi in zip(_W_SPLITS[:-1], _W_SPLITS[1:]))
    i = pl.program_id(1)
    n = pl.num_programs(1)
    tile = x_ref.shape[1]
    bd = _head_block_diag(MXU_WIDTH).astype(BF16)

    xe = jnp.concatenate([xp_ref[0], x_ref[0], xn_ref[0]], axis=0)
    ms = jnp.mean(xe * xe, axis=-1, keepdims=True)
    hne = xe * lax.rsqrt(ms + EPS) * nw_ref[...]
    hn_ref[...] = hne[SUBLANES:SUBLANES + tile].astype(BF16)
    hne = hne.astype(BF16)

    ce = _dot(hne, wc_ref[...])
    ta = [_dot(hn_ref[...], wa_ref[:, j * MXU_WIDTH:(j + 1) * MXU_WIDTH]) for j in range(3 * D_A // MXU_WIDTH)]
    p = _dot(hn_ref[...], wg_ref[...]) + gpar_ref[0:1, :]
    ssa = [_head_sum(t * t, bd) for t in ta[:4]]

    row = _iota2((tile + 2 * SUBLANES, 1), 0)
    valid = ((row >= SUBLANES) | (i > 0)) & ((row < tile + SUBLANES) | (i < n - 1))
    cext_ref[...] = jnp.where(valid, ce, 0.0)
    conv = cw_ref[0:1, :] * cext_ref[pl.ds(SUBLANES - CONV_W // 2, tile), :]
    for t in range(1, CONV_W):
        conv = conv + cw_ref[t:t + 1, :] * cext_ref[pl.ds(SUBLANES - CONV_W // 2 + t, tile), :]
    s = conv * _sigmoid(conv)
    tc = [s[:, j * MXU_WIDTH:(j + 1) * MXU_WIDTH] for j in range(3 * D_C // MXU_WIDTH)]

    ssc = [_head_sum(t * t, bd) for t in tc[:2]]
    tb = _dot(hn_ref[...], wb_ref[...])

    for j, t in enumerate(ta):
        if j < 4:
            t = t * lax.rsqrt(ssa[j] * (1.0 / HEAD_DIM) + EPS) * qkw_ref[j // 2:j // 2 + 1, :]
        if j < 2:
            a_ref[0, :, j * MXU_WIDTH:(j + 1) * MXU_WIDTH] = t
        else:
            for half in range(2):
                kvs_ref[2 * (j - 2) + half] = t[:, half * LANES:(half + 1) * LANES]
    for lt in range(kvs_ref.shape[0]):
        lanes = slice(lt * LANES, (lt + 1) * LANES)
        kv_refs[0][0, :, lanes] = kvs_ref[lt].astype(BF16)
        prev_dil, src = 1, [kvs_ref.at[lt]]
        for ci in range(1, len(DILATED_CFGS)):
            dil = DILATED_CFGS[ci][1]
            step = dil // prev_dil
            rows = tile // dil
            nxt = [None] * dil
            for r in range(dil):
                x = src[r % prev_dil][pl.ds(r // prev_dil, rows, stride=step), :]
                kv_refs[ci][0, r, :, lanes] = x.astype(BF16)
                if ci + 1 < len(DILATED_CFGS):
                    kvg_ref[lt, r] = x
                    nxt[r] = kvg_ref.at[lt, r]
            prev_dil, src = dil, nxt

    b_ref[0, :, 0:D_B] = (tb[:, 0:D_B] * HEAD_DIM ** -0.5).astype(b_ref.dtype)
    b_ref[0, :, D_B:3 * D_B] = tb[:, D_B:3 * D_B].astype(b_ref.dtype)

    lane = _iota2(p.shape, 1)
    first4 = (lane & 7) < 4
    is_b = lane < 16
    val_b = jnp.where(first4, p, -_softplus(-p)) * LOG2E
    val_c = jnp.where(first4, -jnp.exp(gpar_ref[1:2, :]) * _softplus(p) * LOG2E, _sigmoid(p))
    g_ref[0] = jnp.where(is_b, val_b, val_c)

    c_ref[0, :, 0:D_C] = (tc[0] * lax.rsqrt(ssc[0] + EPS) * HEAD_DIM ** -0.5).astype(c_ref.dtype)
    c_ref[0, :, D_C:2 * D_C] = (tc[1] * lax.rsqrt(ssc[1] + EPS)).astype(c_ref.dtype)
    c_ref[0, :, 2 * D_C:3 * D_C] = tc[2].astype(c_ref.dtype)


def _layer_spec(a, layer):
    return pl.BlockSpec((None,) + a.shape[1:], lambda b, i: (layer,) + (0,) * (a.ndim - 1))


def _in_proj(x, layer, nw, w_all, qkw, gpar, cw):
    bsz, seq, d = x.shape
    tile = IN_TILE
    nt = seq // tile
    hb = tile // SUBLANES
    full = lambda a: _layer_spec(a, layer)
    row_spec = lambda w: pl.BlockSpec((1, tile, w), lambda b, i: (b, i, 0))
    outs = [((bsz, seq, D_A), F32, row_spec(D_A))]
    for (_, dil) in DILATED_CFGS:
        if dil == 1:
            outs.append(((bsz, seq, 2 * D_A), BF16, row_spec(2 * D_A)))
        else:
            outs.append(((bsz, dil, seq // dil, 2 * D_A), BF16,
                         pl.BlockSpec((1, dil, tile // dil, 2 * D_A), lambda b, i: (b, 0, i, 0))))
    outs += [((bsz, seq, 3 * D_B), BF16, row_spec(3 * D_B)), ((bsz, seq, 3 * D_C), BF16, row_spec(3 * D_C)),
             ((bsz, seq, LANES), F32, row_spec(LANES))]
    return pl.pallas_call(
        _in_proj_kernel,
        grid=(bsz, nt),
        in_specs=[
            row_spec(d),
            pl.BlockSpec((1, SUBLANES, d), lambda b, i: (b, jnp.maximum(i * hb - 1, 0), 0)),
            pl.BlockSpec((1, SUBLANES, d), lambda b, i: (b, jnp.minimum((i + 1) * hb, nt * hb - 1), 0)),
            full(nw), full(w_all), full(qkw), full(gpar), full(cw),
        ],
        out_specs=[spec for _, _, spec in outs],
        out_shape=[jax.ShapeDtypeStruct(shape, dtype) for shape, dtype, _ in outs],
        scratch_shapes=[pltpu.VMEM((tile + 2 * SUBLANES, 3 * D_C), F32), pltpu.VMEM((2 * D_A // LANES, tile, LANES), F32),
                        pltpu.VMEM((tile, d), BF16),
                        pltpu.VMEM((2 * D_A // LANES, DILATED_CFGS[1][1], tile // DILATED_CFGS[1][1], LANES), F32)],
        compiler_params=pltpu.CompilerParams(
            dimension_semantics=("parallel", "arbitrary"), vmem_limit_bytes=VMEM_LIMIT),
        name="in_proj",
    )(x, x, x, nw, w_all, qkw, gpar, cw)


def _t5_bucket(rel):
    half = NUM_BUCKETS // 2
    max_exact = half // 2
    n = np.abs(rel)
    large = max_exact + (np.log(np.maximum(n, 1) / max_exact) / np.log(REL_MAX_DIST / max_exact)
                         * (half - max_exact)).astype(np.int32)
    large = np.minimum(large, half - 1)
    return (rel > 0).astype(np.int32) * half + np.where(n < max_exact, n, large)


def _attn_bias_tiles(rel_bias):
    period = 5 * LANES + 1
    wide = K_WIN + 2 * N_SIDE
    lead = Q_BLK - 1 + N_SIDE
    cfg_tiles = []
    for (_, dil) in DILATED_CFGS:
        offs = dil * np.arange(-N_SIDE, N_SIDE + 1)
        bias = rel_bias[_t5_bucket(offs)].T.astype(F32) * LOG2E
        heads = bias.shape[0]
        vec = jnp.concatenate([jnp.full((heads, lead + N_SIDE), NEG, F32), bias,
                               jnp.full((heads, period - lead - 3 * N_SIDE - 1), NEG, F32)], axis=1)
        rows = jnp.tile(vec, (1, Q_BLK))[:, :Q_BLK * (period - 1)].reshape(heads, Q_BLK, period - 1)
        ext = rows[:, :, lead:lead + wide]
        variants = [ext[:, :, 2 * N_SIDE:2 * N_SIDE + K_WIN], ext[:, :, N_SIDE:N_SIDE + K_WIN],
                    ext[:, :, 0:K_WIN]]
        cfg_tiles.append(jnp.stack(variants, axis=1))
    return jnp.stack(cfg_tiles, axis=0)


def _attn_kernel(q_ref, *rest):
    ncfg = len(DILATED_CFGS)
    kv_refs = [(rest[2 * c], rest[2 * c + 1]) for c in range(ncfg)]
    bias_ref, o_ref, acc_ref, m_ref, l_ref = rest[2 * ncfg:]
    sb = pl.program_id(2)
    seq = kv_refs[0][0].shape[1]
    qsup = q_ref.shape[1]
    nblk = qsup // Q_BLK
    head0 = _iota2((Q_BLK, LANES), 1) < HEAD_DIM
    sel0 = (_iota2((Q_BLK, 2 * LANES), 1) & HEAD_DIM) == 0

    groups = [(pos, ci, g) for pos, ci in enumerate(reversed(range(len(DILATED_CFGS))))
              for g in range(nblk // ATTN_ILP)]

    def scores(pos, ci, g):
        dil = DILATED_CFGS[ci][1]
        n_idx = seq // dil
        blocks = []
        for u in range(ATTN_ILP):
            t = g * ATTN_ILP + u
            r, j = t % dil, t // dil
            qs = r + dil * (j * Q_BLK)
            i0 = sb * (qsup // dil) + j * Q_BLK
            ws = jnp.clip(i0 - N_SIDE, 0, n_idx - K_WIN)
            var = jnp.where(i0 == 0, 0, jnp.where(i0 == n_idx - Q_BLK, 2, 1))
            qrows = pl.ds(qs, Q_BLK) if dil == 1 else pl.ds(qs, Q_BLK, stride=dil)
            krows = pl.ds(pl.multiple_of(ws, N_SIDE), K_WIN)
            k_ref, v_ref = kv_refs[ci]
            if dil == 1:
                k2, v2 = k_ref[0, krows, :], v_ref[0, krows, :]
            else:
                k2, v2 = k_ref[0, r, krows, :], v_ref[0, r, krows, :]
            q2 = q_ref[0, qrows, :].astype(BF16)
            blocks.append(dict(qrows=qrows, var=var, k2=k2,
                               qh=[jnp.where(head0 if h == 0 else ~head0, q2, jnp.zeros_like(q2)) for h in range(2)],
                               vo=jnp.concatenate([v2, jnp.ones_like(v2)], axis=1)))
        chains = [(blk, h) for blk in blocks for h in range(2)]
        s = [_dot_nt(blk["qh"][h], blk["k2"]) + bias_ref[ci, h, blk["var"]] for blk, h in chains]
        mx = [jnp.max(sh, axis=-1, keepdims=True) for sh in s]
        p = [jnp.exp2(sh - m).astype(BF16) for sh, m in zip(s, mx)]
        return dict(pos=pos, blocks=blocks, chains=chains, mx=mx, p=p)

    def accumulate(st):
        pv = [_dot(ph, blk["vo"]) for ph, (blk, _) in zip(st["p"], st["chains"])]
        for u, blk in enumerate(st["blocks"]):
            qrows = blk["qrows"]
            tot = jnp.where(sel0, pv[2 * u], pv[2 * u + 1])
            num2 = tot[:, :LANES]
            l2 = tot[:, LANES:]
            m2 = jnp.where(head0, st["mx"][2 * u], st["mx"][2 * u + 1])
            if st["pos"] == 0:
                acc_ref[qrows, :] = num2
                m_ref[qrows, :] = m2
                l_ref[qrows, :] = l2
            else:
                mo = m_ref[qrows, :]
                mn = jnp.maximum(mo, m2)
                a = jnp.exp2(mo - mn)
                b = jnp.exp2(m2 - mn)
                acc_ref[qrows, :] = acc_ref[qrows, :] * a + num2 * b
                l_ref[qrows, :] = l_ref[qrows, :] * a + l2 * b
                m_ref[qrows, :] = mn

    in_flight = []
    for grp in groups:
        in_flight.append(scores(*grp))
        if len(in_flight) > ATTN_DEPTH:
            accumulate(in_flight.pop(0))
    for st in in_flight:
        accumulate(st)

    o_ref[0] = (acc_ref[...] / l_ref[...]).astype(o_ref.dtype)


def _attention(aq, akv, bias_tiles):
    bsz, seq, _ = aq.shape
    npair = D_A // LANES
    qsup = Q_SUPER
    assert seq % qsup == 0 and seq // DILATED_CFGS[-1][1] >= K_WIN
    in_specs = [pl.BlockSpec((1, qsup, LANES), lambda b, p, s: (b, s, p))]
    operands = [aq]
    for kv, (_, dil) in zip(akv, DILATED_CFGS):
        for part in range(2):
            if dil == 1:
                in_specs.append(pl.BlockSpec((1, seq, LANES), lambda b, p, s, part=part: (b, 0, part * npair + p)))
            else:
                in_specs.append(pl.BlockSpec((1, dil, seq // dil, LANES),
                                             lambda b, p, s, part=part: (b, 0, 0, part * npair + p)))
            operands.append(kv)
    in_specs.append(pl.BlockSpec((len(DILATED_CFGS), 2, 3, Q_BLK, K_WIN), lambda b, p, s: (0, p, 0, 0, 0)))
    return pl.pallas_call(
        _attn_kernel,
        grid=(bsz, npair, seq // qsup),
        in_specs=in_specs,
        out_specs=pl.BlockSpec((1, qsup, LANES), lambda b, p, s: (b, s, p)),
        out_shape=jax.ShapeDtypeStruct((bsz, seq, D_A), BF16),
        scratch_shapes=[pltpu.VMEM((qsup, LANES), F32)] * 3,
        compiler_params=pltpu.CompilerParams(
            dimension_semantics=("parallel", "parallel", "arbitrary"), vmem_limit_bytes=VMEM_LIMIT),
        name="dilated_attn",
    )(*operands, bias_tiles)


def _direction_masks(backward):
    row = _iota2((CHUNK, CHUNK), 0)
    col = _iota2((CHUNK, CHUNK), 1)
    if backward:
        return row <= col, row < col, row ^ col
    return row >= col, row > col, row ^ col


def _gate_cumsums(g, incl, incl_other):
    tri = incl.astype(BF16)
    tri_t = incl_other.astype(BF16)
    both = _dot(tri, jnp.concatenate(_split(g), axis=1))
    gb = both[:, :LANES] + both[:, LANES:]
    gt = g.T
    thi, tlo = _split(gt)
    gbt = _dot(thi, tri_t) + _dot(tlo, tri_t)
    return gb, gt, gbt


def _gate_lane(mixer, pair, kind, backward, head):
    return 16 * mixer + 8 * pair + 4 * kind + 2 * int(backward) + head


def _mlstm_stages(qf, kf, vf, gf, qb, kb, vb, gb, hf_ref, hb_ref, s_ref, m_ref, fwd_masks, bwd_masks):
    npair = D_B // LANES
    head0 = _iota2((CHUNK, LANES), 1) < HEAD_DIM
    sel0 = (_iota2((CHUNK, 2 * LANES), 1) & HEAD_DIM) == 0
    bd = _pair_block_diag(LANES, 2 * LANES)
    wide = lambda x: jnp.concatenate([x, x], axis=1)

    groups = []
    for backward, (q_ref, k_ref, v_ref, g_ref), h_ref in ((False, (qf, kf, vf, gf), hf_ref),
                                                          (True, (qb, kb, vb, gb), hb_ref)):
        incl = (bwd_masks if backward else fwd_masks)[0]
        incl_other = (fwd_masks if backward else bwd_masks)[0]
        last = 0 if backward else CHUNK - 1
        g, gbc, gt, gbt = g_ref
        i_lanes = [_gate_lane(0, p, 0, backward, h) for p in range(npair) for h in range(2)]
        f_lanes = [_gate_lane(0, p, 1, backward, h) for p in range(npair) for h in range(2)]
        for p in range(npair):
            idx = 2 * int(backward) + p
            heads = []
            for h in range(2):
                j = 2 * p + h
                li = jnp.broadcast_to(g[:, i_lanes[j]:i_lanes[j] + 1], (CHUNK, LANES))
                bc = jnp.broadcast_to(gbc[:, f_lanes[j]:f_lanes[j] + 1], (CHUNK, LANES))
                rrow = gt[i_lanes[j]:i_lanes[j] + 1, :] - gbt[f_lanes[j]:f_lanes[j] + 1, :]
                mst = m_ref[idx, h:h + 1, :]
                dmat = jnp.where(incl, wide(bc) + rrow, -jnp.inf)
                inter = bc + mst
                mt = jnp.maximum(jnp.max(dmat, axis=-1, keepdims=True), inter)
                blast = bc[last:last + 1, :]
                wlog = blast - bc + li
                mn = jnp.maximum(blast + mst, jnp.max(wlog, axis=0, keepdims=True))
                heads.append(dict(decay=jnp.exp2(dmat - wide(mt)), iw=jnp.exp2(inter - mt), emt=jnp.exp2(-mt),
                                  ws=jnp.exp2(wlog - mn), dec=jnp.exp2(blast + mst - mn), mnew=mn))
            cols = slice(p * LANES, (p + 1) * LANES)
            v2 = v_ref[0, :, cols]
            groups.append(dict(q2=q_ref[0, :, cols], k2=k_ref[0, :, cols], heads=heads, state=s_ref[idx], idx=idx,
                               h_ref=h_ref, cols=cols,
                               vo=jnp.concatenate([v2, jnp.ones_like(v2)], axis=1)))
            yield

    chains = [(d, h) for d in groups for h in range(2)]
    q_state = [_dot(d["q2"], d["state"].astype(BF16)) for d in groups]
    yield
    qk = [_dot_nt(jnp.where(head0 if h == 0 else ~head0, d["q2"], jnp.zeros_like(d["q2"])), d["k2"])
          for d, h in chains]
    yield
    sc = [(qkh * d["heads"][h]["decay"]).astype(BF16) for (d, h), qkh in zip(chains, qk)]
    yield
    pv = [_dot(sch, d["vo"]) for (d, _), sch in zip(chains, sc)]
    yield
    upd = [_dot_tn(d["k2"], (jnp.where(sel0, wide(d["heads"][0]["ws"]), wide(d["heads"][1]["ws"]))
                             * d["vo"].astype(F32)).astype(BF16)) for d in groups]
    yield
    for i, d in enumerate(groups):
        h0, h1 = d["heads"]
        tot = jnp.where(sel0, wide(h0["iw"]), wide(h1["iw"])) * q_state[i] + jnp.where(sel0, pv[2 * i], pv[2 * i + 1])
        den = jnp.maximum(jnp.abs(tot[:, LANES:]), jnp.where(head0, h0["emt"], h1["emt"]))
        d["h_ref"][0, :, d["cols"]] = (tot[:, :LANES] / den).astype(d["h_ref"].dtype)
        dec2 = jnp.where(sel0[0:1, :], wide(h0["dec"]), wide(h1["dec"]))
        s_ref[d["idx"]] = dec2 * d["state"] + jnp.where(bd, upd[i], 0.0)
        m_ref[d["idx"], 0:1, :] = h0["mnew"]
        m_ref[d["idx"], 1:2, :] = h1["mnew"]


_INV_BASE = 32


def _unit_tri_inverses(ms, tick):
    n = ms[0].shape[0]
    xor = _iota2((n, n), 0) ^ _iota2((n, n), 1)
    diag = xor == 0
    one = jnp.ones((), BF16)
    zero = jnp.zeros((), BF16)
    base = xor < _INV_BASE
    ps = [jnp.where(base, m, zero) for m in ms]
    ts = [jnp.where(diag, one, -p) for p in ps]
    ps = [_dot(p, p).astype(BF16) for p in ps]
    tick()
    k = 2
    while True:
        both = [_dot(jnp.where(diag, one, p), jnp.concatenate([t, p], axis=1)) for t, p in zip(ts, ps)]
        tick()
        ts = [b[:, :n].astype(BF16) for b in both]
        if 2 * k >= _INV_BASE:
            break
        ps = [(b[:, n:] - p.astype(F32)).astype(BF16) for b, p in zip(both, ps)]
        k *= 2
    b = _INV_BASE
    while b < n:
        level = (xor >= b) & (xor < 2 * b)
        xs = [_dot(jnp.where(level, m, zero), t).astype(BF16) for m, t in zip(ms, ts)]
        tick()
        ts = [_dot(t, jnp.where(diag, one, -x)).astype(BF16) for t, x in zip(ts, xs)]
        tick()
        b *= 2
    return ts


def _gdn_step(qf, kf, vf, gf, qb, kb, vb, gb, of_ref, ob_ref, s_ref, fwd_masks, bwd_masks, tick):
    npair = D_C // LANES
    head0 = _iota2((CHUNK, LANES), 1) < HEAD_DIM
    sel0 = (_iota2((CHUNK, 2 * LANES), 1) & HEAD_DIM) == 0
    bd = _pair_block_diag(LANES, LANES)
    xor = fwd_masks[2]

    dirs = []
    for backward, (q_ref, k_ref, v_ref, g_ref), o_ref in ((False, (qf, kf, vf, gf), of_ref),
                                                          (True, (qb, kb, vb, gb), ob_ref)):
        incl, strict, _ = bwd_masks if backward else fwd_masks
        incl_other = (fwd_masks if backward else bwd_masks)[0]
        last = 0 if backward else CHUNK - 1
        g, gbc, _, gbt = g_ref
        for p in range(npair):
            cols = slice(p * LANES, (p + 1) * LANES)
            q2, k2, v2 = q_ref[0, :, cols], k_ref[0, :, cols], v_ref[0, :, cols]
            a_lane = [_gate_lane(1, p, 0, backward, h) for h in range(2)]
            b_lane = [_gate_lane(1, p, 1, backward, h) for h in range(2)]
            gam = [gbc[:, a:a + 1] for a in a_lane]
            beta = [g[:, b:b + 1] for b in b_lane]
            glast2 = jnp.where(head0[0:1, :], gbc[last:last + 1, a_lane[0]:a_lane[0] + 1],
                               gbc[last:last + 1, a_lane[1]:a_lane[1] + 1])
            gam2 = jnp.where(head0, gam[0], gam[1])
            beta2 = jnp.where(head0, beta[0], beta[1])
            egam2 = jnp.exp2(gam2)
            kf32 = k2.astype(F32)
            rhs = jnp.concatenate([v2.astype(F32) * beta2, kf32 * (beta2 * egam2)], axis=1).astype(BF16)
            decay = [jnp.exp2(jnp.where(incl, gam[h] - gbt[a_lane[h]:a_lane[h] + 1, :], -jnp.inf)) for h in range(2)]
            idx = 2 * int(backward) + p
            state = s_ref[idx]
            dirs.append(dict(q2=q2, k2=k2, kf32=kf32, rhs=rhs, decay=decay, beta=beta, strict=strict, gam2=gam2,
                             egam2=egam2, glast2=glast2, state=state, state_b=state.astype(BF16), o_ref=o_ref,
                             cols=cols, idx=idx, backward=backward))

    chains = [(d, h) for d in dirs for h in range(2)]

    def head_lanes(x, h):
        return jnp.where(head0 if h == 0 else ~head0, x, 0).astype(BF16)

    kk = [_dot_nt(head_lanes(d["k2"], h), d["k2"]) for d, h in chains]
    tick()
    ms = [jnp.where(d["strict"], d["beta"][h] * kkh * d["decay"][h], 0.0).astype(BF16)
          for (d, h), kkh in zip(chains, kk)]
    qk = [_dot_nt(head_lanes(d["q2"], h), d["k2"]) for d, h in chains]
    tick()
    a_intra = [(qkh * d["decay"][h]).astype(BF16) for (d, h), qkh in zip(chains, qk)]
    half = CHUNK // 2
    early = [slice(half, CHUNK) if d["backward"] else slice(0, half) for d, _ in chains]
    late = [slice(0, half) if d["backward"] else slice(half, CHUNK) for d, _ in chains]
    t_blocks = _unit_tri_inverses([m[s, s] for m, el, lt in zip(ms, early, late) for s in (el, lt)], tick)
    t_early, t_late = t_blocks[0::2], t_blocks[1::2]
    y_early = [_dot(t, d["rhs"][el]) for (d, _), t, el in zip(chains, t_early, early)]
    y_late = [_dot(t, d["rhs"][lt]) for (d, _), t, lt in zip(chains, t_late, late)]
    tick()
    z = [_dot(m[lt, el], ye.astype(BF16)).astype(BF16) for m, ye, el, lt in zip(ms, y_early, early, late)]
    tick()
    corr = [_dot(t, zj) for t, zj in zip(t_late, z)]
    tick()
    uw = [jnp.concatenate([yl - c, ye] if d["backward"] else [ye, yl - c], axis=0)
          for (d, _), ye, yl, c in zip(chains, y_early, y_late, corr)]

    uw2 = [jnp.where(sel0, uw[2 * i], uw[2 * i + 1]) for i in range(len(dirs))]
    qg = [(d["q2"].astype(F32) * d["egam2"]).astype(BF16) for d in dirs]
    zeros = jnp.zeros((LANES, LANES), BF16)
    w_state, q_state = [None] * len(dirs), [None] * len(dirs)
    for i in range(0, len(dirs), npair):
        group = range(i, i + npair)
        state_bd = jnp.concatenate([jnp.concatenate([dirs[j]["state_b"] if j == k else zeros for k in group], axis=1)
                                    for j in group], axis=0)
        ws_cat = _dot(jnp.concatenate([uw2[j][:, LANES:].astype(BF16) for j in group], axis=1), state_bd)
        qs_cat = _dot(jnp.concatenate([qg[j] for j in group], axis=1), state_bd)
        for n, j in enumerate(group):
            w_state[j] = ws_cat[:, n * LANES:(n + 1) * LANES]
            q_state[j] = qs_cat[:, n * LANES:(n + 1) * LANES]
    tick()
    v_new = [(uw2[i][:, :LANES] - w_state[i]).astype(BF16) for i in range(len(dirs))]
    av = [_dot(a_intra[j], v_new[j // 2]) for j in range(len(chains))]
    tick()
    upd = [_dot_tn((d["kf32"] * jnp.exp2(d["glast2"] - d["gam2"])).astype(BF16), v_new[i])
           for i, d in enumerate(dirs)]
    for i, d in enumerate(dirs):
        d["o_ref"][0, :, d["cols"]] = (q_state[i] + jnp.where(head0, av[2 * i], av[2 * i + 1])).astype(d["o_ref"].dtype)
        s_ref[d["idx"]] = d["state"] * jnp.exp2(d["glast2"]) + jnp.where(bd, upd[i], 0.0)


def _recurrent_kernel(bqf, bkf, bvf, cqf, ckf, cvf, gf, bqb, bkb, bvb, cqb, ckb, cvb, gb,
                      hf_ref, hb_ref, of_ref, ob_ref, ms_ref, mm_ref, gs_ref):
    @pl.when(pl.program_id(1) == 0)
    def _():
        ms_ref[...] = jnp.zeros_like(ms_ref)
        mm_ref[...] = jnp.zeros_like(mm_ref)
        gs_ref[...] = jnp.zeros_like(gs_ref)

    fwd_masks = _direction_masks(False)
    bwd_masks = _direction_masks(True)
    gf = (gf[0],) + _gate_cumsums(gf[0], fwd_masks[0], bwd_masks[0])
    gb = (gb[0],) + _gate_cumsums(gb[0], bwd_masks[0], fwd_masks[0])
    mlstm = _mlstm_stages(bqf, bkf, bvf, gf, bqb, bkb, bvb, gb, hf_ref, hb_ref, ms_ref, mm_ref, fwd_masks, bwd_masks)
    _gdn_step(cqf, ckf, cvf, gf, cqb, ckb, cvb, gb, of_ref, ob_ref, gs_ref, fwd_masks, bwd_masks,
              lambda: next(mlstm, None))
    for _ in mlstm:
        pass


def _recurrent_mixers(bqkv, cqkv, gates):
    bsz, seq, _ = bqkv.shape
    nc = seq // CHUNK
    nstate = 2 * (D_B // LANES)
    in_specs = []
    for rev in (False, True):
        cidx = (lambda c: nc - 1 - c) if rev else (lambda c: c)
        for width in (D_B, D_C):
            for part in range(3):
                in_specs.append(pl.BlockSpec((1, CHUNK, width), lambda b, c, part=part, cidx=cidx: (b, cidx(c), part)))
        in_specs.append(pl.BlockSpec((1, CHUNK, LANES), lambda b, c, cidx=cidx: (b, cidx(c), 0)))
    out_b = jax.ShapeDtypeStruct((bsz, seq, D_B), BF16)
    out_c = jax.ShapeDtypeStruct((bsz, seq, D_C), BF16)
    fwd_spec = lambda w: pl.BlockSpec((1, CHUNK, w), lambda b, c: (b, c, 0))
    bwd_spec = lambda w: pl.BlockSpec((1, CHUNK, w), lambda b, c: (b, nc - 1 - c, 0))
    return pl.pallas_call(
        _recurrent_kernel,
        grid=(bsz, nc),
        in_specs=in_specs,
        out_specs=[fwd_spec(D_B), bwd_spec(D_B), fwd_spec(D_C), bwd_spec(D_C)],
        out_shape=(out_b, out_b, out_c, out_c),
        scratch_shapes=[pltpu.VMEM((nstate, LANES, 2 * LANES), F32), pltpu.VMEM((nstate, SUBLANES, LANES), F32),
                        pltpu.VMEM((nstate, LANES, LANES), F32)],
        compiler_params=pltpu.CompilerParams(
            dimension_semantics=("parallel", "arbitrary"), vmem_limit_bytes=VMEM_LIMIT),
        name="recurrent_mixers",
    )(bqkv, bqkv, bqkv, cqkv, cqkv, cqkv, gates, bqkv, bqkv, bqkv, cqkv, cqkv, cqkv, gates)


def _out_kernel(x_ref, oa_ref, hf_ref, hb_ref, cf_ref, cb_ref, nw_ref, wz_ref, mw_ref, gw_ref, wo_ref, y_ref):
    bd = _head_block_diag(MXU_WIDTH).astype(BF16)
    x = x_ref[0]
    hn = (x * lax.rsqrt(jnp.mean(x * x, axis=-1, keepdims=True) + EPS) * nw_ref[...]).astype(BF16)
    z = _dot(hn, wz_ref[...])

    def silu(t):
        return t * _sigmoid(t)

    hsum = _sigmoid(z[:, D_A:D_A + D_B]) * (hf_ref[0].astype(F32) + hb_ref[0].astype(F32))
    csum = cf_ref[0].astype(F32) + cb_ref[0].astype(F32)
    ssb = _head_sum(hsum * hsum, bd)
    ssc = _head_sum(csum * csum, bd)
    ya = oa_ref[0].astype(F32) * silu(z[:, 0:D_A])
    yb = hsum * lax.rsqrt(ssb * (1.0 / HEAD_DIM) + EPS) * mw_ref[...] * silu(z[:, D_A + D_B:D_A + 2 * D_B])
    yc = csum * lax.rsqrt(ssc * (1.0 / HEAD_DIM) + EPS) * gw_ref[...] * silu(z[:, D_A + 2 * D_B:])
    y = jnp.concatenate([ya, yb, yc], axis=1).astype(BF16)
    y_ref[0] = x + _dot(y, wo_ref[...])


def _out_proj(x, layer, oa, hf, hb, cf, cb, nw, wz, mw, gw, wo):
    bsz, seq, d = x.shape
    tile = OUT_TILE
    row_spec = lambda a: pl.BlockSpec((1, tile, a.shape[-1]), lambda b, i: (b, i, 0))
    full = lambda a: _layer_spec(a, layer)
    return pl.pallas_call(
        _out_kernel,
        grid=(bsz, seq // tile),
        in_specs=[row_spec(a) for a in (x, oa, hf, hb, cf, cb)] + [full(a) for a in (nw, wz, mw, gw, wo)],
        out_specs=pl.BlockSpec((1, tile, d), lambda b, i: (b, i, 0)),
        out_shape=jax.ShapeDtypeStruct(x.shape, x.dtype),
        compiler_params=pltpu.CompilerParams(
            dimension_semantics=("parallel", "parallel"), vmem_limit_bytes=VMEM_LIMIT),
        name="out_proj",
    )(x, oa, hf, hb, cf, cb, nw, wz, mw, gw, wo)


def _to_gate_lanes(t):
    lead = t.shape[:-3]
    t = t.reshape(lead + (4, 2, 2))
    return jnp.swapaxes(t, -3, -2).reshape(lead + (16,))


def _gate_weights(w_in):
    nh = D_B // HEAD_DIM
    per_mixer = [_to_gate_lanes(w_in[:, off:off + 4 * nh].reshape(-1, 2, 2, nh)) for off in (_OFF_BIF, _OFF_CAB)]
    return jnp.pad(jnp.concatenate(per_mixer, axis=1), ((0, 0), (0, LANES - 2 * 16)))


def _gate_params(i_bias, f_bias, a_log, dt_bias):
    zeros = jnp.zeros_like(dt_bias)
    row0 = jnp.concatenate([_to_gate_lanes(jnp.stack([i_bias, f_bias])), _to_gate_lanes(jnp.stack([dt_bias, zeros]))])
    row1 = jnp.concatenate([jnp.zeros((16,), F32), _to_gate_lanes(jnp.stack([a_log, zeros]))])
    return jnp.pad(jnp.stack([row0, row1]), ((0, SUBLANES - 2), (0, LANES - 2 * 16)))


def _pack_params(norm_w, w_in, w_out, qk_norm_w, i_bias, f_bias, m_norm_w, conv_w, a_log, dt_bias, g_norm_w):
    w_all = jnp.concatenate([w_in[:, _OFF_AQ:_OFF_AZ], w_in[:, _OFF_BQ:_OFF_BIF], w_in[:, _OFF_CQ:_OFF_CAB],
                             _gate_weights(w_in)], axis=1).astype(BF16)
    wz = jnp.concatenate([w_in[:, _OFF_AZ:_OFF_BQ], w_in[:, _OFF_BO:_OFF_CQ], w_in[:, _OFF_CZ:]],
                         axis=1).astype(BF16)
    qkw = jnp.pad(jnp.tile(qk_norm_w, (1, MXU_WIDTH // HEAD_DIM)) * jnp.array([[HEAD_DIM ** -0.5 * LOG2E], [1.0]], F32),
                  ((0, SUBLANES - 2), (0, 0)))
    return dict(nw=norm_w[None, :], w_all=w_all, wz=wz, wo=w_out.astype(BF16), qkw=qkw,
                gpar=_gate_params(i_bias, f_bias, a_log, dt_bias),
                cw=jnp.pad(conv_w, ((0, SUBLANES - CONV_W), (0, 0))),
                mw=jnp.tile(m_norm_w, D_B // HEAD_DIM)[None, :], gw=jnp.tile(g_norm_w, D_C // HEAD_DIM)[None, :])


def _layer(x, bias_tiles, p, layer):
    aq, *akv, bqkv, cqkv, gates = _in_proj(x, layer, p["nw"], p["w_all"], p["qkw"], p["gpar"], p["cw"])
    oa = _attention(aq, akv, bias_tiles)
    hf, hb, cf, cb = _recurrent_mixers(bqkv, cqkv, gates)
    return _out_proj(x, layer, oa, hf, hb, cf, cb, p["nw"], p["wz"], p["mw"], p["gw"], p["wo"])


def kernel(x, norm_w, w_in, w_out, qk_norm_w, rel_bias, mlstm_i_bias, mlstm_f_bias, mlstm_norm_w, gdn_conv_w,
           gdn_a_log, gdn_dt_bias, gdn_norm_w):
    bias_tiles = _attn_bias_tiles(rel_bias)
    params = jax.vmap(_pack_params)(norm_w, w_in, w_out, qk_norm_w, mlstm_i_bias, mlstm_f_bias, mlstm_norm_w,
                                    gdn_conv_w, gdn_a_log, gdn_dt_bias, gdn_norm_w)
    for l in range(DEPTH):
        x = _layer(x, bias_tiles, params, l)
    return x
```

```python
import functools

import numpy as np
import jax
import jax.numpy as jnp
from jax import lax
from jax.experimental import pallas as pl
from jax.experimental.pallas import tpu as pltpu

F32 = jnp.float32
BF16 = jnp.bfloat16

D_MODEL = 1024
DEPTH = 2
EPS = 1e-6
HEAD_DIM = 64
LANES = 128
SUBLANES = 8
MXU_WIDTH = 256
D_A, D_B, D_C = 512, 256, 256
DILATED_CFGS = ((128, 1), (512, 4), (2048, 16))
N_SIDE = 64
NUM_BUCKETS = 32
REL_MAX_DIST = 1024
CONV_W = 5
CHUNK = 256
Q_SUPER = 4096
Q_BLK = 128
K_WIN = 256
ATTN_ILP = 2
ATTN_DEPTH = 1
IN_TILE = 512
OUT_TILE = 1024
NEG = -1e30
LOG2E = 1.4426950408889634
VMEM_LIMIT = 56 * 1024 * 1024

_OFF_AQ, _OFF_AZ = 0, 1536
_OFF_BQ, _OFF_BIF, _OFF_BO, _OFF_BZ = 2048, 2816, 2832, 3088
_OFF_CQ, _OFF_CAB, _OFF_CZ = 3344, 4112, 4128
_W_SPLITS = (0, 3 * D_A, 3 * D_A + 3 * D_B, 3 * D_A + 3 * D_B + 3 * D_C, 3 * D_A + 3 * D_B + 3 * D_C + LANES)


def _dot(a, b):
    return jnp.dot(a, b, preferred_element_type=F32)


def _dot_nt(a, b):
    return lax.dot_general(a, b, (((1,), (1,)), ((), ())), preferred_element_type=F32)


def _dot_tn(a, b):
    return lax.dot_general(a, b, (((0,), (0,)), ((), ())), preferred_element_type=F32)


def _split(a):
    hi = a.astype(BF16)
    lo = (a - hi.astype(F32)).astype(BF16)
    return hi, lo


def _iota2(shape, axis):
    return lax.broadcasted_iota(jnp.int32, shape, axis)


def _pair_block_diag(rows, cols):
    r = _iota2((rows, cols), 0)
    c = _iota2((rows, cols), 1)
    return (r < HEAD_DIM) == ((c & HEAD_DIM) == 0)


def _head_block_diag(n):
    return (_iota2((n, n), 0) ^ _iota2((n, n), 1)) < HEAD_DIM


def _head_sum(t, bd):
    return _dot(t.astype(BF16), bd)


def _softplus(y):
    return jnp.maximum(y, 0.0) + jnp.log1p(jnp.exp(-jnp.abs(y)))


def _sigmoid(y):
    return 1.0 / (1.0 + jnp.exp(-y))


def _in_proj_kernel(x_ref, xp_ref, xn_ref, nw_ref, w_ref, qkw_ref, gpar_ref, cw_ref,
                    a_ref, kv1_ref, kv4_ref, kv16_ref, b_ref, c_ref, g_ref, cext_ref, kvs_ref, hn_ref, kvg_ref):
    kv_refs = (kv1_ref, kv4_ref, kv16_ref)
    wa_ref, wb_ref, wc_ref, wg_ref = (w_ref.at[:, lo:hi] for lo, hi in zip(_W_SPLITS[:-1], _W_SPLITS[1:]))
    i = pl.program_id(1)
    n = pl.num_programs(1)
    tile = x_ref.shape[1]
    bd = _head_block_diag(MXU_WIDTH).astype(BF16)

    xe = jnp.concatenate([xp_ref[0], x_ref[0], xn_ref[0]], axis=0)
    ms = jnp.mean(xe * xe, axis=-1, keepdims=True)
    hne = xe * lax.rsqrt(ms + EPS) * nw_ref[...]
    hn_ref[...] = hne[SUBLANES:SUBLANES + tile].astype(BF16)
    hne = hne.astype(BF16)

    ce = _dot(hne, wc_ref[...])
    ta = [_dot(hn_ref[...], wa_ref[:, j * MXU_WIDTH:(j + 1) * MXU_WIDTH]) for j in range(3 * D_A // MXU_WIDTH)]
    p = _dot(hn_ref[...], wg_ref[...]) + gpar_ref[0:1, :]
    ssa = [_head_sum(t * t, bd) for t in ta[:4]]

    row = _iota2((tile + 2 * SUBLANES, 1), 0)
    valid = ((row >= SUBLANES) | (i > 0)) & ((row < tile + SUBLANES) | (i < n - 1))
    cext_ref[...] = jnp.where(valid, ce, 0.0)
    conv = cw_ref[0:1, :] * cext_ref[pl.ds(SUBLANES - CONV_W // 2, tile), :]
    for t in range(1, CONV_W):
        conv = conv + cw_ref[t:t + 1, :] * cext_ref[pl.ds(SUBLANES - CONV_W // 2 + t, tile), :]
    s = conv * _sigmoid(conv)
    tc = [s[:, j * MXU_WIDTH:(j + 1) * MXU_WIDTH] for j in range(3 * D_C // MXU_WIDTH)]

    ssc = [_head_sum(t * t, bd) for t in tc[:2]]
    tb = _dot(hn_ref[...], wb_ref[...])

    for j, t in enumerate(ta):
        if j < 4:
            t = t * lax.rsqrt(ssa[j] * (1.0 / HEAD_DIM) + EPS) * qkw_ref[j // 2:j // 2 + 1, :]
        if j < 2:
            a_ref[0, :, j * MXU_WIDTH:(j + 1) * MXU_WIDTH] = t
        else:
            for half in range(2):
                kvs_ref[2 * (j - 2) + half] = t[:, half * LANES:(half + 1) * LANES]
    for lt in range(kvs_ref.shape[0]):
        lanes = slice(lt * LANES, (lt + 1) * LANES)
        kv_refs[0][0, :, lanes] = kvs_ref[lt].astype(BF16)
        prev_dil, src = 1, [kvs_ref.at[lt]]
        for ci in range(1, len(DILATED_CFGS)):
            dil = DILATED_CFGS[ci][1]
            step = dil // prev_dil
            rows = tile // dil
            nxt = [None] * dil
            for r in range(dil):
                x = src[r % prev_dil][pl.ds(r // prev_dil, rows, stride=step), :]
                kv_refs[ci][0, r, :, lanes] = x.astype(BF16)
                if ci + 1 < len(DILATED_CFGS):
                    kvg_ref[lt, r] = x
                    nxt[r] = kvg_ref.at[lt, r]
            prev_dil, src = dil, nxt

    b_ref[0, :, 0:D_B] = (tb[:, 0:D_B] * HEAD_DIM ** -0.5).astype(b_ref.dtype)
    b_ref[0, :, D_B:3 * D_B] = tb[:, D_B:3 * D_B].astype(b_ref.dtype)

    lane = _iota2(p.shape, 1)
    first4 = (lane & 7) < 4
    is_b = lane < 16
    val_b = jnp.where(first4, p, -_softplus(-p)) * LOG2E
    val_c = jnp.where(first4, -jnp.exp(gpar_ref[1:2, :]) * _softplus(p) * LOG2E, _sigmoid(p))
    g_ref[0] = jnp.where(is_b, val_b, val_c)

    c_ref[0, :, 0:D_C] = (tc[0] * lax.rsqrt(ssc[0] + EPS) * HEAD_DIM ** -0.5).astype(c_ref.dtype)
    c_ref[0, :, D_C:2 * D_C] = (tc[1] * lax.rsqrt(ssc[1] + EPS)).astype(c_ref.dtype)
    c_ref[0, :, 2 * D_C:3 * D_C] = tc[2].astype(c_ref.dtype)


def _layer_spec(a, layer):
    return pl.BlockSpec((None,) + a.shape[1:], lambda b, i: (layer,) + (0,) * (a.ndim - 1))


def _in_proj(x, layer, nw, w_all, qkw, gpar, cw):
    bsz, seq, d = x.shape
    tile = IN_TILE
    nt = seq // tile
    hb = tile // SUBLANES
    full = lambda a: _layer_spec(a, layer)
    row_spec = lambda w: pl.BlockSpec((1, tile, w), lambda b, i: (b, i, 0))
    outs = [((bsz, seq, D_A), F32, row_spec(D_A))]
    for (_, dil) in DILATED_CFGS:
        if dil == 1:
            outs.append(((bsz, seq, 2 * D_A), BF16, row_spec(2 * D_A)))
        else:
            outs.append(((bsz, dil, seq // dil, 2 * D_A), BF16,
                         pl.BlockSpec((1, dil, tile // dil, 2 * D_A), lambda b, i: (b, 0, i, 0))))
    outs += [((bsz, seq, 3 * D_B), BF16, row_spec(3 * D_B)), ((bsz, seq, 3 * D_C), BF16, row_spec(3 * D_C)),
             ((bsz, seq, LANES), F32, row_spec(LANES))]
    return pl.pallas_call(
        _in_proj_kernel,
        grid=(bsz, nt),
        in_specs=[
            row_spec(d),
            pl.BlockSpec((1, SUBLANES, d), lambda b, i: (b, jnp.maximum(i * hb - 1, 0), 0)),
            pl.BlockSpec((1, SUBLANES, d), lambda b, i: (b, jnp.minimum((i + 1) * hb, nt * hb - 1), 0)),
            full(nw), full(w_all), full(qkw), full(gpar), full(cw),
        ],
        out_specs=[spec for _, _, spec in outs],
        out_shape=[jax.ShapeDtypeStruct(shape, dtype) for shape, dtype, _ in outs],
        scratch_shapes=[pltpu.VMEM((tile + 2 * SUBLANES, 3 * D_C), F32), pltpu.VMEM((2 * D_A // LANES, tile, LANES), F32),
                        pltpu.VMEM((tile, d), BF16),
                        pltpu.VMEM((2 * D_A // LANES, DILATED_CFGS[1][1], tile // DILATED_CFGS[1][1], LANES), F32)],
        compiler_params=pltpu.CompilerParams(
            dimension_semantics=("parallel", "arbitrary"), vmem_limit_bytes=VMEM_LIMIT),
        name="in_proj",
    )(x, x, x, nw, w_all, qkw, gpar, cw)


def _t5_bucket(rel):
    half = NUM_BUCKETS // 2
    max_exact = half // 2
    n = np.abs(rel)
    large = max_exact + (np.log(np.maximum(n, 1) / max_exact) / np.log(REL_MAX_DIST / max_exact)
                         * (half - max_exact)).astype(np.int32)
    large = np.minimum(large, half - 1)
    return (rel > 0).astype(np.int32) * half + np.where(n < max_exact, n, large)


def _attn_bias_tiles(rel_bias):
    period = 5 * LANES + 1
    wide = K_WIN + 2 * N_SIDE
    lead = Q_BLK - 1 + N_SIDE
    cfg_tiles = []
    for (_, dil) in DILATED_CFGS:
        offs = dil * np.arange(-N_SIDE, N_SIDE + 1)
        bias = rel_bias[_t5_bucket(offs)].T.astype(F32) * LOG2E
        heads = bias.shape[0]
        vec = jnp.concatenate([jnp.full((heads, lead + N_SIDE), NEG, F32), bias,
                               jnp.full((heads, period - lead - 3 * N_SIDE - 1), NEG, F32)], axis=1)
        rows = jnp.tile(vec, (1, Q_BLK))[:, :Q_BLK * (period - 1)].reshape(heads, Q_BLK, period - 1)
        ext = rows[:, :, lead:lead + wide]
        variants = [ext[:, :, 2 * N_SIDE:2 * N_SIDE + K_WIN], ext[:, :, N_SIDE:N_SIDE + K_WIN],
                    ext[:, :, 0:K_WIN]]
        cfg_tiles.append(jnp.stack(variants, axis=1))
    return jnp.stack(cfg_tiles, axis=0)


def _attn_kernel(q_ref, *rest):
    ncfg = len(DILATED_CFGS)
    kv_refs = [(rest[2 * c], rest[2 * c + 1]) for c in range(ncfg)]
    bias_ref, o_ref, acc_ref, m_ref, l_ref = rest[2 * ncfg:]
    sb = pl.program_id(2)
    seq = kv_refs[0][0].shape[1]
    qsup = q_ref.shape[1]
    nblk = qsup // Q_BLK
    head0 = _iota2((Q_BLK, LANES), 1) < HEAD_DIM
    sel0 = (_iota2((Q_BLK, 2 * LANES), 1) & HEAD_DIM) == 0

    groups = [(pos, ci, g) for pos, ci in enumerate(reversed(range(len(DILATED_CFGS))))
              for g in range(nblk // ATTN_ILP)]

    def scores(pos, ci, g):
        dil = DILATED_CFGS[ci][1]
        n_idx = seq // dil
        blocks = []
        for u in range(ATTN_ILP):
            t = g * ATTN_ILP + u
            r, j = t % dil, t // dil
            qs = r + dil * (j * Q_BLK)
            i0 = sb * (qsup // dil) + j * Q_BLK
            ws = jnp.clip(i0 - N_SIDE, 0, n_idx - K_WIN)
            var = jnp.where(i0 == 0, 0, jnp.where(i0 == n_idx - Q_BLK, 2, 1))
            qrows = pl.ds(qs, Q_BLK) if dil == 1 else pl.ds(qs, Q_BLK, stride=dil)
            krows = pl.ds(pl.multiple_of(ws, N_SIDE), K_WIN)
            k_ref, v_ref = kv_refs[ci]
            if dil == 1:
                k2, v2 = k_ref[0, krows, :], v_ref[0, krows, :]
            else:
                k2, v2 = k_ref[0, r, krows, :], v_ref[0, r, krows, :]
            q2 = q_ref[0, qrows, :].astype(BF16)
            blocks.append(dict(qrows=qrows, var=var, k2=k2,
                               qh=[jnp.where(head0 if h == 0 else ~head0, q2, jnp.zeros_like(q2)) for h in range(2)],
                               vo=jnp.concatenate([v2, jnp.ones_like(v2)], axis=1)))
        chains = [(blk, h) for blk in blocks for h in range(2)]
        s = [_dot_nt(blk["qh"][h], blk["k2"]) + bias_ref[ci, h, blk["var"]] for blk, h in chains]
        mx = [jnp.max(sh, axis=-1, keepdims=True) for sh in s]
        p = [jnp.exp2(sh - m).astype(BF16) for sh, m in zip(s, mx)]
        return dict(pos=pos, blocks=blocks, chains=chains, mx=mx, p=p)

    def accumulate(st):
        pv = [_dot(ph, blk["vo"]) for ph, (blk, _) in zip(st["p"], st["chains"])]
        for u, blk in enumerate(st["blocks"]):
            qrows = blk["qrows"]
            tot = jnp.where(sel0, pv[2 * u], pv[2 * u + 1])
            num2 = tot[:, :LANES]
            l2 = tot[:, LANES:]
            m2 = jnp.where(head0, st["mx"][2 * u], st["mx"][2 * u + 1])
            if st["pos"] == 0:
                acc_ref[qrows, :] = num2
                m_ref[qrows, :] = m2
                l_ref[qrows, :] = l2
            else:
                mo = m_ref[qrows, :]
                mn = jnp.maximum(mo, m2)
                a = jnp.exp2(mo - mn)
                b = jnp.exp2(m2 - mn)
                acc_ref[qrows, :] = acc_ref[qrows, :] * a + num2 * b
                l_ref[qrows, :] = l_ref[qrows, :] * a + l2 * b
                m_ref[qrows, :] = mn

    in_flight = []
    for grp in groups:
        in_flight.append(scores(*grp))
        if len(in_flight) > ATTN_DEPTH:
            accumulate(in_flight.pop(0))
    for st in in_flight:
        accumulate(st)

    o_ref[0] = (acc_ref[...] / l_ref[...]).astype(o_ref.dtype)


def _attention(aq, akv, bias_tiles):
    bsz, seq, _ = aq.shape
    npair = D_A // LANES
    qsup = Q_SUPER
    assert seq % qsup == 0 and seq // DILATED_CFGS[-1][1] >= K_WIN
    in_specs = [pl.BlockSpec((1, qsup, LANES), lambda b, p, s: (b, s, p))]
    operands = [aq]
    for kv, (_, dil) in zip(akv, DILATED_CFGS):
        for part in range(2):
            if dil == 1:
                in_specs.append(pl.BlockSpec((1, seq, LANES), lambda b, p, s, part=part: (b, 0, part * npair + p)))
            else:
                in_specs.append(pl.BlockSpec((1, dil, seq // dil, LANES),
                                             lambda b, p, s, part=part: (b, 0, 0, part * npair + p)))
            operands.append(kv)
    in_specs.append(pl.BlockSpec((len(DILATED_CFGS), 2, 3, Q_BLK, K_WIN), lambda b, p, s: (0, p, 0, 0, 0)))
    return pl.pallas_call(
        _attn_kernel,
        grid=(bsz, npair, seq // qsup),
        in_specs=in_specs,
        out_specs=pl.BlockSpec((1, qsup, LANES), lambda b, p, s: (b, s, p)),
        out_shape=jax.ShapeDtypeStruct((bsz, seq, D_A), BF16),
        scratch_shapes=[pltpu.VMEM((qsup, LANES), F32)] * 3,
        compiler_params=pltpu.CompilerParams(
            dimension_semantics=("parallel", "parallel", "arbitrary"), vmem_limit_bytes=VMEM_LIMIT),
        name="dilated_attn",
    )(*operands, bias_tiles)


def _direction_masks(backward):
    row = _iota2((CHUNK, CHUNK), 0)
    col = _iota2((CHUNK, CHUNK), 1)
    if backward:
        return row <= col, row < col, row ^ col
    return row >= col, row > col, row ^ col


def _gate_cumsums(g, incl, incl_other):
    tri = incl.astype(BF16)
    tri_t = incl_other.astype(BF16)
    both = _dot(tri, jnp.concatenate(_split(g), axis=1))
    gb = both[:, :LANES] + both[:, LANES:]
    gt = g.T
    thi, tlo = _split(gt)
    gbt = _dot(thi, tri_t) + _dot(tlo, tri_t)
    return gb, gt, gbt


def _gate_lane(mixer, pair, kind, backward, head):
    return 16 * mixer + 8 * pair + 4 * kind + 2 * int(backward) + head


def _mlstm_stages(qf, kf, vf, gf, qb, kb, vb, gb, hf_ref, hb_ref, s_ref, m_ref, fwd_masks, bwd_masks):
    npair = D_B // LANES
    head0 = _iota2((CHUNK, LANES), 1) < HEAD_DIM
    sel0 = (_iota2((CHUNK, 2 * LANES), 1) & HEAD_DIM) == 0
    bd = _pair_block_diag(LANES, 2 * LANES)
    wide = lambda x: jnp.concatenate([x, x], axis=1)

    groups = []
    for backward, (q_ref, k_ref, v_ref, g_ref), h_ref in ((False, (qf, kf, vf, gf), hf_ref),
                                                          (True, (qb, kb, vb, gb), hb_ref)):
        incl = (bwd_masks if backward else fwd_masks)[0]
        incl_other = (fwd_masks if backward else bwd_masks)[0]
        last = 0 if backward else CHUNK - 1
        g, gbc, gt, gbt = g_ref
        i_lanes = [_gate_lane(0, p, 0, backward, h) for p in range(npair) for h in range(2)]
        f_lanes = [_gate_lane(0, p, 1, backward, h) for p in range(npair) for h in range(2)]
        for p in range(npair):
            idx = 2 * int(backward) + p
            heads = []
            for h in range(2):
                j = 2 * p + h
                li = jnp.broadcast_to(g[:, i_lanes[j]:i_lanes[j] + 1], (CHUNK, LANES))
                bc = jnp.broadcast_to(gbc[:, f_lanes[j]:f_lanes[j] + 1], (CHUNK, LANES))
                rrow = gt[i_lanes[j]:i_lanes[j] + 1, :] - gbt[f_lanes[j]:f_lanes[j] + 1, :]
                mst = m_ref[idx, h:h + 1, :]
                dmat = jnp.where(incl, wide(bc) + rrow, -jnp.inf)
                inter = bc + mst
                mt = jnp.maximum(jnp.max(dmat, axis=-1, keepdims=True), inter)
                blast = bc[last:last + 1, :]
                wlog = blast - bc + li
                mn = jnp.maximum(blast + mst, jnp.max(wlog, axis=0, keepdims=True))
                heads.append(dict(decay=jnp.exp2(dmat - wide(mt)), iw=jnp.exp2(inter - mt), emt=jnp.exp2(-mt),
                                  ws=jnp.exp2(wlog - mn), dec=jnp.exp2(blast + mst - mn), mnew=mn))
            cols = slice(p * LANES, (p + 1) * LANES)
            v2 = v_ref[0, :, cols]
            groups.append(dict(q2=q_ref[0, :, cols], k2=k_ref[0, :, cols], heads=heads, state=s_ref[idx], idx=idx,
                               h_ref=h_ref, cols=cols,
                               vo=jnp.concatenate([v2, jnp.ones_like(v2)], axis=1)))
            yield

    chains = [(d, h) for d in groups for h in range(2)]
    q_state = [_dot(d["q2"], d["state"].astype(BF16)) for d in groups]
    yield
    qk = [_dot_nt(jnp.where(head0 if h == 0 else ~head0, d["q2"], jnp.zeros_like(d["q2"])), d["k2"])
          for d, h in chains]
    yield
    sc = [(qkh * d["heads"][h]["decay"]).astype(BF16) for (d, h), qkh in zip(chains, qk)]
    yield
    pv = [_dot(sch, d["vo"]) for (d, _), sch in zip(chains, sc)]
    yield
    upd = [_dot_tn(d["k2"], (jnp.where(sel0, wide(d["heads"][0]["ws"]), wide(d["heads"][1]["ws"]))
                             * d["vo"].astype(F32)).astype(BF16)) for d in groups]
    yield
    for i, d in enumerate(groups):
        h0, h1 = d["heads"]
        tot = jnp.where(sel0, wide(h0["iw"]), wide(h1["iw"])) * q_state[i] + jnp.where(sel0, pv[2 * i], pv[2 * i + 1])
        den = jnp.maximum(jnp.abs(tot[:, LANES:]), jnp.where(head0, h0["emt"], h1["emt"]))
        d["h_ref"][0, :, d["cols"]] = (tot[:, :LANES] / den).astype(d["h_ref"].dtype)
        dec2 = jnp.where(sel0[0:1, :], wide(h0["dec"]), wide(h1["dec"]))
        s_ref[d["idx"]] = dec2 * d["state"] + jnp.where(bd, upd[i], 0.0)
        m_ref[d["idx"], 0:1, :] = h0["mnew"]
        m_ref[d["idx"], 1:2, :] = h1["mnew"]


_INV_BASE = 32


def _unit_tri_inverses(ms, upper, tick):
    n = ms[0].shape[0]
    xor = _iota2((n, n), 0) ^ _iota2((n, n), 1)
    diag = xor == 0
    one = jnp.ones((), BF16)
    zero = jnp.zeros((), BF16)
    base = xor < _INV_BASE
    ps = [jnp.where(base, m, zero) for m in ms]
    ts = [jnp.where(diag, one, -p) for p in ps]
    ps = [_dot(p, p).astype(BF16) for p in ps]
    tick()
    k = 2
    while True:
        both = [_dot(jnp.where(diag, one, p), jnp.concatenate([t, p], axis=1)) for t, p in zip(ts, ps)]
        tick()
        ts = [b[:, :n].astype(BF16) for b in both]
        if 2 * k >= _INV_BASE:
            break
        ps = [(b[:, n:] - p.astype(F32)).astype(BF16) for b, p in zip(both, ps)]
        k *= 2
    b = _INV_BASE
    while b < n:
        level = (xor >= b) & (xor < 2 * b)
        sel = [[r for r in range(n // b) if (r % 2 == 0) == up] for up in upper]
        rows = lambda a, rs: jnp.concatenate([a[r * b:(r + 1) * b] for r in rs], axis=0)
        x_rows = [_dot(rows(jnp.where(level, m, zero), rs), t).astype(BF16) for m, t, rs in zip(ms, ts, sel)]
        tick()
        zero_rows = jnp.zeros((b, n), BF16)
        xs = [jnp.concatenate([xr[rs.index(r) * b:(rs.index(r) + 1) * b] if r in rs else zero_rows
                               for r in range(n // b)], axis=0) for xr, rs in zip(x_rows, sel)]
        upd = [_dot(rows(t, rs), x) for t, x, rs in zip(ts, xs, sel)]
        tick()
        ts = [jnp.concatenate([(t[r * b:(r + 1) * b].astype(F32)
                                - u[rs.index(r) * b:(rs.index(r) + 1) * b]).astype(BF16) if r in rs
                               else t[r * b:(r + 1) * b] for r in range(n // b)], axis=0)
              for t, u, rs in zip(ts, upd, sel)]
        b *= 2
    return ts


def _gdn_step(qf, kf, vf, gf, qb, kb, vb, gb, of_ref, ob_ref, s_ref, fwd_masks, bwd_masks, tick):
    npair = D_C // LANES
    head0 = _iota2((CHUNK, LANES), 1) < HEAD_DIM
    sel0 = (_iota2((CHUNK, 2 * LANES), 1) & HEAD_DIM) == 0
    bd = _pair_block_diag(LANES, LANES)
    xor = fwd_masks[2]

    dirs = []
    for backward, (q_ref, k_ref, v_ref, g_ref), o_ref in ((False, (qf, kf, vf, gf), of_ref),
                                                          (True, (qb, kb, vb, gb), ob_ref)):
        incl, strict, _ = bwd_masks if backward else fwd_masks
        incl_other = (fwd_masks if backward else bwd_masks)[0]
        last = 0 if backward else CHUNK - 1
        g, gbc, _, gbt = g_ref
        for p in range(npair):
            cols = slice(p * LANES, (p + 1) * LANES)
            q2, k2, v2 = q_ref[0, :, cols], k_ref[0, :, cols], v_ref[0, :, cols]
            a_lane = [_gate_lane(1, p, 0, backward, h) for h in range(2)]
            b_lane = [_gate_lane(1, p, 1, backward, h) for h in range(2)]
            gam = [gbc[:, a:a + 1] for a in a_lane]
            beta = [g[:, b:b + 1] for b in b_lane]
            glast2 = jnp.where(head0[0:1, :], gbc[last:last + 1, a_lane[0]:a_lane[0] + 1],
                               gbc[last:last + 1, a_lane[1]:a_lane[1] + 1])
            gam2 = jnp.where(head0, gam[0], gam[1])
            beta2 = jnp.where(head0, beta[0], beta[1])
            egam2 = jnp.exp2(gam2)
            kf32 = k2.astype(F32)
            rhs = jnp.concatenate([v2.astype(F32) * beta2, kf32 * (beta2 * egam2)], axis=1).astype(BF16)
            decay = [jnp.exp2(jnp.where(incl, gam[h] - gbt[a_lane[h]:a_lane[h] + 1, :], -jnp.inf)) for h in range(2)]
            idx = 2 * int(backward) + p
            state = s_ref[idx]
            dirs.append(dict(q2=q2, k2=k2, kf32=kf32, rhs=rhs, decay=decay, beta=beta, strict=strict, gam2=gam2,
                             egam2=egam2, glast2=glast2, state=state, state_b=state.astype(BF16), o_ref=o_ref,
                             cols=cols, idx=idx, backward=backward))

    chains = [(d, h) for d in dirs for h in range(2)]

    def head_lanes(x, h):
        return jnp.where(head0 if h == 0 else ~head0, x, 0).astype(BF16)

    kk = [_dot_nt(head_lanes(d["k2"], h), d["k2"]) for d, h in chains]
    tick()
    ms = [jnp.where(d["strict"], d["beta"][h] * kkh * d["decay"][h], 0.0).astype(BF16)
          for (d, h), kkh in zip(chains, kk)]
    qk = [_dot_nt(head_lanes(d["q2"], h), d["k2"]) for d, h in chains]
    tick()
    a_intra = [(qkh * d["decay"][h]).astype(BF16) for (d, h), qkh in zip(chains, qk)]
    half = CHUNK // 2
    early = [slice(half, CHUNK) if d["backward"] else slice(0, half) for d, _ in chains]
    late = [slice(0, half) if d["backward"] else slice(half, CHUNK) for d, _ in chains]
    t_blocks = _unit_tri_inverses([m[s, s] for m, el, lt in zip(ms, early, late) for s in (el, lt)],
                                  [d["backward"] for d, _ in chains for _ in range(2)], tick)
    t_early, t_late = t_blocks[0::2], t_blocks[1::2]
    y_early = [_dot(t, d["rhs"][el]) for (d, _), t, el in zip(chains, t_early, early)]
    y_late = [_dot(t, d["rhs"][lt]) for (d, _), t, lt in zip(chains, t_late, late)]
    tick()
    z = [_dot(m[lt, el], ye.astype(BF16)).astype(BF16) for m, ye, el, lt in zip(ms, y_early, early, late)]
    tick()
    corr = [_dot(t, zj) for t, zj in zip(t_late, z)]
    tick()
    uw = [jnp.concatenate([yl - c, ye] if d["backward"] else [ye, yl - c], axis=0)
          for (d, _), ye, yl, c in zip(chains, y_early, y_late, corr)]

    uw2 = [jnp.where(sel0, uw[2 * i], uw[2 * i + 1]) for i in range(len(dirs))]
    qg = [(d["q2"].astype(F32) * d["egam2"]).astype(BF16) for d in dirs]
    zeros = jnp.zeros((LANES, LANES), BF16)
    w_state, q_state = [None] * len(dirs), [None] * len(dirs)
    for i in range(0, len(dirs), npair):
        group = range(i, i + npair)
        state_bd = jnp.concatenate([jnp.concatenate([dirs[j]["state_b"] if j == k else zeros for k in group], axis=1)
                                    for j in group], axis=0)
        ws_cat = _dot(jnp.concatenate([uw2[j][:, LANES:].astype(BF16) for j in group], axis=1), state_bd)
        qs_cat = _dot(jnp.concatenate([qg[j] for j in group], axis=1), state_bd)
        for n, j in enumerate(group):
            w_state[j] = ws_cat[:, n * LANES:(n + 1) * LANES]
            q_state[j] = qs_cat[:, n * LANES:(n + 1) * LANES]
    tick()
    v_new = [(uw2[i][:, :LANES] - w_state[i]).astype(BF16) for i in range(len(dirs))]
    av = [_dot(a_intra[j], v_new[j // 2]) for j in range(len(chains))]
    tick()
    upd = [_dot_tn((d["kf32"] * jnp.exp2(d["glast2"] - d["gam2"])).astype(BF16), v_new[i])
           for i, d in enumerate(dirs)]
    for i, d in enumerate(dirs):
        d["o_ref"][0, :, d["cols"]] = (q_state[i] + jnp.where(head0, av[2 * i], av[2 * i + 1])).astype(d["o_ref"].dtype)
        s_ref[d["idx"]] = d["state"] * jnp.exp2(d["glast2"]) + jnp.where(bd, upd[i], 0.0)


def _recurrent_kernel(bqf, bkf, bvf, cqf, ckf, cvf, gf, bqb, bkb, bvb, cqb, ckb, cvb, gb,
                      hf_ref, hb_ref, of_ref, ob_ref, ms_ref, mm_ref, gs_ref):
    @pl.when(pl.program_id(1) == 0)
    def _():
        ms_ref[...] = jnp.zeros_like(ms_ref)
        mm_ref[...] = jnp.zeros_like(mm_ref)
        gs_ref[...] = jnp.zeros_like(gs_ref)

    fwd_masks = _direction_masks(False)
    bwd_masks = _direction_masks(True)
    gf = (gf[0],) + _gate_cumsums(gf[0], fwd_masks[0], bwd_masks[0])
    gb = (gb[0],) + _gate_cumsums(gb[0], bwd_masks[0], fwd_masks[0])
    mlstm = _mlstm_stages(bqf, bkf, bvf, gf, bqb, bkb, bvb, gb, hf_ref, hb_ref, ms_ref, mm_ref, fwd_masks, bwd_masks)
    _gdn_step(cqf, ckf, cvf, gf, cqb, ckb, cvb, gb, of_ref, ob_ref, gs_ref, fwd_masks, bwd_masks,
              lambda: next(mlstm, None))
    for _ in mlstm:
        pass


def _recurrent_mixers(bqkv, cqkv, gates):
    bsz, seq, _ = bqkv.shape
    nc = seq // CHUNK
    nstate = 2 * (D_B // LANES)
    in_specs = []
    for rev in (False, True):
        cidx = (lambda c: nc - 1 - c) if rev else (lambda c: c)
        for width in (D_B, D_C):
            for part in range(3):
                in_specs.append(pl.BlockSpec((1, CHUNK, width), lambda b, c, part=part, cidx=cidx: (b, cidx(c), part)))
        in_specs.append(pl.BlockSpec((1, CHUNK, LANES), lambda b, c, cidx=cidx: (b, cidx(c), 0)))
    out_b = jax.ShapeDtypeStruct((bsz, seq, D_B), BF16)
    out_c = jax.ShapeDtypeStruct((bsz, seq, D_C), BF16)
    fwd_spec = lambda w: pl.BlockSpec((1, CHUNK, w), lambda b, c: (b, c, 0))
    bwd_spec = lambda w: pl.BlockSpec((1, CHUNK, w), lambda b, c: (b, nc - 1 - c, 0))
    return pl.pallas_call(
        _recurrent_kernel,
        grid=(bsz, nc),
        in_specs=in_specs,
        out_specs=[fwd_spec(D_B), bwd_spec(D_B), fwd_spec(D_C), bwd_spec(D_C)],
        out_shape=(out_b, out_b, out_c, out_c),
        scratch_shapes=[pltpu.VMEM((nstate, LANES, 2 * LANES), F32), pltpu.VMEM((nstate, SUBLANES, LANES), F32),
                        pltpu.VMEM((nstate, LANES, LANES), F32)],
        compiler_params=pltpu.CompilerParams(
            dimension_semantics=("parallel", "arbitrary"), vmem_limit_bytes=VMEM_LIMIT),
        name="recurrent_mixers",
    )(bqkv, bqkv, bqkv, cqkv, cqkv, cqkv, gates, bqkv, bqkv, bqkv, cqkv, cqkv, cqkv, gates)


def _out_kernel(x_ref, oa_ref, hf_ref, hb_ref, cf_ref, cb_ref, nw_ref, wz_ref, mw_ref, gw_ref, wo_ref, y_ref):
    bd = _head_block_diag(MXU_WIDTH).astype(BF16)
    x = x_ref[0]
    hn = (x * lax.rsqrt(jnp.mean(x * x, axis=-1, keepdims=True) + EPS) * nw_ref[...]).astype(BF16)
    z = _dot(hn, wz_ref[...])

    def silu(t):
        return t * _sigmoid(t)

    hsum = _sigmoid(z[:, D_A:D_A + D_B]) * (hf_ref[0].astype(F32) + hb_ref[0].astype(F32))
    csum = cf_ref[0].astype(F32) + cb_ref[0].astype(F32)
    ssb = _head_sum(hsum * hsum, bd)
    ssc = _head_sum(csum * csum, bd)
    ya = oa_ref[0].astype(F32) * silu(z[:, 0:D_A])
    yb = hsum * lax.rsqrt(ssb * (1.0 / HEAD_DIM) + EPS) * mw_ref[...] * silu(z[:, D_A + D_B:D_A + 2 * D_B])
    yc = csum * lax.rsqrt(ssc * (1.0 / HEAD_DIM) + EPS) * gw_ref[...] * silu(z[:, D_A + 2 * D_B:])
    y = jnp.concatenate([ya, yb, yc], axis=1).astype(BF16)
    y_ref[0] = x + _dot(y, wo_ref[...])


def _out_proj(x, layer, oa, hf, hb, cf, cb, nw, wz, mw, gw, wo):
    bsz, seq, d = x.shape
    tile = OUT_TILE
    row_spec = lambda a: pl.BlockSpec((1, tile, a.shape[-1]), lambda b, i: (b, i, 0))
    full = lambda a: _layer_spec(a, layer)
    return pl.pallas_call(
        _out_kernel,
        grid=(bsz, seq // tile),
        in_specs=[row_spec(a) for a in (x, oa, hf, hb, cf, cb)] + [full(a) for a in (nw, wz, mw, gw, wo)],
        out_specs=pl.BlockSpec((1, tile, d), lambda b, i: (b, i, 0)),
        out_shape=jax.ShapeDtypeStruct(x.shape, x.dtype),
        compiler_params=pltpu.CompilerParams(
            dimension_semantics=("parallel", "parallel"), vmem_limit_bytes=VMEM_LIMIT),
        name="out_proj",
    )(x, oa, hf, hb, cf, cb, nw, wz, mw, gw, wo)


def _to_gate_lanes(t):
    lead = t.shape[:-3]
    t = t.reshape(lead + (4, 2, 2))
    return jnp.swapaxes(t, -3, -2).reshape(lead + (16,))


def _gate_weights(w_in):
    nh = D_B // HEAD_DIM
    per_mixer = [_to_gate_lanes(w_in[:, off:off + 4 * nh].reshape(-1, 2, 2, nh)) for off in (_OFF_BIF, _OFF_CAB)]
    return jnp.pad(jnp.concatenate(per_mixer, axis=1), ((0, 0), (0, LANES - 2 * 16)))


def _gate_params(i_bias, f_bias, a_log, dt_bias):
    zeros = jnp.zeros_like(dt_bias)
    row0 = jnp.concatenate([_to_gate_lanes(jnp.stack([i_bias, f_bias])), _to_gate_lanes(jnp.stack([dt_bias, zeros]))])
    row1 = jnp.concatenate([jnp.zeros((16,), F32), _to_gate_lanes(jnp.stack([a_log, zeros]))])
    return jnp.pad(jnp.stack([row0, row1]), ((0, SUBLANES - 2), (0, LANES - 2 * 16)))


def _pack_params(norm_w, w_in, w_out, qk_norm_w, i_bias, f_bias, m_norm_w, conv_w, a_log, dt_bias, g_norm_w):
    w_all = jnp.concatenate([w_in[:, _OFF_AQ:_OFF_AZ], w_in[:, _OFF_BQ:_OFF_BIF], w_in[:, _OFF_CQ:_OFF_CAB],
                             _gate_weights(w_in)], axis=1).astype(BF16)
    wz = jnp.concatenate([w_in[:, _OFF_AZ:_OFF_BQ], w_in[:, _OFF_BO:_OFF_CQ], w_in[:, _OFF_CZ:]],
                         axis=1).astype(BF16)
    qkw = jnp.pad(jnp.tile(qk_norm_w, (1, MXU_WIDTH // HEAD_DIM)) * jnp.array([[HEAD_DIM ** -0.5 * LOG2E], [1.0]], F32),
                  ((0, SUBLANES - 2), (0, 0)))
    return dict(nw=norm_w[None, :], w_all=w_all, wz=wz, wo=w_out.astype(BF16), qkw=qkw,
                gpar=_gate_params(i_bias, f_bias, a_log, dt_bias),
                cw=jnp.pad(conv_w, ((0, SUBLANES - CONV_W), (0, 0))),
                mw=jnp.tile(m_norm_w, D_B // HEAD_DIM)[None, :], gw=jnp.tile(g_norm_w, D_C // HEAD_DIM)[None, :])


def _layer(x, bias_tiles, p, layer):
    aq, *akv, bqkv, cqkv, gates = _in_proj(x, layer, p["nw"], p["w_all"], p["qkw"], p["gpar"], p["cw"])
    oa = _attention(aq, akv, bias_tiles)
    hf, hb, cf, cb = _recurrent_mixers(bqkv, cqkv, gates)
    return _out_proj(x, layer, oa, hf, hb, cf, cb, p["nw"], p["wz"], p["mw"], p["gw"], p["wo"])


def kernel(x, norm_w, w_in, w_out, qk_norm_w, rel_bias, mlstm_i_bias, mlstm_f_bias, mlstm_norm_w, gdn_conv_w,
           gdn_a_log, gdn_dt_bias, gdn_norm_w):
    bias_tiles = _attn_bias_tiles(rel_bias)
    params = jax.vmap(_pack_params)(norm_w, w_in, w_out, qk_norm_w, mlstm_i_bias, mlstm_f_bias, mlstm_norm_w,
                                    gdn_conv_w, gdn_a_log, gdn_dt_bias, gdn_norm_w)
    for l in range(DEPTH):
        x = _layer(x, bias_tiles, params, l)
    return x
```

```python
import numpy as np
import jax
import jax.numpy as jnp
from jax import lax
from jax.experimental import pallas as pl
from jax.experimental.pallas import tpu as pltpu

F32 = jnp.float32
BF16 = jnp.bfloat16

D_MODEL = 1024
DEPTH = 2
EPS = 1e-6
HEAD_DIM = 64
LANES = 128
SUBLANES = 8
MXU_WIDTH = 256
D_A, D_B, D_C = 512, 256, 256
DILATED_CFGS = ((128, 1), (512, 4), (2048, 16))
N_SIDE = 64
NUM_BUCKETS = 32
REL_MAX_DIST = 1024
CONV_W = 5
CHUNK = 256
Q_SUPER = 4096
Q_BLK = 128
K_WIN = 256
ATTN_ILP = 2
ATTN_DEPTH = 1
IN_TILE = 512
OUT_TILE = 1024
NEG = -1e30
LOG2E = 1.4426950408889634
VMEM_LIMIT = 56 * 1024 * 1024

_OFF_AQ, _OFF_AZ = 0, 1536
_OFF_BQ, _OFF_BIF, _OFF_BO, _OFF_BZ = 2048, 2816, 2832, 3088
_OFF_CQ, _OFF_CAB, _OFF_CZ = 3344, 4112, 4128
_W_SPLITS = (0, 3 * D_A, 3 * D_A + 3 * D_B, 3 * D_A + 3 * D_B + 3 * D_C, 3 * D_A + 3 * D_B + 3 * D_C + LANES)


def _dot(a, b):
    return jnp.dot(a, b, preferred_element_type=F32)


def _dot_nt(a, b):
    return lax.dot_general(a, b, (((1,), (1,)), ((), ())), preferred_element_type=F32)


def _dot_tn(a, b):
    return lax.dot_general(a, b, (((0,), (0,)), ((), ())), preferred_element_type=F32)


def _split(a):
    hi = a.astype(BF16)
    lo = (a - hi.astype(F32)).astype(BF16)
    return hi, lo


def _iota2(shape, axis):
    return lax.broadcasted_iota(jnp.int32, shape, axis)


def _pair_block_diag(rows, cols):
    r = _iota2((rows, cols), 0)
    c = _iota2((rows, cols), 1)
    return (r < HEAD_DIM) == ((c & HEAD_DIM) == 0)


def _head_block_diag(n):
    return (_iota2((n, n), 0) ^ _iota2((n, n), 1)) < HEAD_DIM


def _head_sum(t, bd):
    return _dot(t.astype(BF16), bd)


def _softplus(y):
    return jnp.maximum(y, 0.0) + jnp.log1p(jnp.exp(-jnp.abs(y)))


def _sigmoid(y):
    return 1.0 / (1.0 + jnp.exp(-y))


def _in_proj_kernel(x_ref, xp_ref, xn_ref, nw_ref, w_ref, qkw_ref, gpar_ref, cw_ref,
                    a_ref, kv1_ref, kv4_ref, kv16_ref, b_ref, c_ref, g_ref, cext_ref, kvs_ref, hn_ref, kvg_ref):
    kv_refs = (kv1_ref, kv4_ref, kv16_ref)
    wa_ref, wb_ref, wc_ref, wg_ref = (w_ref.at[:, lo:hi] for lo, hi in zip(_W_SPLITS[:-1], _W_SPLITS[1:]))
    i = pl.program_id(1)
    n = pl.num_programs(1)
    tile = x_ref.shape[1]
    bd = _head_block_diag(MXU_WIDTH).astype(BF16)

    xe = jnp.concatenate([xp_ref[0], x_ref[0], xn_ref[0]], axis=0)
    ms = jnp.mean(xe * xe, axis=-1, keepdims=True)
    hne = xe * lax.rsqrt(ms + EPS) * nw_ref[...]
    hn_ref[...] = hne[SUBLANES:SUBLANES + tile].astype(BF16)
    hne = hne.astype(BF16)

    ce = _dot(hne, wc_ref[...])
    ta = [_dot(hn_ref[...], wa_ref[:, j * MXU_WIDTH:(j + 1) * MXU_WIDTH]) for j in range(3 * D_A // MXU_WIDTH)]
    p = _dot(hn_ref[...], wg_ref[...]) + gpar_ref[0:1, :]
    ssa = [_head_sum(t * t, bd) for t in ta[:4]]

    row = _iota2((tile + 2 * SUBLANES, 1), 0)
    valid = ((row >= SUBLANES) | (i > 0)) & ((row < tile + SUBLANES) | (i < n - 1))
    cext_ref[...] = jnp.where(valid, ce, 0.0)
    conv = cw_ref[0:1, :] * cext_ref[pl.ds(SUBLANES - CONV_W // 2, tile), :]
    for t in range(1, CONV_W):
        conv = conv + cw_ref[t:t + 1, :] * cext_ref[pl.ds(SUBLANES - CONV_W // 2 + t, tile), :]
    s = conv * _sigmoid(conv)
    tc = [s[:, j * MXU_WIDTH:(j + 1) * MXU_WIDTH] for j in range(3 * D_C // MXU_WIDTH)]

    ssc = [_head_sum(t * t, bd) for t in tc[:2]]
    tb = _dot(hn_ref[...], wb_ref[...])

    for j, t in enumerate(ta):
        if j < 4:
            t = t * lax.rsqrt(ssa[j] * (1.0 / HEAD_DIM) + EPS) * qkw_ref[j // 2:j // 2 + 1, :]
        if j < 2:
            a_ref[0, :, j * MXU_WIDTH:(j + 1) * MXU_WIDTH] = t
        else:
            for half in range(2):
                kvs_ref[2 * (j - 2) + half] = t[:, half * LANES:(half + 1) * LANES]
    for lt in range(kvs_ref.shape[0]):
        lanes = slice(lt * LANES, (lt + 1) * LANES)
        kv_refs[0][0, :, lanes] = kvs_ref[lt].astype(BF16)
        prev_dil, src = 1, [kvs_ref.at[lt]]
        for ci in range(1, len(DILATED_CFGS)):
            dil = DILATED_CFGS[ci][1]
            step = dil // prev_dil
            rows = tile // dil
            nxt = [None] * dil
            for r in range(dil):
                x = src[r % prev_dil][pl.ds(r // prev_dil, rows, stride=step), :]
                kv_refs[ci][0, r, :, lanes] = x.astype(BF16)
                if ci + 1 < len(DILATED_CFGS):
                    kvg_ref[lt, r] = x
                    nxt[r] = kvg_ref.at[lt, r]
            prev_dil, src = dil, nxt

    b_ref[0, :, 0:D_B] = (tb[:, 0:D_B] * HEAD_DIM ** -0.5).astype(b_ref.dtype)
    b_ref[0, :, D_B:3 * D_B] = tb[:, D_B:3 * D_B].astype(b_ref.dtype)

    lane = _iota2(p.shape, 1)
    first4 = (lane & 7) < 4
    is_b = lane < 16
    val_b = jnp.where(first4, p, -_softplus(-p)) * LOG2E
    val_c = jnp.where(first4, -jnp.exp(gpar_ref[1:2, :]) * _softplus(p) * LOG2E, _sigmoid(p))
    g_ref[0] = jnp.where(is_b, val_b, val_c)

    c_ref[0, :, 0:D_C] = (tc[0] * lax.rsqrt(ssc[0] + EPS) * HEAD_DIM ** -0.5).astype(c_ref.dtype)
    c_ref[0, :, D_C:2 * D_C] = (tc[1] * lax.rsqrt(ssc[1] + EPS)).astype(c_ref.dtype)
    c_ref[0, :, 2 * D_C:3 * D_C] = tc[2].astype(c_ref.dtype)


def _layer_spec(a, layer):
    return pl.BlockSpec((None,) + a.shape[1:], lambda b, i: (layer,) + (0,) * (a.ndim - 1))


def _in_proj(x, layer, nw, w_all, qkw, gpar, cw):
    bsz, seq, d = x.shape
    tile = IN_TILE
    nt = seq // tile
    hb = tile // SUBLANES
    full = lambda a: _layer_spec(a, layer)
    row_spec = lambda w: pl.BlockSpec((1, tile, w), lambda b, i: (b, i, 0))
    outs = [((bsz, seq, D_A), F32, row_spec(D_A))]
    for (_, dil) in DILATED_CFGS:
        if dil == 1:
            outs.append(((bsz, seq, 2 * D_A), BF16, row_spec(2 * D_A)))
        else:
            outs.append(((bsz, dil, seq // dil, 2 * D_A), BF16,
                         pl.BlockSpec((1, dil, tile // dil, 2 * D_A), lambda b, i: (b, 0, i, 0))))
    outs += [((bsz, seq, 3 * D_B), BF16, row_spec(3 * D_B)), ((bsz, seq, 3 * D_C), BF16, row_spec(3 * D_C)),
             ((bsz, seq, LANES), F32, row_spec(LANES))]
    return pl.pallas_call(
        _in_proj_kernel,
        grid=(bsz, nt),
        in_specs=[
            row_spec(d),
            pl.BlockSpec((1, SUBLANES, d), lambda b, i: (b, jnp.maximum(i * hb - 1, 0), 0)),
            pl.BlockSpec((1, SUBLANES, d), lambda b, i: (b, jnp.minimum((i + 1) * hb, nt * hb - 1), 0)),
            full(nw), full(w_all), full(qkw), full(gpar), full(cw),
        ],
        out_specs=[spec for _, _, spec in outs],
        out_shape=[jax.ShapeDtypeStruct(shape, dtype) for shape, dtype, _ in outs],
        scratch_shapes=[pltpu.VMEM((tile + 2 * SUBLANES, 3 * D_C), F32), pltpu.VMEM((2 * D_A // LANES, tile, LANES), F32),
                        pltpu.VMEM((tile, d), BF16),
                        pltpu.VMEM((2 * D_A // LANES, DILATED_CFGS[1][1], tile // DILATED_CFGS[1][1], LANES), F32)],
        compiler_params=pltpu.CompilerParams(
            dimension_semantics=("parallel", "arbitrary"), vmem_limit_bytes=VMEM_LIMIT),
        name="in_proj",
    )(x, x, x, nw, w_all, qkw, gpar, cw)


def _t5_bucket(rel):
    half = NUM_BUCKETS // 2
    max_exact = half // 2
    n = np.abs(rel)
    large = max_exact + (np.log(np.maximum(n, 1) / max_exact) / np.log(REL_MAX_DIST / max_exact)
                         * (half - max_exact)).astype(np.int32)
    large = np.minimum(large, half - 1)
    return (rel > 0).astype(np.int32) * half + np.where(n < max_exact, n, large)


def _attn_bias_tiles(rel_bias):
    period = 5 * LANES + 1
    wide = K_WIN + 2 * N_SIDE
    lead = Q_BLK - 1 + N_SIDE
    cfg_tiles = []
    for (_, dil) in DILATED_CFGS:
        offs = dil * np.arange(-N_SIDE, N_SIDE + 1)
        bias = rel_bias[_t5_bucket(offs)].T.astype(F32) * LOG2E
        heads = bias.shape[0]
        vec = jnp.concatenate([jnp.full((heads, lead + N_SIDE), NEG, F32), bias,
                               jnp.full((heads, period - lead - 3 * N_SIDE - 1), NEG, F32)], axis=1)
        rows = jnp.tile(vec, (1, Q_BLK))[:, :Q_BLK * (period - 1)].reshape(heads, Q_BLK, period - 1)
        ext = rows[:, :, lead:lead + wide]
        variants = [ext[:, :, 2 * N_SIDE:2 * N_SIDE + K_WIN], ext[:, :, N_SIDE:N_SIDE + K_WIN],
                    ext[:, :, 0:K_WIN]]
        cfg_tiles.append(jnp.stack(variants, axis=1))
    return jnp.stack(cfg_tiles, axis=0)


def _attn_kernel(q_ref, *rest):
    ncfg = len(DILATED_CFGS)
    kv_refs = [(rest[2 * c], rest[2 * c + 1]) for c in range(ncfg)]
    bias_ref, o_ref, acc_ref, m_ref, l_ref = rest[2 * ncfg:]
    sb = pl.program_id(2)
    seq = kv_refs[0][0].shape[1]
    qsup = q_ref.shape[1]
    nblk = qsup // Q_BLK
    head0 = _iota2((Q_BLK, LANES), 1) < HEAD_DIM
    sel0 = (_iota2((Q_BLK, 2 * LANES), 1) & HEAD_DIM) == 0

    groups = [(pos, ci, g) for pos, ci in enumerate(reversed(range(len(DILATED_CFGS))))
              for g in range(nblk // ATTN_ILP)]

    def scores(pos, ci, g):
        dil = DILATED_CFGS[ci][1]
        n_idx = seq // dil
        blocks = []
        for u in range(ATTN_ILP):
            t = g * ATTN_ILP + u
            r, j = t % dil, t // dil
            qs = r + dil * (j * Q_BLK)
            i0 = sb * (qsup // dil) + j * Q_BLK
            ws = jnp.clip(i0 - N_SIDE, 0, n_idx - K_WIN)
            var = jnp.where(i0 == 0, 0, jnp.where(i0 == n_idx - Q_BLK, 2, 1))
            qrows = pl.ds(qs, Q_BLK) if dil == 1 else pl.ds(qs, Q_BLK, stride=dil)
            krows = pl.ds(pl.multiple_of(ws, N_SIDE), K_WIN)
            k_ref, v_ref = kv_refs[ci]
            if dil == 1:
                k2, v2 = k_ref[0, krows, :], v_ref[0, krows, :]
            else:
                k2, v2 = k_ref[0, r, krows, :], v_ref[0, r, krows, :]
            q2 = q_ref[0, qrows, :].astype(BF16)
            blocks.append(dict(qrows=qrows, var=var, k2=k2,
                               qh=[jnp.where(head0 if h == 0 else ~head0, q2, jnp.zeros_like(q2)) for h in range(2)],
                               vo=jnp.concatenate([v2, jnp.ones_like(v2)], axis=1)))
        chains = [(blk, h) for blk in blocks for h in range(2)]
        s = [_dot_nt(blk["qh"][h], blk["k2"]) + bias_ref[ci, h, blk["var"]] for blk, h in chains]
        mx = [jnp.max(sh, axis=-1, keepdims=True) for sh in s]
        p = [jnp.exp2(sh - m).astype(BF16) for sh, m in zip(s, mx)]
        return dict(pos=pos, blocks=blocks, chains=chains, mx=mx, p=p)

    def accumulate(st):
        pv = [_dot(ph, blk["vo"]) for ph, (blk, _) in zip(st["p"], st["chains"])]
        for u, blk in enumerate(st["blocks"]):
            qrows = blk["qrows"]
            tot = jnp.where(sel0, pv[2 * u], pv[2 * u + 1])
            num2 = tot[:, :LANES]
            l2 = tot[:, LANES:]
            m2 = jnp.where(head0, st["mx"][2 * u], st["mx"][2 * u + 1])
            if st["pos"] == 0:
                acc_ref[qrows, :] = num2
                m_ref[qrows, :] = m2
                l_ref[qrows, :] = l2
            else:
                mo = m_ref[qrows, :]
                mn = jnp.maximum(mo, m2)
                a = jnp.exp2(mo - mn)
                b = jnp.exp2(m2 - mn)
                acc_ref[qrows, :] = acc_ref[qrows, :] * a + num2 * b
                l_ref[qrows, :] = l_ref[qrows, :] * a + l2 * b
                m_ref[qrows, :] = mn

    in_flight = []
    for grp in groups:
        in_flight.append(scores(*grp))
        if len(in_flight) > ATTN_DEPTH:
            accumulate(in_flight.pop(0))
    for st in in_flight:
        accumulate(st)

    o_ref[0] = (acc_ref[...] / l_ref[...]).astype(o_ref.dtype)


def _attention(aq, akv, bias_tiles):
    bsz, seq, _ = aq.shape
    npair = D_A // LANES
    qsup = Q_SUPER
    assert seq % qsup == 0 and seq // DILATED_CFGS[-1][1] >= K_WIN
    in_specs = [pl.BlockSpec((1, qsup, LANES), lambda b, p, s: (b, s, p))]
    operands = [aq]
    for kv, (_, dil) in zip(akv, DILATED_CFGS):
        for part in range(2):
            if dil == 1:
                in_specs.append(pl.BlockSpec((1, seq, LANES), lambda b, p, s, part=part: (b, 0, part * npair + p)))
            else:
                in_specs.append(pl.BlockSpec((1, dil, seq // dil, LANES),
                                             lambda b, p, s, part=part: (b, 0, 0, part * npair + p)))
            operands.append(kv)
    in_specs.append(pl.BlockSpec((len(DILATED_CFGS), 2, 3, Q_BLK, K_WIN), lambda b, p, s: (0, p, 0, 0, 0)))
    return pl.pallas_call(
        _attn_kernel,
        grid=(bsz, npair, seq // qsup),
        in_specs=in_specs,
        out_specs=pl.BlockSpec((1, qsup, LANES), lambda b, p, s: (b, s, p)),
        out_shape=jax.ShapeDtypeStruct((bsz, seq, D_A), BF16),
        scratch_shapes=[pltpu.VMEM((qsup, LANES), F32)] * 3,
        compiler_params=pltpu.CompilerParams(
            dimension_semantics=("parallel", "parallel", "arbitrary"), vmem_limit_bytes=VMEM_LIMIT),
        name="dilated_attn",
    )(*operands, bias_tiles)


def _direction_masks(backward):
    row = _iota2((CHUNK, CHUNK), 0)
    col = _iota2((CHUNK, CHUNK), 1)
    if backward:
        return row <= col, row < col
    return row >= col, row > col


def _gate_cumsums(g, incl, incl_other):
    tri = incl.astype(BF16)
    tri_t = incl_other.astype(BF16)
    both = _dot(tri, jnp.concatenate(_split(g), axis=1))
    gb = both[:, :LANES] + both[:, LANES:]
    gt = g.T
    thi, tlo = _split(gt)
    gbt = _dot(thi, tri_t) + _dot(tlo, tri_t)
    return gb, gt, gbt


def _gate_lane(mixer, pair, kind, backward, head):
    return 16 * mixer + 8 * pair + 4 * kind + 2 * int(backward) + head


def _mlstm_stages(qf, kf, vf, gf, qb, kb, vb, gb, hf_ref, hb_ref, s_ref, m_ref, fwd_masks, bwd_masks):
    npair = D_B // LANES
    head0 = _iota2((CHUNK, LANES), 1) < HEAD_DIM
    sel0 = (_iota2((CHUNK, 2 * LANES), 1) & HEAD_DIM) == 0
    bd = _pair_block_diag(LANES, 2 * LANES)
    wide = lambda x: jnp.concatenate([x, x], axis=1)

    groups = []
    for backward, (q_ref, k_ref, v_ref, g_ref), h_ref in ((False, (qf, kf, vf, gf), hf_ref),
                                                          (True, (qb, kb, vb, gb), hb_ref)):
        incl = (bwd_masks if backward else fwd_masks)[0]
        last = 0 if backward else CHUNK - 1
        g, gbc, gt, gbt = g_ref
        i_lanes = [_gate_lane(0, p, 0, backward, h) for p in range(npair) for h in range(2)]
        f_lanes = [_gate_lane(0, p, 1, backward, h) for p in range(npair) for h in range(2)]
        for p in range(npair):
            idx = 2 * int(backward) + p
            heads = []
            for h in range(2):
                j = 2 * p + h
                li = jnp.broadcast_to(g[:, i_lanes[j]:i_lanes[j] + 1], (CHUNK, LANES))
                bc = jnp.broadcast_to(gbc[:, f_lanes[j]:f_lanes[j] + 1], (CHUNK, LANES))
                rrow = gt[i_lanes[j]:i_lanes[j] + 1, :] - gbt[f_lanes[j]:f_lanes[j] + 1, :]
                mst = m_ref[idx, h:h + 1, :]
                dmat = jnp.where(incl, wide(bc) + rrow, -jnp.inf)
                inter = bc + mst
                mt = jnp.maximum(jnp.max(dmat, axis=-1, keepdims=True), inter)
                blast = bc[last:last + 1, :]
                wlog = blast - bc + li
                mn = jnp.maximum(blast + mst, jnp.max(wlog, axis=0, keepdims=True))
                heads.append(dict(decay=jnp.exp2(dmat - wide(mt)), iw=jnp.exp2(inter - mt), emt=jnp.exp2(-mt),
                                  ws=jnp.exp2(wlog - mn), dec=jnp.exp2(blast + mst - mn), mnew=mn))
            cols = slice(p * LANES, (p + 1) * LANES)
            v2 = v_ref[0, :, cols]
            groups.append(dict(q2=q_ref[0, :, cols], k2=k_ref[0, :, cols], heads=heads, state=s_ref[idx], idx=idx,
                               h_ref=h_ref, cols=cols,
                               vo=jnp.concatenate([v2, jnp.ones_like(v2)], axis=1)))
            yield

    chains = [(d, h) for d in groups for h in range(2)]
    q_state = [_dot(d["q2"], d["state"].astype(BF16)) for d in groups]
    yield
    qk = [_dot_nt(jnp.where(head0 if h == 0 else ~head0, d["q2"], jnp.zeros_like(d["q2"])), d["k2"])
          for d, h in chains]
    yield
    sc = [(qkh * d["heads"][h]["decay"]).astype(BF16) for (d, h), qkh in zip(chains, qk)]
    yield
    pv = [_dot(sch, d["vo"]) for (d, _), sch in zip(chains, sc)]
    yield
    upd = [_dot_tn(d["k2"], (jnp.where(sel0, wide(d["heads"][0]["ws"]), wide(d["heads"][1]["ws"]))
                             * d["vo"].astype(F32)).astype(BF16)) for d in groups]
    yield
    for i, d in enumerate(groups):
        h0, h1 = d["heads"]
        tot = jnp.where(sel0, wide(h0["iw"]), wide(h1["iw"])) * q_state[i] + jnp.where(sel0, pv[2 * i], pv[2 * i + 1])
        den = jnp.maximum(jnp.abs(tot[:, LANES:]), jnp.where(head0, h0["emt"], h1["emt"]))
        d["h_ref"][0, :, d["cols"]] = (tot[:, :LANES] / den).astype(d["h_ref"].dtype)
        dec2 = jnp.where(sel0[0:1, :], wide(h0["dec"]), wide(h1["dec"]))
        s_ref[d["idx"]] = dec2 * d["state"] + jnp.where(bd, upd[i], 0.0)
        m_ref[d["idx"], 0:1, :] = h0["mnew"]
        m_ref[d["idx"], 1:2, :] = h1["mnew"]


_INV_BASE = 32


def _unit_tri_inverses(ms, upper, tick):
    n = ms[0].shape[0]
    xor = _iota2((n, n), 0) ^ _iota2((n, n), 1)
    diag = xor == 0
    one = jnp.ones((), BF16)
    zero = jnp.zeros((), BF16)
    base = xor < _INV_BASE
    ps = [jnp.where(base, m, zero) for m in ms]
    ts = [jnp.where(diag, one, -p) for p in ps]
    ps = [_dot(p, p).astype(BF16) for p in ps]
    tick()
    k = 2
    while True:
        both = [_dot(jnp.where(diag, one, p), jnp.concatenate([t, p], axis=1)) for t, p in zip(ts, ps)]
        tick()
        ts = [b[:, :n].astype(BF16) for b in both]
        if 2 * k >= _INV_BASE:
            break
        ps = [(b[:, n:] - p.astype(F32)).astype(BF16) for b, p in zip(both, ps)]
        k *= 2
    b = _INV_BASE
    while b < n:
        level = (xor >= b) & (xor < 2 * b)
        sel = [[r for r in range(n // b) if (r % 2 == 0) == up] for up in upper]
        rows = lambda a, rs: jnp.concatenate([a[r * b:(r + 1) * b] for r in rs], axis=0)
        x_rows = [_dot(rows(jnp.where(level, m, zero), rs), t).astype(BF16) for m, t, rs in zip(ms, ts, sel)]
        tick()
        zero_rows = jnp.zeros((b, n), BF16)
        xs = [jnp.concatenate([xr[rs.index(r) * b:(rs.index(r) + 1) * b] if r in rs else zero_rows
                               for r in range(n // b)], axis=0) for xr, rs in zip(x_rows, sel)]
        upd = [_dot(rows(t, rs), x) for t, x, rs in zip(ts, xs, sel)]
        tick()
        ts = [jnp.concatenate([(t[r * b:(r + 1) * b].astype(F32)
                                - u[rs.index(r) * b:(rs.index(r) + 1) * b]).astype(BF16) if r in rs
                               else t[r * b:(r + 1) * b] for r in range(n // b)], axis=0)
              for t, u, rs in zip(ts, upd, sel)]
        b *= 2
    return ts


def _gdn_step(qf, kf, vf, gf, qb, kb, vb, gb, of_ref, ob_ref, s_ref, fwd_masks, bwd_masks, tick):
    npair = D_C // LANES
    head0 = _iota2((CHUNK, LANES), 1) < HEAD_DIM
    sel0 = (_iota2((CHUNK, 2 * LANES), 1) & HEAD_DIM) == 0
    bd = _pair_block_diag(LANES, LANES)

    dirs = []
    for backward, (q_ref, k_ref, v_ref, g_ref), o_ref in ((False, (qf, kf, vf, gf), of_ref),
                                                          (True, (qb, kb, vb, gb), ob_ref)):
        incl, strict = bwd_masks if backward else fwd_masks
        last = 0 if backward else CHUNK - 1
        g, gbc, _, gbt = g_ref
        for p in range(npair):
            cols = slice(p * LANES, (p + 1) * LANES)
            q2, k2, v2 = q_ref[0, :, cols], k_ref[0, :, cols], v_ref[0, :, cols]
            a_lane = [_gate_lane(1, p, 0, backward, h) for h in range(2)]
            b_lane = [_gate_lane(1, p, 1, backward, h) for h in range(2)]
            gam = [gbc[:, a:a + 1] for a in a_lane]
            beta = [g[:, b:b + 1] for b in b_lane]
            glast2 = jnp.where(head0[0:1, :], gbc[last:last + 1, a_lane[0]:a_lane[0] + 1],
                               gbc[last:last + 1, a_lane[1]:a_lane[1] + 1])
            gam2 = jnp.where(head0, gam[0], gam[1])
            beta2 = jnp.where(head0, beta[0], beta[1])
            egam2 = jnp.exp2(gam2)
            kf32 = k2.astype(F32)
            rhs = jnp.concatenate([v2.astype(F32) * beta2, kf32 * (beta2 * egam2)], axis=1).astype(BF16)
            decay = [jnp.exp2(jnp.where(incl, gam[h] - gbt[a_lane[h]:a_lane[h] + 1, :], -jnp.inf)) for h in range(2)]
            idx = 2 * int(backward) + p
            state = s_ref[idx]
            dirs.append(dict(q2=q2, k2=k2, kf32=kf32, rhs=rhs, decay=decay, beta=beta, strict=strict, gam2=gam2,
                             egam2=egam2, glast2=glast2, state=state, state_b=state.astype(BF16), o_ref=o_ref,
                             cols=cols, idx=idx, backward=backward))

    chains = [(d, h) for d in dirs for h in range(2)]

    def head_lanes(x, h):
        return jnp.where(head0 if h == 0 else ~head0, x, 0).astype(BF16)

    kk = [_dot_nt(head_lanes(d["k2"], h), d["k2"]) for d, h in chains]
    tick()
    ms = [jnp.where(d["strict"], d["beta"][h] * kkh * d["decay"][h], 0.0).astype(BF16)
          for (d, h), kkh in zip(chains, kk)]
    qk = [_dot_nt(head_lanes(d["q2"], h), d["k2"]) for d, h in chains]
    tick()
    a_intra = [(qkh * d["decay"][h]).astype(BF16) for (d, h), qkh in zip(chains, qk)]
    half = CHUNK // 2
    early = [slice(half, CHUNK) if d["backward"] else slice(0, half) for d, _ in chains]
    late = [slice(0, half) if d["backward"] else slice(half, CHUNK) for d, _ in chains]
    t_blocks = _unit_tri_inverses([m[s, s] for m, el, lt in zip(ms, early, late) for s in (el, lt)],
                                  [d["backward"] for d, _ in chains for _ in range(2)], tick)
    t_early, t_late = t_blocks[0::2], t_blocks[1::2]
    y_early = [_dot(t, d["rhs"][el]) for (d, _), t, el in zip(chains, t_early, early)]
    y_late = [_dot(t, d["rhs"][lt]) for (d, _), t, lt in zip(chains, t_late, late)]
    tick()
    z = [_dot(m[lt, el], ye.astype(BF16)).astype(BF16) for m, ye, el, lt in zip(ms, y_early, early, late)]
    tick()
    corr = [_dot(t, zj) for t, zj in zip(t_late, z)]
    tick()
    uw = [jnp.concatenate([yl - c, ye] if d["backward"] else [ye, yl - c], axis=0)
          for (d, _), ye, yl, c in zip(chains, y_early, y_late, corr)]

    uw2 = [jnp.where(sel0, uw[2 * i], uw[2 * i + 1]) for i in range(len(dirs))]
    qg = [(d["q2"].astype(F32) * d["egam2"]).astype(BF16) for d in dirs]
    zeros = jnp.zeros((LANES, LANES), BF16)
    w_state, q_state = [None] * len(dirs), [None] * len(dirs)
    for i in range(0, len(dirs), npair):
        group = range(i, i + npair)
        state_bd = jnp.concatenate([jnp.concatenate([dirs[j]["state_b"] if j == k else zeros for k in group], axis=1)
                                    for j in group], axis=0)
        ws_cat = _dot(jnp.concatenate([uw2[j][:, LANES:].astype(BF16) for j in group], axis=1), state_bd)
        qs_cat = _dot(jnp.concatenate([qg[j] for j in group], axis=1), state_bd)
        for n, j in enumerate(group):
            w_state[j] = ws_cat[:, n * LANES:(n + 1) * LANES]
            q_state[j] = qs_cat[:, n * LANES:(n + 1) * LANES]
    tick()
    v_new = [(uw2[i][:, :LANES] - w_state[i]).astype(BF16) for i in range(len(dirs))]
    av = [_dot(a_intra[j], v_new[j // 2]) for j in range(len(chains))]
    tick()
    upd = [_dot_tn((d["kf32"] * jnp.exp2(d["glast2"] - d["gam2"])).astype(BF16), v_new[i])
           for i, d in enumerate(dirs)]
    for i, d in enumerate(dirs):
        d["o_ref"][0, :, d["cols"]] = (q_state[i] + jnp.where(head0, av[2 * i], av[2 * i + 1])).astype(d["o_ref"].dtype)
        s_ref[d["idx"]] = d["state"] * jnp.exp2(d["glast2"]) + jnp.where(bd, upd[i], 0.0)


def _recurrent_kernel(bqf, bkf, bvf, cqf, ckf, cvf, gf, bqb, bkb, bvb, cqb, ckb, cvb, gb,
                      hf_ref, hb_ref, of_ref, ob_ref, ms_ref, mm_ref, gs_ref):
    @pl.when(pl.program_id(1) == 0)
    def _():
        ms_ref[...] = jnp.zeros_like(ms_ref)
        mm_ref[...] = jnp.zeros_like(mm_ref)
        gs_ref[...] = jnp.zeros_like(gs_ref)

    fwd_masks = _direction_masks(False)
    bwd_masks = _direction_masks(True)
    gf = (gf[0],) + _gate_cumsums(gf[0], fwd_masks[0], bwd_masks[0])
    gb = (gb[0],) + _gate_cumsums(gb[0], bwd_masks[0], fwd_masks[0])
    mlstm = _mlstm_stages(bqf, bkf, bvf, gf, bqb, bkb, bvb, gb, hf_ref, hb_ref, ms_ref, mm_ref, fwd_masks, bwd_masks)
    _gdn_step(cqf, ckf, cvf, gf, cqb, ckb, cvb, gb, of_ref, ob_ref, gs_ref, fwd_masks, bwd_masks,
              lambda: next(mlstm, None))
    for _ in mlstm:
        pass


def _recurrent_mixers(bqkv, cqkv, gates):
    bsz, seq, _ = bqkv.shape
    nc = seq // CHUNK
    nstate = 2 * (D_B // LANES)
    in_specs = []
    for rev in (False, True):
        cidx = (lambda c: nc - 1 - c) if rev else (lambda c: c)
        for width in (D_B, D_C):
            for part in range(3):
                in_specs.append(pl.BlockSpec((1, CHUNK, width), lambda b, c, part=part, cidx=cidx: (b, cidx(c), part)))
        in_specs.append(pl.BlockSpec((1, CHUNK, LANES), lambda b, c, cidx=cidx: (b, cidx(c), 0)))
    out_b = jax.ShapeDtypeStruct((bsz, seq, D_B), BF16)
    out_c = jax.ShapeDtypeStruct((bsz, seq, D_C), BF16)
    fwd_spec = lambda w: pl.BlockSpec((1, CHUNK, w), lambda b, c: (b, c, 0))
    bwd_spec = lambda w: pl.BlockSpec((1, CHUNK, w), lambda b, c: (b, nc - 1 - c, 0))
    return pl.pallas_call(
        _recurrent_kernel,
        grid=(bsz, nc),
        in_specs=in_specs,
        out_specs=[fwd_spec(D_B), bwd_spec(D_B), fwd_spec(D_C), bwd_spec(D_C)],
        out_shape=(out_b, out_b, out_c, out_c),
        scratch_shapes=[pltpu.VMEM((nstate, LANES, 2 * LANES), F32), pltpu.VMEM((nstate, SUBLANES, LANES), F32),
                        pltpu.VMEM((nstate, LANES, LANES), F32)],
        compiler_params=pltpu.CompilerParams(
            dimension_semantics=("parallel", "arbitrary"), vmem_limit_bytes=VMEM_LIMIT),
        name="recurrent_mixers",
    )(bqkv, bqkv, bqkv, cqkv, cqkv, cqkv, gates, bqkv, bqkv, bqkv, cqkv, cqkv, cqkv, gates)


def _out_kernel(x_ref, oa_ref, hf_ref, hb_ref, cf_ref, cb_ref, nw_ref, wz_ref, mw_ref, gw_ref, wo_ref, y_ref):
    bd = _head_block_diag(MXU_WIDTH).astype(BF16)
    x = x_ref[0]
    hn = (x * lax.rsqrt(jnp.mean(x * x, axis=-1, keepdims=True) + EPS) * nw_ref[...]).astype(BF16)
    z = _dot(hn, wz_ref[...])

    def silu(t):
        return t * _sigmoid(t)

    hsum = _sigmoid(z[:, D_A:D_A + D_B]) * (hf_ref[0].astype(F32) + hb_ref[0].astype(F32))
    csum = cf_ref[0].astype(F32) + cb_ref[0].astype(F32)
    ssb = _head_sum(hsum * hsum, bd)
    ssc = _head_sum(csum * csum, bd)
    ya = oa_ref[0].astype(F32) * silu(z[:, 0:D_A])
    yb = hsum * lax.rsqrt(ssb * (1.0 / HEAD_DIM) + EPS) * mw_ref[...] * silu(z[:, D_A + D_B:D_A + 2 * D_B])
    yc = csum * lax.rsqrt(ssc * (1.0 / HEAD_DIM) + EPS) * gw_ref[...] * silu(z[:, D_A + 2 * D_B:])
    y = jnp.concatenate([ya, yb, yc], axis=1).astype(BF16)
    y_ref[0] = x + _dot(y, wo_ref[...])


def _out_proj(x, layer, oa, hf, hb, cf, cb, nw, wz, mw, gw, wo):
    bsz, seq, d = x.shape
    tile = OUT_TILE
    row_spec = lambda a: pl.BlockSpec((1, tile, a.shape[-1]), lambda b, i: (b, i, 0))
    full = lambda a: _layer_spec(a, layer)
    return pl.pallas_call(
        _out_kernel,
        grid=(bsz, seq // tile),
        in_specs=[row_spec(a) for a in (x, oa, hf, hb, cf, cb)] + [full(a) for a in (nw, wz, mw, gw, wo)],
        out_specs=pl.BlockSpec((1, tile, d), lambda b, i: (b, i, 0)),
        out_shape=jax.ShapeDtypeStruct(x.shape, x.dtype),
        compiler_params=pltpu.CompilerParams(
            dimension_semantics=("parallel", "parallel"), vmem_limit_bytes=VMEM_LIMIT),
        name="out_proj",
    )(x, oa, hf, hb, cf, cb, nw, wz, mw, gw, wo)


def _to_gate_lanes(t):
    lead = t.shape[:-3]
    t = t.reshape(lead + (4, 2, 2))
    return jnp.swapaxes(t, -3, -2).reshape(lead + (16,))


def _gate_weights(w_in):
    nh = D_B // HEAD_DIM
    per_mixer = [_to_gate_lanes(w_in[:, off:off + 4 * nh].reshape(-1, 2, 2, nh)) for off in (_OFF_BIF, _OFF_CAB)]
    return jnp.pad(jnp.concatenate(per_mixer, axis=1), ((0, 0), (0, LANES - 2 * 16)))


def _gate_params(i_bias, f_bias, a_log, dt_bias):
    zeros = jnp.zeros_like(dt_bias)
    row0 = jnp.concatenate([_to_gate_lanes(jnp.stack([i_bias, f_bias])), _to_gate_lanes(jnp.stack([dt_bias, zeros]))])
    row1 = jnp.concatenate([jnp.zeros((16,), F32), _to_gate_lanes(jnp.stack([a_log, zeros]))])
    return jnp.pad(jnp.stack([row0, row1]), ((0, SUBLANES - 2), (0, LANES - 2 * 16)))


def _pack_params(norm_w, w_in, w_out, qk_norm_w, i_bias, f_bias, m_norm_w, conv_w, a_log, dt_bias, g_norm_w):
    w_all = jnp.concatenate([w_in[:, _OFF_AQ:_OFF_AZ], w_in[:, _OFF_BQ:_OFF_BIF], w_in[:, _OFF_CQ:_OFF_CAB],
                             _gate_weights(w_in)], axis=1).astype(BF16)
    wz = jnp.concatenate([w_in[:, _OFF_AZ:_OFF_BQ], w_in[:, _OFF_BO:_OFF_CQ], w_in[:, _OFF_CZ:]],
                         axis=1).astype(BF16)
    qkw = jnp.pad(jnp.tile(qk_norm_w, (1, MXU_WIDTH // HEAD_DIM)) * jnp.array([[HEAD_DIM ** -0.5 * LOG2E], [1.0]], F32),
                  ((0, SUBLANES - 2), (0, 0)))
    return dict(nw=norm_w[None, :], w_all=w_all, wz=wz, wo=w_out.astype(BF16), qkw=qkw,
                gpar=_gate_params(i_bias, f_bias, a_log, dt_bias),
                cw=jnp.pad(conv_w, ((0, SUBLANES - CONV_W), (0, 0))),
                mw=jnp.tile(m_norm_w, D_B // HEAD_DIM)[None, :], gw=jnp.tile(g_norm_w, D_C // HEAD_DIM)[None, :])


def _layer(x, bias_tiles, p, layer):
    aq, *akv, bqkv, cqkv, gates = _in_proj(x, layer, p["nw"], p["w_all"], p["qkw"], p["gpar"], p["cw"])
    oa = _attention(aq, akv, bias_tiles)
    hf, hb, cf, cb = _recurrent_mixers(bqkv, cqkv, gates)
    return _out_proj(x, layer, oa, hf, hb, cf, cb, p["nw"], p["wz"], p["mw"], p["gw"], p["wo"])


def kernel(x, norm_w, w_in, w_out, qk_norm_w, rel_bias, mlstm_i_bias, mlstm_f_bias, mlstm_norm_w, gdn_conv_w,
           gdn_a_log, gdn_dt_bias, gdn_norm_w):
    bias_tiles = _attn_bias_tiles(rel_bias)
    params = jax.vmap(_pack_params)(norm_w, w_in, w_out, qk_norm_w, mlstm_i_bias, mlstm_f_bias, mlstm_norm_w,
                                    gdn_conv_w, gdn_a_log, gdn_dt_bias, gdn_norm_w)
    for l in range(DEPTH):
        x = _layer(x, bias_tiles, params, l)
    return x
```

```python
import numpy as np
import jax
import jax.numpy as jnp
from jax import lax
from jax.experimental import pallas as pl
from jax.experimental.pallas import tpu as pltpu

F32 = jnp.float32
BF16 = jnp.bfloat16

D_MODEL = 1024
DEPTH = 2
EPS = 1e-6
HEAD_DIM = 64
LANES = 128
SUBLANES = 8
MXU_WIDTH = 256
D_A, D_B, D_C = 512, 256, 256
DILATED_CFGS = ((128, 1), (512, 4), (2048, 16))
N_SIDE = 64
NUM_BUCKETS = 32
REL_MAX_DIST = 1024
CONV_W = 5
CHUNK = 256
Q_SUPER = 4096
Q_BLK = 128
K_WIN = 256
ATTN_ILP = 2
ATTN_DEPTH = 1
IN_TILE = 512
OUT_TILE = 1024
NEG = -1e30
LOG2E = 1.4426950408889634
VMEM_LIMIT = 56 * 1024 * 1024

_OFF_AQ, _OFF_AZ = 0, 1536
_OFF_BQ, _OFF_BIF, _OFF_BO, _OFF_BZ = 2048, 2816, 2832, 3088
_OFF_CQ, _OFF_CAB, _OFF_CZ = 3344, 4112, 4128
_W_SPLITS = (0, 3 * D_A, 3 * D_A + 3 * D_B, 3 * D_A + 3 * D_B + 3 * D_C, 3 * D_A + 3 * D_B + 3 * D_C + LANES)


def _dot(a, b):
    return jnp.dot(a, b, preferred_element_type=F32)


def _dot_nt(a, b):
    return lax.dot_general(a, b, (((1,), (1,)), ((), ())), preferred_element_type=F32)


def _dot_tn(a, b):
    return lax.dot_general(a, b, (((0,), (0,)), ((), ())), preferred_element_type=F32)


def _split(a):
    hi = a.astype(BF16)
    lo = (a - hi.astype(F32)).astype(BF16)
    return hi, lo


def _iota2(shape, axis):
    return lax.broadcasted_iota(jnp.int32, shape, axis)


def _pair_block_diag(rows, cols):
    r = _iota2((rows, cols), 0)
    c = _iota2((rows, cols), 1)
    return (r < HEAD_DIM) == ((c & HEAD_DIM) == 0)


def _head_block_diag(n):
    return (_iota2((n, n), 0) ^ _iota2((n, n), 1)) < HEAD_DIM


def _head_sum(t, bd):
    return _dot(t.astype(BF16), bd)


def _softplus(y):
    return jnp.maximum(y, 0.0) + jnp.log1p(jnp.exp(-jnp.abs(y)))


def _sigmoid(y):
    return 1.0 / (1.0 + jnp.exp(-y))


def _in_proj_kernel(x_ref, xp_ref, xn_ref, nw_ref, w_ref, qkw_ref, gpar_ref, cw_ref,
                    a_ref, kv1_ref, kv4_ref, kv16_ref, b_ref, c_ref, g_ref, cext_ref, kvs_ref, hn_ref, kvg_ref):
    kv_refs = (kv1_ref, kv4_ref, kv16_ref)
    wa_ref, wb_ref, wc_ref, wg_ref = (w_ref.at[:, lo:hi] for lo, hi in zip(_W_SPLITS[:-1], _W_SPLITS[1:]))
    i = pl.program_id(1)
    n = pl.num_programs(1)
    tile = x_ref.shape[1]
    bd = _head_block_diag(MXU_WIDTH).astype(BF16)

    xe = jnp.concatenate([xp_ref[0], x_ref[0], xn_ref[0]], axis=0)
    ms = jnp.mean(xe * xe, axis=-1, keepdims=True)
    hne = xe * lax.rsqrt(ms + EPS) * nw_ref[...]
    hn_ref[...] = hne[SUBLANES:SUBLANES + tile].astype(BF16)
    hne = hne.astype(BF16)

    ce = _dot(hne, wc_ref[...])
    ta = [_dot(hn_ref[...], wa_ref[:, j * MXU_WIDTH:(j + 1) * MXU_WIDTH]) for j in range(3 * D_A // MXU_WIDTH)]
    p = _dot(hn_ref[...], wg_ref[...]) + gpar_ref[0:1, :]
    ssa = [_head_sum(t * t, bd) for t in ta[:4]]

    row = _iota2((tile + 2 * SUBLANES, 1), 0)
    valid = ((row >= SUBLANES) | (i > 0)) & ((row < tile + SUBLANES) | (i < n - 1))
    cext_ref[...] = jnp.where(valid, ce, 0.0)
    conv = cw_ref[0:1, :] * cext_ref[pl.ds(SUBLANES - CONV_W // 2, tile), :]
    for t in range(1, CONV_W):
        conv = conv + cw_ref[t:t + 1, :] * cext_ref[pl.ds(SUBLANES - CONV_W // 2 + t, tile), :]
    s = conv * _sigmoid(conv)
    tc = [s[:, j * MXU_WIDTH:(j + 1) * MXU_WIDTH] for j in range(3 * D_C // MXU_WIDTH)]

    ssc = [_head_sum(t * t, bd) for t in tc[:2]]
    tb = _dot(hn_ref[...], wb_ref[...])

    for j, t in enumerate(ta):
        if j < 4:
            t = t * lax.rsqrt(ssa[j] * (1.0 / HEAD_DIM) + EPS) * qkw_ref[j // 2:j // 2 + 1, :]
        if j < 2:
            a_ref[0, :, j * MXU_WIDTH:(j + 1) * MXU_WIDTH] = t
        else:
            for half in range(2):
                kvs_ref[2 * (j - 2) + half] = t[:, half * LANES:(half + 1) * LANES]
    for lt in range(kvs_ref.shape[0]):
        lanes = slice(lt * LANES, (lt + 1) * LANES)
        kv_refs[0][0, :, lanes] = kvs_ref[lt].astype(BF16)
        prev_dil, src = 1, [kvs_ref.at[lt]]
        for ci in range(1, len(DILATED_CFGS)):
            dil = DILATED_CFGS[ci][1]
            step = dil // prev_dil
            rows = tile // dil
            nxt = [None] * dil
            for r in range(dil):
                x = src[r % prev_dil][pl.ds(r // prev_dil, rows, stride=step), :]
                kv_refs[ci][0, r, :, lanes] = x.astype(BF16)
                if ci + 1 < len(DILATED_CFGS):
                    kvg_ref[lt, r] = x
                    nxt[r] = kvg_ref.at[lt, r]
            prev_dil, src = dil, nxt

    b_ref[0, :, 0:D_B] = (tb[:, 0:D_B] * HEAD_DIM ** -0.5).astype(b_ref.dtype)
    b_ref[0, :, D_B:3 * D_B] = tb[:, D_B:3 * D_B].astype(b_ref.dtype)

    lane = _iota2(p.shape, 1)
    first4 = (lane & 7) < 4
    is_b = lane < 16
    val_b = jnp.where(first4, p, -_softplus(-p)) * LOG2E
    val_c = jnp.where(first4, -jnp.exp(gpar_ref[1:2, :]) * _softplus(p) * LOG2E, _sigmoid(p))
    g_ref[0] = jnp.where(is_b, val_b, val_c)

    c_ref[0, :, 0:D_C] = (tc[0] * lax.rsqrt(ssc[0] + EPS) * HEAD_DIM ** -0.5).astype(c_ref.dtype)
    c_ref[0, :, D_C:2 * D_C] = (tc[1] * lax.rsqrt(ssc[1] + EPS)).astype(c_ref.dtype)
    c_ref[0, :, 2 * D_C:3 * D_C] = tc[2].astype(c_ref.dtype)


def _layer_spec(a, layer):
    return pl.BlockSpec((None,) + a.shape[1:], lambda b, i: (layer,) + (0,) * (a.ndim - 1))


def _in_proj(x, layer, nw, w_all, qkw, gpar, cw):
    bsz, seq, d = x.shape
    tile = IN_TILE
    nt = seq // tile
    hb = tile // SUBLANES
    full = lambda a: _layer_spec(a, layer)
    row_spec = lambda w: pl.BlockSpec((1, tile, w), lambda b, i: (b, i, 0))
    outs = [((bsz, seq, D_A), F32, row_spec(D_A))]
    for (_, dil) in DILATED_CFGS:
        if dil == 1:
            outs.append(((bsz, seq, 2 * D_A), BF16, row_spec(2 * D_A)))
        else:
            outs.append(((bsz, dil, seq // dil, 2 * D_A), BF16,
                         pl.BlockSpec((1, dil, tile // dil, 2 * D_A), lambda b, i: (b, 0, i, 0))))
    outs += [((bsz, seq, 3 * D_B), BF16, row_spec(3 * D_B)), ((bsz, seq, 3 * D_C), BF16, row_spec(3 * D_C)),
             ((bsz, seq, LANES), F32, row_spec(LANES))]
    return pl.pallas_call(
        _in_proj_kernel,
        grid=(bsz, nt),
        in_specs=[
            row_spec(d),
            pl.BlockSpec((1, SUBLANES, d), lambda b, i: (b, jnp.maximum(i * hb - 1, 0), 0)),
            pl.BlockSpec((1, SUBLANES, d), lambda b, i: (b, jnp.minimum((i + 1) * hb, nt * hb - 1), 0)),
            full(nw), full(w_all), full(qkw), full(gpar), full(cw),
        ],
        out_specs=[spec for _, _, spec in outs],
        out_shape=[jax.ShapeDtypeStruct(shape, dtype) for shape, dtype, _ in outs],
        scratch_shapes=[pltpu.VMEM((tile + 2 * SUBLANES, 3 * D_C), F32), pltpu.VMEM((2 * D_A // LANES, tile, LANES), F32),
                        pltpu.VMEM((tile, d), BF16),
                        pltpu.VMEM((2 * D_A // LANES, DILATED_CFGS[1][1], tile // DILATED_CFGS[1][1], LANES), F32)],
        compiler_params=pltpu.CompilerParams(
            dimension_semantics=("parallel", "arbitrary"), vmem_limit_bytes=VMEM_LIMIT,
            allow_input_fusion=[False, False, False, True, True, True, True, True]),
        name="in_proj",
    )(x, x, x, nw, w_all, qkw, gpar, cw)


def _t5_bucket(rel):
    half = NUM_BUCKETS // 2
    max_exact = half // 2
    n = np.abs(rel)
    large = max_exact + (np.log(np.maximum(n, 1) / max_exact) / np.log(REL_MAX_DIST / max_exact)
                         * (half - max_exact)).astype(np.int32)
    large = np.minimum(large, half - 1)
    return (rel > 0).astype(np.int32) * half + np.where(n < max_exact, n, large)


def _attn_bias_tiles(rel_bias):
    period = 5 * LANES + 1
    wide = K_WIN + 2 * N_SIDE
    lead = Q_BLK - 1 + N_SIDE
    cfg_tiles = []
    for (_, dil) in DILATED_CFGS:
        offs = dil * np.arange(-N_SIDE, N_SIDE + 1)
        bias = rel_bias[_t5_bucket(offs)].T.astype(F32) * LOG2E
        heads = bias.shape[0]
        vec = jnp.concatenate([jnp.full((heads, lead + N_SIDE), NEG, F32), bias,
                               jnp.full((heads, period - lead - 3 * N_SIDE - 1), NEG, F32)], axis=1)
        rows = jnp.tile(vec, (1, Q_BLK))[:, :Q_BLK * (period - 1)].reshape(heads, Q_BLK, period - 1)
        ext = rows[:, :, lead:lead + wide]
        variants = [ext[:, :, 2 * N_SIDE:2 * N_SIDE + K_WIN], ext[:, :, N_SIDE:N_SIDE + K_WIN],
                    ext[:, :, 0:K_WIN]]
        cfg_tiles.append(jnp.stack(variants, axis=1))
    return jnp.stack(cfg_tiles, axis=0)


def _attn_kernel(q_ref, *rest):
    ncfg = len(DILATED_CFGS)
    kv_refs = [(rest[2 * c], rest[2 * c + 1]) for c in range(ncfg)]
    bias_ref, o_ref, acc_ref, m_ref, l_ref = rest[2 * ncfg:]
    sb = pl.program_id(2)
    seq = kv_refs[0][0].shape[1]
    qsup = q_ref.shape[1]
    nblk = qsup // Q_BLK
    head0 = _iota2((Q_BLK, LANES), 1) < HEAD_DIM
    sel0 = (_iota2((Q_BLK, 2 * LANES), 1) & HEAD_DIM) == 0

    groups = [(pos, ci, g) for pos, ci in enumerate(reversed(range(len(DILATED_CFGS))))
              for g in range(nblk // ATTN_ILP)]

    def scores(pos, ci, g):
        dil = DILATED_CFGS[ci][1]
        n_idx = seq // dil
        blocks = []
        for u in range(ATTN_ILP):
            t = g * ATTN_ILP + u
            r, j = t % dil, t // dil
            qs = r + dil * (j * Q_BLK)
            i0 = sb * (qsup // dil) + j * Q_BLK
            ws = jnp.clip(i0 - N_SIDE, 0, n_idx - K_WIN)
            var = jnp.where(i0 == 0, 0, jnp.where(i0 == n_idx - Q_BLK, 2, 1))
            qrows = pl.ds(qs, Q_BLK) if dil == 1 else pl.ds(qs, Q_BLK, stride=dil)
            krows = pl.ds(pl.multiple_of(ws, N_SIDE), K_WIN)
            k_ref, v_ref = kv_refs[ci]
            if dil == 1:
                k2, v2 = k_ref[0, krows, :], v_ref[0, krows, :]
            else:
                k2, v2 = k_ref[0, r, krows, :], v_ref[0, r, krows, :]
            q2 = q_ref[0, qrows, :].astype(BF16)
            blocks.append(dict(qrows=qrows, var=var, k2=k2,
                               qh=[jnp.where(head0 if h == 0 else ~head0, q2, jnp.zeros_like(q2)) for h in range(2)],
                               vo=jnp.concatenate([v2, jnp.ones_like(v2)], axis=1)))
        chains = [(blk, h) for blk in blocks for h in range(2)]
        s = [_dot_nt(blk["qh"][h], blk["k2"]) + bias_ref[ci, h, blk["var"]] for blk, h in chains]
        mx = [jnp.max(sh, axis=-1, keepdims=True) for sh in s]
        p = [jnp.exp2(sh - m).astype(BF16) for sh, m in zip(s, mx)]
        return dict(pos=pos, blocks=blocks, chains=chains, mx=mx, p=p)

    def accumulate(st):
        pv = [_dot(ph, blk["vo"]) for ph, (blk, _) in zip(st["p"], st["chains"])]
        for u, blk in enumerate(st["blocks"]):
            qrows = blk["qrows"]
            tot = jnp.where(sel0, pv[2 * u], pv[2 * u + 1])
            num2 = tot[:, :LANES]
            l2 = tot[:, LANES:]
            m2 = jnp.where(head0, st["mx"][2 * u], st["mx"][2 * u + 1])
            if st["pos"] == 0:
                acc_ref[qrows, :] = num2
                m_ref[qrows, :] = m2
                l_ref[qrows, :] = l2
            else:
                mo = m_ref[qrows, :]
                mn = jnp.maximum(mo, m2)
                a = jnp.exp2(mo - mn)
                b = jnp.exp2(m2 - mn)
                acc_ref[qrows, :] = acc_ref[qrows, :] * a + num2 * b
                l_ref[qrows, :] = l_ref[qrows, :] * a + l2 * b
                m_ref[qrows, :] = mn

    in_flight = []
    for grp in groups:
        in_flight.append(scores(*grp))
        if len(in_flight) > ATTN_DEPTH:
            accumulate(in_flight.pop(0))
    for st in in_flight:
        accumulate(st)

    o_ref[0] = (acc_ref[...] / l_ref[...]).astype(o_ref.dtype)


def _attention(aq, akv, bias_tiles):
    bsz, seq, _ = aq.shape
    npair = D_A // LANES
    qsup = Q_SUPER
    assert seq % qsup == 0 and seq // DILATED_CFGS[-1][1] >= K_WIN
    in_specs = [pl.BlockSpec((1, qsup, LANES), lambda b, p, s: (b, s, p))]
    operands = [aq]
    for kv, (_, dil) in zip(akv, DILATED_CFGS):
        for part in range(2):
            if dil == 1:
                in_specs.append(pl.BlockSpec((1, seq, LANES), lambda b, p, s, part=part: (b, 0, part * npair + p)))
            else:
                in_specs.append(pl.BlockSpec((1, dil, seq // dil, LANES),
                                             lambda b, p, s, part=part: (b, 0, 0, part * npair + p)))
            operands.append(kv)
    in_specs.append(pl.BlockSpec((len(DILATED_CFGS), 2, 3, Q_BLK, K_WIN), lambda b, p, s: (0, p, 0, 0, 0)))
    return pl.pallas_call(
        _attn_kernel,
        grid=(bsz, npair, seq // qsup),
        in_specs=in_specs,
        out_specs=pl.BlockSpec((1, qsup, LANES), lambda b, p, s: (b, s, p)),
        out_shape=jax.ShapeDtypeStruct((bsz, seq, D_A), BF16),
        scratch_shapes=[pltpu.VMEM((qsup, LANES), F32)] * 3,
        compiler_params=pltpu.CompilerParams(
            dimension_semantics=("parallel", "parallel", "arbitrary"), vmem_limit_bytes=VMEM_LIMIT),
        name="dilated_attn",
    )(*operands, bias_tiles)


def _direction_masks(backward):
    row = _iota2((CHUNK, CHUNK), 0)
    col = _iota2((CHUNK, CHUNK), 1)
    if backward:
        return row <= col, row < col
    return row >= col, row > col


def _gate_cumsums(g, incl, incl_other):
    tri = incl.astype(BF16)
    tri_t = incl_other.astype(BF16)
    both = _dot(tri, jnp.concatenate(_split(g), axis=1))
    gb = both[:, :LANES] + both[:, LANES:]
    gt = g.T
    thi, tlo = _split(gt)
    gbt = _dot(thi, tri_t) + _dot(tlo, tri_t)
    return gb, gt, gbt


def _gate_lane(mixer, pair, kind, backward, head):
    return 16 * mixer + 8 * pair + 4 * kind + 2 * int(backward) + head


def _mlstm_stages(qf, kf, vf, gf, qb, kb, vb, gb, hf_ref, hb_ref, s_ref, m_ref, fwd_masks, bwd_masks):
    npair = D_B // LANES
    head0 = _iota2((CHUNK, LANES), 1) < HEAD_DIM
    sel0 = (_iota2((CHUNK, 2 * LANES), 1) & HEAD_DIM) == 0
    bd = _pair_block_diag(LANES, 2 * LANES)
    wide = lambda x: jnp.concatenate([x, x], axis=1)

    groups = []
    for backward, (q_ref, k_ref, v_ref, g_ref), h_ref in ((False, (qf, kf, vf, gf), hf_ref),
                                                          (True, (qb, kb, vb, gb), hb_ref)):
        incl = (bwd_masks if backward else fwd_masks)[0]
        last = 0 if backward else CHUNK - 1
        g, gbc, gt, gbt = g_ref
        i_lanes = [_gate_lane(0, p, 0, backward, h) for p in range(npair) for h in range(2)]
        f_lanes = [_gate_lane(0, p, 1, backward, h) for p in range(npair) for h in range(2)]
        for p in range(npair):
            idx = 2 * int(backward) + p
            heads = []
            for h in range(2):
                j = 2 * p + h
                li = jnp.broadcast_to(g[:, i_lanes[j]:i_lanes[j] + 1], (CHUNK, LANES))
                bc = jnp.broadcast_to(gbc[:, f_lanes[j]:f_lanes[j] + 1], (CHUNK, LANES))
                rrow = gt[i_lanes[j]:i_lanes[j] + 1, :] - gbt[f_lanes[j]:f_lanes[j] + 1, :]
                mst = m_ref[idx, h:h + 1, :]
                dmat = jnp.where(incl, wide(bc) + rrow, -jnp.inf)
                inter = bc + mst
                mt = jnp.maximum(jnp.max(dmat, axis=-1, keepdims=True), inter)
                blast = bc[last:last + 1, :]
                wlog = blast - bc + li
                mn = jnp.maximum(blast + mst, jnp.max(wlog, axis=0, keepdims=True))
                heads.append(dict(decay=jnp.exp2(dmat - wide(mt)), iw=jnp.exp2(inter - mt), emt=jnp.exp2(-mt),
                                  ws=jnp.exp2(wlog - mn), dec=jnp.exp2(blast + mst - mn), mnew=mn))
            cols = slice(p * LANES, (p + 1) * LANES)
            v2 = v_ref[0, :, cols]
            groups.append(dict(q2=q_ref[0, :, cols], k2=k_ref[0, :, cols], heads=heads, state=s_ref[idx], idx=idx,
                               h_ref=h_ref, cols=cols,
                               vo=jnp.concatenate([v2, jnp.ones_like(v2)], axis=1)))
            yield

    chains = [(d, h) for d in groups for h in range(2)]
    q_state = [_dot(d["q2"], d["state"].astype(BF16)) for d in groups]
    yield
    qk = [_dot_nt(jnp.where(head0 if h == 0 else ~head0, d["q2"], jnp.zeros_like(d["q2"])), d["k2"])
          for d, h in chains]
    yield
    sc = [(qkh * d["heads"][h]["decay"]).astype(BF16) for (d, h), qkh in zip(chains, qk)]
    yield
    pv = [_dot(sch, d["vo"]) for (d, _), sch in zip(chains, sc)]
    yield
    upd = [_dot_tn(d["k2"], (jnp.where(sel0, wide(d["heads"][0]["ws"]), wide(d["heads"][1]["ws"]))
                             * d["vo"].astype(F32)).astype(BF16)) for d in groups]
    yield
    for i, d in enumerate(groups):
        h0, h1 = d["heads"]
        tot = jnp.where(sel0, wide(h0["iw"]), wide(h1["iw"])) * q_state[i] + jnp.where(sel0, pv[2 * i], pv[2 * i + 1])
        den = jnp.maximum(jnp.abs(tot[:, LANES:]), jnp.where(head0, h0["emt"], h1["emt"]))
        d["h_ref"][0, :, d["cols"]] = (tot[:, :LANES] / den).astype(d["h_ref"].dtype)
        dec2 = jnp.where(sel0[0:1, :], wide(h0["dec"]), wide(h1["dec"]))
        s_ref[d["idx"]] = dec2 * d["state"] + jnp.where(bd, upd[i], 0.0)
        m_ref[d["idx"], 0:1, :] = h0["mnew"]
        m_ref[d["idx"], 1:2, :] = h1["mnew"]


_INV_BASE = 32


def _unit_tri_inverses(ms, upper, tick):
    n = ms[0].shape[0]
    xor = _iota2((n, n), 0) ^ _iota2((n, n), 1)
    diag = xor == 0
    one = jnp.ones((), BF16)
    zero = jnp.zeros((), BF16)
    base = xor < _INV_BASE
    ps = [jnp.where(base, m, zero) for m in ms]
    ts = [jnp.where(diag, one, -p) for p in ps]
    ps = [_dot(p, p).astype(BF16) for p in ps]
    tick()
    k = 2
    while True:
        both = [_dot(jnp.where(diag, one, p), jnp.concatenate([t, p], axis=1)) for t, p in zip(ts, ps)]
        tick()
        ts = [b[:, :n].astype(BF16) for b in both]
        if 2 * k >= _INV_BASE:
            break
        ps = [(b[:, n:] - p.astype(F32)).astype(BF16) for b, p in zip(both, ps)]
        k *= 2
    b = _INV_BASE
    while b < n:
        level = (xor >= b) & (xor < 2 * b)
        sel = [[r for r in range(n // b) if (r % 2 == 0) == up] for up in upper]
        rows = lambda a, rs: jnp.concatenate([a[r * b:(r + 1) * b] for r in rs], axis=0)
        x_rows = [_dot(rows(jnp.where(level, m, zero), rs), t).astype(BF16) for m, t, rs in zip(ms, ts, sel)]
        tick()
        zero_rows = jnp.zeros((b, n), BF16)
        xs = [jnp.concatenate([xr[rs.index(r) * b:(rs.index(r) + 1) * b] if r in rs else zero_rows
                               for r in range(n // b)], axis=0) for xr, rs in zip(x_rows, sel)]
        upd = [_dot(rows(t, rs), x) for t, x, rs in zip(ts, xs, sel)]
        tick()
        ts = [jnp.concatenate([(t[r * b:(r + 1) * b].astype(F32)
                                - u[rs.index(r) * b:(rs.index(r) + 1) * b]).astype(BF16) if r in rs
                               else t[r * b:(r + 1) * b] for r in range(n // b)], axis=0)
              for t, u, rs in zip(ts, upd, sel)]
        b *= 2
    return ts


def _gdn_step(qf, kf, vf, gf, qb, kb, vb, gb, of_ref, ob_ref, s_ref, fwd_masks, bwd_masks, tick):
    npair = D_C // LANES
    head0 = _iota2((CHUNK, LANES), 1) < HEAD_DIM
    sel0 = (_iota2((CHUNK, 2 * LANES), 1) & HEAD_DIM) == 0
    bd = _pair_block_diag(LANES, LANES)

    dirs = []
    for backward, (q_ref, k_ref, v_ref, g_ref), o_ref in ((False, (qf, kf, vf, gf), of_ref),
                                                          (True, (qb, kb, vb, gb), ob_ref)):
        incl, strict = bwd_masks if backward else fwd_masks
        last = 0 if backward else CHUNK - 1
        g, gbc, _, gbt = g_ref
        for p in range(npair):
            cols = slice(p * LANES, (p + 1) * LANES)
            q2, k2, v2 = q_ref[0, :, cols], k_ref[0, :, cols], v_ref[0, :, cols]
            a_lane = [_gate_lane(1, p, 0, backward, h) for h in range(2)]
            b_lane = [_gate_lane(1, p, 1, backward, h) for h in range(2)]
            gam = [gbc[:, a:a + 1] for a in a_lane]
            beta = [g[:, b:b + 1] for b in b_lane]
            glast2 = jnp.where(head0[0:1, :], gbc[last:last + 1, a_lane[0]:a_lane[0] + 1],
                               gbc[last:last + 1, a_lane[1]:a_lane[1] + 1])
            gam2 = jnp.where(head0, gam[0], gam[1])
            beta2 = jnp.where(head0, beta[0], beta[1])
            egam2 = jnp.exp2(gam2)
            kf32 = k2.astype(F32)
            rhs = jnp.concatenate([v2.astype(F32) * beta2, kf32 * (beta2 * egam2)], axis=1).astype(BF16)
            decay = [jnp.exp2(jnp.where(incl, gam[h] - gbt[a_lane[h]:a_lane[h] + 1, :], -jnp.inf)) for h in range(2)]
            idx = 2 * int(backward) + p
            state = s_ref[idx]
            dirs.append(dict(q2=q2, k2=k2, kf32=kf32, rhs=rhs, decay=decay, beta=beta, strict=strict, gam2=gam2,
                             egam2=egam2, glast2=glast2, state=state, state_b=state.astype(BF16), o_ref=o_ref,
                             cols=cols, idx=idx, backward=backward))

    chains = [(d, h) for d in dirs for h in range(2)]

    def head_lanes(x, h):
        return jnp.where(head0 if h == 0 else ~head0, x, 0).astype(BF16)

    kk = [_dot_nt(head_lanes(d["k2"], h), d["k2"]) for d, h in chains]
    tick()
    ms = [jnp.where(d["strict"], d["beta"][h] * kkh * d["decay"][h], 0.0).astype(BF16)
          for (d, h), kkh in zip(chains, kk)]
    qk = [_dot_nt(head_lanes(d["q2"], h), d["k2"]) for d, h in chains]
    tick()
    a_intra = [(qkh * d["decay"][h]).astype(BF16) for (d, h), qkh in zip(chains, qk)]
    half = CHUNK // 2
    early = [slice(half, CHUNK) if d["backward"] else slice(0, half) for d, _ in chains]
    late = [slice(0, half) if d["backward"] else slice(half, CHUNK) for d, _ in chains]
    t_blocks = _unit_tri_inverses([m[s, s] for m, el, lt in zip(ms, early, late) for s in (el, lt)],
                                  [d["backward"] for d, _ in chains for _ in range(2)], tick)
    t_early, t_late = t_blocks[0::2], t_blocks[1::2]
    y_early = [_dot(t, d["rhs"][el]) for (d, _), t, el in zip(chains, t_early, early)]
    y_late = [_dot(t, d["rhs"][lt]) for (d, _), t, lt in zip(chains, t_late, late)]
    tick()
    z = [_dot(m[lt, el], ye.astype(BF16)).astype(BF16) for m, ye, el, lt in zip(ms, y_early, early, late)]
    tick()
    corr = [_dot(t, zj) for t, zj in zip(t_late, z)]
    tick()
    uw = [jnp.concatenate([yl - c, ye] if d["backward"] else [ye, yl - c], axis=0)
          for (d, _), ye, yl, c in zip(chains, y_early, y_late, corr)]

    uw2 = [jnp.where(sel0, uw[2 * i], uw[2 * i + 1]) for i in range(len(dirs))]
    qg = [(d["q2"].astype(F32) * d["egam2"]).astype(BF16) for d in dirs]
    zeros = jnp.zeros((LANES, LANES), BF16)
    w_state, q_state = [None] * len(dirs), [None] * len(dirs)
    for i in range(0, len(dirs), npair):
        group = range(i, i + npair)
        state_bd = jnp.concatenate([jnp.concatenate([dirs[j]["state_b"] if j == k else zeros for k in group], axis=1)
                                    for j in group], axis=0)
        ws_cat = _dot(jnp.concatenate([uw2[j][:, LANES:].astype(BF16) for j in group], axis=1), state_bd)
        qs_cat = _dot(jnp.concatenate([qg[j] for j in group], axis=1), state_bd)
        for n, j in enumerate(group):
            w_state[j] = ws_cat[:, n * LANES:(n + 1) * LANES]
            q_state[j] = qs_cat[:, n * LANES:(n + 1) * LANES]
    tick()
    v_new = [(uw2[i][:, :LANES] - w_state[i]).astype(BF16) for i in range(len(dirs))]
    av = [_dot(a_intra[j], v_new[j // 2]) for j in range(len(chains))]
    tick()
    upd = [_dot_tn((d["kf32"] * jnp.exp2(d["glast2"] - d["gam2"])).astype(BF16), v_new[i])
           for i, d in enumerate(dirs)]
    for i, d in enumerate(dirs):
        d["o_ref"][0, :, d["cols"]] = (q_state[i] + jnp.where(head0, av[2 * i], av[2 * i + 1])).astype(d["o_ref"].dtype)
        s_ref[d["idx"]] = d["state"] * jnp.exp2(d["glast2"]) + jnp.where(bd, upd[i], 0.0)


def _recurrent_kernel(bqf, bkf, bvf, cqf, ckf, cvf, gf, bqb, bkb, bvb, cqb, ckb, cvb, gb,
                      hf_ref, hb_ref, of_ref, ob_ref, ms_ref, mm_ref, gs_ref):
    @pl.when(pl.program_id(1) == 0)
    def _():
        ms_ref[...] = jnp.zeros_like(ms_ref)
        mm_ref[...] = jnp.zeros_like(mm_ref)
        gs_ref[...] = jnp.zeros_like(gs_ref)

    fwd_masks = _direction_masks(False)
    bwd_masks = _direction_masks(True)
    gf = (gf[0],) + _gate_cumsums(gf[0], fwd_masks[0], bwd_masks[0])
    gb = (gb[0],) + _gate_cumsums(gb[0], bwd_masks[0], fwd_masks[0])
    mlstm = _mlstm_stages(bqf, bkf, bvf, gf, bqb, bkb, bvb, gb, hf_ref, hb_ref, ms_ref, mm_ref, fwd_masks, bwd_masks)
    _gdn_step(cqf, ckf, cvf, gf, cqb, ckb, cvb, gb, of_ref, ob_ref, gs_ref, fwd_masks, bwd_masks,
              lambda: next(mlstm, None))
    for _ in mlstm:
        pass


def _recurrent_mixers(bqkv, cqkv, gates):
    bsz, seq, _ = bqkv.shape
    nc = seq // CHUNK
    nstate = 2 * (D_B // LANES)
    in_specs = []
    for rev in (False, True):
        cidx = (lambda c: nc - 1 - c) if rev else (lambda c: c)
        for width in (D_B, D_C):
            for part in range(3):
                in_specs.append(pl.BlockSpec((1, CHUNK, width), lambda b, c, part=part, cidx=cidx: (b, cidx(c), part)))
        in_specs.append(pl.BlockSpec((1, CHUNK, LANES), lambda b, c, cidx=cidx: (b, cidx(c), 0)))
    out_b = jax.ShapeDtypeStruct((bsz, seq, D_B), BF16)
    out_c = jax.ShapeDtypeStruct((bsz, seq, D_C), BF16)
    fwd_spec = lambda w: pl.BlockSpec((1, CHUNK, w), lambda b, c: (b, c, 0))
    bwd_spec = lambda w: pl.BlockSpec((1, CHUNK, w), lambda b, c: (b, nc - 1 - c, 0))
    return pl.pallas_call(
        _recurrent_kernel,
        grid=(bsz, nc),
        in_specs=in_specs,
        out_specs=[fwd_spec(D_B), bwd_spec(D_B), fwd_spec(D_C), bwd_spec(D_C)],
        out_shape=(out_b, out_b, out_c, out_c),
        scratch_shapes=[pltpu.VMEM((nstate, LANES, 2 * LANES), F32), pltpu.VMEM((nstate, SUBLANES, LANES), F32),
                        pltpu.VMEM((nstate, LANES, LANES), F32)],
        compiler_params=pltpu.CompilerParams(
            dimension_semantics=("parallel", "arbitrary"), vmem_limit_bytes=VMEM_LIMIT),
        name="recurrent_mixers",
    )(bqkv, bqkv, bqkv, cqkv, cqkv, cqkv, gates, bqkv, bqkv, bqkv, cqkv, cqkv, cqkv, gates)


def _out_kernel(x_ref, oa_ref, hf_ref, hb_ref, cf_ref, cb_ref, nw_ref, wz_ref, mw_ref, gw_ref, wo_ref, y_ref):
    bd = _head_block_diag(MXU_WIDTH).astype(BF16)
    x = x_ref[0]
    hn = (x * lax.rsqrt(jnp.mean(x * x, axis=-1, keepdims=True) + EPS) * nw_ref[...]).astype(BF16)
    z = _dot(hn, wz_ref[...])

    def silu(t):
        return t * _sigmoid(t)

    hsum = _sigmoid(z[:, D_A:D_A + D_B]) * (hf_ref[0].astype(F32) + hb_ref[0].astype(F32))
    csum = cf_ref[0].astype(F32) + cb_ref[0].astype(F32)
    ssb = _head_sum(hsum * hsum, bd)
    ssc = _head_sum(csum * csum, bd)
    ya = oa_ref[0].astype(F32) * silu(z[:, 0:D_A])
    yb = hsum * lax.rsqrt(ssb * (1.0 / HEAD_DIM) + EPS) * mw_ref[...] * silu(z[:, D_A + D_B:D_A + 2 * D_B])
    yc = csum * lax.rsqrt(ssc * (1.0 / HEAD_DIM) + EPS) * gw_ref[...] * silu(z[:, D_A + 2 * D_B:])
    y = jnp.concatenate([ya, yb, yc], axis=1).astype(BF16)
    y_ref[0] = x + _dot(y, wo_ref[...])


def _out_proj(x, layer, oa, hf, hb, cf, cb, nw, wz, mw, gw, wo):
    bsz, seq, d = x.shape
    tile = OUT_TILE
    row_spec = lambda a: pl.BlockSpec((1, tile, a.shape[-1]), lambda b, i: (b, i, 0))
    full = lambda a: _layer_spec(a, layer)
    return pl.pallas_call(
        _out_kernel,
        grid=(bsz, seq // tile),
        in_specs=[row_spec(a) for a in (x, oa, hf, hb, cf, cb)] + [full(a) for a in (nw, wz, mw, gw, wo)],
        out_specs=pl.BlockSpec((1, tile, d), lambda b, i: (b, i, 0)),
        out_shape=jax.ShapeDtypeStruct(x.shape, x.dtype),
        compiler_params=pltpu.CompilerParams(
            dimension_semantics=("parallel", "parallel"), vmem_limit_bytes=VMEM_LIMIT,
            allow_input_fusion=[False] * 6 + [True] * 5),
        name="out_proj",
    )(x, oa, hf, hb, cf, cb, nw, wz, mw, gw, wo)


def _to_gate_lanes(t):
    lead = t.shape[:-3]
    t = t.reshape(lead + (4, 2, 2))
    return jnp.swapaxes(t, -3, -2).reshape(lead + (16,))


def _gate_weights(w_in):
    nh = D_B // HEAD_DIM
    per_mixer = [_to_gate_lanes(w_in[:, off:off + 4 * nh].reshape(-1, 2, 2, nh)) for off in (_OFF_BIF, _OFF_CAB)]
    return jnp.pad(jnp.concatenate(per_mixer, axis=1), ((0, 0), (0, LANES - 2 * 16)))


def _gate_params(i_bias, f_bias, a_log, dt_bias):
    zeros = jnp.zeros_like(dt_bias)
    row0 = jnp.concatenate([_to_gate_lanes(jnp.stack([i_bias, f_bias])), _to_gate_lanes(jnp.stack([dt_bias, zeros]))])
    row1 = jnp.concatenate([jnp.zeros((16,), F32), _to_gate_lanes(jnp.stack([a_log, zeros]))])
    return jnp.pad(jnp.stack([row0, row1]), ((0, SUBLANES - 2), (0, LANES - 2 * 16)))


def _pack_params(norm_w, w_in, w_out, qk_norm_w, i_bias, f_bias, m_norm_w, conv_w, a_log, dt_bias, g_norm_w):
    w_all = jnp.concatenate([w_in[:, _OFF_AQ:_OFF_AZ], w_in[:, _OFF_BQ:_OFF_BIF], w_in[:, _OFF_CQ:_OFF_CAB],
                             _gate_weights(w_in)], axis=1).astype(BF16)
    wz = jnp.concatenate([w_in[:, _OFF_AZ:_OFF_BQ], w_in[:, _OFF_BO:_OFF_CQ], w_in[:, _OFF_CZ:]],
                         axis=1).astype(BF16)
    qkw = jnp.pad(jnp.tile(qk_norm_w, (1, MXU_WIDTH // HEAD_DIM)) * jnp.array([[HEAD_DIM ** -0.5 * LOG2E], [1.0]], F32),
                  ((0, SUBLANES - 2), (0, 0)))
    return dict(nw=norm_w[None, :], w_all=w_all, wz=wz, wo=w_out.astype(BF16), qkw=qkw,
                gpar=_gate_params(i_bias, f_bias, a_log, dt_bias),
                cw=jnp.pad(conv_w, ((0, SUBLANES - CONV_W), (0, 0))),
                mw=jnp.tile(m_norm_w, D_B // HEAD_DIM)[None, :], gw=jnp.tile(g_norm_w, D_C // HEAD_DIM)[None, :])


def _layer(x, bias_tiles, p, layer):
    aq, *akv, bqkv, cqkv, gates = _in_proj(x, layer, p["nw"], p["w_all"], p["qkw"], p["gpar"], p["cw"])
    oa = _attention(aq, akv, bias_tiles)
    hf, hb, cf, cb = _recurrent_mixers(bqkv, cqkv, gates)
    return _out_proj(x, layer, oa, hf, hb, cf, cb, p["nw"], p["wz"], p["mw"], p["gw"], p["wo"])


def kernel(x, norm_w, w_in, w_out, qk_norm_w, rel_bias, mlstm_i_bias, mlstm_f_bias, mlstm_norm_w, gdn_conv_w,
           gdn_a_log, gdn_dt_bias, gdn_norm_w):
    bias_tiles = _attn_bias_tiles(rel_bias)
    params = jax.vmap(_pack_params)(norm_w, w_in, w_out, qk_norm_w, mlstm_i_bias, mlstm_f_bias, mlstm_norm_w,
                                    gdn_conv_w, gdn_a_log, gdn_dt_bias, gdn_norm_w)
    for l in range(DEPTH):
        x = _layer(x, bias_tiles, params, l)
    return x
```
